```python
import jax, jax.numpy as jnp
from jax import lax
import numpy as np

D_MODEL = 1024
BATCH = 8
SEQ = 2048
DEPTH = 2
DEC_BATCH = 32
DEC_SEQ = 4
PAST_LEN = 8192
PAGE_SIZE = 128

N_EVEN = (DEPTH + 1) // 2
N_ODD = DEPTH // 2

HEAD_DIM = 64
N_Q_HEADS = 8
N_KV_HEADS = 2
Q_PER_KV = N_Q_HEADS // N_KV_HEADS
ROT_DIM = HEAD_DIM // 4
ROPE_THETA = 500000.0
N_IDX_HEADS = 4
IDX_DIM = 64
TOPK_MAX = 256
Q_BLOCK = 128
ATTN_DIM = N_Q_HEADS * HEAD_DIM
KV_DIM = N_KV_HEADS * HEAD_DIM
CONV_DIM = D_MODEL // 2
CONV_W = 3
IN_EVEN_SIZES = (ATTN_DIM, KV_DIM, KV_DIM, N_IDX_HEADS * IDX_DIM, IDX_DIM, N_IDX_HEADS, CONV_DIM, CONV_DIM, CONV_DIM)
IN_EVEN_DIM = sum(IN_EVEN_SIZES)
IN_EVEN_SPLITS = tuple(int(c) for c in np.cumsum(IN_EVEN_SIZES)[:-1])
OUT_EVEN_DIM = ATTN_DIM + CONV_DIM
RWKV_HEAD = 64
RWKV_HEADS = D_MODEL // RWKV_HEAD
LORA_W = 64
LORA_A = 64
LORA_G = 128
GN_EPS = 64e-5
D_FF = 2816
N_EXPERTS = 8
TOP_K_EXPERTS = 2
RMS_EPS = 1e-6

RWKV_NAMES = ('rwkv_mu', 'rwkv_w_rkv', 'rwkv_w0', 'rwkv_w1', 'rwkv_w2', 'rwkv_a0', 'rwkv_a1', 'rwkv_a2',
              'rwkv_g1', 'rwkv_g2', 'rwkv_k_k', 'rwkv_k_a', 'rwkv_r_k', 'rwkv_ln_w', 'rwkv_ln_b', 'rwkv_w_o')

kernel_name = 'dsa_shortconv_rwkv7_moe_step'


def rms_norm(x, g):
    xf = x.astype(jnp.float32)
    xf = xf * lax.rsqrt(jnp.mean(xf * xf, axis=-1, keepdims=True) + RMS_EPS)
    return (xf * g.astype(jnp.float32)).astype(x.dtype)


def partial_rope(x, pos):
    half = ROT_DIM // 2
    inv_freq = ROPE_THETA ** (-jnp.arange(half, dtype=jnp.float32) / half)
    ang = pos.astype(jnp.float32)[:, None] * inv_freq[None, :]
    cos = jnp.cos(ang)[:, None, :]
    sin = jnp.sin(ang)[:, None, :]
    xr = x[..., :ROT_DIM].astype(jnp.float32)
    x1, x2 = xr[..., :half], xr[..., half:]
    rot = jnp.concatenate([x1 * cos - x2 * sin, x2 * cos + x1 * sin], axis=-1)
    return jnp.concatenate([rot.astype(x.dtype), x[..., ROT_DIM:]], axis=-1)


def take_rows(rows, idx):
    return jax.vmap(lambda r, i: r[i])(rows, idx)


def project_even(xn, pos, w_in):
    B, T, _ = xn.shape
    h = xn @ w_in
    q, k, v, iq, ik, iw, cb, cc, ch = jnp.split(h, IN_EVEN_SPLITS, axis=-1)
    q = partial_rope(q.reshape(B, T, N_Q_HEADS, HEAD_DIM), pos)
    k = partial_rope(k.reshape(B, T, N_KV_HEADS, HEAD_DIM), pos)
    v = v.reshape(B, T, N_KV_HEADS, HEAD_DIM)
    iq = partial_rope(iq.reshape(B, T, N_IDX_HEADS, IDX_DIM), pos)
    ik = partial_rope(ik[:, :, None, :], pos)[:, :, 0]
    return q, k, v, iq, ik, iw, cb, cc, ch


def index_scores(iq, ik, iw):
    s = jnp.einsum('bthd,bsd->bths', iq, ik).astype(jnp.float32) * (IDX_DIM ** -0.5)
    w = iw.astype(jnp.float32) * (N_IDX_HEADS ** -0.5)
    return jnp.einsum('bths,bth->bts', jax.nn.relu(s), w)


def sparse_attend(q, k_sel, v_sel, valid):
    B, T = q.shape[:2]
    qg = q.reshape(B, T, N_KV_HEADS, Q_PER_KV, HEAD_DIM)
    s = jnp.einsum('btgrd,btkgd->btgrk', qg, k_sel).astype(jnp.float32) * (HEAD_DIM ** -0.5)
    s = jnp.where(valid[:, :, None, None, :], s, -jnp.inf)
    p = jax.nn.softmax(s, axis=-1).astype(v_sel.dtype)
    o = jnp.einsum('btgrk,btkgd->btgrd', p, v_sel)
    return o.reshape(B, T, ATTN_DIM)


def prompt_sparse_attention(q, k, v, iq, ik, iw):
    B, T = q.shape[:2]
    n_sel = min(TOPK_MAX, T // 4)
    key_pos = jnp.arange(T)

    def block(i):
        t0 = i * Q_BLOCK
        qb = lax.dynamic_slice_in_dim(q, t0, Q_BLOCK, axis=1)
        iqb = lax.dynamic_slice_in_dim(iq, t0, Q_BLOCK, axis=1)
        iwb = lax.dynamic_slice_in_dim(iw, t0, Q_BLOCK, axis=1)
        tq = t0 + jnp.arange(Q_BLOCK)
        causal = key_pos[None, :] <= tq[:, None]
        scores = jnp.where(causal[None], index_scores(iqb, ik, iwb), -jnp.inf)
        _, sel = lax.top_k(scores, n_sel)
        valid = sel <= tq[None, :, None]
        return sparse_attend(qb, take_rows(k, sel), take_rows(v, sel), valid)

    out = lax.map(block, jnp.arange(T // Q_BLOCK))
    return jnp.transpose(out, (1, 0, 2, 3)).reshape(B, T, ATTN_DIM)


def sample_sparse_attention(q, k_new, v_new, iq, ik_new, iw, cache_k, cache_v, cache_ik, page_table):
    Bd, Tn = q.shape[:2]
    past = page_table.shape[1] * PAGE_SIZE
    n_keys = past + Tn
    n_sel = min(TOPK_MAX, n_keys // 4)
    ik_past = cache_ik[page_table].reshape(Bd, past, IDX_DIM).astype(ik_new.dtype)
    ik_all = jnp.concatenate([ik_past, ik_new], axis=1)
    tq = past + jnp.arange(Tn)
    causal = jnp.arange(n_keys)[None, :] <= tq[:, None]
    scores = jnp.where(causal[None], index_scores(iq, ik_all, iw), -jnp.inf)
    _, sel = lax.top_k(scores, n_sel)
    valid = sel <= tq[None, :, None]
    in_past = (sel < past)[..., None, None]
    sp = jnp.minimum(sel, past - 1)
    phys = take_rows(page_table, sp // PAGE_SIZE) * PAGE_SIZE + sp % PAGE_SIZE
    sn = jnp.clip(sel - past, 0, Tn - 1)
    k_pool = cache_k.reshape(-1, N_KV_HEADS, HEAD_DIM)
    v_pool = cache_v.reshape(-1, N_KV_HEADS, HEAD_DIM)
    k_sel = jnp.where(in_past, k_pool[phys].astype(k_new.dtype), take_rows(k_new, sn))
    v_sel = jnp.where(in_past, v_pool[phys].astype(v_new.dtype), take_rows(v_new, sn))
    return sparse_attend(q, k_sel, v_sel, valid)


def short_conv(cb, cc, ch, conv_w, prev):
    T = cb.shape[1]
    u = jnp.concatenate([prev.astype(cb.dtype), cc * ch], axis=1)
    y = sum(conv_w[j] * u[:, j:j + T] for j in range(CONV_W))
    return cb * y, u[:, T:]


def swiglu(x, w_gate, w_up, w_down):
    return (jax.nn.silu(x @ w_gate) * (x @ w_up)) @ w_down


def moe_swiglu(x, w_router, w_gate, w_up, w_down):
    logits = (x @ w_router).astype(jnp.float32)
    top_val, top_idx = lax.top_k(logits, TOP_K_EXPERTS)
    gate = jax.nn.softmax(top_val, axis=-1)
    combine = jnp.einsum('btk,btke->bte', gate, jax.nn.one_hot(top_idx, N_EXPERTS, dtype=jnp.float32)).astype(x.dtype)
    out = jnp.zeros_like(x)
    for e in range(N_EXPERTS):
        out = out + combine[..., e:e + 1] * swiglu(x, w_gate[e], w_up[e], w_down[e])
    return out


def rwkv7_time_mix(xn, shift_prev, wkv_prev, mu, w_rkv, w0, w1, w2, a0, a1, a2, g1, g2,
                   k_k, k_a, r_k, ln_w, ln_b, w_o):
    B, T, D = xn.shape
    f32 = jnp.float32
    x_prev = jnp.concatenate([shift_prev[:, None].astype(xn.dtype), xn[:, :-1]], axis=1)
    xx = x_prev - xn
    xr, xw, xk, xv, xa, xg = [xn + xx * mu[i] for i in range(6)]
    r = xr @ w_rkv[0]
    k = xk @ w_rkv[1]
    v = xv @ w_rkv[2]
    w_log = -jax.nn.softplus(-(w0 + jnp.tanh(xw @ w1) @ w2).astype(f32)) - 0.5
    decay = jnp.exp(-jnp.exp(w_log))
    a = jax.nn.sigmoid(a0 + (xa @ a1) @ a2)
    g = jax.nn.sigmoid(xg @ g1) @ g2

    def heads(z):
        return z.reshape(B, T, RWKV_HEADS, RWKV_HEAD).astype(f32)

    kk = heads(k * k_k)
    kk = kk * lax.rsqrt(jnp.maximum(jnp.sum(kk * kk, axis=-1, keepdims=True), 1e-24))
    k = k * (1 + (a - 1) * k_a)
    rh, kh, vh, ah, wh = heads(r), heads(k), heads(v), heads(a), heads(decay)

    def step(S, inp):
        r_t, w_t, k_t, v_t, kk_t, a_t = inp
        sa = jnp.einsum('bhvk,bhk->bhv', S, -kk_t)
        S = S * w_t[:, :, None, :] + sa[..., None] * (kk_t * a_t)[:, :, None, :] + v_t[..., None] * k_t[:, :, None, :]
        return S, jnp.einsum('bhvk,bhk->bhv', S, r_t)

    seq = tuple(jnp.swapaxes(z, 0, 1) for z in (rh, wh, kh, vh, kk, ah))
    S, y = lax.scan(step, wkv_prev.astype(f32), seq)
    y = jnp.swapaxes(y, 0, 1)
    mean = jnp.mean(y, axis=-1, keepdims=True)
    var = jnp.mean(jnp.square(y - mean), axis=-1, keepdims=True)
    y = (y - mean) * lax.rsqrt(var + GN_EPS)
    y = y * ln_w.astype(f32).reshape(RWKV_HEADS, RWKV_HEAD) + ln_b.astype(f32).reshape(RWKV_HEADS, RWKV_HEAD)
    y = y + jnp.sum(rh * kh * r_k.astype(f32), axis=-1, keepdims=True) * vh
    out = (y.reshape(B, T, D).astype(xn.dtype) * g) @ w_o
    return out, xn[:, -1], S.astype(wkv_prev.dtype)


def trunk(x, pos, conv_prev, shift_prev, wkv_prev, attend, p):
    rows_k, rows_v, rows_ik, conv_new, shift_new, wkv_new = [], [], [], [], [], []
    for layer in range(DEPTH):
        i = layer // 2
        if layer % 2 == 0:
            xn = rms_norm(x, p['norm_mix_even'][i])
            q, k, v, iq, ik, iw, cb, cc, ch = project_even(xn, pos, p['w_in_even'][i])
            attn = attend(i, q, k, v, iq, ik, iw)
            conv, conv_state = short_conv(cb, cc, ch, p['conv_w'][i], conv_prev[i])
            x = x + jnp.concatenate([attn, conv], axis=-1) @ p['w_out_even'][i]
            x = x + swiglu(rms_norm(x, p['norm_ffn_even'][i]), p['ffn_gate'][i], p['ffn_up'][i], p['ffn_down'][i])
            rows_k.append(k)
            rows_v.append(v)
            rows_ik.append(ik)
            conv_new.append(conv_state)
        else:
            xn = rms_norm(x, p['norm_mix_odd'][i])
            mix, shift_state, wkv_state = rwkv7_time_mix(xn, shift_prev[i], wkv_prev[i], *(p[n][i] for n in RWKV_NAMES))
            x = x + mix
            x = x + moe_swiglu(rms_norm(x, p['norm_ffn_odd'][i]), p['moe_router'][i], p['moe_gate'][i],
                               p['moe_up'][i], p['moe_down'][i])
            shift_new.append(shift_state)
            wkv_new.append(wkv_state)
    y = rms_norm(x, p['norm_final'])
    return (y, jnp.stack(rows_k), jnp.stack(rows_v), jnp.stack(rows_ik), jnp.stack(conv_new),
            jnp.stack(shift_new), jnp.stack(wkv_new))


def setup_inputs(seed: int = 0) -> dict:
    key = jax.random.key(seed)
    ks = iter(jax.random.split(key, 64))

    def nrm(shape, scale):
        return jax.random.normal(next(ks), shape, jnp.float32) * scale

    def gain(shape):
        return 1.0 + nrm(shape, 0.05)

    n_pages = PAST_LEN // PAGE_SIZE
    used = DEC_BATCH * n_pages
    n_pool = used + max(1, used // 4)
    E, O = N_EVEN, N_ODD
    D, H, N = D_MODEL, RWKV_HEADS, RWKV_HEAD
    return {
        'x_prompt': nrm((BATCH, SEQ, D), 1.0),
        'x_sample': nrm((DEC_BATCH, DEC_SEQ, D), 1.0),
        'cache_k': nrm((E, n_pool, PAGE_SIZE, N_KV_HEADS, HEAD_DIM), 1.0),
        'cache_v': nrm((E, n_pool, PAGE_SIZE, N_KV_HEADS, HEAD_DIM), 1.0),
        'cache_idx_k': nrm((E, n_pool, PAGE_SIZE, IDX_DIM), 1.0),
        'state_conv': nrm((E, DEC_BATCH, CONV_W - 1, CONV_DIM), 1.0),
        'state_shift': nrm((O, DEC_BATCH, D), 1.0),
        'state_wkv': nrm((O, DEC_BATCH, H, N, N), 0.5),
        'page_table': jax.random.permutation(next(ks), n_pool)[:used].reshape(DEC_BATCH, n_pages).astype(jnp.int32),
        'norm_mix_even': gain((E, D)),
        'w_in_even': nrm((E, D, IN_EVEN_DIM), D ** -0.5),
        'conv_w': nrm((E, CONV_W, CONV_DIM), CONV_W ** -0.5),
        'w_out_even': nrm((E, OUT_EVEN_DIM, D), OUT_EVEN_DIM ** -0.5),
        'norm_ffn_even': gain((E, D)),
        'ffn_gate': nrm((E, D, D_FF), D ** -0.5),
        'ffn_up': nrm((E, D, D_FF), D ** -0.5),
        'ffn_down': nrm((E, D_FF, D), D_FF ** -0.5),
        'norm_mix_odd': gain((O, D)),
        'rwkv_mu': jax.random.uniform(next(ks), (O, 6, D), jnp.float32),
        'rwkv_w_rkv': nrm((O, 3, D, D), D ** -0.5),
        'rwkv_w0': nrm((O, D), 0.5),
        'rwkv_w1': nrm((O, D, LORA_W), D ** -0.5),
        'rwkv_w2': nrm((O, LORA_W, D), 0.5 * LORA_W ** -0.5),
        'rwkv_a0': nrm((O, D), 0.1),
        'rwkv_a1': nrm((O, D, LORA_A), D ** -0.5),
        'rwkv_a2': nrm((O, LORA_A, D), LORA_A ** -0.5),
        'rwkv_g1': nrm((O, D, LORA_G), D ** -0.5),
        'rwkv_g2': nrm((O, LORA_G, D), LORA_G ** -0.5),
        'rwkv_k_k': 0.85 + nrm((O, D), 0.05),
        'rwkv_k_a': gain((O, D)),
        'rwkv_r_k': nrm((O, H, N), 0.1),
        'rwkv_ln_w': gain((O, D)),
        'rwkv_ln_b': nrm((O, D), 0.02),
        'rwkv_w_o': nrm((O, D, D), D ** -0.5),
        'norm_ffn_odd': gain((O, D)),
        'moe_router': nrm((O, D, N_EXPERTS), D ** -0.5),
        'moe_gate': nrm((O, N_EXPERTS, D, D_FF), D ** -0.5),
        'moe_up': nrm((O, N_EXPERTS, D, D_FF), D ** -0.5),
        'moe_down': nrm((O, N_EXPERTS, D_FF, D), D_FF ** -0.5),
        'norm_final': gain((D,)),
    }


def reference(x_prompt, x_sample, cache_k, cache_v, cache_idx_k, state_conv, state_shift, state_wkv, page_table,
              norm_mix_even, w_in_even, conv_w, w_out_even, norm_ffn_even, ffn_gate, ffn_up, ffn_down,
              norm_mix_odd, rwkv_mu, rwkv_w_rkv, rwkv_w0, rwkv_w1, rwkv_w2, rwkv_a0, rwkv_a1, rwkv_a2,
              rwkv_g1, rwkv_g2, rwkv_k_k, rwkv_k_a, rwkv_r_k, rwkv_ln_w, rwkv_ln_b, rwkv_w_o,
              norm_ffn_odd, moe_router, moe_gate, moe_up, moe_down, norm_final):
    p = dict(norm_mix_even=norm_mix_even, w_in_even=w_in_even, conv_w=conv_w, w_out_even=w_out_even,
             norm_ffn_even=norm_ffn_even, ffn_gate=ffn_gate, ffn_up=ffn_up, ffn_down=ffn_down,
             norm_mix_odd=norm_mix_odd, rwkv_mu=rwkv_mu, rwkv_w_rkv=rwkv_w_rkv, rwkv_w0=rwkv_w0,
             rwkv_w1=rwkv_w1, rwkv_w2=rwkv_w2, rwkv_a0=rwkv_a0, rwkv_a1=rwkv_a1, rwkv_a2=rwkv_a2,
             rwkv_g1=rwkv_g1, rwkv_g2=rwkv_g2, rwkv_k_k=rwkv_k_k, rwkv_k_a=rwkv_k_a, rwkv_r_k=rwkv_r_k,
             rwkv_ln_w=rwkv_ln_w, rwkv_ln_b=rwkv_ln_b, rwkv_w_o=rwkv_w_o, norm_ffn_odd=norm_ffn_odd,
             moe_router=moe_router, moe_gate=moe_gate, moe_up=moe_up, moe_down=moe_down, norm_final=norm_final)
    B, T = x_prompt.shape[:2]
    Tn = x_sample.shape[1]
    past = page_table.shape[1] * PAGE_SIZE
    pos_prompt = jnp.arange(T, dtype=jnp.int32)
    pos_sample = past + jnp.arange(Tn, dtype=jnp.int32)

    conv0 = jnp.zeros((N_EVEN, B, CONV_W - 1, CONV_DIM), x_prompt.dtype)
    shift0 = jnp.zeros((N_ODD, B, D_MODEL), x_prompt.dtype)
    wkv0 = jnp.zeros((N_ODD, B, RWKV_HEADS, RWKV_HEAD, RWKV_HEAD), state_wkv.dtype)

    def attend_prompt(i, q, k, v, iq, ik, iw):
        return prompt_sparse_attention(q, k, v, iq, ik, iw)

    def attend_sample(i, q, k, v, iq, ik, iw):
        return sample_sparse_attention(q, k, v, iq, ik, iw, cache_k[i], cache_v[i], cache_idx_k[i], page_table)

    y_prompt, k_p, v_p, ik_p, conv_p, shift_p, wkv_p = trunk(
        x_prompt, pos_prompt, conv0, shift0, wkv0, attend_prompt, p)
    y_sample, k_s, v_s, ik_s, conv_s, shift_s, wkv_s = trunk(
        x_sample, pos_sample, state_conv, state_shift, state_wkv, attend_sample, p)
    return (y_prompt, y_sample, k_p, v_p, ik_p, k_s, v_s, ik_s, conv_p, conv_s, shift_p, shift_s, wkv_p, wkv_s)
```

```python
import functools

import numpy as np
import jax
import jax.numpy as jnp
from jax import lax
from jax.experimental import pallas as pl
from jax.experimental.pallas import tpu as pltpu

F32 = jnp.float32
BF16 = jnp.bfloat16
HI = lax.Precision.HIGHEST

D_MODEL = 1024
PAGE_SIZE = 128
HEAD_DIM = 64
N_Q_HEADS = 8
N_KV_HEADS = 2
Q_PER_KV = N_Q_HEADS // N_KV_HEADS
ROT_DIM = HEAD_DIM // 4
ROPE_THETA = 500000.0
N_IDX_HEADS = 4
IDX_DIM = 64
TOPK_MAX = 256
Q_BLOCK = 128
ATTN_DIM = N_Q_HEADS * HEAD_DIM
KV_DIM = N_KV_HEADS * HEAD_DIM
CONV_DIM = D_MODEL // 2
CONV_W = 3
RWKV_HEAD = 64
RWKV_HEADS = D_MODEL // RWKV_HEAD
GN_EPS = 64e-5
D_FF = 2816
N_EXPERTS = 8
RMS_EPS = 1e-6

LANES = 128
SUBLANES = 8
VMEM_LIMIT = 56 * 1024 * 1024
INT_MIN = -2 ** 31
KEY_NEG_INF = INT_MIN + 0x7FFFFF
PROJ_COLS = ATTN_DIM + KV_DIM + KV_DIM + N_IDX_HEADS * IDX_DIM + LANES + 3 * CONV_DIM
WKV_CHUNK = 64


def _cparams(sem):
    return pltpu.CompilerParams(dimension_semantics=sem, vmem_limit_bytes=VMEM_LIMIT)


def _rms(x, g):
    return x * lax.rsqrt(jnp.mean(x * x, axis=-1, keepdims=True) + RMS_EPS) * g


def _dot(a, b):
    return jnp.dot(a, b, preferred_element_type=F32)


def _dot_nt(a, b, precision=None):
    return lax.dot_general(a, b, (((1,), (1,)), ((), ())), preferred_element_type=F32, precision=precision)


def _dot_tn(a, b, precision=None):
    return lax.dot_general(a, b, (((0,), (0,)), ((), ())), preferred_element_type=F32, precision=precision)


def _group_sum(x, gmat):
    outs = []
    for c in range(x.shape[1] // LANES):
        xc = x[:, c * LANES:(c + 1) * LANES]
        hi = xc.astype(BF16)
        lo = (xc - hi.astype(F32)).astype(BF16)
        outs.append(_dot(hi, gmat) + _dot(lo, gmat))
    return jnp.concatenate(outs, axis=1)


def _rope_chunk(xc, rc, rp, rm):
    return xc * rc + pltpu.roll(xc, 8, 1) * rp + pltpu.roll(xc, LANES - 8, 1) * rm


def _proj_even_body(x_ref, g_ref, w_ref, rc_ref, rp_ref, rm_ref,
                    q_ref, k_ref, v_ref, iq_ref, ikw_ref, cb_ref, u_ref):
    xn = _rms(x_ref[...], g_ref[...]).astype(BF16)
    h = _dot(xn, w_ref[...])
    rc, rp, rm = rc_ref[...], rp_ref[...], rm_ref[...]
    col = 0
    for c in range(ATTN_DIM // LANES):
        q_ref[:, c * LANES:(c + 1) * LANES] = _rope_chunk(h[:, col:col + LANES], rc, rp, rm).astype(BF16)
        col += LANES
    k_ref[...] = _rope_chunk(h[:, col:col + LANES], rc, rp, rm)
    col += LANES
    v_ref[...] = h[:, col:col + LANES]
    col += LANES
    for c in range(N_IDX_HEADS * IDX_DIM // LANES):
        iq_ref[:, c * LANES:(c + 1) * LANES] = _rope_chunk(h[:, col:col + LANES], rc, rp, rm).astype(BF16)
        col += LANES
    ikw = h[:, col:col + LANES]
    lane = lax.broadcasted_iota(jnp.int32, ikw.shape, 1)
    ikw_ref[...] = jnp.where(lane < IDX_DIM, _rope_chunk(ikw, rc, rp, rm), ikw)
    col += LANES
    cb_ref[...] = h[:, col:col + CONV_DIM]
    col += CONV_DIM
    u_ref[...] = h[:, col:col + CONV_DIM] * h[:, col + CONV_DIM:col + 2 * CONV_DIM]


def _proj_even(x, g, w, tabs, tm):
    n = x.shape[0]
    nt = tabs[0].shape[0] // tm
    row = lambda i: (i, 0)
    const = lambda i: (0, 0)
    tab = lambda i: (i % nt, 0)
    widths = (ATTN_DIM, KV_DIM, KV_DIM, N_IDX_HEADS * IDX_DIM, LANES, CONV_DIM, CONV_DIM)
    dtypes = (BF16, F32, F32, BF16, F32, F32, F32)
    return pl.pallas_call(
        _proj_even_body,
        grid=(n // tm,),
        in_specs=[pl.BlockSpec((tm, D_MODEL), row), pl.BlockSpec((1, D_MODEL), const),
                  pl.BlockSpec((D_MODEL, PROJ_COLS), const)] + [pl.BlockSpec((tm, LANES), tab)] * 3,
        out_specs=[pl.BlockSpec((tm, wd), row) for wd in widths],
        out_shape=[jax.ShapeDtypeStruct((n, wd), dt) for wd, dt in zip(widths, dtypes)],
        compiler_params=_cparams(("parallel",)),
        name="proj_even",
    )(x, g, w, *tabs)


def _rope_tables(pos):
    half = ROT_DIM // 2
    inv_freq = ROPE_THETA ** (-jnp.arange(half, dtype=F32) / half)
    ang = pos.astype(F32)[:, None] * inv_freq[None, :]
    cos, sin = jnp.cos(ang), jnp.sin(ang)
    t = pos.shape[0]
    pad = jnp.zeros((t, HEAD_DIM - ROT_DIM), F32)
    zero = jnp.zeros((t, half), F32)
    rc = jnp.concatenate([cos, cos, pad + 1.0], axis=1)
    rp = jnp.concatenate([zero, sin, pad], axis=1)
    rm = jnp.concatenate([-sin, zero, pad], axis=1)
    return tuple(jnp.tile(a, (1, LANES // HEAD_DIM)) for a in (rc, rp, rm))


def _conv_prompt_body(cb_ref, u_ref, up_ref, w_ref, y_ref, *, tiles_per_seq):
    u = u_ref[...]
    first = pl.program_id(0) % tiles_per_seq == 0
    prev = jnp.where(first, 0.0, up_ref[...])
    p1, p2 = prev[SUBLANES - 1:SUBLANES, :], prev[SUBLANES - 2:SUBLANES - 1, :]
    r = lax.broadcasted_iota(jnp.int32, u.shape, 0)
    u1 = jnp.where(r == 0, p1, pltpu.roll(u, 1, 0))
    u2 = jnp.where(r == 0, p2, jnp.where(r == 1, p1, pltpu.roll(u, 2, 0)))
    w = w_ref[...]
    y_ref[...] = cb_ref[...] * (w[0:1, :] * u2 + w[1:2, :] * u1 + w[2:3, :] * u)


def _conv_prompt(cb, u, w, seq, tc):
    n = u.shape[0]
    row = lambda i: (i, 0)
    prev = lambda i: (jnp.maximum(i * (tc // SUBLANES) - 1, 0), 0)
    return pl.pallas_call(
        functools.partial(_conv_prompt_body, tiles_per_seq=seq // tc),
        grid=(n // tc,),
        in_specs=[pl.BlockSpec((tc, CONV_DIM), row), pl.BlockSpec((tc, CONV_DIM), row),
                  pl.BlockSpec((SUBLANES, CONV_DIM), prev), pl.BlockSpec((CONV_W, CONV_DIM), lambda i: (0, 0))],
        out_specs=pl.BlockSpec((tc, CONV_DIM), row),
        out_shape=jax.ShapeDtypeStruct((n, CONV_DIM), F32),
        compiler_params=_cparams(("parallel",)),
        name="conv_prompt",
    )(cb, u, u, w)


def _conv_sample_body(cb_ref, ue_ref, w_ref, y_ref):
    w = w_ref[...]
    for t in range(y_ref.shape[0]):
        acc = w[0:1, :] * ue_ref[t] + w[1:2, :] * ue_ref[t + 1] + w[2:3, :] * ue_ref[t + 2]
        y_ref[t] = cb_ref[t] * acc


def _conv_sample(cb_t, ue_t, w):
    return pl.pallas_call(
        _conv_sample_body,
        out_shape=jax.ShapeDtypeStruct(cb_t.shape, F32),
        name="conv_sample",
    )(cb_t, ue_t, w)


def _topk_select(scores, n_sel, tri):
    rows, width = scores.shape
    sc = scores

    def key_to_float(key):
        return lax.bitcast_convert_type(key ^ ((key >> 31) & 0x7FFFFFFF), F32)

    def count_ge(key):
        ge = (sc >= key_to_float(key)) | (key <= KEY_NEG_INF)
        return jnp.sum(jnp.where(ge, 1.0, 0.0), axis=1, keepdims=True)

    thr = jnp.where(count_ge(jnp.zeros((rows, 1), jnp.int32)) >= n_sel, 0, INT_MIN).astype(jnp.int32)

    def body(i, thr):
        cand = thr + lax.shift_left(jnp.int32(1), 30 - i)
        return jnp.where(count_ge(cand) >= n_sel, cand, thr)

    thr = lax.fori_loop(0, 31, body, thr)
    thr_f = key_to_float(thr)
    gt = sc > thr_f
    eq = sc == thr_f
    need = n_sel - jnp.sum(jnp.where(gt, 1.0, 0.0), axis=1, keepdims=True)
    off = jnp.zeros((rows, 1), F32)
    parts = []
    for c in range(width // LANES):
        eqc = jnp.where(eq[:, c * LANES:(c + 1) * LANES], 1.0, 0.0)
        incl = _dot(eqc.astype(BF16), tri)
        parts.append((incl - eqc + off) < need)
        off = off + incl[:, LANES - 1:LANES]
    tie = jnp.concatenate(parts, axis=1)
    return (gt | (eq & tie)) & (sc > -jnp.inf)


def _tri_incl():
    i = np.arange(LANES)
    return jnp.asarray((i[:, None] <= i[None, :]).astype(np.float32), BF16)


def _stack_heads(x, n):
    return jnp.concatenate([x[:, h * HEAD_DIM:(h + 1) * HEAD_DIM] for h in range(n)], axis=0)


def _index_scores(iq_st, iw_st, ik_b):
    rows = iq_st.shape[0] // N_IDX_HEADS
    s = _dot_nt(iq_st, ik_b)
    term = jnp.maximum(s, 0.0) * (IDX_DIM ** -0.5) * (iw_st * (N_IDX_HEADS ** -0.5))
    acc = term[0:rows, :]
    for h in range(1, N_IDX_HEADS):
        acc = acc + term[h * rows:(h + 1) * rows, :]
    return acc


def _group_logits(qs, kg, sel):
    s = _dot_nt(qs, kg) * (HEAD_DIM ** -0.5)
    return jnp.where(jnp.concatenate([sel] * Q_PER_KV, axis=0), s, -jnp.inf)


def _masked_attention(q, kb, vb, sel):
    rows = q.shape[0]
    outs = []
    for g in range(N_KV_HEADS):
        qs = _stack_heads(q[:, g * Q_PER_KV * HEAD_DIM:(g + 1) * Q_PER_KV * HEAD_DIM], Q_PER_KV)
        s = _group_logits(qs, kb[:, g * HEAD_DIM:(g + 1) * HEAD_DIM], sel)
        p = jnp.exp(s - jnp.max(s, axis=1, keepdims=True))
        o = _dot(p.astype(BF16), vb[:, g * HEAD_DIM:(g + 1) * HEAD_DIM]) / jnp.sum(p, axis=1, keepdims=True)
        outs.extend(o[r * rows:(r + 1) * rows, :] for r in range(Q_PER_KV))
    return jnp.concatenate(outs, axis=1)


def _prompt_attn_body(q_ref, iq_ref, iwq_ref, k_ref, v_ref, ikw_ref, tri_ref, o_ref,
                      kb_scr, vb_scr, ikb_scr, *, n_sel):
    i = pl.program_id(1)

    @pl.when(i == 0)
    def _():
        kb_scr[...] = k_ref[...].astype(BF16)
        vb_scr[...] = v_ref[...].astype(BF16)
        ikb_scr[...] = ikw_ref[...].astype(BF16)

    iw = iwq_ref[...]
    iw_st = jnp.concatenate([iw[:, IDX_DIM + h:IDX_DIM + h + 1] for h in range(N_IDX_HEADS)], axis=0)
    scores = _index_scores(_stack_heads(iq_ref[...], N_IDX_HEADS), iw_st, ikb_scr[:, :IDX_DIM])
    tq = i * Q_BLOCK + lax.broadcasted_iota(jnp.int32, scores.shape, 0)
    key_pos = lax.broadcasted_iota(jnp.int32, scores.shape, 1)
    scores = jnp.where(key_pos <= tq, scores, -jnp.inf)
    sel = _topk_select(scores, n_sel, tri_ref[...])
    o_ref[...] = _masked_attention(q_ref[...], kb_scr[...], vb_scr[...], sel).astype(o_ref.dtype)


def _prompt_attention(q, iq, ikw, k, v, batch, seq):
    n_sel = min(TOPK_MAX, seq // 4)
    nqb = seq // Q_BLOCK
    blk = lambda b, i: (b * nqb + i, 0)
    full = lambda b, i: (b, 0)
    return pl.pallas_call(
        functools.partial(_prompt_attn_body, n_sel=n_sel),
        grid=(batch, nqb),
        in_specs=[pl.BlockSpec((Q_BLOCK, ATTN_DIM), blk), pl.BlockSpec((Q_BLOCK, N_IDX_HEADS * IDX_DIM), blk),
                  pl.BlockSpec((Q_BLOCK, LANES), blk),
                  pl.BlockSpec((seq, KV_DIM), full), pl.BlockSpec((seq, KV_DIM), full),
                  pl.BlockSpec((seq, LANES), full), pl.BlockSpec((LANES, LANES), lambda b, i: (0, 0))],
        out_specs=pl.BlockSpec((Q_BLOCK, ATTN_DIM), blk),
        out_shape=jax.ShapeDtypeStruct((batch * seq, ATTN_DIM), BF16),
        scratch_shapes=[pltpu.VMEM((seq, KV_DIM), BF16), pltpu.VMEM((seq, KV_DIM), BF16),
                        pltpu.VMEM((seq, LANES), BF16)],
        compiler_params=_cparams(("parallel", "arbitrary")),
        name="prompt_attention",
    )(q, iq, ikw, k, v, ikw, _tri_incl())


PAGES_PER_STEP = 16


def _sample_scores_body(pt_ref, iq_ref, iw_ref, *rest):
    page_refs, s_ref = rest[:PAGES_PER_STEP], rest[PAGES_PER_STEP]
    for j, pr in enumerate(page_refs):
        s_ref[0, :, j * PAGE_SIZE:(j + 1) * PAGE_SIZE] = _index_scores(iq_ref[0], iw_ref[0], pr[0].astype(BF16))


def _sample_scores(page_table, iq_st, iw_st, cache_ik):
    bd, n_pages = page_table.shape
    rows = iq_st.shape[1] // N_IDX_HEADS
    steps = n_pages // PAGES_PER_STEP
    per_b = lambda b, p, pt: (b, 0, 0)
    page = lambda j: (lambda b, p, pt: (pt[b, p * PAGES_PER_STEP + j], 0, 0))
    return pl.pallas_call(
        _sample_scores_body,
        grid_spec=pltpu.PrefetchScalarGridSpec(
            num_scalar_prefetch=1,
            grid=(bd, steps),
            in_specs=[pl.BlockSpec((1,) + iq_st.shape[1:], per_b), pl.BlockSpec((1,) + iw_st.shape[1:], per_b)]
                     + [pl.BlockSpec((1, PAGE_SIZE, IDX_DIM), page(j)) for j in range(PAGES_PER_STEP)],
            out_specs=pl.BlockSpec((1, rows, PAGES_PER_STEP * PAGE_SIZE), lambda b, p, pt: (b, 0, p)),
        ),
        out_shape=jax.ShapeDtypeStruct((bd, rows, n_pages * PAGE_SIZE), F32),
        compiler_params=_cparams(("parallel", "arbitrary")),
        name="sample_scores",
    )(page_table, iq_st, iw_st, *([cache_ik] * PAGES_PER_STEP))


def _sample_select_body(sp_ref, iq_ref, iwq_ref, ikn_ref, tri_ref, m_ref, *, n_sel, tn):
    rows = sp_ref.shape[0]
    iw = iwq_ref[...]
    iw_st = jnp.concatenate([iw[:, IDX_DIM + h:IDX_DIM + h + 1] for h in range(N_IDX_HEADS)], axis=0)
    s_new = _index_scores(_stack_heads(iq_ref[...], N_IDX_HEADS), iw_st, ikn_ref[...].astype(BF16))
    r = lax.broadcasted_iota(jnp.int32, s_new.shape, 0)
    c = lax.broadcasted_iota(jnp.int32, s_new.shape, 1)
    same = (r // tn == c // tn) & (c <= r)
    fold = jnp.where((lax.broadcasted_iota(jnp.int32, (rows, LANES), 0) % tn)
                     == lax.broadcasted_iota(jnp.int32, (rows, LANES), 1), 1.0, 0.0)
    picked = jnp.where(same, s_new, 0.0)
    hi = picked.astype(BF16)
    mid = (picked - hi.astype(F32)).astype(BF16)
    lo = (picked - hi.astype(F32) - mid.astype(F32)).astype(BF16)
    fb = fold.astype(BF16)
    new_chunk = _dot(hi, fb) + _dot(mid, fb) + _dot(lo, fb)
    lane = lax.broadcasted_iota(jnp.int32, (rows, LANES), 1)
    tpos = lax.broadcasted_iota(jnp.int32, (rows, LANES), 0) % tn
    new_chunk = jnp.where(lane <= tpos, new_chunk, -jnp.inf)
    scores = jnp.concatenate([sp_ref[...], new_chunk], axis=1)
    sel = _topk_select(scores, n_sel, tri_ref[...])
    m_ref[...] = jnp.where(sel, 1.0, 0.0)


def _sample_select(s_past, iq, ikw, tn):
    rows, past = s_past.shape
    n_sel = min(TOPK_MAX, (past + tn) // 4)
    return pl.pallas_call(
        functools.partial(_sample_select_body, n_sel=n_sel, tn=tn),
        out_shape=jax.ShapeDtypeStruct((rows, past + LANES), F32),
        compiler_params=pltpu.CompilerParams(vmem_limit_bytes=VMEM_LIMIT),
        name="sample_select",
    )(s_past, iq, ikw, ikw[:, :IDX_DIM], _tri_incl())


def _sample_attn_body(pt_ref, q_ref, m_ref, kn_ref, vn_ref, mn_ref, *rest):
    k_refs, v_refs = rest[:PAGES_PER_STEP], rest[PAGES_PER_STEP:2 * PAGES_PER_STEP]
    o_ref, m_scr, l_scr, acc_scr = rest[2 * PAGES_PER_STEP:]
    p = pl.program_id(1)

    @pl.when(p == 0)
    def _():
        m_scr[...] = jnp.full(m_scr.shape, -jnp.inf, F32)
        l_scr[...] = jnp.zeros(l_scr.shape, F32)
        acc_scr[...] = jnp.zeros(acc_scr.shape, F32)

    def update(kb, vb, sel):
        for g in range(N_KV_HEADS):
            s = _group_logits(q_ref[0, g], kb[:, g * HEAD_DIM:(g + 1) * HEAD_DIM], sel)
            m_old = m_scr[g]
            m_new = jnp.maximum(m_old, jnp.max(s, axis=1, keepdims=True))
            m_safe = jnp.where(m_new == -jnp.inf, 0.0, m_new)
            alpha = jnp.exp(m_old - m_safe)
            pe = jnp.exp(s - m_safe)
            l_scr[g] = alpha * l_scr[g] + jnp.sum(pe, axis=1, keepdims=True)
            acc_scr[g] = alpha * acc_scr[g] + _dot(pe.astype(BF16), vb[:, g * HEAD_DIM:(g + 1) * HEAD_DIM])
            m_scr[g] = m_new

    kb = jnp.concatenate([r[0] for r in k_refs], axis=0).astype(BF16)
    vb = jnp.concatenate([r[0] for r in v_refs], axis=0).astype(BF16)
    update(kb, vb, m_ref[0] > 0.5)

    @pl.when(p == pl.num_programs(1) - 1)
    def _():
        update(kn_ref[0].astype(BF16), vn_ref[0].astype(BF16), mn_ref[0] > 0.5)
        for g in range(N_KV_HEADS):
            l = l_scr[g]
            o_ref[0, g] = acc_scr[g] / jnp.where(l == 0.0, 1.0, l)


def _sample_attention(page_table, q_st, mask, k_new, v_new, cache_k, cache_v):
    bd, n_pages = page_table.shape
    steps = n_pages // PAGES_PER_STEP
    per_b = lambda b, p, pt: (b, 0, 0)
    per_b4 = lambda b, p, pt: (b, 0, 0, 0)
    page = lambda j: (lambda b, p, pt: (pt[b, p * PAGES_PER_STEP + j], 0, 0))
    width = PAGES_PER_STEP * PAGE_SIZE
    rows = q_st.shape[2]
    mrows = mask.shape[1]
    return pl.pallas_call(
        _sample_attn_body,
        grid_spec=pltpu.PrefetchScalarGridSpec(
            num_scalar_prefetch=1,
            grid=(bd, steps),
            in_specs=[pl.BlockSpec((1,) + q_st.shape[1:], per_b4),
                      pl.BlockSpec((1, mrows, width), lambda b, p, pt: (b, 0, p)),
                      pl.BlockSpec((1, PAGE_SIZE, KV_DIM), per_b), pl.BlockSpec((1, PAGE_SIZE, KV_DIM), per_b),
                      pl.BlockSpec((1, mrows, LANES), lambda b, p, pt: (b, 0, n_pages))]
                     + [pl.BlockSpec((1, PAGE_SIZE, KV_DIM), page(j)) for j in range(PAGES_PER_STEP)] * 2,
            out_specs=pl.BlockSpec((1, N_KV_HEADS, rows, HEAD_DIM), per_b4),
            scratch_shapes=[pltpu.VMEM((N_KV_HEADS, rows, 1), F32), pltpu.VMEM((N_KV_HEADS, rows, 1), F32),
                            pltpu.VMEM((N_KV_HEADS, rows, HEAD_DIM), F32)],
        ),
        out_shape=jax.ShapeDtypeStruct((bd, N_KV_HEADS, rows, HEAD_DIM), F32),
        compiler_params=_cparams(("parallel", "arbitrary")),
        name="sample_attention",
    )(page_table, q_st, mask, k_new, v_new, mask, *([cache_k] * PAGES_PER_STEP), *([cache_v] * PAGES_PER_STEP))


def _out_even_body(x_ref, a_ref, c_ref, wa_ref, wc_ref, o_ref):
    o_ref[...] = x_ref[...] + _dot(a_ref[...], wa_ref[...]) + _dot(c_ref[...].astype(BF16), wc_ref[...])


def _out_even(x, attn, conv, wa, wc, tm):
    n = x.shape[0]
    row = lambda i: (i, 0)
    const = lambda i: (0, 0)
    return pl.pallas_call(
        _out_even_body,
        grid=(n // tm,),
        in_specs=[pl.BlockSpec((tm, D_MODEL), row), pl.BlockSpec((tm, ATTN_DIM), row),
                  pl.BlockSpec((tm, CONV_DIM), row), pl.BlockSpec((ATTN_DIM, D_MODEL), const),
                  pl.BlockSpec((CONV_DIM, D_MODEL), const)],
        out_specs=pl.BlockSpec((tm, D_MODEL), row),
        out_shape=jax.ShapeDtypeStruct((n, D_MODEL), F32),
        compiler_params=_cparams(("parallel",)),
        name="out_even",
    )(x, attn, conv, wa, wc)


def _mixer_body(*refs, n_exp, emit_sum):
    if n_exp > 1:
        x_ref, g_ref, wr_ref, wg_ref, wu_ref, wd_ref, g2_ref = refs[:7]
        outs = refs[7:]
    else:
        x_ref, g_ref, wg_ref, wu_ref, wd_ref, g2_ref = refs[:6]
        outs = refs[6:]
    n_out = 2 if emit_sum else 1
    out_refs, (xn_scr, acc_scr, cw_scr) = outs[:n_out], outs[n_out:]
    e, f = pl.program_id(1), pl.program_id(2)

    @pl.when((e == 0) & (f == 0))
    def _():
        xn = _rms(x_ref[...], g_ref[...]).astype(BF16)
        xn_scr[...] = xn
        acc_scr[...] = jnp.zeros(acc_scr.shape, F32)
        if n_exp > 1:
            logits = _dot(xn, wr_ref[...])
            lane = lax.broadcasted_iota(jnp.int32, logits.shape, 1).astype(F32)
            lg = jnp.where(lane < n_exp, logits, -jnp.inf)
            m1 = jnp.max(lg, axis=1, keepdims=True)
            i1 = jnp.min(jnp.where(lg == m1, lane, float(LANES)), axis=1, keepdims=True)
            lg2 = jnp.where(lane == i1, -jnp.inf, lg)
            m2 = jnp.max(lg2, axis=1, keepdims=True)
            i2 = jnp.min(jnp.where(lg2 == m2, lane, float(LANES)), axis=1, keepdims=True)
            e2 = jnp.exp(m2 - m1)
            cw_scr[...] = jnp.where(lane == i1, 1.0 / (1.0 + e2), 0.0) + jnp.where(lane == i2, e2 / (1.0 + e2), 0.0)

    xn = xn_scr[...]
    gate = _dot(xn, wg_ref[0])
    h = gate * jax.nn.sigmoid(gate) * _dot(xn, wu_ref[0])
    if n_exp > 1:
        lane = lax.broadcasted_iota(jnp.int32, cw_scr.shape, 1)
        h = h * jnp.sum(jnp.where(lane == e, cw_scr[...], 0.0), axis=1, keepdims=True)
    acc_scr[...] += _dot(h.astype(BF16), wd_ref[0])

    @pl.when((e == pl.num_programs(1) - 1) & (f == pl.num_programs(2) - 1))
    def _():
        y = x_ref[...] + acc_scr[...]
        if emit_sum:
            out_refs[0][...] = y
        out_refs[-1][...] = _rms(y, g2_ref[...])


def _mixer(x, g, wr, wg, wu, wd, g2, tm, tf, emit_sum):
    n = x.shape[0]
    n_exp = wg.shape[0]
    row = lambda i, e, f: (i, 0)
    const = lambda i, e, f: (0, 0)
    in_specs = [pl.BlockSpec((tm, D_MODEL), row), pl.BlockSpec((1, D_MODEL), const)]
    args = [x, g]
    if n_exp > 1:
        in_specs.append(pl.BlockSpec((D_MODEL, LANES), const))
        args.append(wr)
    in_specs += [pl.BlockSpec((1, D_MODEL, tf), lambda i, e, f: (e, 0, f)),
                 pl.BlockSpec((1, D_MODEL, tf), lambda i, e, f: (e, 0, f)),
                 pl.BlockSpec((1, tf, D_MODEL), lambda i, e, f: (e, f, 0)),
                 pl.BlockSpec((1, D_MODEL), const)]
    args += [wg, wu, wd, g2]
    n_out = 2 if emit_sum else 1
    return pl.pallas_call(
        functools.partial(_mixer_body, n_exp=n_exp, emit_sum=emit_sum),
        grid=(n // tm, n_exp, D_FF // tf),
        in_specs=in_specs,
        out_specs=[pl.BlockSpec((tm, D_MODEL), row)] * n_out,
        out_shape=[jax.ShapeDtypeStruct((n, D_MODEL), F32)] * n_out,
        scratch_shapes=[pltpu.VMEM((tm, D_MODEL), BF16), pltpu.VMEM((tm, D_MODEL), F32),
                        pltpu.VMEM((tm, LANES), F32)],
        compiler_params=_cparams(("parallel", "arbitrary", "arbitrary")),
        name="mixer_moe" if n_exp > 1 else "mixer_ffn",
    )(*args)


def _group_ones():
    i = np.arange(LANES) // RWKV_HEAD
    return jnp.asarray((i[:, None] == i[None, :]).astype(np.float32), BF16)


def _rwkv_proj_body(xn_ref, xp_ref, mu_ref, vec_ref, wrkv_ref, w1_ref, w2_ref, a1_ref, a2_ref, g1_ref, g2_ref,
                    gm_ref, r_ref, lw_ref, k_ref, v_ref, kk_ref, ka_ref, g_ref):
    xn = xn_ref[...]
    xx = xp_ref[...] - xn
    mu = mu_ref[...]
    mix = lambda i: (xn + xx * mu[i:i + 1, :]).astype(BF16)
    vec = vec_ref[...]
    w0, a0, k_k, k_a = vec[0:1, :], vec[1:2, :], vec[2:3, :], vec[3:4, :]
    r_ref[...] = _dot(mix(0), wrkv_ref[0])
    z = w0 + _dot(jnp.tanh(_dot(mix(1), w1_ref[...])).astype(BF16), w2_ref[...])
    softplus = jnp.maximum(-z, 0.0) + jnp.log(1.0 + jnp.exp(-jnp.abs(z)))
    lw_ref[...] = -jnp.exp(-softplus - 0.5)
    k = _dot(mix(2), wrkv_ref[1])
    v_ref[...] = _dot(mix(3), wrkv_ref[2])
    a = jax.nn.sigmoid(a0 + _dot(_dot(mix(4), a1_ref[...]).astype(BF16), a2_ref[...]))
    g_ref[...] = _dot(jax.nn.sigmoid(_dot(mix(5), g1_ref[...])).astype(BF16), g2_ref[...])
    kk = k * k_k
    kk = kk * lax.rsqrt(jnp.maximum(_group_sum(kk * kk, gm_ref[...]), 1e-24))
    kk_ref[...] = kk
    ka_ref[...] = kk * a
    k_ref[...] = k * (1.0 + (a - 1.0) * k_a)


def _rwkv_proj(xn, xp, mu, vec, wrkv, w1, w2, a1, a2, g1, g2, tm):
    n = xn.shape[0]
    row = lambda i: (i, 0)
    c2 = lambda i: (0, 0)
    c3 = lambda i: (0, 0, 0)
    full = lambda a: pl.BlockSpec(a.shape, c3 if a.ndim == 3 else c2)
    gm = _group_ones()
    consts = [mu, vec, wrkv, w1, w2, a1, a2, g1, g2, gm]
    return pl.pallas_call(
        _rwkv_proj_body,
        grid=(n // tm,),
        in_specs=[pl.BlockSpec((tm, D_MODEL), row)] * 2 + [full(a) for a in consts],
        out_specs=[pl.BlockSpec((tm, D_MODEL), row)] * 7,
        out_shape=[jax.ShapeDtypeStruct((n, D_MODEL), F32)] * 7,
        compiler_params=_cparams(("parallel",)),
        name="rwkv_proj",
    )(xn, xp, *consts)


def _wkv_body(r_ref, lw_ref, k_ref, v_ref, kk_ref, ka_ref, s0_ref, y_ref, sT_ref,
              s_scr, ar_scr, bk_scr, wc_scr, *, chunk):
    c = pl.program_id(1)

    @pl.when(c == 0)
    def _():
        s_scr[...] = s0_ref[0]

    row = lax.broadcasted_iota(jnp.int32, (chunk, chunk), 0)
    colm = lax.broadcasted_iota(jnp.int32, (chunk, chunk), 1)
    lower_incl = row >= colm
    lower_strict = row > colm
    lw = lw_ref[...]
    cum = jnp.dot(jnp.where(lower_incl, 1.0, 0.0), lw, preferred_element_type=F32, precision=HI)
    w_incl = jnp.exp(cum)
    w_inv = jnp.exp(-cum)
    kk = kk_ref[...]
    ar_scr[0:chunk, :] = -kk * jnp.exp(cum - lw)
    ar_scr[chunk:2 * chunk, :] = r_ref[...] * w_incl
    bk_scr[0:chunk, :] = ka_ref[...] * w_inv
    bk_scr[chunk:2 * chunk, :] = k_ref[...] * w_inv
    wc_scr[...] = jnp.broadcast_to(w_incl[chunk - 1:chunk, :], wc_scr.shape)
    n_dbl = max(1, int(np.ceil(np.log2(chunk))))

    def pair(p, carry):
        lanes = pl.ds(pl.multiple_of(p * LANES, LANES), LANES)
        ar2, bk2, v2, wc2 = ar_scr[:, lanes], bk_scr[:, lanes], v_ref[:, lanes], wc_scr[:, lanes]
        ys = []
        for j in range(LANES // RWKV_HEAD):
            sl = slice(j * RWKV_HEAD, (j + 1) * RWKV_HEAD)
            ar, bk, v = ar2[:, sl], bk2[:, sl], v2[:, sl]
            s0 = s_scr[2 * p + j]
            gram = _dot_nt(ar, bk, HI)
            l_ab = jnp.where(lower_strict, gram[:chunk, :chunk], 0.0)
            l_ak = jnp.where(lower_strict, gram[:chunk, chunk:], 0.0)
            m_rb = jnp.where(lower_incl, gram[chunk:, :chunk], 0.0)
            m_rk = jnp.where(lower_incl, gram[chunk:, chunk:], 0.0)
            xs = _dot_nt(ar, s0, HI)
            u = xs[:chunk, :] + jnp.dot(l_ak, v, preferred_element_type=F32, precision=HI)
            lp = l_ab
            for d in range(n_dbl):
                u = u + jnp.dot(lp, u, preferred_element_type=F32, precision=HI)
                if d + 1 < n_dbl:
                    lp = jnp.dot(lp, lp, preferred_element_type=F32, precision=HI)
            uv = jnp.concatenate([u, v], axis=0)
            y = xs[chunk:, :] + jnp.dot(jnp.concatenate([m_rb, m_rk], axis=1), uv,
                                        preferred_element_type=F32, precision=HI)
            s_scr[2 * p + j] = (s0 + _dot_tn(uv, bk, HI)) * wc2[0:RWKV_HEAD, sl]
            ys.append(y)
        y_ref[:, lanes] = jnp.concatenate(ys, axis=1)
        return carry

    lax.fori_loop(0, D_MODEL // LANES, pair, 0)

    @pl.when(c == pl.num_programs(1) - 1)
    def _():
        sT_ref[0] = s_scr[...]


def _wkv(r, lw, k, v, kk, ka, s0, batch, seq, chunk):
    nc = seq // chunk
    blk = lambda b, c: (b * nc + c, 0)
    st = lambda b, c: (b, 0, 0, 0)
    state = pl.BlockSpec((1, RWKV_HEADS, RWKV_HEAD, RWKV_HEAD), st)
    return pl.pallas_call(
        functools.partial(_wkv_body, chunk=chunk),
        grid=(batch, nc),
        in_specs=[pl.BlockSpec((chunk, D_MODEL), blk)] * 6 + [state],
        out_specs=[pl.BlockSpec((chunk, D_MODEL), blk), state],
        out_shape=[jax.ShapeDtypeStruct((batch * seq, D_MODEL), F32),
                   jax.ShapeDtypeStruct(s0.shape, F32)],
        scratch_shapes=[pltpu.VMEM((RWKV_HEADS, RWKV_HEAD, RWKV_HEAD), F32),
                        pltpu.VMEM((2 * chunk, D_MODEL), F32), pltpu.VMEM((2 * chunk, D_MODEL), F32),
                        pltpu.VMEM((RWKV_HEAD, D_MODEL), F32)],
        compiler_params=_cparams(("parallel", "arbitrary")),
        name="wkv",
    )(r, lw, k, v, kk, ka, s0)


def _rwkv_out_body(x_ref, y_ref, r_ref, k_ref, v_ref, g_ref, vec_ref, gm_ref, wo_ref, o_ref):
    gm = gm_ref[...]
    vec = vec_ref[...]
    ln_w, ln_b, r_k = vec[0:1, :], vec[1:2, :], vec[2:3, :]
    y = y_ref[...]
    mean = _group_sum(y, gm) * (1.0 / RWKV_HEAD)
    d = y - mean
    var = _group_sum(d * d, gm) * (1.0 / RWKV_HEAD)
    yn = d * lax.rsqrt(var + GN_EPS) * ln_w + ln_b
    yn = yn + _group_sum(r_ref[...] * k_ref[...] * r_k, gm) * v_ref[...]
    o_ref[...] = x_ref[...] + _dot((yn * g_ref[...]).astype(BF16), wo_ref[...])


def _rwkv_out(x, y, r, k, v, g, vec, wo, tm):
    n = x.shape[0]
    row = lambda i: (i, 0)
    const = lambda i: (0, 0)
    gm = _group_ones()
    return pl.pallas_call(
        _rwkv_out_body,
        grid=(n // tm,),
        in_specs=[pl.BlockSpec((tm, D_MODEL), row)] * 6
                 + [pl.BlockSpec(vec.shape, const), pl.BlockSpec(gm.shape, const), pl.BlockSpec(wo.shape, const)],
        out_specs=pl.BlockSpec((tm, D_MODEL), row),
        out_shape=jax.ShapeDtypeStruct((n, D_MODEL), F32),
        compiler_params=_cparams(("parallel",)),
        name="rwkv_out",
    )(x, y, r, k, v, g, vec, gm, wo)


def _pick_tile(n, want):
    t = min(n, want)
    while n % t:
        t //= 2
    return t


def _trunk(x, pos, conv_prev, shift_prev, wkv_prev, wts, sample):
    batch, seq, _ = x.shape
    n = batch * seq
    xf = x.reshape(n, D_MODEL)
    tm = _pick_tile(n, 512)

    tabs = _rope_tables(pos)
    if sample is not None:
        tabs = tuple(jnp.tile(t, (batch, 1)) for t in tabs)
    q, k, v, iq, ikw, cb, u = _proj_even(xf, wts['norm_mix_even'], wts['w_in'], tabs, tm)
    if sample is None:
        attn = _prompt_attention(q, iq, ikw, k, v, batch, seq)
        conv = _conv_prompt(cb, u, wts['conv_w'], seq, _pick_tile(seq, 512))
        conv_state = u.reshape(batch, seq, CONV_DIM)[:, seq - (CONV_W - 1):]
    else:
        cache_k, cache_v, cache_ik, page_table = sample
        tpad = SUBLANES - seq

        def stack(a, heads):
            a = a.reshape(batch, seq, heads, -1).transpose(0, 2, 1, 3)
            a = jnp.pad(a, ((0, 0), (0, 0), (0, tpad), (0, 0)))
            return a.reshape(batch, heads * SUBLANES, a.shape[-1])

        iq_st = stack(iq, N_IDX_HEADS)
        iw_st = stack(ikw[:, IDX_DIM:IDX_DIM + N_IDX_HEADS], N_IDX_HEADS)
        s_past = _sample_scores(page_table, iq_st, iw_st, cache_ik)[:, :seq]
        mask = _sample_select(s_past.reshape(n, -1), iq, ikw, seq)
        mask = jnp.pad(mask.reshape(batch, seq, -1), ((0, 0), (0, tpad), (0, 0)))
        q_st = stack(q, N_Q_HEADS).reshape(batch, N_KV_HEADS, Q_PER_KV * SUBLANES, HEAD_DIM)
        padk = lambda a: jnp.pad(a.reshape(batch, seq, KV_DIM), ((0, 0), (0, PAGE_SIZE - seq), (0, 0)))
        o = _sample_attention(page_table, q_st, mask, padk(k), padk(v), cache_k, cache_v)
        attn = (o.reshape(batch, N_Q_HEADS, SUBLANES, HEAD_DIM)[:, :, :seq].transpose(0, 2, 1, 3)
                .reshape(n, ATTN_DIM).astype(BF16))
        u3 = jnp.concatenate([conv_prev, u.reshape(batch, seq, CONV_DIM)], axis=1)
        conv_state = u3[:, seq:]
        conv_t = _conv_sample(jnp.swapaxes(cb.reshape(batch, seq, CONV_DIM), 0, 1), jnp.swapaxes(u3, 0, 1),
                              wts['conv_w'])
        conv = jnp.swapaxes(conv_t, 0, 1).reshape(n, CONV_DIM)
    x1 = _out_even(xf, attn, conv, wts['w_out_attn'], wts['w_out_conv'], tm)
    x2, xn2 = _mixer(x1, wts['norm_ffn_even'], None, wts['ffn_gate'], wts['ffn_up'], wts['ffn_down'],
                     wts['norm_mix_odd'], tm, D_FF // 2, True)

    xn3 = xn2.reshape(batch, seq, D_MODEL)
    xp = jnp.concatenate([shift_prev[:, None, :], xn3[:, :-1]], axis=1).reshape(n, D_MODEL)
    r, lw, kr, vr, kk, ka, g = _rwkv_proj(xn2, xp, wts['rwkv_mu'], wts['rwkv_vec_in'], wts['rwkv_w_rkv'],
                                          wts['rwkv_w1'], wts['rwkv_w2'], wts['rwkv_a1'], wts['rwkv_a2'],
                                          wts['rwkv_g1'], wts['rwkv_g2'], _pick_tile(n, 256))
    chunk = min(WKV_CHUNK, seq)
    if chunk < SUBLANES:
        padded = [jnp.pad(a.reshape(batch, seq, D_MODEL), ((0, 0), (0, SUBLANES - seq), (0, 0)))
                  .reshape(batch * SUBLANES, D_MODEL) for a in (r, lw, kr, vr, kk, ka)]
        y, wkv_state = _wkv(*padded, wkv_prev, batch, SUBLANES, SUBLANES)
        y = y.reshape(batch, SUBLANES, D_MODEL)[:, :seq].reshape(n, D_MODEL)
    else:
        y, wkv_state = _wkv(r, lw, kr, vr, kk, ka, wkv_prev, batch, seq, chunk)
    x3 = _rwkv_out(x2, y, r, kr, vr, g, wts['rwkv_vec_out'], wts['rwkv_w_o'], _pick_tile(n, 256))
    (yf,) = _mixer(x3, wts['norm_ffn_odd'], wts['moe_router'], wts['moe_gate'], wts['moe_up'], wts['moe_down'],
                   wts['norm_final'], tm, D_FF // 2, False)

    k4 = k.reshape(1, batch, seq, N_KV_HEADS, HEAD_DIM)
    v4 = v.reshape(1, batch, seq, N_KV_HEADS, HEAD_DIM)
    ik3 = ikw[:, :IDX_DIM].reshape(1, batch, seq, IDX_DIM)
    return (yf.reshape(batch, seq, D_MODEL), k4, v4, ik3, conv_state[None],
            xn3[:, -1][None], wkv_state[None])


def _prepare_weights(norm_mix_even, w_in_even, conv_w, w_out_even, norm_ffn_even, ffn_gate, ffn_up, ffn_down,
                     norm_mix_odd, rwkv_mu, rwkv_w_rkv, rwkv_w0, rwkv_w1, rwkv_w2, rwkv_a0, rwkv_a1, rwkv_a2,
                     rwkv_g1, rwkv_g2, rwkv_k_k, rwkv_k_a, rwkv_r_k, rwkv_ln_w, rwkv_ln_b, rwkv_w_o,
                     norm_ffn_odd, moe_router, moe_gate, moe_up, moe_down, norm_final):
    w_in = w_in_even[0]
    o = np.cumsum((0, ATTN_DIM, KV_DIM, KV_DIM, N_IDX_HEADS * IDX_DIM, IDX_DIM, N_IDX_HEADS,
                   CONV_DIM, CONV_DIM, CONV_DIM))
    pad = jnp.zeros((D_MODEL, LANES - IDX_DIM - N_IDX_HEADS), F32)
    w_in = jnp.concatenate([w_in[:, :o[4]], w_in[:, o[4]:o[6]], pad, w_in[:, o[6]:]], axis=1).astype(BF16)
    row = lambda a: a.reshape(1, -1)
    zeros = jnp.zeros((1, D_MODEL), F32)
    return dict(
        norm_mix_even=row(norm_mix_even[0]), w_in=w_in, conv_w=conv_w[0],
        w_out_attn=w_out_even[0, :ATTN_DIM].astype(BF16), w_out_conv=w_out_even[0, ATTN_DIM:].astype(BF16),
        norm_ffn_even=row(norm_ffn_even[0]),
        ffn_gate=ffn_gate.astype(BF16), ffn_up=ffn_up.astype(BF16), ffn_down=ffn_down.astype(BF16),
        norm_mix_odd=row(norm_mix_odd[0]), rwkv_mu=jnp.concatenate([rwkv_mu[0], zeros, zeros], axis=0),
        rwkv_vec_in=jnp.concatenate([row(rwkv_w0[0]), row(rwkv_a0[0]), row(rwkv_k_k[0]), row(rwkv_k_a[0]),
                                     zeros, zeros, zeros, zeros], axis=0),
        rwkv_w_rkv=rwkv_w_rkv[0].astype(BF16),
        rwkv_w1=rwkv_w1[0].astype(BF16), rwkv_w2=rwkv_w2[0].astype(BF16),
        rwkv_a1=rwkv_a1[0].astype(BF16), rwkv_a2=rwkv_a2[0].astype(BF16),
        rwkv_g1=rwkv_g1[0].astype(BF16), rwkv_g2=rwkv_g2[0].astype(BF16),
        rwkv_vec_out=jnp.concatenate([row(rwkv_ln_w[0]), row(rwkv_ln_b[0]), row(rwkv_r_k[0]),
                                      zeros, zeros, zeros, zeros, zeros], axis=0),
        rwkv_w_o=rwkv_w_o[0].astype(BF16),
        norm_ffn_odd=row(norm_ffn_odd[0]),
        moe_router=jnp.pad(moe_router[0], ((0, 0), (0, LANES - N_EXPERTS))).astype(BF16),
        moe_gate=moe_gate[0].astype(BF16), moe_up=moe_up[0].astype(BF16), moe_down=moe_down[0].astype(BF16),
        norm_final=row(norm_final),
    )


def kernel(x_prompt, x_sample, cache_k, cache_v, cache_idx_k, state_conv, state_shift, state_wkv, page_table, norm_mix_even, w_in_even, conv_w, w_out_even, norm_ffn_even, ffn_gate, ffn_up, ffn_down, norm_mix_odd, rwkv_mu, rwkv_w_rkv, rwkv_w0, rwkv_w1, rwkv_w2, rwkv_a0, rwkv_a1, rwkv_a2, rwkv_g1, rwkv_g2, rwkv_k_k, rwkv_k_a, rwkv_r_k, rwkv_ln_w, rwkv_ln_b, rwkv_w_o, norm_ffn_odd, moe_router, moe_gate, moe_up, moe_down, norm_final):
    assert w_in_even.shape[0] == 1 and rwkv_mu.shape[0] == 1, "one even and one odd layer"
    wts = _prepare_weights(norm_mix_even, w_in_even, conv_w, w_out_even, norm_ffn_even, ffn_gate, ffn_up, ffn_down,
                           norm_mix_odd, rwkv_mu, rwkv_w_rkv, rwkv_w0, rwkv_w1, rwkv_w2, rwkv_a0, rwkv_a1, rwkv_a2,
                           rwkv_g1, rwkv_g2, rwkv_k_k, rwkv_k_a, rwkv_r_k, rwkv_ln_w, rwkv_ln_b, rwkv_w_o,
                           norm_ffn_odd, moe_router, moe_gate, moe_up, moe_down, norm_final)
    b, t = x_prompt.shape[:2]
    bd, tn = x_sample.shape[:2]
    n_pool = cache_k.shape[1]
    past = page_table.shape[1] * PAGE_SIZE
    pos_prompt = jnp.arange(t, dtype=jnp.int32)
    pos_sample = past + jnp.arange(tn, dtype=jnp.int32)
    zeros = lambda *s: jnp.zeros(s, F32)
    out_p = _trunk(x_prompt, pos_prompt, zeros(b, CONV_W - 1, CONV_DIM), zeros(b, D_MODEL),
                   zeros(b, RWKV_HEADS, RWKV_HEAD, RWKV_HEAD), wts, None)
    sample = (cache_k[0].reshape(n_pool, PAGE_SIZE, KV_DIM), cache_v[0].reshape(n_pool, PAGE_SIZE, KV_DIM),
              cache_idx_k[0], page_table)
    out_s = _trunk(x_sample, pos_sample, state_conv[0], state_shift[0], state_wkv[0], wts, sample)
    y_p, k_p, v_p, ik_p, conv_p, shift_p, wkv_p = out_p
    y_s, k_s, v_s, ik_s, conv_s, shift_s, wkv_s = out_s
    return (y_p, y_s, k_p, v_p, ik_p, k_s, v_s, ik_s, conv_p, conv_s, shift_p, shift_s, wkv_p, wkv_s)
```

```python
import functools

import numpy as np
import jax
import jax.numpy as jnp
from jax import lax
from jax.experimental import pallas as pl
from jax.experimental.pallas import tpu as pltpu

F32 = jnp.float32
BF16 = jnp.bfloat16
HI = lax.Precision.HIGHEST

D_MODEL = 1024
PAGE_SIZE = 128
HEAD_DIM = 64
N_Q_HEADS = 8
N_KV_HEADS = 2
Q_PER_KV = N_Q_HEADS // N_KV_HEADS
ROT_DIM = HEAD_DIM // 4
ROPE_THETA = 500000.0
N_IDX_HEADS = 4
IDX_DIM = 64
TOPK_MAX = 256
Q_BLOCK = 128
ATTN_DIM = N_Q_HEADS * HEAD_DIM
KV_DIM = N_KV_HEADS * HEAD_DIM
CONV_DIM = D_MODEL // 2
CONV_W = 3
RWKV_HEAD = 64
RWKV_HEADS = D_MODEL // RWKV_HEAD
GN_EPS = 64e-5
D_FF = 2816
N_EXPERTS = 8
RMS_EPS = 1e-6

LANES = 128
SUBLANES = 8
VMEM_LIMIT = 56 * 1024 * 1024
INT_MIN = -2 ** 31
KEY_NEG_INF = INT_MIN + 0x7FFFFF
PROJ_COLS = ATTN_DIM + KV_DIM + KV_DIM + N_IDX_HEADS * IDX_DIM + LANES + 3 * CONV_DIM
WKV_CHUNK = 64


def _cparams(sem):
    return pltpu.CompilerParams(dimension_semantics=sem, vmem_limit_bytes=VMEM_LIMIT)


def _rms(x, g):
    return x * lax.rsqrt(jnp.mean(x * x, axis=-1, keepdims=True) + RMS_EPS) * g


def _dot(a, b):
    return jnp.dot(a, b, preferred_element_type=F32)


def _dot_nt(a, b, precision=None):
    return lax.dot_general(a, b, (((1,), (1,)), ((), ())), preferred_element_type=F32, precision=precision)


def _dot_tn(a, b, precision=None):
    return lax.dot_general(a, b, (((0,), (0,)), ((), ())), preferred_element_type=F32, precision=precision)


def _group_sum(x, gmat):
    outs = []
    for c in range(x.shape[1] // LANES):
        xc = x[:, c * LANES:(c + 1) * LANES]
        hi = xc.astype(BF16)
        lo = (xc - hi.astype(F32)).astype(BF16)
        outs.append(_dot(hi, gmat) + _dot(lo, gmat))
    return jnp.concatenate(outs, axis=1)


def _rope_chunk(xc, rc, rp, rm):
    return xc * rc + pltpu.roll(xc, 8, 1) * rp + pltpu.roll(xc, LANES - 8, 1) * rm


def _proj_even_body(x_ref, g_ref, w_ref, rc_ref, rp_ref, rm_ref,
                    q_ref, k_ref, v_ref, iq_ref, ikw_ref, cb_ref, u_ref):
    xn = _rms(x_ref[...], g_ref[...]).astype(BF16)
    h = _dot(xn, w_ref[...])
    rc, rp, rm = rc_ref[...], rp_ref[...], rm_ref[...]
    col = 0
    for c in range(ATTN_DIM // LANES):
        q_ref[:, c * LANES:(c + 1) * LANES] = _rope_chunk(h[:, col:col + LANES], rc, rp, rm).astype(BF16)
        col += LANES
    k_ref[...] = _rope_chunk(h[:, col:col + LANES], rc, rp, rm)
    col += LANES
    v_ref[...] = h[:, col:col + LANES]
    col += LANES
    for c in range(N_IDX_HEADS * IDX_DIM // LANES):
        iq_ref[:, c * LANES:(c + 1) * LANES] = _rope_chunk(h[:, col:col + LANES], rc, rp, rm).astype(BF16)
        col += LANES
    ikw = h[:, col:col + LANES]
    lane = lax.broadcasted_iota(jnp.int32, ikw.shape, 1)
    ikw_ref[...] = jnp.where(lane < IDX_DIM, _rope_chunk(ikw, rc, rp, rm), ikw)
    col += LANES
    cb_ref[...] = h[:, col:col + CONV_DIM]
    col += CONV_DIM
    u_ref[...] = h[:, col:col + CONV_DIM] * h[:, col + CONV_DIM:col + 2 * CONV_DIM]


def _proj_even(x, g, w, tabs, tm):
    n = x.shape[0]
    nt = tabs[0].shape[0] // tm
    row = lambda i: (i, 0)
    const = lambda i: (0, 0)
    tab = lambda i: (i % nt, 0)
    widths = (ATTN_DIM, KV_DIM, KV_DIM, N_IDX_HEADS * IDX_DIM, LANES, CONV_DIM, CONV_DIM)
    dtypes = (BF16, F32, F32, BF16, F32, F32, F32)
    return pl.pallas_call(
        _proj_even_body,
        grid=(n // tm,),
        in_specs=[pl.BlockSpec((tm, D_MODEL), row), pl.BlockSpec((1, D_MODEL), const),
                  pl.BlockSpec((D_MODEL, PROJ_COLS), const)] + [pl.BlockSpec((tm, LANES), tab)] * 3,
        out_specs=[pl.BlockSpec((tm, wd), row) for wd in widths],
        out_shape=[jax.ShapeDtypeStruct((n, wd), dt) for wd, dt in zip(widths, dtypes)],
        compiler_params=_cparams(("parallel",)),
        name="proj_even",
    )(x, g, w, *tabs)


def _rope_tables(pos):
    half = ROT_DIM // 2
    inv_freq = ROPE_THETA ** (-jnp.arange(half, dtype=F32) / half)
    ang = pos.astype(F32)[:, None] * inv_freq[None, :]
    cos, sin = jnp.cos(ang), jnp.sin(ang)
    t = pos.shape[0]
    pad = jnp.zeros((t, HEAD_DIM - ROT_DIM), F32)
    zero = jnp.zeros((t, half), F32)
    rc = jnp.concatenate([cos, cos, pad + 1.0], axis=1)
    rp = jnp.concatenate([zero, sin, pad], axis=1)
    rm = jnp.concatenate([-sin, zero, pad], axis=1)
    return tuple(jnp.tile(a, (1, LANES // HEAD_DIM)) for a in (rc, rp, rm))


def _conv_prompt_body(cb_ref, u_ref, up_ref, w_ref, y_ref, *, tiles_per_seq):
    u = u_ref[...]
    first = pl.program_id(0) % tiles_per_seq == 0
    prev = jnp.where(first, 0.0, up_ref[...])
    p1, p2 = prev[SUBLANES - 1:SUBLANES, :], prev[SUBLANES - 2:SUBLANES - 1, :]
    r = lax.broadcasted_iota(jnp.int32, u.shape, 0)
    u1 = jnp.where(r == 0, p1, pltpu.roll(u, 1, 0))
    u2 = jnp.where(r == 0, p2, jnp.where(r == 1, p1, pltpu.roll(u, 2, 0)))
    w = w_ref[...]
    y_ref[...] = cb_ref[...] * (w[0:1, :] * u2 + w[1:2, :] * u1 + w[2:3, :] * u)


def _conv_prompt(cb, u, w, seq, tc):
    n = u.shape[0]
    row = lambda i: (i, 0)
    prev = lambda i: (jnp.maximum(i * (tc // SUBLANES) - 1, 0), 0)
    return pl.pallas_call(
        functools.partial(_conv_prompt_body, tiles_per_seq=seq // tc),
        grid=(n // tc,),
        in_specs=[pl.BlockSpec((tc, CONV_DIM), row), pl.BlockSpec((tc, CONV_DIM), row),
                  pl.BlockSpec((SUBLANES, CONV_DIM), prev), pl.BlockSpec((CONV_W, CONV_DIM), lambda i: (0, 0))],
        out_specs=pl.BlockSpec((tc, CONV_DIM), row),
        out_shape=jax.ShapeDtypeStruct((n, CONV_DIM), F32),
        compiler_params=_cparams(("parallel",)),
        name="conv_prompt",
    )(cb, u, u, w)


def _conv_sample_body(cb_ref, ue_ref, w_ref, y_ref):
    w = w_ref[...]
    for t in range(y_ref.shape[0]):
        acc = w[0:1, :] * ue_ref[t] + w[1:2, :] * ue_ref[t + 1] + w[2:3, :] * ue_ref[t + 2]
        y_ref[t] = cb_ref[t] * acc


def _conv_sample(cb_t, ue_t, w):
    return pl.pallas_call(
        _conv_sample_body,
        out_shape=jax.ShapeDtypeStruct(cb_t.shape, F32),
        name="conv_sample",
    )(cb_t, ue_t, w)


def _topk_select(scores, n_sel, tri):
    rows, width = scores.shape
    sc = scores

    def key_to_float(key):
        return lax.bitcast_convert_type(key ^ ((key >> 31) & 0x7FFFFFFF), F32)

    def count_ge(key):
        ge = (sc >= key_to_float(key)) | (key <= KEY_NEG_INF)
        return jnp.sum(jnp.where(ge, 1.0, 0.0), axis=1, keepdims=True)

    thr = jnp.where(count_ge(jnp.zeros((rows, 1), jnp.int32)) >= n_sel, 0, INT_MIN).astype(jnp.int32)

    def body(i, thr):
        cand = thr + lax.shift_left(jnp.int32(1), 30 - i)
        return jnp.where(count_ge(cand) >= n_sel, cand, thr)

    thr = lax.fori_loop(0, 31, body, thr)
    thr_f = key_to_float(thr)
    gt = sc > thr_f
    eq = sc == thr_f
    need = n_sel - jnp.sum(jnp.where(gt, 1.0, 0.0), axis=1, keepdims=True)
    off = jnp.zeros((rows, 1), F32)
    parts = []
    for c in range(width // LANES):
        eqc = jnp.where(eq[:, c * LANES:(c + 1) * LANES], 1.0, 0.0)
        incl = _dot(eqc.astype(BF16), tri)
        parts.append((incl - eqc + off) < need)
        off = off + incl[:, LANES - 1:LANES]
    tie = jnp.concatenate(parts, axis=1)
    return (gt | (eq & tie)) & (sc > -jnp.inf)


def _tri_incl():
    i = np.arange(LANES)
    return jnp.asarray((i[:, None] <= i[None, :]).astype(np.float32), BF16)


def _stack_heads(x, n):
    return jnp.concatenate([x[:, h * HEAD_DIM:(h + 1) * HEAD_DIM] for h in range(n)], axis=0)


def _index_scores(iq_st, iw_st, ik_b):
    rows = iq_st.shape[0] // N_IDX_HEADS
    s = _dot_nt(iq_st, ik_b)
    term = jnp.maximum(s, 0.0) * (IDX_DIM ** -0.5) * (iw_st * (N_IDX_HEADS ** -0.5))
    acc = term[0:rows, :]
    for h in range(1, N_IDX_HEADS):
        acc = acc + term[h * rows:(h + 1) * rows, :]
    return acc


def _group_logits(qs, kg, sel):
    s = _dot_nt(qs, kg) * (HEAD_DIM ** -0.5)
    return jnp.where(jnp.concatenate([sel] * Q_PER_KV, axis=0), s, -jnp.inf)


def _masked_attention(q, kb, vb, sel):
    rows = q.shape[0]
    outs = []
    for g in range(N_KV_HEADS):
        qs = _stack_heads(q[:, g * Q_PER_KV * HEAD_DIM:(g + 1) * Q_PER_KV * HEAD_DIM], Q_PER_KV)
        s = _group_logits(qs, kb[:, g * HEAD_DIM:(g + 1) * HEAD_DIM], sel)
        p = jnp.exp(s - jnp.max(s, axis=1, keepdims=True))
        o = _dot(p.astype(BF16), vb[:, g * HEAD_DIM:(g + 1) * HEAD_DIM]) / jnp.sum(p, axis=1, keepdims=True)
        outs.extend(o[r * rows:(r + 1) * rows, :] for r in range(Q_PER_KV))
    return jnp.concatenate(outs, axis=1)


def _prompt_attn_body(q_ref, iq_ref, iwq_ref, k_ref, v_ref, ikw_ref, tri_ref, o_ref,
                      kb_scr, vb_scr, ikb_scr, *, n_sel):
    i = pl.program_id(1)

    @pl.when(i == 0)
    def _():
        kb_scr[...] = k_ref[...].astype(BF16)
        vb_scr[...] = v_ref[...].astype(BF16)
        ikb_scr[...] = ikw_ref[...].astype(BF16)

    iw = iwq_ref[...]
    iw_st = jnp.concatenate([iw[:, IDX_DIM + h:IDX_DIM + h + 1] for h in range(N_IDX_HEADS)], axis=0)
    scores = _index_scores(_stack_heads(iq_ref[...], N_IDX_HEADS), iw_st, ikb_scr[:, :IDX_DIM])
    tq = i * Q_BLOCK + lax.broadcasted_iota(jnp.int32, scores.shape, 0)
    key_pos = lax.broadcasted_iota(jnp.int32, scores.shape, 1)
    scores = jnp.where(key_pos <= tq, scores, -jnp.inf)
    sel = _topk_select(scores, n_sel, tri_ref[...])
    o_ref[...] = _masked_attention(q_ref[...], kb_scr[...], vb_scr[...], sel).astype(o_ref.dtype)


def _prompt_attention(q, iq, ikw, k, v, batch, seq):
    n_sel = min(TOPK_MAX, seq // 4)
    nqb = seq // Q_BLOCK
    blk = lambda b, i: (b * nqb + i, 0)
    full = lambda b, i: (b, 0)
    return pl.pallas_call(
        functools.partial(_prompt_attn_body, n_sel=n_sel),
        grid=(batch, nqb),
        in_specs=[pl.BlockSpec((Q_BLOCK, ATTN_DIM), blk), pl.BlockSpec((Q_BLOCK, N_IDX_HEADS * IDX_DIM), blk),
                  pl.BlockSpec((Q_BLOCK, LANES), blk),
                  pl.BlockSpec((seq, KV_DIM), full), pl.BlockSpec((seq, KV_DIM), full),
                  pl.BlockSpec((seq, LANES), full), pl.BlockSpec((LANES, LANES), lambda b, i: (0, 0))],
        out_specs=pl.BlockSpec((Q_BLOCK, ATTN_DIM), blk),
        out_shape=jax.ShapeDtypeStruct((batch * seq, ATTN_DIM), BF16),
        scratch_shapes=[pltpu.VMEM((seq, KV_DIM), BF16), pltpu.VMEM((seq, KV_DIM), BF16),
                        pltpu.VMEM((seq, LANES), BF16)],
        compiler_params=_cparams(("parallel", "arbitrary")),
        name="prompt_attention",
    )(q, iq, ikw, k, v, ikw, _tri_incl())


PAGES_PER_STEP = 16


def _sample_scores_body(pt_ref, iq_ref, iw_ref, *rest):
    page_refs, s_ref = rest[:PAGES_PER_STEP], rest[PAGES_PER_STEP]
    for j, pr in enumerate(page_refs):
        s_ref[0, :, j * PAGE_SIZE:(j + 1) * PAGE_SIZE] = _index_scores(iq_ref[0], iw_ref[0], pr[0].astype(BF16))


def _sample_scores(page_table, iq_st, iw_st, cache_ik):
    bd, n_pages = page_table.shape
    rows = iq_st.shape[1] // N_IDX_HEADS
    steps = n_pages // PAGES_PER_STEP
    per_b = lambda b, p, pt: (b, 0, 0)
    page = lambda j: (lambda b, p, pt: (pt[b, p * PAGES_PER_STEP + j], 0, 0))
    return pl.pallas_call(
        _sample_scores_body,
        grid_spec=pltpu.PrefetchScalarGridSpec(
            num_scalar_prefetch=1,
            grid=(bd, steps),
            in_specs=[pl.BlockSpec((1,) + iq_st.shape[1:], per_b), pl.BlockSpec((1,) + iw_st.shape[1:], per_b)]
                     + [pl.BlockSpec((1, PAGE_SIZE, IDX_DIM), page(j)) for j in range(PAGES_PER_STEP)],
            out_specs=pl.BlockSpec((1, rows, PAGES_PER_STEP * PAGE_SIZE), lambda b, p, pt: (b, 0, p)),
        ),
        out_shape=jax.ShapeDtypeStruct((bd, rows, n_pages * PAGE_SIZE), F32),
        compiler_params=_cparams(("parallel", "arbitrary")),
        name="sample_scores",
    )(page_table, iq_st, iw_st, *([cache_ik] * PAGES_PER_STEP))


def _sample_select_body(sp_ref, iq_ref, iwq_ref, ikn_ref, tri_ref, m_ref, *, n_sel, tn):
    rows = sp_ref.shape[0]
    iw = iwq_ref[...]
    iw_st = jnp.concatenate([iw[:, IDX_DIM + h:IDX_DIM + h + 1] for h in range(N_IDX_HEADS)], axis=0)
    s_new = _index_scores(_stack_heads(iq_ref[...], N_IDX_HEADS), iw_st, ikn_ref[...].astype(BF16))
    r = lax.broadcasted_iota(jnp.int32, s_new.shape, 0)
    c = lax.broadcasted_iota(jnp.int32, s_new.shape, 1)
    same = (r // tn == c // tn) & (c <= r)
    fold = jnp.where((lax.broadcasted_iota(jnp.int32, (rows, LANES), 0) % tn)
                     == lax.broadcasted_iota(jnp.int32, (rows, LANES), 1), 1.0, 0.0)
    picked = jnp.where(same, s_new, 0.0)
    hi = picked.astype(BF16)
    mid = (picked - hi.astype(F32)).astype(BF16)
    lo = (picked - hi.astype(F32) - mid.astype(F32)).astype(BF16)
    fb = fold.astype(BF16)
    new_chunk = _dot(hi, fb) + _dot(mid, fb) + _dot(lo, fb)
    lane = lax.broadcasted_iota(jnp.int32, (rows, LANES), 1)
    tpos = lax.broadcasted_iota(jnp.int32, (rows, LANES), 0) % tn
    new_chunk = jnp.where(lane <= tpos, new_chunk, -jnp.inf)
    scores = jnp.concatenate([sp_ref[...], new_chunk], axis=1)
    sel = _topk_select(scores, n_sel, tri_ref[...])
    m_ref[...] = jnp.where(sel, 1.0, 0.0)


def _sample_select(s_past, iq, ikw, tn):
    rows, past = s_past.shape
    n_sel = min(TOPK_MAX, (past + tn) // 4)
    return pl.pallas_call(
        functools.partial(_sample_select_body, n_sel=n_sel, tn=tn),
        out_shape=jax.ShapeDtypeStruct((rows, past + LANES), F32),
        compiler_params=pltpu.CompilerParams(vmem_limit_bytes=VMEM_LIMIT),
        name="sample_select",
    )(s_past, iq, ikw, ikw[:, :IDX_DIM], _tri_incl())


def _sample_attn_body(pt_ref, q_ref, m_ref, kn_ref, vn_ref, mn_ref, *rest):
    k_refs, v_refs = rest[:PAGES_PER_STEP], rest[PAGES_PER_STEP:2 * PAGES_PER_STEP]
    o_ref, m_scr, l_scr, acc_scr = rest[2 * PAGES_PER_STEP:]
    p = pl.program_id(1)

    @pl.when(p == 0)
    def _():
        m_scr[...] = jnp.full(m_scr.shape, -jnp.inf, F32)
        l_scr[...] = jnp.zeros(l_scr.shape, F32)
        acc_scr[...] = jnp.zeros(acc_scr.shape, F32)

    def update(kb, vb, sel):
        for g in range(N_KV_HEADS):
            s = _group_logits(q_ref[0, g], kb[:, g * HEAD_DIM:(g + 1) * HEAD_DIM], sel)
            m_old = m_scr[g]
            m_new = jnp.maximum(m_old, jnp.max(s, axis=1, keepdims=True))
            m_safe = jnp.where(m_new == -jnp.inf, 0.0, m_new)
            alpha = jnp.exp(m_old - m_safe)
            pe = jnp.exp(s - m_safe)
            l_scr[g] = alpha * l_scr[g] + jnp.sum(pe, axis=1, keepdims=True)
            acc_scr[g] = alpha * acc_scr[g] + _dot(pe.astype(BF16), vb[:, g * HEAD_DIM:(g + 1) * HEAD_DIM])
            m_scr[g] = m_new

    kb = jnp.concatenate([r[0] for r in k_refs], axis=0).astype(BF16)
    vb = jnp.concatenate([r[0] for r in v_refs], axis=0).astype(BF16)
    update(kb, vb, m_ref[0] > 0.5)

    @pl.when(p == pl.num_programs(1) - 1)
    def _():
        update(kn_ref[0].astype(BF16), vn_ref[0].astype(BF16), mn_ref[0] > 0.5)
        for g in range(N_KV_HEADS):
            l = l_scr[g]
            o_ref[0, g] = acc_scr[g] / jnp.where(l == 0.0, 1.0, l)


def _sample_attention(page_table, q_st, mask, k_new, v_new, cache_k, cache_v):
    bd, n_pages = page_table.shape
    steps = n_pages // PAGES_PER_STEP
    per_b = lambda b, p, pt: (b, 0, 0)
    per_b4 = lambda b, p, pt: (b, 0, 0, 0)
    page = lambda j: (lambda b, p, pt: (pt[b, p * PAGES_PER_STEP + j], 0, 0))
    width = PAGES_PER_STEP * PAGE_SIZE
    rows = q_st.shape[2]
    mrows = mask.shape[1]
    return pl.pallas_call(
        _sample_attn_body,
        grid_spec=pltpu.PrefetchScalarGridSpec(
            num_scalar_prefetch=1,
            grid=(bd, steps),
            in_specs=[pl.BlockSpec((1,) + q_st.shape[1:], per_b4),
                      pl.BlockSpec((1, mrows, width), lambda b, p, pt: (b, 0, p)),
                      pl.BlockSpec((1, PAGE_SIZE, KV_DIM), per_b), pl.BlockSpec((1, PAGE_SIZE, KV_DIM), per_b),
                      pl.BlockSpec((1, mrows, LANES), lambda b, p, pt: (b, 0, n_pages))]
                     + [pl.BlockSpec((1, PAGE_SIZE, KV_DIM), page(j)) for j in range(PAGES_PER_STEP)] * 2,
            out_specs=pl.BlockSpec((1, N_KV_HEADS, rows, HEAD_DIM), per_b4),
            scratch_shapes=[pltpu.VMEM((N_KV_HEADS, rows, 1), F32), pltpu.VMEM((N_KV_HEADS, rows, 1), F32),
                            pltpu.VMEM((N_KV_HEADS, rows, HEAD_DIM), F32)],
        ),
        out_shape=jax.ShapeDtypeStruct((bd, N_KV_HEADS, rows, HEAD_DIM), F32),
        compiler_params=_cparams(("parallel", "arbitrary")),
        name="sample_attention",
    )(page_table, q_st, mask, k_new, v_new, mask, *([cache_k] * PAGES_PER_STEP), *([cache_v] * PAGES_PER_STEP))


def _out_even_body(x_ref, a_ref, c_ref, wa_ref, wc_ref, o_ref):
    o_ref[...] = x_ref[...] + _dot(a_ref[...], wa_ref[...]) + _dot(c_ref[...].astype(BF16), wc_ref[...])


def _out_even(x, attn, conv, wa, wc, tm):
    n = x.shape[0]
    row = lambda i: (i, 0)
    const = lambda i: (0, 0)
    return pl.pallas_call(
        _out_even_body,
        grid=(n // tm,),
        in_specs=[pl.BlockSpec((tm, D_MODEL), row), pl.BlockSpec((tm, ATTN_DIM), row),
                  pl.BlockSpec((tm, CONV_DIM), row), pl.BlockSpec((ATTN_DIM, D_MODEL), const),
                  pl.BlockSpec((CONV_DIM, D_MODEL), const)],
        out_specs=pl.BlockSpec((tm, D_MODEL), row),
        out_shape=jax.ShapeDtypeStruct((n, D_MODEL), F32),
        compiler_params=_cparams(("parallel",)),
        name="out_even",
    )(x, attn, conv, wa, wc)


def _mixer_body(*refs, n_exp, emit_sum):
    if n_exp > 1:
        x_ref, g_ref, wr_ref, wg_ref, wu_ref, wd_ref, g2_ref = refs[:7]
        outs = refs[7:]
    else:
        x_ref, g_ref, wg_ref, wu_ref, wd_ref, g2_ref = refs[:6]
        outs = refs[6:]
    n_out = 2 if emit_sum else 1
    out_refs, (xn_scr, acc_scr, cw_scr) = outs[:n_out], outs[n_out:]
    e, f = pl.program_id(1), pl.program_id(2)

    @pl.when((e == 0) & (f == 0))
    def _():
        xn = _rms(x_ref[...], g_ref[...]).astype(BF16)
        xn_scr[...] = xn
        acc_scr[...] = jnp.zeros(acc_scr.shape, F32)
        if n_exp > 1:
            logits = _dot(xn, wr_ref[...])
            lane = lax.broadcasted_iota(jnp.int32, logits.shape, 1).astype(F32)
            lg = jnp.where(lane < n_exp, logits, -jnp.inf)
            m1 = jnp.max(lg, axis=1, keepdims=True)
            i1 = jnp.min(jnp.where(lg == m1, lane, float(LANES)), axis=1, keepdims=True)
            lg2 = jnp.where(lane == i1, -jnp.inf, lg)
            m2 = jnp.max(lg2, axis=1, keepdims=True)
            i2 = jnp.min(jnp.where(lg2 == m2, lane, float(LANES)), axis=1, keepdims=True)
            e2 = jnp.exp(m2 - m1)
            cw_scr[...] = jnp.where(lane == i1, 1.0 / (1.0 + e2), 0.0) + jnp.where(lane == i2, e2 / (1.0 + e2), 0.0)

    xn = xn_scr[...]
    gate = _dot(xn, wg_ref[0])
    h = gate * jax.nn.sigmoid(gate) * _dot(xn, wu_ref[0])
    if n_exp > 1:
        lane = lax.broadcasted_iota(jnp.int32, cw_scr.shape, 1)
        h = h * jnp.sum(jnp.where(lane == e, cw_scr[...], 0.0), axis=1, keepdims=True)
    acc_scr[...] += _dot(h.astype(BF16), wd_ref[0])

    @pl.when((e == pl.num_programs(1) - 1) & (f == pl.num_programs(2) - 1))
    def _():
        y = x_ref[...] + acc_scr[...]
        if emit_sum:
            out_refs[0][...] = y
        out_refs[-1][...] = _rms(y, g2_ref[...])


def _mixer(x, g, wr, wg, wu, wd, g2, tm, tf, emit_sum):
    n = x.shape[0]
    n_exp = wg.shape[0]
    row = lambda i, e, f: (i, 0)
    const = lambda i, e, f: (0, 0)
    in_specs = [pl.BlockSpec((tm, D_MODEL), row), pl.BlockSpec((1, D_MODEL), const)]
    args = [x, g]
    if n_exp > 1:
        in_specs.append(pl.BlockSpec((D_MODEL, LANES), const))
        args.append(wr)
    in_specs += [pl.BlockSpec((1, D_MODEL, tf), lambda i, e, f: (e, 0, f)),
                 pl.BlockSpec((1, D_MODEL, tf), lambda i, e, f: (e, 0, f)),
                 pl.BlockSpec((1, tf, D_MODEL), lambda i, e, f: (e, f, 0)),
                 pl.BlockSpec((1, D_MODEL), const)]
    args += [wg, wu, wd, g2]
    n_out = 2 if emit_sum else 1
    return pl.pallas_call(
        functools.partial(_mixer_body, n_exp=n_exp, emit_sum=emit_sum),
        grid=(n // tm, n_exp, D_FF // tf),
        in_specs=in_specs,
        out_specs=[pl.BlockSpec((tm, D_MODEL), row)] * n_out,
        out_shape=[jax.ShapeDtypeStruct((n, D_MODEL), F32)] * n_out,
        scratch_shapes=[pltpu.VMEM((tm, D_MODEL), BF16), pltpu.VMEM((tm, D_MODEL), F32),
                        pltpu.VMEM((tm, LANES), F32)],
        compiler_params=_cparams(("parallel", "arbitrary", "arbitrary")),
        name="mixer_moe" if n_exp > 1 else "mixer_ffn",
    )(*args)


def _group_ones():
    i = np.arange(LANES) // RWKV_HEAD
    return jnp.asarray((i[:, None] == i[None, :]).astype(np.float32), BF16)


def _rwkv_proj_body(xn_ref, xp_ref, mu_ref, vec_ref, wrkv_ref, w1_ref, w2_ref, a1_ref, a2_ref, g1_ref, g2_ref,
                    gm_ref, r_ref, lw_ref, k_ref, v_ref, kk_ref, ka_ref, g_ref):
    xn = xn_ref[...]
    xx = xp_ref[...] - xn
    mu = mu_ref[...]
    mix = lambda i: (xn + xx * mu[i:i + 1, :]).astype(BF16)
    vec = vec_ref[...]
    w0, a0, k_k, k_a = vec[0:1, :], vec[1:2, :], vec[2:3, :], vec[3:4, :]
    r_ref[...] = _dot(mix(0), wrkv_ref[0])
    z = w0 + _dot(jnp.tanh(_dot(mix(1), w1_ref[...])).astype(BF16), w2_ref[...])
    softplus = jnp.maximum(-z, 0.0) + jnp.log(1.0 + jnp.exp(-jnp.abs(z)))
    lw_ref[...] = -jnp.exp(-softplus - 0.5)
    k = _dot(mix(2), wrkv_ref[1])
    v_ref[...] = _dot(mix(3), wrkv_ref[2])
    a = jax.nn.sigmoid(a0 + _dot(_dot(mix(4), a1_ref[...]).astype(BF16), a2_ref[...]))
    g_ref[...] = _dot(jax.nn.sigmoid(_dot(mix(5), g1_ref[...])).astype(BF16), g2_ref[...])
    kk = k * k_k
    kk = kk * lax.rsqrt(jnp.maximum(_group_sum(kk * kk, gm_ref[...]), 1e-24))
    kk_ref[...] = kk
    ka_ref[...] = kk * a
    k_ref[...] = k * (1.0 + (a - 1.0) * k_a)


def _rwkv_proj(xn, xp, mu, vec, wrkv, w1, w2, a1, a2, g1, g2, tm):
    n = xn.shape[0]
    row = lambda i: (i, 0)
    c2 = lambda i: (0, 0)
    c3 = lambda i: (0, 0, 0)
    full = lambda a: pl.BlockSpec(a.shape, c3 if a.ndim == 3 else c2)
    gm = _group_ones()
    consts = [mu, vec, wrkv, w1, w2, a1, a2, g1, g2, gm]
    return pl.pallas_call(
        _rwkv_proj_body,
        grid=(n // tm,),
        in_specs=[pl.BlockSpec((tm, D_MODEL), row)] * 2 + [full(a) for a in consts],
        out_specs=[pl.BlockSpec((tm, D_MODEL), row)] * 7,
        out_shape=[jax.ShapeDtypeStruct((n, D_MODEL), F32)] * 7,
        compiler_params=_cparams(("parallel",)),
        name="rwkv_proj",
    )(xn, xp, *consts)


def _wkv_body(r_ref, lw_ref, k_ref, v_ref, kk_ref, ka_ref, s0_ref, y_ref, sT_ref,
              s_scr, a_scr, r_scr, b_scr, k_scr, *, chunk):
    c = pl.program_id(1)
    n_pairs = D_MODEL // LANES
    hd = RWKV_HEAD
    zeros = jnp.zeros((hd, hd), F32)

    @pl.when(c == 0)
    def _():
        for p in range(n_pairs):
            top = jnp.concatenate([s0_ref[0, 2 * p], zeros], axis=1)
            bot = jnp.concatenate([zeros, s0_ref[0, 2 * p + 1]], axis=1)
            s_scr[p] = jnp.concatenate([top, bot], axis=0)

    row = lax.broadcasted_iota(jnp.int32, (chunk, chunk), 0)
    colm = lax.broadcasted_iota(jnp.int32, (chunk, chunk), 1)
    lw = lw_ref[...]
    cum = jnp.dot(jnp.where(row >= colm, 1.0, 0.0), lw, preferred_element_type=F32, precision=HI)
    w_incl = jnp.exp(cum)
    w_inv = jnp.exp(-cum)
    a_scr[...] = -kk_ref[...] * jnp.exp(cum - lw)
    r_scr[...] = r_ref[...] * w_incl
    b_scr[...] = ka_ref[...] * w_inv
    k_scr[...] = k_ref[...] * w_inv
    w_last = w_incl[chunk - 1:chunk, :]
    n_dbl = max(1, int(np.ceil(np.log2(chunk))))

    c2 = 2 * chunk
    head_of_row = lax.broadcasted_iota(jnp.int32, (c2, LANES), 0) // chunk
    head_of_lane = lax.broadcasted_iota(jnp.int32, (c2, LANES), 1) // hd
    own_lanes = head_of_row == head_of_lane
    tr = lax.broadcasted_iota(jnp.int32, (c2, c2), 0)
    tc = lax.broadcasted_iota(jnp.int32, (c2, c2), 1)
    same = (tr // chunk) == (tc // chunk)
    strict = same & (tr % chunk > tc % chunk)
    incl = same & (tr % chunk >= tc % chunk)

    def block_diag(x):
        return jnp.where(own_lanes, jnp.concatenate([x, x], axis=0), 0.0).astype(BF16)

    pairs = range(n_pairs)
    sls = [slice(p * LANES, (p + 1) * LANES) for p in pairs]
    ar = [jnp.concatenate([block_diag(a_scr[:, sl]), block_diag(r_scr[:, sl])], axis=0) for sl in sls]
    bk = [jnp.concatenate([block_diag(b_scr[:, sl]), block_diag(k_scr[:, sl])], axis=0) for sl in sls]
    vm = [block_diag(v_ref[:, sl]) for sl in sls]
    s_old = [s_scr[p] for p in pairs]
    gram = [_dot_nt(ar[p], bk[p]) for p in pairs]
    xs = [_dot_nt(ar[p], s_old[p].astype(BF16)) for p in pairs]
    u = [xs[p][:c2, :] + _dot(jnp.where(strict, gram[p][:c2, c2:], 0.0).astype(BF16), vm[p]) for p in pairs]
    lp = [jnp.where(strict, gram[p][:c2, :c2], 0.0).astype(BF16) for p in pairs]
    for d in range(n_dbl):
        if d + 1 < n_dbl:
            t = [_dot(lp[p], jnp.concatenate([u[p].astype(BF16), lp[p]], axis=1)) for p in pairs]
            u = [u[p] + t[p][:, :LANES] for p in pairs]
            lp = [t[p][:, LANES:].astype(BF16) for p in pairs]
        else:
            u = [u[p] + _dot(lp[p], u[p].astype(BF16)) for p in pairs]
    uv = [jnp.concatenate([u[p].astype(BF16), vm[p]], axis=0) for p in pairs]
    m_r = [jnp.concatenate([jnp.where(incl, gram[p][c2:, :c2], 0.0), jnp.where(incl, gram[p][c2:, c2:], 0.0)],
                           axis=1).astype(BF16) for p in pairs]
    y = [xs[p][c2:, :] + _dot(m_r[p], uv[p]) for p in pairs]
    for p in pairs:
        y_ref[:, sls[p]] = y[p][:chunk, :] + y[p][chunk:, :]
    s_new = [(s_old[p] + _dot_tn(uv[p], bk[p])) * w_last[:, sls[p]] for p in pairs]
    for p in pairs:
        s_scr[p] = s_new[p]

    @pl.when(c == pl.num_programs(1) - 1)
    def _():
        for p in range(n_pairs):
            s = s_scr[p]
            sT_ref[0, 2 * p] = s[:hd, :hd]
            sT_ref[0, 2 * p + 1] = s[hd:, hd:]


def _wkv(r, lw, k, v, kk, ka, s0, batch, seq, chunk):
    nc = seq // chunk
    blk = lambda b, c: (b * nc + c, 0)
    st = lambda b, c: (b, 0, 0, 0)
    state = pl.BlockSpec((1, RWKV_HEADS, RWKV_HEAD, RWKV_HEAD), st)
    return pl.pallas_call(
        functools.partial(_wkv_body, chunk=chunk),
        grid=(batch, nc),
        in_specs=[pl.BlockSpec((chunk, D_MODEL), blk)] * 6 + [state],
        out_specs=[pl.BlockSpec((chunk, D_MODEL), blk), state],
        out_shape=[jax.ShapeDtypeStruct((batch * seq, D_MODEL), F32),
                   jax.ShapeDtypeStruct(s0.shape, F32)],
        scratch_shapes=[pltpu.VMEM((D_MODEL // LANES, LANES, LANES), F32)]
                       + [pltpu.VMEM((chunk, D_MODEL), F32)] * 4,
        compiler_params=_cparams(("parallel", "arbitrary")),
        name="wkv",
    )(r, lw, k, v, kk, ka, s0)


def _rwkv_out_body(x_ref, y_ref, r_ref, k_ref, v_ref, g_ref, vec_ref, gm_ref, wo_ref, o_ref):
    gm = gm_ref[...]
    vec = vec_ref[...]
    ln_w, ln_b, r_k = vec[0:1, :], vec[1:2, :], vec[2:3, :]
    y = y_ref[...]
    mean = _group_sum(y, gm) * (1.0 / RWKV_HEAD)
    d = y - mean
    var = _group_sum(d * d, gm) * (1.0 / RWKV_HEAD)
    yn = d * lax.rsqrt(var + GN_EPS) * ln_w + ln_b
    yn = yn + _group_sum(r_ref[...] * k_ref[...] * r_k, gm) * v_ref[...]
    o_ref[...] = x_ref[...] + _dot((yn * g_ref[...]).astype(BF16), wo_ref[...])


def _rwkv_out(x, y, r, k, v, g, vec, wo, tm):
    n = x.shape[0]
    row = lambda i: (i, 0)
    const = lambda i: (0, 0)
    gm = _group_ones()
    return pl.pallas_call(
        _rwkv_out_body,
        grid=(n // tm,),
        in_specs=[pl.BlockSpec((tm, D_MODEL), row)] * 6
                 + [pl.BlockSpec(vec.shape, const), pl.BlockSpec(gm.shape, const), pl.BlockSpec(wo.shape, const)],
        out_specs=pl.BlockSpec((tm, D_MODEL), row),
        out_shape=jax.ShapeDtypeStruct((n, D_MODEL), F32),
        compiler_params=_cparams(("parallel",)),
        name="rwkv_out",
    )(x, y, r, k, v, g, vec, gm, wo)


def _pick_tile(n, want):
    t = min(n, want)
    while n % t:
        t //= 2
    return t


def _trunk(x, pos, conv_prev, shift_prev, wkv_prev, wts, sample):
    batch, seq, _ = x.shape
    n = batch * seq
    xf = x.reshape(n, D_MODEL)
    tm = _pick_tile(n, 512)

    tabs = _rope_tables(pos)
    if sample is not None:
        tabs = tuple(jnp.tile(t, (batch, 1)) for t in tabs)
    q, k, v, iq, ikw, cb, u = _proj_even(xf, wts['norm_mix_even'], wts['w_in'], tabs, tm)
    if sample is None:
        attn = _prompt_attention(q, iq, ikw, k, v, batch, seq)
        conv = _conv_prompt(cb, u, wts['conv_w'], seq, _pick_tile(seq, 512))
        conv_state = u.reshape(batch, seq, CONV_DIM)[:, seq - (CONV_W - 1):]
    else:
        cache_k, cache_v, cache_ik, page_table = sample
        tpad = SUBLANES - seq

        def stack(a, heads):
            a = a.reshape(batch, seq, heads, -1).transpose(0, 2, 1, 3)
            a = jnp.pad(a, ((0, 0), (0, 0), (0, tpad), (0, 0)))
            return a.reshape(batch, heads * SUBLANES, a.shape[-1])

        iq_st = stack(iq, N_IDX_HEADS)
        iw_st = stack(ikw[:, IDX_DIM:IDX_DIM + N_IDX_HEADS], N_IDX_HEADS)
        s_past = _sample_scores(page_table, iq_st, iw_st, cache_ik)[:, :seq]
        mask = _sample_select(s_past.reshape(n, -1), iq, ikw, seq)
        mask = jnp.pad(mask.reshape(batch, seq, -1), ((0, 0), (0, tpad), (0, 0)))
        q_st = stack(q, N_Q_HEADS).reshape(batch, N_KV_HEADS, Q_PER_KV * SUBLANES, HEAD_DIM)
        padk = lambda a: jnp.pad(a.reshape(batch, seq, KV_DIM), ((0, 0), (0, PAGE_SIZE - seq), (0, 0)))
        o = _sample_attention(page_table, q_st, mask, padk(k), padk(v), cache_k, cache_v)
        attn = (o.reshape(batch, N_Q_HEADS, SUBLANES, HEAD_DIM)[:, :, :seq].transpose(0, 2, 1, 3)
                .reshape(n, ATTN_DIM).astype(BF16))
        u3 = jnp.concatenate([conv_prev, u.reshape(batch, seq, CONV_DIM)], axis=1)
        conv_state = u3[:, seq:]
        conv_t = _conv_sample(jnp.swapaxes(cb.reshape(batch, seq, CONV_DIM), 0, 1), jnp.swapaxes(u3, 0, 1),
                              wts['conv_w'])
        conv = jnp.swapaxes(conv_t, 0, 1).reshape(n, CONV_DIM)
    x1 = _out_even(xf, attn, conv, wts['w_out_attn'], wts['w_out_conv'], tm)
    x2, xn2 = _mixer(x1, wts['norm_ffn_even'], None, wts['ffn_gate'], wts['ffn_up'], wts['ffn_down'],
                     wts['norm_mix_odd'], tm, D_FF // 2, True)

    xn3 = xn2.reshape(batch, seq, D_MODEL)
    xp = jnp.concatenate([shift_prev[:, None, :], xn3[:, :-1]], axis=1).reshape(n, D_MODEL)
    r, lw, kr, vr, kk, ka, g = _rwkv_proj(xn2, xp, wts['rwkv_mu'], wts['rwkv_vec_in'], wts['rwkv_w_rkv'],
                                          wts['rwkv_w1'], wts['rwkv_w2'], wts['rwkv_a1'], wts['rwkv_a2'],
                                          wts['rwkv_g1'], wts['rwkv_g2'], _pick_tile(n, 256))
    if seq % WKV_CHUNK:
        sp = -(-seq // WKV_CHUNK) * WKV_CHUNK
        padded = [jnp.pad(a.reshape(batch, seq, D_MODEL), ((0, 0), (0, sp - seq), (0, 0)))
                  .reshape(batch * sp, D_MODEL) for a in (r, lw, kr, vr, kk, ka)]
        y, wkv_state = _wkv(*padded, wkv_prev, batch, sp, WKV_CHUNK)
        y = y.reshape(batch, sp, D_MODEL)[:, :seq].reshape(n, D_MODEL)
    else:
        y, wkv_state = _wkv(r, lw, kr, vr, kk, ka, wkv_prev, batch, seq, WKV_CHUNK)
    x3 = _rwkv_out(x2, y, r, kr, vr, g, wts['rwkv_vec_out'], wts['rwkv_w_o'], _pick_tile(n, 256))
    (yf,) = _mixer(x3, wts['norm_ffn_odd'], wts['moe_router'], wts['moe_gate'], wts['moe_up'], wts['moe_down'],
                   wts['norm_final'], tm, D_FF // 2, False)

    k4 = k.reshape(1, batch, seq, N_KV_HEADS, HEAD_DIM)
    v4 = v.reshape(1, batch, seq, N_KV_HEADS, HEAD_DIM)
    ik3 = ikw[:, :IDX_DIM].reshape(1, batch, seq, IDX_DIM)
    return (yf.reshape(batch, seq, D_MODEL), k4, v4, ik3, conv_state[None],
            xn3[:, -1][None], wkv_state[None])


def _prepare_weights(norm_mix_even, w_in_even, conv_w, w_out_even, norm_ffn_even, ffn_gate, ffn_up, ffn_down,
                     norm_mix_odd, rwkv_mu, rwkv_w_rkv, rwkv_w0, rwkv_w1, rwkv_w2, rwkv_a0, rwkv_a1, rwkv_a2,
                     rwkv_g1, rwkv_g2, rwkv_k_k, rwkv_k_a, rwkv_r_k, rwkv_ln_w, rwkv_ln_b, rwkv_w_o,
                     norm_ffn_odd, moe_router, moe_gate, moe_up, moe_down, norm_final):
    w_in = w_in_even[0]
    o = np.cumsum((0, ATTN_DIM, KV_DIM, KV_DIM, N_IDX_HEADS * IDX_DIM, IDX_DIM, N_IDX_HEADS,
                   CONV_DIM, CONV_DIM, CONV_DIM))
    pad = jnp.zeros((D_MODEL, LANES - IDX_DIM - N_IDX_HEADS), F32)
    w_in = jnp.concatenate([w_in[:, :o[4]], w_in[:, o[4]:o[6]], pad, w_in[:, o[6]:]], axis=1).astype(BF16)
    row = lambda a: a.reshape(1, -1)
    zeros = jnp.zeros((1, D_MODEL), F32)
    return dict(
        norm_mix_even=row(norm_mix_even[0]), w_in=w_in, conv_w=conv_w[0],
        w_out_attn=w_out_even[0, :ATTN_DIM].astype(BF16), w_out_conv=w_out_even[0, ATTN_DIM:].astype(BF16),
        norm_ffn_even=row(norm_ffn_even[0]),
        ffn_gate=ffn_gate.astype(BF16), ffn_up=ffn_up.astype(BF16), ffn_down=ffn_down.astype(BF16),
        norm_mix_odd=row(norm_mix_odd[0]), rwkv_mu=jnp.concatenate([rwkv_mu[0], zeros, zeros], axis=0),
        rwkv_vec_in=jnp.concatenate([row(rwkv_w0[0]), row(rwkv_a0[0]), row(rwkv_k_k[0]), row(rwkv_k_a[0]),
                                     zeros, zeros, zeros, zeros], axis=0),
        rwkv_w_rkv=rwkv_w_rkv[0].astype(BF16),
        rwkv_w1=rwkv_w1[0].astype(BF16), rwkv_w2=rwkv_w2[0].astype(BF16),
        rwkv_a1=rwkv_a1[0].astype(BF16), rwkv_a2=rwkv_a2[0].astype(BF16),
        rwkv_g1=rwkv_g1[0].astype(BF16), rwkv_g2=rwkv_g2[0].astype(BF16),
        rwkv_vec_out=jnp.concatenate([row(rwkv_ln_w[0]), row(rwkv_ln_b[0]), row(rwkv_r_k[0]),
                                      zeros, zeros, zeros, zeros, zeros], axis=0),
        rwkv_w_o=rwkv_w_o[0].astype(BF16),
        norm_ffn_odd=row(norm_ffn_odd[0]),
        moe_router=jnp.pad(moe_router[0], ((0, 0), (0, LANES - N_EXPERTS))).astype(BF16),
        moe_gate=moe_gate[0].astype(BF16), moe_up=moe_up[0].astype(BF16), moe_down=moe_down[0].astype(BF16),
        norm_final=row(norm_final),
    )


def kernel(x_prompt, x_sample, cache_k, cache_v, cache_idx_k, state_conv, state_shift, state_wkv, page_table, norm_mix_even, w_in_even, conv_w, w_out_even, norm_ffn_even, ffn_gate, ffn_up, ffn_down, norm_mix_odd, rwkv_mu, rwkv_w_rkv, rwkv_w0, rwkv_w1, rwkv_w2, rwkv_a0, rwkv_a1, rwkv_a2, rwkv_g1, rwkv_g2, rwkv_k_k, rwkv_k_a, rwkv_r_k, rwkv_ln_w, rwkv_ln_b, rwkv_w_o, norm_ffn_odd, moe_router, moe_gate, moe_up, moe_down, norm_final):
    assert w_in_even.shape[0] == 1 and rwkv_mu.shape[0] == 1, "one even and one odd layer"
    wts = _prepare_weights(norm_mix_even, w_in_even, conv_w, w_out_even, norm_ffn_even, ffn_gate, ffn_up, ffn_down,
                           norm_mix_odd, rwkv_mu, rwkv_w_rkv, rwkv_w0, rwkv_w1, rwkv_w2, rwkv_a0, rwkv_a1, rwkv_a2,
                           rwkv_g1, rwkv_g2, rwkv_k_k, rwkv_k_a, rwkv_r_k, rwkv_ln_w, rwkv_ln_b, rwkv_w_o,
                           norm_ffn_odd, moe_router, moe_gate, moe_up, moe_down, norm_final)
    b, t = x_prompt.shape[:2]
    bd, tn = x_sample.shape[:2]
    n_pool = cache_k.shape[1]
    past = page_table.shape[1] * PAGE_SIZE
    pos_prompt = jnp.arange(t, dtype=jnp.int32)
    pos_sample = past + jnp.arange(tn, dtype=jnp.int32)
    zeros = lambda *s: jnp.zeros(s, F32)
    out_p = _trunk(x_prompt, pos_prompt, zeros(b, CONV_W - 1, CONV_DIM), zeros(b, D_MODEL),
                   zeros(b, RWKV_HEADS, RWKV_HEAD, RWKV_HEAD), wts, None)
    sample = (cache_k[0].reshape(n_pool, PAGE_SIZE, KV_DIM), cache_v[0].reshape(n_pool, PAGE_SIZE, KV_DIM),
              cache_idx_k[0], page_table)
    out_s = _trunk(x_sample, pos_sample, state_conv[0], state_shift[0], state_wkv[0], wts, sample)
    y_p, k_p, v_p, ik_p, conv_p, shift_p, wkv_p = out_p
    y_s, k_s, v_s, ik_s, conv_s, shift_s, wkv_s = out_s
    return (y_p, y_s, k_p, v_p, ik_p, k_s, v_s, ik_s, conv_p, conv_s, shift_p, shift_s, wkv_p, wkv_s)
```

```python
import functools

import numpy as np
import jax
import jax.numpy as jnp
from jax import lax
from jax.experimental import pallas as pl
from jax.experimental.pallas import tpu as pltpu

F32 = jnp.float32
BF16 = jnp.bfloat16
HI = lax.Precision.HIGHEST

D_MODEL = 1024
PAGE_SIZE = 128
HEAD_DIM = 64
N_Q_HEADS = 8
N_KV_HEADS = 2
Q_PER_KV = N_Q_HEADS // N_KV_HEADS
ROT_DIM = HEAD_DIM // 4
ROPE_THETA = 500000.0
N_IDX_HEADS = 4
IDX_DIM = 64
TOPK_MAX = 256
Q_BLOCK = 128
ATTN_DIM = N_Q_HEADS * HEAD_DIM
KV_DIM = N_KV_HEADS * HEAD_DIM
CONV_DIM = D_MODEL // 2
CONV_W = 3
RWKV_HEAD = 64
RWKV_HEADS = D_MODEL // RWKV_HEAD
GN_EPS = 64e-5
D_FF = 2816
N_EXPERTS = 8
RMS_EPS = 1e-6

LANES = 128
SUBLANES = 8
VMEM_LIMIT = 56 * 1024 * 1024
INT_MIN = -2 ** 31
KEY_NEG_INF = INT_MIN + 0x7FFFFF
PROJ_COLS = ATTN_DIM + KV_DIM + KV_DIM + N_IDX_HEADS * IDX_DIM + LANES + 3 * CONV_DIM
WKV_CHUNK = 64
MOE_GROUP = 1024
MOE_BLOCK_ROWS = 128


def _cparams(sem):
    return pltpu.CompilerParams(dimension_semantics=sem, vmem_limit_bytes=VMEM_LIMIT)


def _rms(x, g):
    return x * lax.rsqrt(jnp.mean(x * x, axis=-1, keepdims=True) + RMS_EPS) * g


def _dot(a, b):
    return jnp.dot(a, b, preferred_element_type=F32)


def _dot_nt(a, b, precision=None):
    return lax.dot_general(a, b, (((1,), (1,)), ((), ())), preferred_element_type=F32, precision=precision)


def _dot_tn(a, b, precision=None):
    return lax.dot_general(a, b, (((0,), (0,)), ((), ())), preferred_element_type=F32, precision=precision)


def _group_sum(x, gmat):
    outs = []
    for c in range(x.shape[1] // LANES):
        xc = x[:, c * LANES:(c + 1) * LANES]
        hi = xc.astype(BF16)
        lo = (xc - hi.astype(F32)).astype(BF16)
        outs.append(_dot(hi, gmat) + _dot(lo, gmat))
    return jnp.concatenate(outs, axis=1)


def _rope_chunk(xc, rc, rp, rm):
    return xc * rc + pltpu.roll(xc, 8, 1) * rp + pltpu.roll(xc, LANES - 8, 1) * rm


def _proj_even_body(x_ref, g_ref, w_ref, rc_ref, rp_ref, rm_ref,
                    q_ref, k_ref, v_ref, iq_ref, ikw_ref, cb_ref, u_ref):
    xn = _rms(x_ref[...], g_ref[...]).astype(BF16)
    h = _dot(xn, w_ref[...])
    rc, rp, rm = rc_ref[...], rp_ref[...], rm_ref[...]
    col = 0
    for c in range(ATTN_DIM // LANES):
        q_ref[:, c * LANES:(c + 1) * LANES] = _rope_chunk(h[:, col:col + LANES], rc, rp, rm).astype(BF16)
        col += LANES
    k_ref[...] = _rope_chunk(h[:, col:col + LANES], rc, rp, rm)
    col += LANES
    v_ref[...] = h[:, col:col + LANES]
    col += LANES
    for c in range(N_IDX_HEADS * IDX_DIM // LANES):
        iq_ref[:, c * LANES:(c + 1) * LANES] = _rope_chunk(h[:, col:col + LANES], rc, rp, rm).astype(BF16)
        col += LANES
    ikw = h[:, col:col + LANES]
    lane = lax.broadcasted_iota(jnp.int32, ikw.shape, 1)
    ikw_ref[...] = jnp.where(lane < IDX_DIM, _rope_chunk(ikw, rc, rp, rm), ikw)
    col += LANES
    cb_ref[...] = h[:, col:col + CONV_DIM]
    col += CONV_DIM
    u_ref[...] = h[:, col:col + CONV_DIM] * h[:, col + CONV_DIM:col + 2 * CONV_DIM]


def _proj_even(x, g, w, tabs, tm):
    n = x.shape[0]
    nt = tabs[0].shape[0] // tm
    row = lambda i: (i, 0)
    const = lambda i: (0, 0)
    tab = lambda i: (i % nt, 0)
    widths = (ATTN_DIM, KV_DIM, KV_DIM, N_IDX_HEADS * IDX_DIM, LANES, CONV_DIM, CONV_DIM)
    dtypes = (BF16, F32, F32, BF16, F32, F32, F32)
    return pl.pallas_call(
        _proj_even_body,
        grid=(n // tm,),
        in_specs=[pl.BlockSpec((tm, D_MODEL), row), pl.BlockSpec((1, D_MODEL), const),
                  pl.BlockSpec((D_MODEL, PROJ_COLS), const)] + [pl.BlockSpec((tm, LANES), tab)] * 3,
        out_specs=[pl.BlockSpec((tm, wd), row) for wd in widths],
        out_shape=[jax.ShapeDtypeStruct((n, wd), dt) for wd, dt in zip(widths, dtypes)],
        compiler_params=_cparams(("parallel",)),
        name="proj_even",
    )(x, g, w, *tabs)


def _rope_tables(pos):
    half = ROT_DIM // 2
    inv_freq = ROPE_THETA ** (-jnp.arange(half, dtype=F32) / half)
    ang = pos.astype(F32)[:, None] * inv_freq[None, :]
    cos, sin = jnp.cos(ang), jnp.sin(ang)
    t = pos.shape[0]
    pad = jnp.zeros((t, HEAD_DIM - ROT_DIM), F32)
    zero = jnp.zeros((t, half), F32)
    rc = jnp.concatenate([cos, cos, pad + 1.0], axis=1)
    rp = jnp.concatenate([zero, sin, pad], axis=1)
    rm = jnp.concatenate([-sin, zero, pad], axis=1)
    return tuple(jnp.tile(a, (1, LANES // HEAD_DIM)) for a in (rc, rp, rm))


def _conv_prompt_body(cb_ref, u_ref, up_ref, w_ref, y_ref, *, tiles_per_seq):
    u = u_ref[...]
    first = pl.program_id(0) % tiles_per_seq == 0
    prev = jnp.where(first, 0.0, up_ref[...])
    p1, p2 = prev[SUBLANES - 1:SUBLANES, :], prev[SUBLANES - 2:SUBLANES - 1, :]
    r = lax.broadcasted_iota(jnp.int32, u.shape, 0)
    u1 = jnp.where(r == 0, p1, pltpu.roll(u, 1, 0))
    u2 = jnp.where(r == 0, p2, jnp.where(r == 1, p1, pltpu.roll(u, 2, 0)))
    w = w_ref[...]
    y_ref[...] = cb_ref[...] * (w[0:1, :] * u2 + w[1:2, :] * u1 + w[2:3, :] * u)


def _conv_prompt(cb, u, w, seq, tc):
    n = u.shape[0]
    row = lambda i: (i, 0)
    prev = lambda i: (jnp.maximum(i * (tc // SUBLANES) - 1, 0), 0)
    return pl.pallas_call(
        functools.partial(_conv_prompt_body, tiles_per_seq=seq // tc),
        grid=(n // tc,),
        in_specs=[pl.BlockSpec((tc, CONV_DIM), row), pl.BlockSpec((tc, CONV_DIM), row),
                  pl.BlockSpec((SUBLANES, CONV_DIM), prev), pl.BlockSpec((CONV_W, CONV_DIM), lambda i: (0, 0))],
        out_specs=pl.BlockSpec((tc, CONV_DIM), row),
        out_shape=jax.ShapeDtypeStruct((n, CONV_DIM), F32),
        compiler_params=_cparams(("parallel",)),
        name="conv_prompt",
    )(cb, u, u, w)


def _conv_sample_body(cb_ref, ue_ref, w_ref, y_ref):
    w = w_ref[...]
    for t in range(y_ref.shape[0]):
        acc = w[0:1, :] * ue_ref[t] + w[1:2, :] * ue_ref[t + 1] + w[2:3, :] * ue_ref[t + 2]
        y_ref[t] = cb_ref[t] * acc


def _conv_sample(cb_t, ue_t, w):
    return pl.pallas_call(
        _conv_sample_body,
        out_shape=jax.ShapeDtypeStruct(cb_t.shape, F32),
        name="conv_sample",
    )(cb_t, ue_t, w)


def _topk_select(scores, n_sel, tri):
    rows, width = scores.shape
    sc = scores

    def key_to_float(key):
        return lax.bitcast_convert_type(key ^ ((key >> 31) & 0x7FFFFFFF), F32)

    def count_ge(key):
        cnt = jnp.sum(jnp.where(sc >= key_to_float(key), 1.0, 0.0), axis=1, keepdims=True)
        return jnp.where(key <= KEY_NEG_INF, float(width), cnt)

    thr = jnp.where(count_ge(jnp.zeros((rows, 1), jnp.int32)) >= n_sel, 0, INT_MIN).astype(jnp.int32)

    def body(i, thr):
        cand = thr + lax.shift_left(jnp.int32(1), 30 - i)
        return jnp.where(count_ge(cand) >= n_sel, cand, thr)

    thr = lax.fori_loop(0, 31, body, thr)
    thr_f = key_to_float(thr)
    gt = sc > thr_f
    eq = sc == thr_f
    need = n_sel - jnp.sum(jnp.where(gt, 1.0, 0.0), axis=1, keepdims=True)
    off = jnp.zeros((rows, 1), F32)
    parts = []
    for c in range(width // LANES):
        eqc = jnp.where(eq[:, c * LANES:(c + 1) * LANES], 1.0, 0.0)
        incl = _dot(eqc.astype(BF16), tri)
        parts.append((incl - eqc + off) < need)
        off = off + incl[:, LANES - 1:LANES]
    tie = jnp.concatenate(parts, axis=1)
    return (gt | (eq & tie)) & (sc > -jnp.inf)


def _tri_incl():
    i = np.arange(LANES)
    return jnp.asarray((i[:, None] <= i[None, :]).astype(np.float32), BF16)


def _stack_heads(x, n):
    return jnp.concatenate([x[:, h * HEAD_DIM:(h + 1) * HEAD_DIM] for h in range(n)], axis=0)


def _index_scores(iq_st, iw_st, ik_b):
    rows = iq_st.shape[0] // N_IDX_HEADS
    s = _dot_nt(iq_st, ik_b)
    term = jnp.maximum(s, 0.0) * (IDX_DIM ** -0.5) * (iw_st * (N_IDX_HEADS ** -0.5))
    acc = term[0:rows, :]
    for h in range(1, N_IDX_HEADS):
        acc = acc + term[h * rows:(h + 1) * rows, :]
    return acc


def _group_logits(qs, kg, sel):
    s = _dot_nt(qs, kg) * (HEAD_DIM ** -0.5)
    return jnp.where(jnp.concatenate([sel] * Q_PER_KV, axis=0), s, -jnp.inf)


def _masked_attention(q, kgs, vgs, sel):
    rows = q.shape[0]
    outs = []
    for g in range(N_KV_HEADS):
        qs = _stack_heads(q[:, g * Q_PER_KV * HEAD_DIM:(g + 1) * Q_PER_KV * HEAD_DIM], Q_PER_KV)
        s = _group_logits(qs, kgs[g], sel)
        p = jnp.exp(s - jnp.max(s, axis=1, keepdims=True))
        o = _dot(p.astype(BF16), vgs[g]) / jnp.sum(p, axis=1, keepdims=True)
        outs.extend(o[r * rows:(r + 1) * rows, :] for r in range(Q_PER_KV))
    return jnp.concatenate(outs, axis=1)


def _prompt_attn_body(q_ref, iq_ref, iwq_ref, k_ref, v_ref, ikw_ref, tri_ref, o_ref,
                      kb_scr, vb_scr, ikb_scr, *, n_sel, widths):
    i = pl.program_id(1)

    @pl.when(i == 0)
    def _():
        for g in range(N_KV_HEADS):
            kb_scr[g] = k_ref[:, g * HEAD_DIM:(g + 1) * HEAD_DIM].astype(BF16)
            vb_scr[g] = v_ref[:, g * HEAD_DIM:(g + 1) * HEAD_DIM].astype(BF16)
        ikb_scr[...] = ikw_ref[:, :IDX_DIM].astype(BF16)

    def run(width):
        iw = iwq_ref[...]
        iw_st = jnp.concatenate([iw[:, IDX_DIM + h:IDX_DIM + h + 1] for h in range(N_IDX_HEADS)], axis=0)
        scores = _index_scores(_stack_heads(iq_ref[...], N_IDX_HEADS), iw_st, ikb_scr[0:width, :])
        tq = i * Q_BLOCK + lax.broadcasted_iota(jnp.int32, scores.shape, 0)
        key_pos = lax.broadcasted_iota(jnp.int32, scores.shape, 1)
        scores = jnp.where(key_pos <= tq, scores, -jnp.inf)
        sel = _topk_select(scores, n_sel, tri_ref[...])
        kgs = [kb_scr[g, 0:width, :] for g in range(N_KV_HEADS)]
        vgs = [vb_scr[g, 0:width, :] for g in range(N_KV_HEADS)]
        o_ref[...] = _masked_attention(q_ref[...], kgs, vgs, sel).astype(o_ref.dtype)

    lo = 0
    for width in widths:
        hi = width // Q_BLOCK
        pl.when((i >= lo) & (i < hi))(functools.partial(run, width))
        lo = hi


def _prompt_attention(q, iq, ikw, k, v, batch, seq):
    n_sel = min(TOPK_MAX, seq // 4)
    nqb = seq // Q_BLOCK
    n_widths = min(4, nqb)
    widths = tuple(seq * (j + 1) // n_widths for j in range(n_widths))
    blk = lambda b, i: (b * nqb + i, 0)
    full = lambda b, i: (b, 0)
    return pl.pallas_call(
        functools.partial(_prompt_attn_body, n_sel=n_sel, widths=widths),
        grid=(batch, nqb),
        in_specs=[pl.BlockSpec((Q_BLOCK, ATTN_DIM), blk), pl.BlockSpec((Q_BLOCK, N_IDX_HEADS * IDX_DIM), blk),
                  pl.BlockSpec((Q_BLOCK, LANES), blk),
                  pl.BlockSpec((seq, KV_DIM), full), pl.BlockSpec((seq, KV_DIM), full),
                  pl.BlockSpec((seq, LANES), full), pl.BlockSpec((LANES, LANES), lambda b, i: (0, 0))],
        out_specs=pl.BlockSpec((Q_BLOCK, ATTN_DIM), blk),
        out_shape=jax.ShapeDtypeStruct((batch * seq, ATTN_DIM), BF16),
        scratch_shapes=[pltpu.VMEM((N_KV_HEADS, seq, HEAD_DIM), BF16), pltpu.VMEM((N_KV_HEADS, seq, HEAD_DIM), BF16),
                        pltpu.VMEM((seq, IDX_DIM), BF16)],
        compiler_params=_cparams(("parallel", "arbitrary")),
        name="prompt_attention",
    )(q, iq, ikw, k, v, ikw, _tri_incl())


PAGES_PER_STEP = 16


def _sample_scores_body(pt_ref, iq_ref, iw_ref, *rest):
    page_refs, s_ref = rest[:PAGES_PER_STEP], rest[PAGES_PER_STEP]
    for j, pr in enumerate(page_refs):
        s_ref[0, :, j * PAGE_SIZE:(j + 1) * PAGE_SIZE] = _index_scores(iq_ref[0], iw_ref[0], pr[0].astype(BF16))


def _sample_scores(page_table, iq_st, iw_st, cache_ik):
    bd, n_pages = page_table.shape
    rows = iq_st.shape[1] // N_IDX_HEADS
    steps = n_pages // PAGES_PER_STEP
    per_b = lambda b, p, pt: (b, 0, 0)
    page = lambda j: (lambda b, p, pt: (pt[b, p * PAGES_PER_STEP + j], 0, 0))
    return pl.pallas_call(
        _sample_scores_body,
        grid_spec=pltpu.PrefetchScalarGridSpec(
            num_scalar_prefetch=1,
            grid=(bd, steps),
            in_specs=[pl.BlockSpec((1,) + iq_st.shape[1:], per_b), pl.BlockSpec((1,) + iw_st.shape[1:], per_b)]
                     + [pl.BlockSpec((1, PAGE_SIZE, IDX_DIM), page(j)) for j in range(PAGES_PER_STEP)],
            out_specs=pl.BlockSpec((1, rows, PAGES_PER_STEP * PAGE_SIZE), lambda b, p, pt: (b, 0, p)),
        ),
        out_shape=jax.ShapeDtypeStruct((bd, rows, n_pages * PAGE_SIZE), F32),
        compiler_params=_cparams(("parallel", "arbitrary")),
        name="sample_scores",
    )(page_table, iq_st, iw_st, *([cache_ik] * PAGES_PER_STEP))


def _sample_select_body(sp_ref, iq_ref, iwq_ref, ikn_ref, tri_ref, m_ref, *, n_sel, tn):
    rows = sp_ref.shape[0]
    iw = iwq_ref[...]
    iw_st = jnp.concatenate([iw[:, IDX_DIM + h:IDX_DIM + h + 1] for h in range(N_IDX_HEADS)], axis=0)
    s_new = _index_scores(_stack_heads(iq_ref[...], N_IDX_HEADS), iw_st, ikn_ref[...].astype(BF16))
    r = lax.broadcasted_iota(jnp.int32, s_new.shape, 0)
    c = lax.broadcasted_iota(jnp.int32, s_new.shape, 1)
    same = (r // tn == c // tn) & (c <= r)
    fold = jnp.where((lax.broadcasted_iota(jnp.int32, (rows, LANES), 0) % tn)
                     == lax.broadcasted_iota(jnp.int32, (rows, LANES), 1), 1.0, 0.0)
    picked = jnp.where(same, s_new, 0.0)
    hi = picked.astype(BF16)
    mid = (picked - hi.astype(F32)).astype(BF16)
    lo = (picked - hi.astype(F32) - mid.astype(F32)).astype(BF16)
    fb = fold.astype(BF16)
    new_chunk = _dot(hi, fb) + _dot(mid, fb) + _dot(lo, fb)
    lane = lax.broadcasted_iota(jnp.int32, (rows, LANES), 1)
    tpos = lax.broadcasted_iota(jnp.int32, (rows, LANES), 0) % tn
    new_chunk = jnp.where(lane <= tpos, new_chunk, -jnp.inf)
    scores = jnp.concatenate([sp_ref[...], new_chunk], axis=1)
    sel = _topk_select(scores, n_sel, tri_ref[...])
    m_ref[...] = jnp.where(sel, 1.0, 0.0)


def _sample_select(s_past, iq, ikw, tn):
    rows, past = s_past.shape
    n_sel = min(TOPK_MAX, (past + tn) // 4)
    return pl.pallas_call(
        functools.partial(_sample_select_body, n_sel=n_sel, tn=tn),
        out_shape=jax.ShapeDtypeStruct((rows, past + LANES), F32),
        compiler_params=pltpu.CompilerParams(vmem_limit_bytes=VMEM_LIMIT),
        name="sample_select",
    )(s_past, iq, ikw, ikw[:, :IDX_DIM], _tri_incl())


def _sample_attn_body(pt_ref, q_ref, m_ref, kn_ref, vn_ref, mn_ref, *rest):
    k_refs, v_refs = rest[:PAGES_PER_STEP], rest[PAGES_PER_STEP:2 * PAGES_PER_STEP]
    o_ref, m_scr, l_scr, acc_scr = rest[2 * PAGES_PER_STEP:]
    p = pl.program_id(1)

    @pl.when(p == 0)
    def _():
        m_scr[...] = jnp.full(m_scr.shape, -jnp.inf, F32)
        l_scr[...] = jnp.zeros(l_scr.shape, F32)
        acc_scr[...] = jnp.zeros(acc_scr.shape, F32)

    def update(kb, vb, sel):
        for g in range(N_KV_HEADS):
            s = _group_logits(q_ref[0, g], kb[:, g * HEAD_DIM:(g + 1) * HEAD_DIM], sel)
            m_old = m_scr[g]
            m_new = jnp.maximum(m_old, jnp.max(s, axis=1, keepdims=True))
            m_safe = jnp.where(m_new == -jnp.inf, 0.0, m_new)
            alpha = jnp.exp(m_old - m_safe)
            pe = jnp.exp(s - m_safe)
            l_scr[g] = alpha * l_scr[g] + jnp.sum(pe, axis=1, keepdims=True)
            acc_scr[g] = alpha * acc_scr[g] + _dot(pe.astype(BF16), vb[:, g * HEAD_DIM:(g + 1) * HEAD_DIM])
            m_scr[g] = m_new

    kb = jnp.concatenate([r[0] for r in k_refs], axis=0).astype(BF16)
    vb = jnp.concatenate([r[0] for r in v_refs], axis=0).astype(BF16)
    update(kb, vb, m_ref[0] > 0.5)

    @pl.when(p == pl.num_programs(1) - 1)
    def _():
        update(kn_ref[0].astype(BF16), vn_ref[0].astype(BF16), mn_ref[0] > 0.5)
        for g in range(N_KV_HEADS):
            l = l_scr[g]
            o_ref[0, g] = acc_scr[g] / jnp.where(l == 0.0, 1.0, l)


def _sample_attention(page_table, q_st, mask, k_new, v_new, cache_k, cache_v):
    bd, n_pages = page_table.shape
    steps = n_pages // PAGES_PER_STEP
    per_b = lambda b, p, pt: (b, 0, 0)
    per_b4 = lambda b, p, pt: (b, 0, 0, 0)
    page = lambda j: (lambda b, p, pt: (pt[b, p * PAGES_PER_STEP + j], 0, 0))
    width = PAGES_PER_STEP * PAGE_SIZE
    rows = q_st.shape[2]
    mrows = mask.shape[1]
    return pl.pallas_call(
        _sample_attn_body,
        grid_spec=pltpu.PrefetchScalarGridSpec(
            num_scalar_prefetch=1,
            grid=(bd, steps),
            in_specs=[pl.BlockSpec((1,) + q_st.shape[1:], per_b4),
                      pl.BlockSpec((1, mrows, width), lambda b, p, pt: (b, 0, p)),
                      pl.BlockSpec((1, PAGE_SIZE, KV_DIM), per_b), pl.BlockSpec((1, PAGE_SIZE, KV_DIM), per_b),
                      pl.BlockSpec((1, mrows, LANES), lambda b, p, pt: (b, 0, n_pages))]
                     + [pl.BlockSpec((1, PAGE_SIZE, KV_DIM), page(j)) for j in range(PAGES_PER_STEP)] * 2,
            out_specs=pl.BlockSpec((1, N_KV_HEADS, rows, HEAD_DIM), per_b4),
            scratch_shapes=[pltpu.VMEM((N_KV_HEADS, rows, 1), F32), pltpu.VMEM((N_KV_HEADS, rows, 1), F32),
                            pltpu.VMEM((N_KV_HEADS, rows, HEAD_DIM), F32)],
        ),
        out_shape=jax.ShapeDtypeStruct((bd, N_KV_HEADS, rows, HEAD_DIM), F32),
        compiler_params=_cparams(("parallel", "arbitrary")),
        name="sample_attention",
    )(page_table, q_st, mask, k_new, v_new, mask, *([cache_k] * PAGES_PER_STEP), *([cache_v] * PAGES_PER_STEP))


def _out_even_body(x_ref, a_ref, c_ref, wa_ref, wc_ref, o_ref):
    o_ref[...] = x_ref[...] + _dot(a_ref[...], wa_ref[...]) + _dot(c_ref[...].astype(BF16), wc_ref[...])


def _out_even(x, attn, conv, wa, wc, tm):
    n = x.shape[0]
    row = lambda i: (i, 0)
    const = lambda i: (0, 0)
    return pl.pallas_call(
        _out_even_body,
        grid=(n // tm,),
        in_specs=[pl.BlockSpec((tm, D_MODEL), row), pl.BlockSpec((tm, ATTN_DIM), row),
                  pl.BlockSpec((tm, CONV_DIM), row), pl.BlockSpec((ATTN_DIM, D_MODEL), const),
                  pl.BlockSpec((CONV_DIM, D_MODEL), const)],
        out_specs=pl.BlockSpec((tm, D_MODEL), row),
        out_shape=jax.ShapeDtypeStruct((n, D_MODEL), F32),
        compiler_params=_cparams(("parallel",)),
        name="out_even",
    )(x, attn, conv, wa, wc)


def _mixer_body(*refs, n_exp, emit_sum):
    if n_exp > 1:
        x_ref, g_ref, wr_ref, wg_ref, wu_ref, wd_ref, g2_ref = refs[:7]
        outs = refs[7:]
    else:
        x_ref, g_ref, wg_ref, wu_ref, wd_ref, g2_ref = refs[:6]
        outs = refs[6:]
    n_out = 2 if emit_sum else 1
    out_refs, (xn_scr, acc_scr, cw_scr) = outs[:n_out], outs[n_out:]
    e, f = pl.program_id(1), pl.program_id(2)

    @pl.when((e == 0) & (f == 0))
    def _():
        xn = _rms(x_ref[...], g_ref[...]).astype(BF16)
        xn_scr[...] = xn
        acc_scr[...] = jnp.zeros(acc_scr.shape, F32)
        if n_exp > 1:
            logits = _dot(xn, wr_ref[...])
            lane = lax.broadcasted_iota(jnp.int32, logits.shape, 1).astype(F32)
            lg = jnp.where(lane < n_exp, logits, -jnp.inf)
            m1 = jnp.max(lg, axis=1, keepdims=True)
            i1 = jnp.min(jnp.where(lg == m1, lane, float(LANES)), axis=1, keepdims=True)
            lg2 = jnp.where(lane == i1, -jnp.inf, lg)
            m2 = jnp.max(lg2, axis=1, keepdims=True)
            i2 = jnp.min(jnp.where(lg2 == m2, lane, float(LANES)), axis=1, keepdims=True)
            e2 = jnp.exp(m2 - m1)
            cw_scr[...] = jnp.where(lane == i1, 1.0 / (1.0 + e2), 0.0) + jnp.where(lane == i2, e2 / (1.0 + e2), 0.0)

    xn = xn_scr[...]
    gate = _dot(xn, wg_ref[0])
    h = gate * jax.nn.sigmoid(gate) * _dot(xn, wu_ref[0])
    if n_exp > 1:
        lane = lax.broadcasted_iota(jnp.int32, cw_scr.shape, 1)
        h = h * jnp.sum(jnp.where(lane == e, cw_scr[...], 0.0), axis=1, keepdims=True)
    acc_scr[...] += _dot(h.astype(BF16), wd_ref[0])

    @pl.when((e == pl.num_programs(1) - 1) & (f == pl.num_programs(2) - 1))
    def _():
        y = x_ref[...] + acc_scr[...]
        if emit_sum:
            out_refs[0][...] = y
        out_refs[-1][...] = _rms(y, g2_ref[...])


def _mixer(x, g, wr, wg, wu, wd, g2, tm, tf, emit_sum):
    n = x.shape[0]
    n_exp = wg.shape[0]
    row = lambda i, e, f: (i, 0)
    const = lambda i, e, f: (0, 0)
    in_specs = [pl.BlockSpec((tm, D_MODEL), row), pl.BlockSpec((1, D_MODEL), const)]
    args = [x, g]
    if n_exp > 1:
        in_specs.append(pl.BlockSpec((D_MODEL, LANES), const))
        args.append(wr)
    in_specs += [pl.BlockSpec((1, D_MODEL, tf), lambda i, e, f: (e, 0, f)),
                 pl.BlockSpec((1, D_MODEL, tf), lambda i, e, f: (e, 0, f)),
                 pl.BlockSpec((1, tf, D_MODEL), lambda i, e, f: (e, f, 0)),
                 pl.BlockSpec((1, D_MODEL), const)]
    args += [wg, wu, wd, g2]
    n_out = 2 if emit_sum else 1
    return pl.pallas_call(
        functools.partial(_mixer_body, n_exp=n_exp, emit_sum=emit_sum),
        grid=(n // tm, n_exp, D_FF // tf),
        in_specs=in_specs,
        out_specs=[pl.BlockSpec((tm, D_MODEL), row)] * n_out,
        out_shape=[jax.ShapeDtypeStruct((n, D_MODEL), F32)] * n_out,
        scratch_shapes=[pltpu.VMEM((tm, D_MODEL), BF16), pltpu.VMEM((tm, D_MODEL), F32),
                        pltpu.VMEM((tm, LANES), F32)],
        compiler_params=_cparams(("parallel", "arbitrary", "arbitrary")),
        name="mixer_moe" if n_exp > 1 else "mixer_ffn",
    )(*args)


def _moe_route_body(x_ref, g_ref, wr_ref, tri_ref, etri_ref,
                    xn_ref, slot_ref, slot_t_ref, comb_ref, meta_ref, *, n_exp, rows_blk):
    xn = _rms(x_ref[...], g_ref[...]).astype(BF16)
    xn_ref[...] = xn
    logits = _dot(xn, wr_ref[...])
    lane = lax.broadcasted_iota(jnp.int32, logits.shape, 1).astype(F32)
    lg = jnp.where(lane < n_exp, logits, -jnp.inf)
    m1 = jnp.max(lg, axis=1, keepdims=True)
    i1 = jnp.min(jnp.where(lg == m1, lane, float(LANES)), axis=1, keepdims=True)
    lg2 = jnp.where(lane == i1, -jnp.inf, lg)
    m2 = jnp.max(lg2, axis=1, keepdims=True)
    i2 = jnp.min(jnp.where(lg2 == m2, lane, float(LANES)), axis=1, keepdims=True)
    e2 = jnp.exp(m2 - m1)
    comb_ref[...] = jnp.where(lane == i1, 1.0 / (1.0 + e2), 0.0) + jnp.where(lane == i2, e2 / (1.0 + e2), 0.0)
    member = jnp.where((lane == i1) | (lane == i2), 1.0, 0.0)
    n_tok = member.shape[0]
    pos = _dot(tri_ref[...], member.astype(BF16))
    cnt = pos[n_tok - 1:n_tok, :] + member[n_tok - 1:n_tok, :]
    padded = jnp.floor((cnt + (rows_blk - 1)) * (1.0 / rows_blk)) * rows_blk
    start = _dot(jnp.broadcast_to(padded, (SUBLANES, LANES)).astype(BF16), etri_ref[...])[0:1, :]
    slot = jnp.where(member > 0.0, start + pos, -1.0)
    slot_ref[...] = slot
    slot_t_ref[0] = jnp.transpose(slot)[0:SUBLANES, :]
    ends = start + padded
    first_row = lax.broadcasted_iota(jnp.int32, (1, LANES), 1).astype(F32) * rows_blk
    blk_exp = jnp.zeros((1, LANES), F32)
    for e in range(n_exp - 1):
        blk_exp = blk_exp + jnp.where(ends[:, e:e + 1] <= first_row, 1.0, 0.0)
    n_blk = ends[:, n_exp - 1:n_exp] * (1.0 / rows_blk)
    row = lax.broadcasted_iota(jnp.int32, (SUBLANES, LANES), 0)
    meta = jnp.where(row == 0, blk_exp, jnp.where(row == 1, n_blk, 0.0))
    meta_ref[0] = meta.astype(jnp.int32)


def _moe_experts_body(be_ref, nb_ref, xn_ref, slot_ref, slot_t_ref, comb_ref, x_ref, wg_ref, wu_ref, wd_ref,
                      g2_ref, o_ref, xs_scr, ys_scr, *, rows_blk, nf):
    g, f, j = pl.program_id(0), pl.program_id(1), pl.program_id(2)
    nj = pl.num_programs(2)
    e = be_ref[g, j]
    used = j < nb_ref[g]
    n_tok = xn_ref.shape[0]
    rows = pl.ds(pl.multiple_of(j * rows_blk, rows_blk), rows_blk)
    first = j * rows_blk

    @pl.when((f == 0) & (j == 0))
    def _():
        o_ref[...] = x_ref[...]

    @pl.when(used & (f == 0))
    def _():
        srow = slot_t_ref[0, pl.ds(e, 1), :]
        want = (first + lax.broadcasted_iota(jnp.int32, (rows_blk, n_tok), 0)).astype(F32)
        onehot = jnp.where(srow == want, 1.0, 0.0).astype(BF16)
        xs_scr[rows, :] = _dot(onehot, xn_ref[...]).astype(BF16)

    @pl.when(used)
    def _():
        xs = xs_scr[rows, :]
        gate = _dot(xs, wg_ref[0])
        h = gate * jax.nn.sigmoid(gate) * _dot(xs, wu_ref[0])
        part = _dot(h.astype(BF16), wd_ref[0])

        if nf > 1:
            @pl.when(f == 0)
            def _():
                ys_scr[rows, :] = part.astype(BF16)

        if nf > 2:
            @pl.when((f > 0) & (f < nf - 1))
            def _():
                ys_scr[rows, :] = (ys_scr[rows, :].astype(F32) + part).astype(BF16)

        def scatter(total):
            lane = lax.broadcasted_iota(jnp.int32, (n_tok, LANES), 1)
            scol = jnp.sum(jnp.where(lane == e, slot_ref[...], 0.0), axis=1, keepdims=True)
            ccol = jnp.sum(jnp.where(lane == e, comb_ref[...], 0.0), axis=1, keepdims=True)
            want = (first + lax.broadcasted_iota(jnp.int32, (n_tok, rows_blk), 1)).astype(F32)
            onehot = jnp.where(scol == want, 1.0, 0.0).astype(BF16)
            o_ref[...] += ccol * _dot(onehot, total.astype(BF16))

        @pl.when(f == nf - 1)
        def _():
            scatter(part + ys_scr[rows, :].astype(F32) if nf > 1 else part)

    @pl.when((f == nf - 1) & (j == nj - 1))
    def _():
        o_ref[...] = _rms(o_ref[...], g2_ref[...])


def _moe(x, g, wr, wg, wu, wd, g2, tg, rows_blk, tf):
    n = x.shape[0]
    n_exp = wg.shape[0]
    n_groups = n // tg
    n_blk = 2 * tg // rows_blk + n_exp
    i = np.arange(tg)
    tri = jnp.asarray((i[None, :] < i[:, None]).astype(np.float32), BF16)
    i = np.arange(LANES)
    etri = jnp.asarray((i[:, None] < i[None, :]).astype(np.float32), BF16)
    row = lambda i: (i, 0)
    const = lambda i: (0, 0)
    xn, slot, slot_t, comb, meta = pl.pallas_call(
        functools.partial(_moe_route_body, n_exp=n_exp, rows_blk=rows_blk),
        grid=(n_groups,),
        in_specs=[pl.BlockSpec((tg, D_MODEL), row), pl.BlockSpec((1, D_MODEL), const),
                  pl.BlockSpec((D_MODEL, LANES), const), pl.BlockSpec((tg, tg), const),
                  pl.BlockSpec((LANES, LANES), const)],
        out_specs=[pl.BlockSpec((tg, D_MODEL), row), pl.BlockSpec((tg, LANES), row),
                   pl.BlockSpec((1, SUBLANES, tg), lambda i: (i, 0, 0)), pl.BlockSpec((tg, LANES), row),
                   pl.BlockSpec((1, SUBLANES, LANES), lambda i: (i, 0, 0))],
        out_shape=[jax.ShapeDtypeStruct((n, D_MODEL), BF16), jax.ShapeDtypeStruct((n, LANES), F32),
                   jax.ShapeDtypeStruct((n_groups, SUBLANES, tg), F32), jax.ShapeDtypeStruct((n, LANES), F32),
                   jax.ShapeDtypeStruct((n_groups, SUBLANES, LANES), jnp.int32)],
        compiler_params=_cparams(("parallel",)),
        name="moe_route",
    )(x, g, wr, tri, etri)
    blk_exp = meta[:, 0, :]
    blk_cnt = meta[:, 1, 0]
    per_g = lambda g, f, j, be, nb: (g, 0)
    once = pl.Buffered(1)
    return pl.pallas_call(
        functools.partial(_moe_experts_body, rows_blk=rows_blk, nf=D_FF // tf),
        grid_spec=pltpu.PrefetchScalarGridSpec(
            num_scalar_prefetch=2,
            grid=(n_groups, D_FF // tf, n_blk),
            in_specs=[pl.BlockSpec((tg, D_MODEL), per_g, pipeline_mode=once),
                      pl.BlockSpec((tg, LANES), per_g, pipeline_mode=once),
                      pl.BlockSpec((1, SUBLANES, tg), lambda g, f, j, be, nb: (g, 0, 0), pipeline_mode=once),
                      pl.BlockSpec((tg, LANES), per_g, pipeline_mode=once),
                      pl.BlockSpec((tg, D_MODEL), per_g, pipeline_mode=once),
                      pl.BlockSpec((1, D_MODEL, tf), lambda g, f, j, be, nb: (be[g, j], 0, f)),
                      pl.BlockSpec((1, D_MODEL, tf), lambda g, f, j, be, nb: (be[g, j], 0, f)),
                      pl.BlockSpec((1, tf, D_MODEL), lambda g, f, j, be, nb: (be[g, j], f, 0)),
                      pl.BlockSpec((1, D_MODEL), lambda g, f, j, be, nb: (0, 0))],
            out_specs=pl.BlockSpec((tg, D_MODEL), per_g, pipeline_mode=once),
            scratch_shapes=[pltpu.VMEM((n_blk * rows_blk, D_MODEL), BF16),
                            pltpu.VMEM((n_blk * rows_blk, D_MODEL), BF16)],
        ),
        out_shape=jax.ShapeDtypeStruct((n, D_MODEL), F32),
        compiler_params=_cparams(("parallel", "arbitrary", "arbitrary")),
        name="moe_experts",
    )(blk_exp, blk_cnt, xn, slot, slot_t, comb, x, wg, wu, wd, g2)


def _group_ones():
    i = np.arange(LANES) // RWKV_HEAD
    return jnp.asarray((i[:, None] == i[None, :]).astype(np.float32), BF16)


def _rwkv_proj_body(xn_ref, xp_ref, mu_ref, vec_ref, wrkv_ref, w1_ref, w2_ref, a1_ref, a2_ref, g1_ref, g2_ref,
                    gm_ref, r_ref, lw_ref, k_ref, v_ref, kk_ref, ka_ref, g_ref):
    xn = xn_ref[...]
    xx = xp_ref[...] - xn
    mu = mu_ref[...]
    mix = lambda i: (xn + xx * mu[i:i + 1, :]).astype(BF16)
    vec = vec_ref[...]
    w0, a0, k_k, k_a = vec[0:1, :], vec[1:2, :], vec[2:3, :], vec[3:4, :]
    r_ref[...] = _dot(mix(0), wrkv_ref[0])
    z = w0 + _dot(jnp.tanh(_dot(mix(1), w1_ref[...])).astype(BF16), w2_ref[...])
    softplus = jnp.maximum(-z, 0.0) + jnp.log(1.0 + jnp.exp(-jnp.abs(z)))
    lw_ref[...] = -jnp.exp(-softplus - 0.5)
    k = _dot(mix(2), wrkv_ref[1])
    v_ref[...] = _dot(mix(3), wrkv_ref[2])
    a = jax.nn.sigmoid(a0 + _dot(_dot(mix(4), a1_ref[...]).astype(BF16), a2_ref[...]))
    g_ref[...] = _dot(jax.nn.sigmoid(_dot(mix(5), g1_ref[...])).astype(BF16), g2_ref[...])
    kk = k * k_k
    kk = kk * lax.rsqrt(jnp.maximum(_group_sum(kk * kk, gm_ref[...]), 1e-24))
    kk_ref[...] = kk
    ka_ref[...] = kk * a
    k_ref[...] = k * (1.0 + (a - 1.0) * k_a)


def _rwkv_proj(xn, xp, mu, vec, wrkv, w1, w2, a1, a2, g1, g2, tm):
    n = xn.shape[0]
    row = lambda i: (i, 0)
    c2 = lambda i: (0, 0)
    c3 = lambda i: (0, 0, 0)
    full = lambda a: pl.BlockSpec(a.shape, c3 if a.ndim == 3 else c2)
    gm = _group_ones()
    consts = [mu, vec, wrkv, w1, w2, a1, a2, g1, g2, gm]
    return pl.pallas_call(
        _rwkv_proj_body,
        grid=(n // tm,),
        in_specs=[pl.BlockSpec((tm, D_MODEL), row)] * 2 + [full(a) for a in consts],
        out_specs=[pl.BlockSpec((tm, D_MODEL), row)] * 7,
        out_shape=[jax.ShapeDtypeStruct((n, D_MODEL), F32)] * 7,
        compiler_params=_cparams(("parallel",)),
        name="rwkv_proj",
    )(xn, xp, *consts)


def _wkv_body(r_ref, lw_ref, k_ref, v_ref, kk_ref, ka_ref, s0_ref, y_ref, sT_ref,
              s_scr, a_scr, r_scr, b_scr, k_scr, *, chunk):
    c = pl.program_id(1)
    n_pairs = D_MODEL // LANES
    hd = RWKV_HEAD
    zeros = jnp.zeros((hd, hd), F32)

    @pl.when(c == 0)
    def _():
        for p in range(n_pairs):
            top = jnp.concatenate([s0_ref[0, 2 * p], zeros], axis=1)
            bot = jnp.concatenate([zeros, s0_ref[0, 2 * p + 1]], axis=1)
            s_scr[p] = jnp.concatenate([top, bot], axis=0)

    row = lax.broadcasted_iota(jnp.int32, (chunk, chunk), 0)
    colm = lax.broadcasted_iota(jnp.int32, (chunk, chunk), 1)
    lw = lw_ref[...]
    cum = jnp.dot(jnp.where(row >= colm, 1.0, 0.0), lw, preferred_element_type=F32, precision=HI)
    w_incl = jnp.exp(cum)
    w_inv = jnp.exp(-cum)
    a_scr[...] = -kk_ref[...] * jnp.exp(cum - lw)
    r_scr[...] = r_ref[...] * w_incl
    b_scr[...] = ka_ref[...] * w_inv
    k_scr[...] = k_ref[...] * w_inv
    w_last = w_incl[chunk - 1:chunk, :]
    n_dbl = max(1, int(np.ceil(np.log2(chunk))))

    c2 = 2 * chunk
    head_of_row = lax.broadcasted_iota(jnp.int32, (c2, LANES), 0) // chunk
    head_of_lane = lax.broadcasted_iota(jnp.int32, (c2, LANES), 1) // hd
    own_lanes = head_of_row == head_of_lane
    tr = lax.broadcasted_iota(jnp.int32, (c2, c2), 0)
    tc = lax.broadcasted_iota(jnp.int32, (c2, c2), 1)
    same = (tr // chunk) == (tc // chunk)
    strict = same & (tr % chunk > tc % chunk)
    incl = same & (tr % chunk >= tc % chunk)

    def block_diag(x):
        return jnp.where(own_lanes, jnp.concatenate([x, x], axis=0), 0.0).astype(BF16)

    pairs = range(n_pairs)
    sls = [slice(p * LANES, (p + 1) * LANES) for p in pairs]
    ar = [jnp.concatenate([block_diag(a_scr[:, sl]), block_diag(r_scr[:, sl])], axis=0) for sl in sls]
    bk = [jnp.concatenate([block_diag(b_scr[:, sl]), block_diag(k_scr[:, sl])], axis=0) for sl in sls]
    vm = [block_diag(v_ref[:, sl]) for sl in sls]
    s_old = [s_scr[p] for p in pairs]
    gram = [_dot_nt(ar[p], bk[p]) for p in pairs]
    xs = [_dot_nt(ar[p], s_old[p].astype(BF16)) for p in pairs]
    u = [xs[p][:c2, :] + _dot(jnp.where(strict, gram[p][:c2, c2:], 0.0).astype(BF16), vm[p]) for p in pairs]
    lp = [jnp.where(strict, gram[p][:c2, :c2], 0.0).astype(BF16) for p in pairs]
    for d in range(n_dbl):
        if d + 1 < n_dbl:
            t = [_dot(lp[p], jnp.concatenate([u[p].astype(BF16), lp[p]], axis=1)) for p in pairs]
            u = [u[p] + t[p][:, :LANES] for p in pairs]
            lp = [t[p][:, LANES:].astype(BF16) for p in pairs]
        else:
            u = [u[p] + _dot(lp[p], u[p].astype(BF16)) for p in pairs]
    uv = [jnp.concatenate([u[p].astype(BF16), vm[p]], axis=0) for p in pairs]
    m_r = [jnp.concatenate([jnp.where(incl, gram[p][c2:, :c2], 0.0), jnp.where(incl, gram[p][c2:, c2:], 0.0)],
                           axis=1).astype(BF16) for p in pairs]
    y = [xs[p][c2:, :] + _dot(m_r[p], uv[p]) for p in pairs]
    for p in pairs:
        y_ref[:, sls[p]] = y[p][:chunk, :] + y[p][chunk:, :]
    s_new = [(s_old[p] + _dot_tn(uv[p], bk[p])) * w_last[:, sls[p]] for p in pairs]
    for p in pairs:
        s_scr[p] = s_new[p]

    @pl.when(c == pl.num_programs(1) - 1)
    def _():
        for p in range(n_pairs):
            s = s_scr[p]
            sT_ref[0, 2 * p] = s[:hd, :hd]
            sT_ref[0, 2 * p + 1] = s[hd:, hd:]


def _wkv(r, lw, k, v, kk, ka, s0, batch, seq, chunk):
    nc = seq // chunk
    blk = lambda b, c: (b * nc + c, 0)
    st = lambda b, c: (b, 0, 0, 0)
    state = pl.BlockSpec((1, RWKV_HEADS, RWKV_HEAD, RWKV_HEAD), st)
    return pl.pallas_call(
        functools.partial(_wkv_body, chunk=chunk),
        grid=(batch, nc),
        in_specs=[pl.BlockSpec((chunk, D_MODEL), blk)] * 6 + [state],
        out_specs=[pl.BlockSpec((chunk, D_MODEL), blk), state],
        out_shape=[jax.ShapeDtypeStruct((batch * seq, D_MODEL), F32),
                   jax.ShapeDtypeStruct(s0.shape, F32)],
        scratch_shapes=[pltpu.VMEM((D_MODEL // LANES, LANES, LANES), F32)]
                       + [pltpu.VMEM((chunk, D_MODEL), F32)] * 4,
        compiler_params=_cparams(("parallel", "arbitrary")),
        name="wkv",
    )(r, lw, k, v, kk, ka, s0)


def _rwkv_out_body(x_ref, y_ref, r_ref, k_ref, v_ref, g_ref, vec_ref, gm_ref, wo_ref, o_ref):
    gm = gm_ref[...]
    vec = vec_ref[...]
    ln_w, ln_b, r_k = vec[0:1, :], vec[1:2, :], vec[2:3, :]
    y = y_ref[...]
    mean = _group_sum(y, gm) * (1.0 / RWKV_HEAD)
    d = y - mean
    var = _group_sum(d * d, gm) * (1.0 / RWKV_HEAD)
    yn = d * lax.rsqrt(var + GN_EPS) * ln_w + ln_b
    yn = yn + _group_sum(r_ref[...] * k_ref[...] * r_k, gm) * v_ref[...]
    o_ref[...] = x_ref[...] + _dot((yn * g_ref[...]).astype(BF16), wo_ref[...])


def _rwkv_out(x, y, r, k, v, g, vec, wo, tm):
    n = x.shape[0]
    row = lambda i: (i, 0)
    const = lambda i: (0, 0)
    gm = _group_ones()
    return pl.pallas_call(
        _rwkv_out_body,
        grid=(n // tm,),
        in_specs=[pl.BlockSpec((tm, D_MODEL), row)] * 6
                 + [pl.BlockSpec(vec.shape, const), pl.BlockSpec(gm.shape, const), pl.BlockSpec(wo.shape, const)],
        out_specs=pl.BlockSpec((tm, D_MODEL), row),
        out_shape=jax.ShapeDtypeStruct((n, D_MODEL), F32),
        compiler_params=_cparams(("parallel",)),
        name="rwkv_out",
    )(x, y, r, k, v, g, vec, gm, wo)


def _pick_tile(n, want):
    t = min(n, want)
    while n % t:
        t //= 2
    return t


def _trunk(x, pos, conv_prev, shift_prev, wkv_prev, wts, sample):
    batch, seq, _ = x.shape
    n = batch * seq
    xf = x.reshape(n, D_MODEL)
    tm = _pick_tile(n, 512)

    tabs = _rope_tables(pos)
    if sample is not None:
        tabs = tuple(jnp.tile(t, (batch, 1)) for t in tabs)
    q, k, v, iq, ikw, cb, u = _proj_even(xf, wts['norm_mix_even'], wts['w_in'], tabs, tm)
    if sample is None:
        attn = _prompt_attention(q, iq, ikw, k, v, batch, seq)
        conv = _conv_prompt(cb, u, wts['conv_w'], seq, _pick_tile(seq, 512))
        conv_state = u.reshape(batch, seq, CONV_DIM)[:, seq - (CONV_W - 1):]
    else:
        cache_k, cache_v, cache_ik, page_table = sample
        tpad = SUBLANES - seq

        def stack(a, heads):
            a = a.reshape(batch, seq, heads, -1).transpose(0, 2, 1, 3)
            a = jnp.pad(a, ((0, 0), (0, 0), (0, tpad), (0, 0)))
            return a.reshape(batch, heads * SUBLANES, a.shape[-1])

        iq_st = stack(iq, N_IDX_HEADS)
        iw_st = stack(ikw[:, IDX_DIM:IDX_DIM + N_IDX_HEADS], N_IDX_HEADS)
        s_past = _sample_scores(page_table, iq_st, iw_st, cache_ik)[:, :seq]
        mask = _sample_select(s_past.reshape(n, -1), iq, ikw, seq)
        mask = jnp.pad(mask.reshape(batch, seq, -1), ((0, 0), (0, tpad), (0, 0)))
        q_st = stack(q, N_Q_HEADS).reshape(batch, N_KV_HEADS, Q_PER_KV * SUBLANES, HEAD_DIM)
        padk = lambda a: jnp.pad(a.reshape(batch, seq, KV_DIM), ((0, 0), (0, PAGE_SIZE - seq), (0, 0)))
        o = _sample_attention(page_table, q_st, mask, padk(k), padk(v), cache_k, cache_v)
        attn = (o.reshape(batch, N_Q_HEADS, SUBLANES, HEAD_DIM)[:, :, :seq].transpose(0, 2, 1, 3)
                .reshape(n, ATTN_DIM).astype(BF16))
        u3 = jnp.concatenate([conv_prev, u.reshape(batch, seq, CONV_DIM)], axis=1)
        conv_state = u3[:, seq:]
        conv_t = _conv_sample(jnp.swapaxes(cb.reshape(batch, seq, CONV_DIM), 0, 1), jnp.swapaxes(u3, 0, 1),
                              wts['conv_w'])
        conv = jnp.swapaxes(conv_t, 0, 1).reshape(n, CONV_DIM)
    x1 = _out_even(xf, attn, conv, wts['w_out_attn'], wts['w_out_conv'], tm)
    x2, xn2 = _mixer(x1, wts['norm_ffn_even'], None, wts['ffn_gate'], wts['ffn_up'], wts['ffn_down'],
                     wts['norm_mix_odd'], tm, D_FF // 2, True)

    xn3 = xn2.reshape(batch, seq, D_MODEL)
    xp = jnp.concatenate([shift_prev[:, None, :], xn3[:, :-1]], axis=1).reshape(n, D_MODEL)
    r, lw, kr, vr, kk, ka, g = _rwkv_proj(xn2, xp, wts['rwkv_mu'], wts['rwkv_vec_in'], wts['rwkv_w_rkv'],
                                          wts['rwkv_w1'], wts['rwkv_w2'], wts['rwkv_a1'], wts['rwkv_a2'],
                                          wts['rwkv_g1'], wts['rwkv_g2'], _pick_tile(n, 256))
    if seq % WKV_CHUNK:
        sp = -(-seq // WKV_CHUNK) * WKV_CHUNK
        padded = [jnp.pad(a.reshape(batch, seq, D_MODEL), ((0, 0), (0, sp - seq), (0, 0)))
                  .reshape(batch * sp, D_MODEL) for a in (r, lw, kr, vr, kk, ka)]
        y, wkv_state = _wkv(*padded, wkv_prev, batch, sp, WKV_CHUNK)
        y = y.reshape(batch, sp, D_MODEL)[:, :seq].reshape(n, D_MODEL)
    else:
        y, wkv_state = _wkv(r, lw, kr, vr, kk, ka, wkv_prev, batch, seq, WKV_CHUNK)
    x3 = _rwkv_out(x2, y, r, kr, vr, g, wts['rwkv_vec_out'], wts['rwkv_w_o'], _pick_tile(n, 256))
    yf = _moe(x3, wts['norm_ffn_odd'], wts['moe_router'], wts['moe_gate'], wts['moe_up'], wts['moe_down'],
              wts['norm_final'], _pick_tile(n, MOE_GROUP), MOE_BLOCK_ROWS, D_FF // 2)

    k4 = k.reshape(1, batch, seq, N_KV_HEADS, HEAD_DIM)
    v4 = v.reshape(1, batch, seq, N_KV_HEADS, HEAD_DIM)
    ik3 = ikw[:, :IDX_DIM].reshape(1, batch, seq, IDX_DIM)
    return (yf.reshape(batch, seq, D_MODEL), k4, v4, ik3, conv_state[None],
            xn3[:, -1][None], wkv_state[None])


def _prepare_weights(norm_mix_even, w_in_even, conv_w, w_out_even, norm_ffn_even, ffn_gate, ffn_up, ffn_down,
                     norm_mix_odd, rwkv_mu, rwkv_w_rkv, rwkv_w0, rwkv_w1, rwkv_w2, rwkv_a0, rwkv_a1, rwkv_a2,
                     rwkv_g1, rwkv_g2, rwkv_k_k, rwkv_k_a, rwkv_r_k, rwkv_ln_w, rwkv_ln_b, rwkv_w_o,
                     norm_ffn_odd, moe_router, moe_gate, moe_up, moe_down, norm_final):
    w_in = w_in_even[0]
    o = np.cumsum((0, ATTN_DIM, KV_DIM, KV_DIM, N_IDX_HEADS * IDX_DIM, IDX_DIM, N_IDX_HEADS,
                   CONV_DIM, CONV_DIM, CONV_DIM))
    pad = jnp.zeros((D_MODEL, LANES - IDX_DIM - N_IDX_HEADS), F32)
    w_in = jnp.concatenate([w_in[:, :o[4]], w_in[:, o[4]:o[6]], pad, w_in[:, o[6]:]], axis=1).astype(BF16)
    row = lambda a: a.reshape(1, -1)
    zeros = jnp.zeros((1, D_MODEL), F32)
    return dict(
        norm_mix_even=row(norm_mix_even[0]), w_in=w_in, conv_w=conv_w[0],
        w_out_attn=w_out_even[0, :ATTN_DIM].astype(BF16), w_out_conv=w_out_even[0, ATTN_DIM:].astype(BF16),
        norm_ffn_even=row(norm_ffn_even[0]),
        ffn_gate=ffn_gate.astype(BF16), ffn_up=ffn_up.astype(BF16), ffn_down=ffn_down.astype(BF16),
        norm_mix_odd=row(norm_mix_odd[0]), rwkv_mu=jnp.concatenate([rwkv_mu[0], zeros, zeros], axis=0),
        rwkv_vec_in=jnp.concatenate([row(rwkv_w0[0]), row(rwkv_a0[0]), row(rwkv_k_k[0]), row(rwkv_k_a[0]),
                                     zeros, zeros, zeros, zeros], axis=0),
        rwkv_w_rkv=rwkv_w_rkv[0].astype(BF16),
        rwkv_w1=rwkv_w1[0].astype(BF16), rwkv_w2=rwkv_w2[0].astype(BF16),
        rwkv_a1=rwkv_a1[0].astype(BF16), rwkv_a2=rwkv_a2[0].astype(BF16),
        rwkv_g1=rwkv_g1[0].astype(BF16), rwkv_g2=rwkv_g2[0].astype(BF16),
        rwkv_vec_out=jnp.concatenate([row(rwkv_ln_w[0]), row(rwkv_ln_b[0]), row(rwkv_r_k[0]),
                                      zeros, zeros, zeros, zeros, zeros], axis=0),
        rwkv_w_o=rwkv_w_o[0].astype(BF16),
        norm_ffn_odd=row(norm_ffn_odd[0]),
        moe_router=jnp.pad(moe_router[0], ((0, 0), (0, LANES - N_EXPERTS))).astype(BF16),
        moe_gate=moe_gate[0].astype(BF16), moe_up=moe_up[0].astype(BF16), moe_down=moe_down[0].astype(BF16),
        norm_final=row(norm_final),
    )


def kernel(x_prompt, x_sample, cache_k, cache_v, cache_idx_k, state_conv, state_shift, state_wkv, page_table, norm_mix_even, w_in_even, conv_w, w_out_even, norm_ffn_even, ffn_gate, ffn_up, ffn_down, norm_mix_odd, rwkv_mu, rwkv_w_rkv, rwkv_w0, rwkv_w1, rwkv_w2, rwkv_a0, rwkv_a1, rwkv_a2, rwkv_g1, rwkv_g2, rwkv_k_k, rwkv_k_a, rwkv_r_k, rwkv_ln_w, rwkv_ln_b, rwkv_w_o, norm_ffn_odd, moe_router, moe_gate, moe_up, moe_down, norm_final):
    assert w_in_even.shape[0] == 1 and rwkv_mu.shape[0] == 1, "one even and one odd layer"
    wts = _prepare_weights(norm_mix_even, w_in_even, conv_w, w_out_even, norm_ffn_even, ffn_gate, ffn_up, ffn_down,
                           norm_mix_odd, rwkv_mu, rwkv_w_rkv, rwkv_w0, rwkv_w1, rwkv_w2, rwkv_a0, rwkv_a1, rwkv_a2,
                           rwkv_g1, rwkv_g2, rwkv_k_k, rwkv_k_a, rwkv_r_k, rwkv_ln_w, rwkv_ln_b, rwkv_w_o,
                           norm_ffn_odd, moe_router, moe_gate, moe_up, moe_down, norm_final)
    b, t = x_prompt.shape[:2]
    bd, tn = x_sample.shape[:2]
    n_pool = cache_k.shape[1]
    past = page_table.shape[1] * PAGE_SIZE
    pos_prompt = jnp.arange(t, dtype=jnp.int32)
    pos_sample = past + jnp.arange(tn, dtype=jnp.int32)
    zeros = lambda *s: jnp.zeros(s, F32)
    out_p = _trunk(x_prompt, pos_prompt, zeros(b, CONV_W - 1, CONV_DIM), zeros(b, D_MODEL),
                   zeros(b, RWKV_HEADS, RWKV_HEAD, RWKV_HEAD), wts, None)
    sample = (cache_k[0].reshape(n_pool, PAGE_SIZE, KV_DIM), cache_v[0].reshape(n_pool, PAGE_SIZE, KV_DIM),
              cache_idx_k[0], page_table)
    out_s = _trunk(x_sample, pos_sample, state_conv[0], state_shift[0], state_wkv[0], wts, sample)
    y_p, k_p, v_p, ik_p, conv_p, shift_p, wkv_p = out_p
    y_s, k_s, v_s, ik_s, conv_s, shift_s, wkv_s = out_s
    return (y_p, y_s, k_p, v_p, ik_p, k_s, v_s, ik_s, conv_p, conv_s, shift_p, shift_s, wkv_p, wkv_s)
```

```python
import functools

import numpy as np
import jax
import jax.numpy as jnp
from jax import lax
from jax.experimental import pallas as pl
from jax.experimental.pallas import tpu as pltpu

F32 = jnp.float32
BF16 = jnp.bfloat16
HI = lax.Precision.HIGHEST

D_MODEL = 1024
PAGE_SIZE = 128
HEAD_DIM = 64
N_Q_HEADS = 8
N_KV_HEADS = 2
Q_PER_KV = N_Q_HEADS // N_KV_HEADS
ROT_DIM = HEAD_DIM // 4
ROPE_THETA = 500000.0
N_IDX_HEADS = 4
IDX_DIM = 64
TOPK_MAX = 256
Q_BLOCK = 128
ATTN_DIM = N_Q_HEADS * HEAD_DIM
KV_DIM = N_KV_HEADS * HEAD_DIM
CONV_DIM = D_MODEL // 2
CONV_W = 3
RWKV_HEAD = 64
RWKV_HEADS = D_MODEL // RWKV_HEAD
GN_EPS = 64e-5
D_FF = 2816
N_EXPERTS = 8
RMS_EPS = 1e-6

LANES = 128
SUBLANES = 8
VMEM_LIMIT = 56 * 1024 * 1024
INT_MIN = -2 ** 31
KEY_NEG_INF = INT_MIN + 0x7FFFFF
PROJ_COLS = ATTN_DIM + KV_DIM + KV_DIM + N_IDX_HEADS * IDX_DIM + LANES + 3 * CONV_DIM
WKV_CHUNK = 64
MOE_GROUP = 1024
MOE_BLOCK_ROWS = 256


def _cparams(sem):
    return pltpu.CompilerParams(dimension_semantics=sem, vmem_limit_bytes=VMEM_LIMIT)


def _rms(x, g):
    return x * lax.rsqrt(jnp.mean(x * x, axis=-1, keepdims=True) + RMS_EPS) * g


def _dot(a, b):
    return jnp.dot(a, b, preferred_element_type=F32)


def _dot_nt(a, b, precision=None):
    return lax.dot_general(a, b, (((1,), (1,)), ((), ())), preferred_element_type=F32, precision=precision)


def _dot_tn(a, b, precision=None):
    return lax.dot_general(a, b, (((0,), (0,)), ((), ())), preferred_element_type=F32, precision=precision)


def _group_sum(x, gmat):
    outs = []
    for c in range(x.shape[1] // LANES):
        xc = x[:, c * LANES:(c + 1) * LANES]
        hi = xc.astype(BF16)
        lo = (xc - hi.astype(F32)).astype(BF16)
        outs.append(_dot(hi, gmat) + _dot(lo, gmat))
    return jnp.concatenate(outs, axis=1)


def _rope_chunk(xc, rc, rp, rm):
    return xc * rc + pltpu.roll(xc, 8, 1) * rp + pltpu.roll(xc, LANES - 8, 1) * rm


def _proj_even_body(x_ref, g_ref, w_ref, rc_ref, rp_ref, rm_ref,
                    q_ref, k_ref, v_ref, iq_ref, ikw_ref, cb_ref, u_ref):
    xn = _rms(x_ref[...], g_ref[...]).astype(BF16)
    h = _dot(xn, w_ref[...])
    rc, rp, rm = rc_ref[...], rp_ref[...], rm_ref[...]
    col = 0
    for c in range(ATTN_DIM // LANES):
        q_ref[:, c * LANES:(c + 1) * LANES] = _rope_chunk(h[:, col:col + LANES], rc, rp, rm).astype(BF16)
        col += LANES
    k_ref[...] = _rope_chunk(h[:, col:col + LANES], rc, rp, rm)
    col += LANES
    v_ref[...] = h[:, col:col + LANES]
    col += LANES
    for c in range(N_IDX_HEADS * IDX_DIM // LANES):
        iq_ref[:, c * LANES:(c + 1) * LANES] = _rope_chunk(h[:, col:col + LANES], rc, rp, rm).astype(BF16)
        col += LANES
    ikw = h[:, col:col + LANES]
    lane = lax.broadcasted_iota(jnp.int32, ikw.shape, 1)
    ikw_ref[...] = jnp.where(lane < IDX_DIM, _rope_chunk(ikw, rc, rp, rm), ikw)
    col += LANES
    cb_ref[...] = h[:, col:col + CONV_DIM]
    col += CONV_DIM
    u_ref[...] = h[:, col:col + CONV_DIM] * h[:, col + CONV_DIM:col + 2 * CONV_DIM]


def _proj_even(x, g, w, tabs, tm):
    n = x.shape[0]
    nt = tabs[0].shape[0] // tm
    row = lambda i: (i, 0)
    const = lambda i: (0, 0)
    tab = lambda i: (i % nt, 0)
    widths = (ATTN_DIM, KV_DIM, KV_DIM, N_IDX_HEADS * IDX_DIM, LANES, CONV_DIM, CONV_DIM)
    dtypes = (BF16, F32, F32, BF16, F32, F32, F32)
    return pl.pallas_call(
        _proj_even_body,
        grid=(n // tm,),
        in_specs=[pl.BlockSpec((tm, D_MODEL), row), pl.BlockSpec((1, D_MODEL), const),
                  pl.BlockSpec((D_MODEL, PROJ_COLS), const)] + [pl.BlockSpec((tm, LANES), tab)] * 3,
        out_specs=[pl.BlockSpec((tm, wd), row) for wd in widths],
        out_shape=[jax.ShapeDtypeStruct((n, wd), dt) for wd, dt in zip(widths, dtypes)],
        compiler_params=_cparams(("parallel",)),
        name="proj_even",
    )(x, g, w, *tabs)


def _rope_tables(pos):
    half = ROT_DIM // 2
    inv_freq = ROPE_THETA ** (-jnp.arange(half, dtype=F32) / half)
    ang = pos.astype(F32)[:, None] * inv_freq[None, :]
    cos, sin = jnp.cos(ang), jnp.sin(ang)
    t = pos.shape[0]
    pad = jnp.zeros((t, HEAD_DIM - ROT_DIM), F32)
    zero = jnp.zeros((t, half), F32)
    rc = jnp.concatenate([cos, cos, pad + 1.0], axis=1)
    rp = jnp.concatenate([zero, sin, pad], axis=1)
    rm = jnp.concatenate([-sin, zero, pad], axis=1)
    return tuple(jnp.tile(a, (1, LANES // HEAD_DIM)) for a in (rc, rp, rm))


def _conv_prompt_body(cb_ref, u_ref, up_ref, w_ref, y_ref, *, tiles_per_seq):
    u = u_ref[...]
    first = pl.program_id(0) % tiles_per_seq == 0
    prev = jnp.where(first, 0.0, up_ref[...])
    p1, p2 = prev[SUBLANES - 1:SUBLANES, :], prev[SUBLANES - 2:SUBLANES - 1, :]
    r = lax.broadcasted_iota(jnp.int32, u.shape, 0)
    u1 = jnp.where(r == 0, p1, pltpu.roll(u, 1, 0))
    u2 = jnp.where(r == 0, p2, jnp.where(r == 1, p1, pltpu.roll(u, 2, 0)))
    w = w_ref[...]
    y_ref[...] = cb_ref[...] * (w[0:1, :] * u2 + w[1:2, :] * u1 + w[2:3, :] * u)


def _conv_prompt(cb, u, w, seq, tc):
    n = u.shape[0]
    row = lambda i: (i, 0)
    prev = lambda i: (jnp.maximum(i * (tc // SUBLANES) - 1, 0), 0)
    return pl.pallas_call(
        functools.partial(_conv_prompt_body, tiles_per_seq=seq // tc),
        grid=(n // tc,),
        in_specs=[pl.BlockSpec((tc, CONV_DIM), row), pl.BlockSpec((tc, CONV_DIM), row),
                  pl.BlockSpec((SUBLANES, CONV_DIM), prev), pl.BlockSpec((CONV_W, CONV_DIM), lambda i: (0, 0))],
        out_specs=pl.BlockSpec((tc, CONV_DIM), row),
        out_shape=jax.ShapeDtypeStruct((n, CONV_DIM), F32),
        compiler_params=_cparams(("parallel",)),
        name="conv_prompt",
    )(cb, u, u, w)


def _conv_sample_body(cb_ref, ue_ref, w_ref, y_ref):
    w = w_ref[...]
    for t in range(y_ref.shape[0]):
        acc = w[0:1, :] * ue_ref[t] + w[1:2, :] * ue_ref[t + 1] + w[2:3, :] * ue_ref[t + 2]
        y_ref[t] = cb_ref[t] * acc


def _conv_sample(cb_t, ue_t, w):
    return pl.pallas_call(
        _conv_sample_body,
        out_shape=jax.ShapeDtypeStruct(cb_t.shape, F32),
        name="conv_sample",
    )(cb_t, ue_t, w)


def _topk_select(scores, n_sel, tri):
    rows, width = scores.shape
    sc = scores

    def key_to_float(key):
        return lax.bitcast_convert_type(key ^ ((key >> 31) & 0x7FFFFFFF), F32)

    def count_ge(key):
        cnt = jnp.sum(jnp.where(sc >= key_to_float(key), 1.0, 0.0), axis=1, keepdims=True)
        return jnp.where(key <= KEY_NEG_INF, float(width), cnt)

    thr = jnp.where(count_ge(jnp.zeros((rows, 1), jnp.int32)) >= n_sel, 0, INT_MIN).astype(jnp.int32)

    def body(i, thr):
        cand = thr + lax.shift_left(jnp.int32(1), 30 - i)
        return jnp.where(count_ge(cand) >= n_sel, cand, thr)

    thr = lax.fori_loop(0, 31, body, thr)
    thr_f = key_to_float(thr)
    gt = sc > thr_f
    eq = sc == thr_f
    need = n_sel - jnp.sum(jnp.where(gt, 1.0, 0.0), axis=1, keepdims=True)
    off = jnp.zeros((rows, 1), F32)
    parts = []
    for c in range(width // LANES):
        eqc = jnp.where(eq[:, c * LANES:(c + 1) * LANES], 1.0, 0.0)
        incl = _dot(eqc.astype(BF16), tri)
        parts.append((incl - eqc + off) < need)
        off = off + incl[:, LANES - 1:LANES]
    tie = jnp.concatenate(parts, axis=1)
    return (gt | (eq & tie)) & (sc > -jnp.inf)


def _tri_incl():
    i = np.arange(LANES)
    return jnp.asarray((i[:, None] <= i[None, :]).astype(np.float32), BF16)


def _stack_heads(x, n):
    return jnp.concatenate([x[:, h * HEAD_DIM:(h + 1) * HEAD_DIM] for h in range(n)], axis=0)


def _index_scores(iq_st, iw_st, ik_b, keys_on_lanes=False):
    rows = iq_st.shape[0] // N_IDX_HEADS
    s = _dot(iq_st, ik_b) if keys_on_lanes else _dot_nt(iq_st, ik_b)
    term = jnp.maximum(s, 0.0) * (IDX_DIM ** -0.5) * (iw_st * (N_IDX_HEADS ** -0.5))
    acc = term[0:rows, :]
    for h in range(1, N_IDX_HEADS):
        acc = acc + term[h * rows:(h + 1) * rows, :]
    return acc


def _group_logits(qs, kg, sel, keys_on_lanes=False):
    s = (_dot(qs, kg) if keys_on_lanes else _dot_nt(qs, kg)) * (HEAD_DIM ** -0.5)
    return jnp.where(jnp.concatenate([sel] * Q_PER_KV, axis=0), s, -jnp.inf)


def _masked_attention(q, kgs, vgs, sel):
    rows = q.shape[0]
    outs = []
    for g in range(N_KV_HEADS):
        qs = _stack_heads(q[:, g * Q_PER_KV * HEAD_DIM:(g + 1) * Q_PER_KV * HEAD_DIM], Q_PER_KV)
        s = _group_logits(qs, kgs[g], sel)
        p = jnp.exp(s - jnp.max(s, axis=1, keepdims=True))
        o = _dot(p.astype(BF16), vgs[g]) / jnp.sum(p, axis=1, keepdims=True)
        outs.extend(o[r * rows:(r + 1) * rows, :] for r in range(Q_PER_KV))
    return jnp.concatenate(outs, axis=1)


def _prompt_attn_body(q_ref, iq_ref, iwq_ref, k_ref, v_ref, ikw_ref, tri_ref, o_ref,
                      kb_scr, vb_scr, ikb_scr, *, n_sel, widths):
    i = pl.program_id(1)

    @pl.when(i == 0)
    def _():
        for g in range(N_KV_HEADS):
            kb_scr[g] = k_ref[:, g * HEAD_DIM:(g + 1) * HEAD_DIM].astype(BF16)
            vb_scr[g] = v_ref[:, g * HEAD_DIM:(g + 1) * HEAD_DIM].astype(BF16)
        ikb_scr[...] = ikw_ref[:, :IDX_DIM].astype(BF16)

    def run(width):
        iw = iwq_ref[...]
        iw_st = jnp.concatenate([iw[:, IDX_DIM + h:IDX_DIM + h + 1] for h in range(N_IDX_HEADS)], axis=0)
        scores = _index_scores(_stack_heads(iq_ref[...], N_IDX_HEADS), iw_st, ikb_scr[0:width, :])
        tq = i * Q_BLOCK + lax.broadcasted_iota(jnp.int32, scores.shape, 0)
        key_pos = lax.broadcasted_iota(jnp.int32, scores.shape, 1)
        scores = jnp.where(key_pos <= tq, scores, -jnp.inf)
        sel = _topk_select(scores, n_sel, tri_ref[...])
        kgs = [kb_scr[g, 0:width, :] for g in range(N_KV_HEADS)]
        vgs = [vb_scr[g, 0:width, :] for g in range(N_KV_HEADS)]
        o_ref[...] = _masked_attention(q_ref[...], kgs, vgs, sel).astype(o_ref.dtype)

    lo = 0
    for width in widths:
        hi = width // Q_BLOCK
        pl.when((i >= lo) & (i < hi))(functools.partial(run, width))
        lo = hi


def _prompt_attention(q, iq, ikw, k, v, batch, seq):
    n_sel = min(TOPK_MAX, seq // 4)
    nqb = seq // Q_BLOCK
    n_widths = min(8, nqb)
    widths = tuple(seq * (j + 1) // n_widths for j in range(n_widths))
    blk = lambda b, i: (b * nqb + i, 0)
    full = lambda b, i: (b, 0)
    return pl.pallas_call(
        functools.partial(_prompt_attn_body, n_sel=n_sel, widths=widths),
        grid=(batch, nqb),
        in_specs=[pl.BlockSpec((Q_BLOCK, ATTN_DIM), blk), pl.BlockSpec((Q_BLOCK, N_IDX_HEADS * IDX_DIM), blk),
                  pl.BlockSpec((Q_BLOCK, LANES), blk),
                  pl.BlockSpec((seq, KV_DIM), full), pl.BlockSpec((seq, KV_DIM), full),
                  pl.BlockSpec((seq, LANES), full), pl.BlockSpec((LANES, LANES), lambda b, i: (0, 0))],
        out_specs=pl.BlockSpec((Q_BLOCK, ATTN_DIM), blk),
        out_shape=jax.ShapeDtypeStruct((batch * seq, ATTN_DIM), BF16),
        scratch_shapes=[pltpu.VMEM((N_KV_HEADS, seq, HEAD_DIM), BF16), pltpu.VMEM((N_KV_HEADS, seq, HEAD_DIM), BF16),
                        pltpu.VMEM((seq, IDX_DIM), BF16)],
        compiler_params=_cparams(("parallel", "arbitrary")),
        name="prompt_attention",
    )(q, iq, ikw, k, v, ikw, _tri_incl())


PAGES_PER_STEP = 16


def _sample_scores_body(pt_ref, iq_ref, iw_ref, *rest):
    page_refs, s_ref = rest[:PAGES_PER_STEP], rest[PAGES_PER_STEP]
    for j, pr in enumerate(page_refs):
        s_ref[0, :, j * PAGE_SIZE:(j + 1) * PAGE_SIZE] = _index_scores(iq_ref[0], iw_ref[0], pr[0].astype(BF16), True)


def _sample_scores(page_table, iq_st, iw_st, cache_ik):
    bd, n_pages = page_table.shape
    rows = iq_st.shape[1] // N_IDX_HEADS
    steps = n_pages // PAGES_PER_STEP
    per_b = lambda b, p, pt: (b, 0, 0)
    page = lambda j: (lambda b, p, pt: (pt[b, p * PAGES_PER_STEP + j], 0, 0))
    return pl.pallas_call(
        _sample_scores_body,
        grid_spec=pltpu.PrefetchScalarGridSpec(
            num_scalar_prefetch=1,
            grid=(bd, steps),
            in_specs=[pl.BlockSpec((1,) + iq_st.shape[1:], per_b), pl.BlockSpec((1,) + iw_st.shape[1:], per_b)]
                     + [pl.BlockSpec((1, IDX_DIM, PAGE_SIZE), page(j)) for j in range(PAGES_PER_STEP)],
            out_specs=pl.BlockSpec((1, rows, PAGES_PER_STEP * PAGE_SIZE), lambda b, p, pt: (b, 0, p)),
        ),
        out_shape=jax.ShapeDtypeStruct((bd, rows, n_pages * PAGE_SIZE), F32),
        compiler_params=_cparams(("parallel", "arbitrary")),
        name="sample_scores",
    )(page_table, iq_st, iw_st, *([cache_ik] * PAGES_PER_STEP))


def _sample_select_body(sp_ref, iq_ref, iwq_ref, ikn_ref, tri_ref, m_ref, *, n_sel, tn):
    rows = sp_ref.shape[0]
    iw = iwq_ref[...]
    iw_st = jnp.concatenate([iw[:, IDX_DIM + h:IDX_DIM + h + 1] for h in range(N_IDX_HEADS)], axis=0)
    s_new = _index_scores(_stack_heads(iq_ref[...], N_IDX_HEADS), iw_st, ikn_ref[...].astype(BF16))
    r = lax.broadcasted_iota(jnp.int32, s_new.shape, 0)
    c = lax.broadcasted_iota(jnp.int32, s_new.shape, 1)
    same = (r // tn == c // tn) & (c <= r)
    fold = jnp.where((lax.broadcasted_iota(jnp.int32, (rows, LANES), 0) % tn)
                     == lax.broadcasted_iota(jnp.int32, (rows, LANES), 1), 1.0, 0.0)
    picked = jnp.where(same, s_new, 0.0)
    hi = picked.astype(BF16)
    mid = (picked - hi.astype(F32)).astype(BF16)
    lo = (picked - hi.astype(F32) - mid.astype(F32)).astype(BF16)
    fb = fold.astype(BF16)
    new_chunk = _dot(hi, fb) + _dot(mid, fb) + _dot(lo, fb)
    lane = lax.broadcasted_iota(jnp.int32, (rows, LANES), 1)
    tpos = lax.broadcasted_iota(jnp.int32, (rows, LANES), 0) % tn
    new_chunk = jnp.where(lane <= tpos, new_chunk, -jnp.inf)
    scores = jnp.concatenate([sp_ref[...], new_chunk], axis=1)
    sel = _topk_select(scores, n_sel, tri_ref[...])
    m_ref[...] = jnp.where(sel, 1.0, 0.0)


def _sample_select(s_past, iq, ikw, tn):
    rows, past = s_past.shape
    n_sel = min(TOPK_MAX, (past + tn) // 4)
    return pl.pallas_call(
        functools.partial(_sample_select_body, n_sel=n_sel, tn=tn),
        out_shape=jax.ShapeDtypeStruct((rows, past + LANES), F32),
        compiler_params=pltpu.CompilerParams(vmem_limit_bytes=VMEM_LIMIT),
        name="sample_select",
    )(s_past, iq, ikw, ikw[:, :IDX_DIM], _tri_incl())


def _sample_attn_body(pt_ref, q_ref, m_ref, kn_ref, vn_ref, mn_ref, *rest):
    k_refs, v_refs = rest[:PAGES_PER_STEP], rest[PAGES_PER_STEP:2 * PAGES_PER_STEP]
    o_ref, m_scr, l_scr, acc_scr = rest[2 * PAGES_PER_STEP:]
    p = pl.program_id(1)

    @pl.when(p == 0)
    def _():
        m_scr[...] = jnp.full(m_scr.shape, -jnp.inf, F32)
        l_scr[...] = jnp.zeros(l_scr.shape, F32)
        acc_scr[...] = jnp.zeros(acc_scr.shape, F32)

    def update(kb, vb, sel):
        for g in range(N_KV_HEADS):
            s = _group_logits(q_ref[0, g], kb[g * HEAD_DIM:(g + 1) * HEAD_DIM, :], sel, True)
            m_old = m_scr[g]
            m_new = jnp.maximum(m_old, jnp.max(s, axis=1, keepdims=True))
            m_safe = jnp.where(m_new == -jnp.inf, 0.0, m_new)
            alpha = jnp.exp(m_old - m_safe)
            pe = jnp.exp(s - m_safe)
            l_scr[g] = alpha * l_scr[g] + jnp.sum(pe, axis=1, keepdims=True)
            acc_scr[g] = alpha * acc_scr[g] + _dot_nt(pe.astype(BF16), vb[g * HEAD_DIM:(g + 1) * HEAD_DIM, :])
            m_scr[g] = m_new

    kb = jnp.concatenate([r[0].astype(BF16) for r in k_refs], axis=1)
    vb = jnp.concatenate([r[0].astype(BF16) for r in v_refs], axis=1)
    update(kb, vb, m_ref[0] > 0.5)

    @pl.when(p == pl.num_programs(1) - 1)
    def _():
        update(kn_ref[0].astype(BF16), vn_ref[0].astype(BF16), mn_ref[0] > 0.5)
        for g in range(N_KV_HEADS):
            l = l_scr[g]
            o_ref[0, g] = acc_scr[g] / jnp.where(l == 0.0, 1.0, l)


def _sample_attention(page_table, q_st, mask, k_new, v_new, cache_k, cache_v):
    bd, n_pages = page_table.shape
    steps = n_pages // PAGES_PER_STEP
    per_b = lambda b, p, pt: (b, 0, 0)
    per_b4 = lambda b, p, pt: (b, 0, 0, 0)
    page = lambda j: (lambda b, p, pt: (pt[b, p * PAGES_PER_STEP + j], 0, 0))
    width = PAGES_PER_STEP * PAGE_SIZE
    rows = q_st.shape[2]
    mrows = mask.shape[1]
    return pl.pallas_call(
        _sample_attn_body,
        grid_spec=pltpu.PrefetchScalarGridSpec(
            num_scalar_prefetch=1,
            grid=(bd, steps),
            in_specs=[pl.BlockSpec((1,) + q_st.shape[1:], per_b4),
                      pl.BlockSpec((1, mrows, width), lambda b, p, pt: (b, 0, p)),
                      pl.BlockSpec((1, KV_DIM, PAGE_SIZE), per_b), pl.BlockSpec((1, KV_DIM, PAGE_SIZE), per_b),
                      pl.BlockSpec((1, mrows, LANES), lambda b, p, pt: (b, 0, n_pages))]
                     + [pl.BlockSpec((1, KV_DIM, PAGE_SIZE), page(j)) for j in range(PAGES_PER_STEP)] * 2,
            out_specs=pl.BlockSpec((1, N_KV_HEADS, rows, HEAD_DIM), per_b4),
            scratch_shapes=[pltpu.VMEM((N_KV_HEADS, rows, 1), F32), pltpu.VMEM((N_KV_HEADS, rows, 1), F32),
                            pltpu.VMEM((N_KV_HEADS, rows, HEAD_DIM), F32)],
        ),
        out_shape=jax.ShapeDtypeStruct((bd, N_KV_HEADS, rows, HEAD_DIM), F32),
        compiler_params=_cparams(("parallel", "arbitrary")),
        name="sample_attention",
    )(page_table, q_st, mask, k_new, v_new, mask, *([cache_k] * PAGES_PER_STEP), *([cache_v] * PAGES_PER_STEP))


def _out_even_body(x_ref, a_ref, c_ref, wa_ref, wc_ref, o_ref):
    o_ref[...] = x_ref[...] + _dot(a_ref[...], wa_ref[...]) + _dot(c_ref[...].astype(BF16), wc_ref[...])


def _out_even(x, attn, conv, wa, wc, tm):
    n = x.shape[0]
    row = lambda i: (i, 0)
    const = lambda i: (0, 0)
    return pl.pallas_call(
        _out_even_body,
        grid=(n // tm,),
        in_specs=[pl.BlockSpec((tm, D_MODEL), row), pl.BlockSpec((tm, ATTN_DIM), row),
                  pl.BlockSpec((tm, CONV_DIM), row), pl.BlockSpec((ATTN_DIM, D_MODEL), const),
                  pl.BlockSpec((CONV_DIM, D_MODEL), const)],
        out_specs=pl.BlockSpec((tm, D_MODEL), row),
        out_shape=jax.ShapeDtypeStruct((n, D_MODEL), F32),
        compiler_params=_cparams(("parallel",)),
        name="out_even",
    )(x, attn, conv, wa, wc)


def _mixer_body(*refs, n_exp, emit_sum):
    if n_exp > 1:
        x_ref, g_ref, wr_ref, wg_ref, wu_ref, wd_ref, g2_ref = refs[:7]
        outs = refs[7:]
    else:
        x_ref, g_ref, wg_ref, wu_ref, wd_ref, g2_ref = refs[:6]
        outs = refs[6:]
    n_out = 2 if emit_sum else 1
    out_refs, (xn_scr, acc_scr, cw_scr) = outs[:n_out], outs[n_out:]
    e, f = pl.program_id(1), pl.program_id(2)

    @pl.when((e == 0) & (f == 0))
    def _():
        xn = _rms(x_ref[...], g_ref[...]).astype(BF16)
        xn_scr[...] = xn
        acc_scr[...] = jnp.zeros(acc_scr.shape, F32)
        if n_exp > 1:
            logits = _dot(xn, wr_ref[...])
            lane = lax.broadcasted_iota(jnp.int32, logits.shape, 1).astype(F32)
            lg = jnp.where(lane < n_exp, logits, -jnp.inf)
            m1 = jnp.max(lg, axis=1, keepdims=True)
            i1 = jnp.min(jnp.where(lg == m1, lane, float(LANES)), axis=1, keepdims=True)
            lg2 = jnp.where(lane == i1, -jnp.inf, lg)
            m2 = jnp.max(lg2, axis=1, keepdims=True)
            i2 = jnp.min(jnp.where(lg2 == m2, lane, float(LANES)), axis=1, keepdims=True)
            e2 = jnp.exp(m2 - m1)
            cw_scr[...] = jnp.where(lane == i1, 1.0 / (1.0 + e2), 0.0) + jnp.where(lane == i2, e2 / (1.0 + e2), 0.0)

    xn = xn_scr[...]
    gate = _dot(xn, wg_ref[0])
    h = gate * jax.nn.sigmoid(gate) * _dot(xn, wu_ref[0])
    if n_exp > 1:
        lane = lax.broadcasted_iota(jnp.int32, cw_scr.shape, 1)
        h = h * jnp.sum(jnp.where(lane == e, cw_scr[...], 0.0), axis=1, keepdims=True)
    acc_scr[...] += _dot(h.astype(BF16), wd_ref[0])

    @pl.when((e == pl.num_programs(1) - 1) & (f == pl.num_programs(2) - 1))
    def _():
        y = x_ref[...] + acc_scr[...]
        if emit_sum:
            out_refs[0][...] = y
        out_refs[-1][...] = _rms(y, g2_ref[...])


def _mixer(x, g, wr, wg, wu, wd, g2, tm, tf, emit_sum):
    n = x.shape[0]
    n_exp = wg.shape[0]
    row = lambda i, e, f: (i, 0)
    const = lambda i, e, f: (0, 0)
    in_specs = [pl.BlockSpec((tm, D_MODEL), row), pl.BlockSpec((1, D_MODEL), const)]
    args = [x, g]
    if n_exp > 1:
        in_specs.append(pl.BlockSpec((D_MODEL, LANES), const))
        args.append(wr)
    in_specs += [pl.BlockSpec((1, D_MODEL, tf), lambda i, e, f: (e, 0, f)),
                 pl.BlockSpec((1, D_MODEL, tf), lambda i, e, f: (e, 0, f)),
                 pl.BlockSpec((1, tf, D_MODEL), lambda i, e, f: (e, f, 0)),
                 pl.BlockSpec((1, D_MODEL), const)]
    args += [wg, wu, wd, g2]
    n_out = 2 if emit_sum else 1
    return pl.pallas_call(
        functools.partial(_mixer_body, n_exp=n_exp, emit_sum=emit_sum),
        grid=(n // tm, n_exp, D_FF // tf),
        in_specs=in_specs,
        out_specs=[pl.BlockSpec((tm, D_MODEL), row)] * n_out,
        out_shape=[jax.ShapeDtypeStruct((n, D_MODEL), F32)] * n_out,
        scratch_shapes=[pltpu.VMEM((tm, D_MODEL), BF16), pltpu.VMEM((tm, D_MODEL), F32),
                        pltpu.VMEM((tm, LANES), F32)],
        compiler_params=_cparams(("parallel", "arbitrary", "arbitrary")),
        name="mixer_moe" if n_exp > 1 else "mixer_ffn",
    )(*args)


def _moe_route_body(x_ref, g_ref, wr_ref, tri_ref, etri_ref,
                    xn_ref, slot_ref, slot_t_ref, comb_ref, meta_ref, *, n_exp, rows_blk):
    xn = _rms(x_ref[...], g_ref[...]).astype(BF16)
    xn_ref[...] = xn
    logits = _dot(xn, wr_ref[...])
    lane = lax.broadcasted_iota(jnp.int32, logits.shape, 1).astype(F32)
    lg = jnp.where(lane < n_exp, logits, -jnp.inf)
    m1 = jnp.max(lg, axis=1, keepdims=True)
    i1 = jnp.min(jnp.where(lg == m1, lane, float(LANES)), axis=1, keepdims=True)
    lg2 = jnp.where(lane == i1, -jnp.inf, lg)
    m2 = jnp.max(lg2, axis=1, keepdims=True)
    i2 = jnp.min(jnp.where(lg2 == m2, lane, float(LANES)), axis=1, keepdims=True)
    e2 = jnp.exp(m2 - m1)
    comb_ref[...] = jnp.where(lane == i1, 1.0 / (1.0 + e2), 0.0) + jnp.where(lane == i2, e2 / (1.0 + e2), 0.0)
    member = jnp.where((lane == i1) | (lane == i2), 1.0, 0.0)
    n_tok = member.shape[0]
    pos = _dot(tri_ref[...], member.astype(BF16))
    cnt = pos[n_tok - 1:n_tok, :] + member[n_tok - 1:n_tok, :]
    padded = jnp.floor((cnt + (rows_blk - 1)) * (1.0 / rows_blk)) * rows_blk
    start = _dot(jnp.broadcast_to(padded, (SUBLANES, LANES)).astype(BF16), etri_ref[...])[0:1, :]
    slot = jnp.where(member > 0.0, start + pos, -1.0)
    slot_ref[...] = slot
    slot_t_ref[0] = jnp.transpose(slot)[0:SUBLANES, :]
    ends = start + padded
    first_row = lax.broadcasted_iota(jnp.int32, (1, LANES), 1).astype(F32) * rows_blk
    blk_exp = jnp.zeros((1, LANES), F32)
    for e in range(n_exp - 1):
        blk_exp = blk_exp + jnp.where(ends[:, e:e + 1] <= first_row, 1.0, 0.0)
    n_blk = ends[:, n_exp - 1:n_exp] * (1.0 / rows_blk)
    row = lax.broadcasted_iota(jnp.int32, (SUBLANES, LANES), 0)
    meta = jnp.where(row == 0, blk_exp, jnp.where(row == 1, n_blk, 0.0))
    meta_ref[0] = meta.astype(jnp.int32)


def _moe_experts_body(be_ref, nb_ref, xn_ref, slot_ref, slot_t_ref, comb_ref, x_ref, wg_ref, wu_ref, wd_ref,
                      g2_ref, o_ref, xs_scr, ys_scr, *, rows_blk, nf):
    g, f, j = pl.program_id(0), pl.program_id(1), pl.program_id(2)
    nj = pl.num_programs(2)
    e = be_ref[g, j]
    used = j < nb_ref[g]
    n_tok = xn_ref.shape[0]
    rows = pl.ds(pl.multiple_of(j * rows_blk, rows_blk), rows_blk)
    first = j * rows_blk

    @pl.when((f == 0) & (j == 0))
    def _():
        o_ref[...] = x_ref[...]

    @pl.when(used & (f == 0))
    def _():
        srow = slot_t_ref[0, pl.ds(e, 1), :]
        want = (first + lax.broadcasted_iota(jnp.int32, (rows_blk, n_tok), 0)).astype(F32)
        onehot = jnp.where(srow == want, 1.0, 0.0).astype(BF16)
        xs_scr[rows, :] = _dot(onehot, xn_ref[...]).astype(BF16)

    @pl.when(used)
    def _():
        xs = xs_scr[rows, :]
        gate = _dot(xs, wg_ref[0])
        h = gate * jax.nn.sigmoid(gate) * _dot(xs, wu_ref[0])
        part = _dot(h.astype(BF16), wd_ref[0])

        if nf > 1:
            @pl.when(f == 0)
            def _():
                ys_scr[rows, :] = part.astype(BF16)

        if nf > 2:
            @pl.when((f > 0) & (f < nf - 1))
            def _():
                ys_scr[rows, :] = (ys_scr[rows, :].astype(F32) + part).astype(BF16)

        def scatter(total):
            lane = lax.broadcasted_iota(jnp.int32, (n_tok, LANES), 1)
            scol = jnp.sum(jnp.where(lane == e, slot_ref[...], 0.0), axis=1, keepdims=True)
            ccol = jnp.sum(jnp.where(lane == e, comb_ref[...], 0.0), axis=1, keepdims=True)
            want = (first + lax.broadcasted_iota(jnp.int32, (n_tok, rows_blk), 1)).astype(F32)
            onehot = jnp.where(scol == want, 1.0, 0.0).astype(BF16)
            tb = total.astype(BF16)
            for c in range(D_MODEL // (2 * LANES)):
                cols = slice(c * 2 * LANES, (c + 1) * 2 * LANES)
                o_ref[:, cols] += ccol * _dot(onehot, tb[:, cols])

        @pl.when(f == nf - 1)
        def _():
            scatter(part + ys_scr[rows, :].astype(F32) if nf > 1 else part)

    @pl.when((f == nf - 1) & (j == nj - 1))
    def _():
        o_ref[...] = _rms(o_ref[...], g2_ref[...])


def _moe(x, g, wr, wg, wu, wd, g2, tg, rows_blk, tf):
    n = x.shape[0]
    n_exp = wg.shape[0]
    n_groups = n // tg
    n_blk = 2 * tg // rows_blk + n_exp
    i = np.arange(tg)
    tri = jnp.asarray((i[None, :] < i[:, None]).astype(np.float32), BF16)
    i = np.arange(LANES)
    etri = jnp.asarray((i[:, None] < i[None, :]).astype(np.float32), BF16)
    row = lambda i: (i, 0)
    const = lambda i: (0, 0)
    xn, slot, slot_t, comb, meta = pl.pallas_call(
        functools.partial(_moe_route_body, n_exp=n_exp, rows_blk=rows_blk),
        grid=(n_groups,),
        in_specs=[pl.BlockSpec((tg, D_MODEL), row), pl.BlockSpec((1, D_MODEL), const),
                  pl.BlockSpec((D_MODEL, LANES), const), pl.BlockSpec((tg, tg), const),
                  pl.BlockSpec((LANES, LANES), const)],
        out_specs=[pl.BlockSpec((tg, D_MODEL), row), pl.BlockSpec((tg, LANES), row),
                   pl.BlockSpec((1, SUBLANES, tg), lambda i: (i, 0, 0)), pl.BlockSpec((tg, LANES), row),
                   pl.BlockSpec((1, SUBLANES, LANES), lambda i: (i, 0, 0))],
        out_shape=[jax.ShapeDtypeStruct((n, D_MODEL), BF16), jax.ShapeDtypeStruct((n, LANES), F32),
                   jax.ShapeDtypeStruct((n_groups, SUBLANES, tg), F32), jax.ShapeDtypeStruct((n, LANES), F32),
                   jax.ShapeDtypeStruct((n_groups, SUBLANES, LANES), jnp.int32)],
        compiler_params=_cparams(("parallel",)),
        name="moe_route",
    )(x, g, wr, tri, etri)
    blk_exp = meta[:, 0, :]
    blk_cnt = meta[:, 1, 0]
    per_g = lambda g, f, j, be, nb: (g, 0)
    once = pl.Buffered(1)
    return pl.pallas_call(
        functools.partial(_moe_experts_body, rows_blk=rows_blk, nf=D_FF // tf),
        grid_spec=pltpu.PrefetchScalarGridSpec(
            num_scalar_prefetch=2,
            grid=(n_groups, D_FF // tf, n_blk),
            in_specs=[pl.BlockSpec((tg, D_MODEL), per_g, pipeline_mode=once),
                      pl.BlockSpec((tg, LANES), per_g, pipeline_mode=once),
                      pl.BlockSpec((1, SUBLANES, tg), lambda g, f, j, be, nb: (g, 0, 0), pipeline_mode=once),
                      pl.BlockSpec((tg, LANES), per_g, pipeline_mode=once),
                      pl.BlockSpec((tg, D_MODEL), per_g, pipeline_mode=once),
                      pl.BlockSpec((1, D_MODEL, tf), lambda g, f, j, be, nb: (be[g, j], 0, f)),
                      pl.BlockSpec((1, D_MODEL, tf), lambda g, f, j, be, nb: (be[g, j], 0, f)),
                      pl.BlockSpec((1, tf, D_MODEL), lambda g, f, j, be, nb: (be[g, j], f, 0)),
                      pl.BlockSpec((1, D_MODEL), lambda g, f, j, be, nb: (0, 0))],
            out_specs=pl.BlockSpec((tg, D_MODEL), per_g, pipeline_mode=once),
            scratch_shapes=[pltpu.VMEM((n_blk * rows_blk, D_MODEL), BF16),
                            pltpu.VMEM((n_blk * rows_blk, D_MODEL), BF16)],
        ),
        out_shape=jax.ShapeDtypeStruct((n, D_MODEL), F32),
        compiler_params=_cparams(("parallel", "arbitrary", "arbitrary")),
        name="moe_experts",
    )(blk_exp, blk_cnt, xn, slot, slot_t, comb, x, wg, wu, wd, g2)


def _group_ones():
    i = np.arange(LANES) // RWKV_HEAD
    return jnp.asarray((i[:, None] == i[None, :]).astype(np.float32), BF16)


def _rwkv_proj_body(xn_ref, xp_ref, mu_ref, vec_ref, wrkv_ref, w1_ref, w2_ref, a1_ref, a2_ref, g1_ref, g2_ref,
                    gm_ref, r_ref, lw_ref, k_ref, v_ref, kk_ref, ka_ref, g_ref):
    xn = xn_ref[...]
    xx = xp_ref[...] - xn
    mu = mu_ref[...]
    mix = lambda i: (xn + xx * mu[i:i + 1, :]).astype(BF16)
    vec = vec_ref[...]
    w0, a0, k_k, k_a = vec[0:1, :], vec[1:2, :], vec[2:3, :], vec[3:4, :]
    r_ref[...] = _dot(mix(0), wrkv_ref[0])
    z = w0 + _dot(jnp.tanh(_dot(mix(1), w1_ref[...])).astype(BF16), w2_ref[...])
    softplus = jnp.maximum(-z, 0.0) + jnp.log(1.0 + jnp.exp(-jnp.abs(z)))
    lw_ref[...] = -jnp.exp(-softplus - 0.5)
    k = _dot(mix(2), wrkv_ref[1])
    v_ref[...] = _dot(mix(3), wrkv_ref[2])
    a = jax.nn.sigmoid(a0 + _dot(_dot(mix(4), a1_ref[...]).astype(BF16), a2_ref[...]))
    g_ref[...] = _dot(jax.nn.sigmoid(_dot(mix(5), g1_ref[...])).astype(BF16), g2_ref[...])
    kk = k * k_k
    kk = kk * lax.rsqrt(jnp.maximum(_group_sum(kk * kk, gm_ref[...]), 1e-24))
    kk_ref[...] = kk
    ka_ref[...] = kk * a
    k_ref[...] = k * (1.0 + (a - 1.0) * k_a)


def _rwkv_proj(xn, xp, mu, vec, wrkv, w1, w2, a1, a2, g1, g2, tm):
    n = xn.shape[0]
    row = lambda i: (i, 0)
    c2 = lambda i: (0, 0)
    c3 = lambda i: (0, 0, 0)
    full = lambda a: pl.BlockSpec(a.shape, c3 if a.ndim == 3 else c2)
    gm = _group_ones()
    consts = [mu, vec, wrkv, w1, w2, a1, a2, g1, g2, gm]
    return pl.pallas_call(
        _rwkv_proj_body,
        grid=(n // tm,),
        in_specs=[pl.BlockSpec((tm, D_MODEL), row)] * 2 + [full(a) for a in consts],
        out_specs=[pl.BlockSpec((tm, D_MODEL), row)] * 7,
        out_shape=[jax.ShapeDtypeStruct((n, D_MODEL), F32)] * 7,
        compiler_params=_cparams(("parallel",)),
        name="rwkv_proj",
    )(xn, xp, *consts)


def _wkv_body(r_ref, lw_ref, k_ref, v_ref, kk_ref, ka_ref, s0_ref, y_ref, sT_ref,
              s_scr, a_scr, r_scr, b_scr, k_scr, *, chunk):
    c = pl.program_id(1)
    n_pairs = D_MODEL // LANES
    hd = RWKV_HEAD
    zeros = jnp.zeros((hd, hd), F32)

    @pl.when(c == 0)
    def _():
        for p in range(n_pairs):
            top = jnp.concatenate([s0_ref[0, 2 * p], zeros], axis=1)
            bot = jnp.concatenate([zeros, s0_ref[0, 2 * p + 1]], axis=1)
            s_scr[p] = jnp.concatenate([top, bot], axis=0)

    row = lax.broadcasted_iota(jnp.int32, (chunk, chunk), 0)
    colm = lax.broadcasted_iota(jnp.int32, (chunk, chunk), 1)
    lw = lw_ref[...]
    cum = jnp.dot(jnp.where(row >= colm, 1.0, 0.0), lw, preferred_element_type=F32, precision=HI)
    w_incl = jnp.exp(cum)
    w_inv = jnp.exp(-cum)
    a_scr[...] = -kk_ref[...] * jnp.exp(cum - lw)
    r_scr[...] = r_ref[...] * w_incl
    b_scr[...] = ka_ref[...] * w_inv
    k_scr[...] = k_ref[...] * w_inv
    w_last = w_incl[chunk - 1:chunk, :]
    n_dbl = max(1, int(np.ceil(np.log2(chunk))))

    c2 = 2 * chunk
    head_of_row = lax.broadcasted_iota(jnp.int32, (c2, LANES), 0) // chunk
    head_of_lane = lax.broadcasted_iota(jnp.int32, (c2, LANES), 1) // hd
    own_lanes = head_of_row == head_of_lane
    tr = lax.broadcasted_iota(jnp.int32, (c2, c2), 0)
    tc = lax.broadcasted_iota(jnp.int32, (c2, c2), 1)
    same = (tr // chunk) == (tc // chunk)
    strict = same & (tr % chunk > tc % chunk)
    incl = same & (tr % chunk >= tc % chunk)

    def block_diag(x):
        return jnp.where(own_lanes, jnp.concatenate([x, x], axis=0), 0.0).astype(BF16)

    pairs = range(n_pairs)
    sls = [slice(p * LANES, (p + 1) * LANES) for p in pairs]
    ar = [jnp.concatenate([block_diag(a_scr[:, sl]), block_diag(r_scr[:, sl])], axis=0) for sl in sls]
    bk = [jnp.concatenate([block_diag(b_scr[:, sl]), block_diag(k_scr[:, sl])], axis=0) for sl in sls]
    vm = [block_diag(v_ref[:, sl]) for sl in sls]
    s_old = [s_scr[p] for p in pairs]
    gram = [_dot_nt(ar[p], bk[p]) for p in pairs]
    xs = [_dot_nt(ar[p], s_old[p].astype(BF16)) for p in pairs]
    u = [xs[p][:c2, :] + _dot(jnp.where(strict, gram[p][:c2, c2:], 0.0).astype(BF16), vm[p]) for p in pairs]
    lp = [jnp.where(strict, gram[p][:c2, :c2], 0.0).astype(BF16) for p in pairs]
    for d in range(n_dbl):
        if d + 1 < n_dbl:
            t = [_dot(lp[p], jnp.concatenate([u[p].astype(BF16), lp[p]], axis=1)) for p in pairs]
            u = [u[p] + t[p][:, :LANES] for p in pairs]
            lp = [t[p][:, LANES:].astype(BF16) for p in pairs]
        else:
            u = [u[p] + _dot(lp[p], u[p].astype(BF16)) for p in pairs]
    uv = [jnp.concatenate([u[p].astype(BF16), vm[p]], axis=0) for p in pairs]
    m_r = [jnp.concatenate([jnp.where(incl, gram[p][c2:, :c2], 0.0), jnp.where(incl, gram[p][c2:, c2:], 0.0)],
                           axis=1).astype(BF16) for p in pairs]
    y = [xs[p][c2:, :] + _dot(m_r[p], uv[p]) for p in pairs]
    for p in pairs:
        y_ref[:, sls[p]] = y[p][:chunk, :] + y[p][chunk:, :]
    s_new = [(s_old[p] + _dot_tn(uv[p], bk[p])) * w_last[:, sls[p]] for p in pairs]
    for p in pairs:
        s_scr[p] = s_new[p]

    @pl.when(c == pl.num_programs(1) - 1)
    def _():
        for p in range(n_pairs):
            s = s_scr[p]
            sT_ref[0, 2 * p] = s[:hd, :hd]
            sT_ref[0, 2 * p + 1] = s[hd:, hd:]


def _wkv(r, lw, k, v, kk, ka, s0, batch, seq, chunk):
    nc = seq // chunk
    blk = lambda b, c: (b * nc + c, 0)
    st = lambda b, c: (b, 0, 0, 0)
    state = pl.BlockSpec((1, RWKV_HEADS, RWKV_HEAD, RWKV_HEAD), st)
    return pl.pallas_call(
        functools.partial(_wkv_body, chunk=chunk),
        grid=(batch, nc),
        in_specs=[pl.BlockSpec((chunk, D_MODEL), blk)] * 6 + [state],
        out_specs=[pl.BlockSpec((chunk, D_MODEL), blk), state],
        out_shape=[jax.ShapeDtypeStruct((batch * seq, D_MODEL), F32),
                   jax.ShapeDtypeStruct(s0.shape, F32)],
        scratch_shapes=[pltpu.VMEM((D_MODEL // LANES, LANES, LANES), F32)]
                       + [pltpu.VMEM((chunk, D_MODEL), F32)] * 4,
        compiler_params=_cparams(("parallel", "arbitrary")),
        name="wkv",
    )(r, lw, k, v, kk, ka, s0)


def _rwkv_out_body(x_ref, y_ref, r_ref, k_ref, v_ref, g_ref, vec_ref, gm_ref, wo_ref, o_ref):
    gm = gm_ref[...]
    vec = vec_ref[...]
    ln_w, ln_b, r_k = vec[0:1, :], vec[1:2, :], vec[2:3, :]
    y = y_ref[...]
    mean = _group_sum(y, gm) * (1.0 / RWKV_HEAD)
    d = y - mean
    var = _group_sum(d * d, gm) * (1.0 / RWKV_HEAD)
    yn = d * lax.rsqrt(var + GN_EPS) * ln_w + ln_b
    yn = yn + _group_sum(r_ref[...] * k_ref[...] * r_k, gm) * v_ref[...]
    o_ref[...] = x_ref[...] + _dot((yn * g_ref[...]).astype(BF16), wo_ref[...])


def _rwkv_out(x, y, r, k, v, g, vec, wo, tm):
    n = x.shape[0]
    row = lambda i: (i, 0)
    const = lambda i: (0, 0)
    gm = _group_ones()
    return pl.pallas_call(
        _rwkv_out_body,
        grid=(n // tm,),
        in_specs=[pl.BlockSpec((tm, D_MODEL), row)] * 6
                 + [pl.BlockSpec(vec.shape, const), pl.BlockSpec(gm.shape, const), pl.BlockSpec(wo.shape, const)],
        out_specs=pl.BlockSpec((tm, D_MODEL), row),
        out_shape=jax.ShapeDtypeStruct((n, D_MODEL), F32),
        compiler_params=_cparams(("parallel",)),
        name="rwkv_out",
    )(x, y, r, k, v, g, vec, gm, wo)


def _pick_tile(n, want):
    t = min(n, want)
    while n % t:
        t //= 2
    return t


def _trunk(x, pos, conv_prev, shift_prev, wkv_prev, wts, sample):
    batch, seq, _ = x.shape
    n = batch * seq
    xf = x.reshape(n, D_MODEL)
    tm = _pick_tile(n, 512)

    tabs = _rope_tables(pos)
    if sample is not None:
        tabs = tuple(jnp.tile(t, (batch, 1)) for t in tabs)
    q, k, v, iq, ikw, cb, u = _proj_even(xf, wts['norm_mix_even'], wts['w_in'], tabs, tm)
    if sample is None:
        attn = _prompt_attention(q, iq, ikw, k, v, batch, seq)
        conv = _conv_prompt(cb, u, wts['conv_w'], seq, _pick_tile(seq, 512))
        conv_state = u.reshape(batch, seq, CONV_DIM)[:, seq - (CONV_W - 1):]
    else:
        cache_k, cache_v, cache_ik, page_table = sample
        tpad = SUBLANES - seq

        def stack(a, heads):
            a = a.reshape(batch, seq, heads, -1).transpose(0, 2, 1, 3)
            a = jnp.pad(a, ((0, 0), (0, 0), (0, tpad), (0, 0)))
            return a.reshape(batch, heads * SUBLANES, a.shape[-1])

        iq_st = stack(iq, N_IDX_HEADS)
        iw_st = stack(ikw[:, IDX_DIM:IDX_DIM + N_IDX_HEADS], N_IDX_HEADS)
        s_past = _sample_scores(page_table, iq_st, iw_st, cache_ik)[:, :seq]
        mask = _sample_select(s_past.reshape(n, -1), iq, ikw, seq)
        mask = jnp.pad(mask.reshape(batch, seq, -1), ((0, 0), (0, tpad), (0, 0)))
        q_st = stack(q, N_Q_HEADS).reshape(batch, N_KV_HEADS, Q_PER_KV * SUBLANES, HEAD_DIM)
        padk = lambda a: jnp.pad(jnp.swapaxes(a.reshape(batch, seq, KV_DIM), 1, 2),
                                 ((0, 0), (0, 0), (0, PAGE_SIZE - seq)))
        o = _sample_attention(page_table, q_st, mask, padk(k), padk(v), cache_k, cache_v)
        attn = (o.reshape(batch, N_Q_HEADS, SUBLANES, HEAD_DIM)[:, :, :seq].transpose(0, 2, 1, 3)
                .reshape(n, ATTN_DIM).astype(BF16))
        u3 = jnp.concatenate([conv_prev, u.reshape(batch, seq, CONV_DIM)], axis=1)
        conv_state = u3[:, seq:]
        conv_t = _conv_sample(jnp.swapaxes(cb.reshape(batch, seq, CONV_DIM), 0, 1), jnp.swapaxes(u3, 0, 1),
                              wts['conv_w'])
        conv = jnp.swapaxes(conv_t, 0, 1).reshape(n, CONV_DIM)
    x1 = _out_even(xf, attn, conv, wts['w_out_attn'], wts['w_out_conv'], tm)
    x2, xn2 = _mixer(x1, wts['norm_ffn_even'], None, wts['ffn_gate'], wts['ffn_up'], wts['ffn_down'],
                     wts['norm_mix_odd'], tm, D_FF // 2, True)

    xn3 = xn2.reshape(batch, seq, D_MODEL)
    xp = jnp.concatenate([shift_prev[:, None, :], xn3[:, :-1]], axis=1).reshape(n, D_MODEL)
    r, lw, kr, vr, kk, ka, g = _rwkv_proj(xn2, xp, wts['rwkv_mu'], wts['rwkv_vec_in'], wts['rwkv_w_rkv'],
                                          wts['rwkv_w1'], wts['rwkv_w2'], wts['rwkv_a1'], wts['rwkv_a2'],
                                          wts['rwkv_g1'], wts['rwkv_g2'], _pick_tile(n, 256))
    if seq % WKV_CHUNK:
        sp = -(-seq // WKV_CHUNK) * WKV_CHUNK
        padded = [jnp.pad(a.reshape(batch, seq, D_MODEL), ((0, 0), (0, sp - seq), (0, 0)))
                  .reshape(batch * sp, D_MODEL) for a in (r, lw, kr, vr, kk, ka)]
        y, wkv_state = _wkv(*padded, wkv_prev, batch, sp, WKV_CHUNK)
        y = y.reshape(batch, sp, D_MODEL)[:, :seq].reshape(n, D_MODEL)
    else:
        y, wkv_state = _wkv(r, lw, kr, vr, kk, ka, wkv_prev, batch, seq, WKV_CHUNK)
    x3 = _rwkv_out(x2, y, r, kr, vr, g, wts['rwkv_vec_out'], wts['rwkv_w_o'], _pick_tile(n, 256))
    yf = _moe(x3, wts['norm_ffn_odd'], wts['moe_router'], wts['moe_gate'], wts['moe_up'], wts['moe_down'],
              wts['norm_final'], _pick_tile(n, MOE_GROUP), min(MOE_BLOCK_ROWS, _pick_tile(n, MOE_GROUP)), D_FF // 2)

    k4 = k.reshape(1, batch, seq, N_KV_HEADS, HEAD_DIM)
    v4 = v.reshape(1, batch, seq, N_KV_HEADS, HEAD_DIM)
    ik3 = ikw[:, :IDX_DIM].reshape(1, batch, seq, IDX_DIM)
    return (yf.reshape(batch, seq, D_MODEL), k4, v4, ik3, conv_state[None],
            xn3[:, -1][None], wkv_state[None])


def _prepare_weights(norm_mix_even, w_in_even, conv_w, w_out_even, norm_ffn_even, ffn_gate, ffn_up, ffn_down,
                     norm_mix_odd, rwkv_mu, rwkv_w_rkv, rwkv_w0, rwkv_w1, rwkv_w2, rwkv_a0, rwkv_a1, rwkv_a2,
                     rwkv_g1, rwkv_g2, rwkv_k_k, rwkv_k_a, rwkv_r_k, rwkv_ln_w, rwkv_ln_b, rwkv_w_o,
                     norm_ffn_odd, moe_router, moe_gate, moe_up, moe_down, norm_final):
    w_in = w_in_even[0]
    o = np.cumsum((0, ATTN_DIM, KV_DIM, KV_DIM, N_IDX_HEADS * IDX_DIM, IDX_DIM, N_IDX_HEADS,
                   CONV_DIM, CONV_DIM, CONV_DIM))
    pad = jnp.zeros((D_MODEL, LANES - IDX_DIM - N_IDX_HEADS), F32)
    w_in = jnp.concatenate([w_in[:, :o[4]], w_in[:, o[4]:o[6]], pad, w_in[:, o[6]:]], axis=1).astype(BF16)
    row = lambda a: a.reshape(1, -1)
    zeros = jnp.zeros((1, D_MODEL), F32)
    return dict(
        norm_mix_even=row(norm_mix_even[0]), w_in=w_in, conv_w=conv_w[0],
        w_out_attn=w_out_even[0, :ATTN_DIM].astype(BF16), w_out_conv=w_out_even[0, ATTN_DIM:].astype(BF16),
        norm_ffn_even=row(norm_ffn_even[0]),
        ffn_gate=ffn_gate.astype(BF16), ffn_up=ffn_up.astype(BF16), ffn_down=ffn_down.astype(BF16),
        norm_mix_odd=row(norm_mix_odd[0]), rwkv_mu=jnp.concatenate([rwkv_mu[0], zeros, zeros], axis=0),
        rwkv_vec_in=jnp.concatenate([row(rwkv_w0[0]), row(rwkv_a0[0]), row(rwkv_k_k[0]), row(rwkv_k_a[0]),
                                     zeros, zeros, zeros, zeros], axis=0),
        rwkv_w_rkv=rwkv_w_rkv[0].astype(BF16),
        rwkv_w1=rwkv_w1[0].astype(BF16), rwkv_w2=rwkv_w2[0].astype(BF16),
        rwkv_a1=rwkv_a1[0].astype(BF16), rwkv_a2=rwkv_a2[0].astype(BF16),
        rwkv_g1=rwkv_g1[0].astype(BF16), rwkv_g2=rwkv_g2[0].astype(BF16),
        rwkv_vec_out=jnp.concatenate([row(rwkv_ln_w[0]), row(rwkv_ln_b[0]), row(rwkv_r_k[0]),
                                      zeros, zeros, zeros, zeros, zeros], axis=0),
        rwkv_w_o=rwkv_w_o[0].astype(BF16),
        norm_ffn_odd=row(norm_ffn_odd[0]),
        moe_router=jnp.pad(moe_router[0], ((0, 0), (0, LANES - N_EXPERTS))).astype(BF16),
        moe_gate=moe_gate[0].astype(BF16), moe_up=moe_up[0].astype(BF16), moe_down=moe_down[0].astype(BF16),
        norm_final=row(norm_final),
    )


def kernel(x_prompt, x_sample, cache_k, cache_v, cache_idx_k, state_conv, state_shift, state_wkv, page_table, norm_mix_even, w_in_even, conv_w, w_out_even, norm_ffn_even, ffn_gate, ffn_up, ffn_down, norm_mix_odd, rwkv_mu, rwkv_w_rkv, rwkv_w0, rwkv_w1, rwkv_w2, rwkv_a0, rwkv_a1, rwkv_a2, rwkv_g1, rwkv_g2, rwkv_k_k, rwkv_k_a, rwkv_r_k, rwkv_ln_w, rwkv_ln_b, rwkv_w_o, norm_ffn_odd, moe_router, moe_gate, moe_up, moe_down, norm_final):
    assert w_in_even.shape[0] == 1 and rwkv_mu.shape[0] == 1, "one even and one odd layer"
    wts = _prepare_weights(norm_mix_even, w_in_even, conv_w, w_out_even, norm_ffn_even, ffn_gate, ffn_up, ffn_down,
                           norm_mix_odd, rwkv_mu, rwkv_w_rkv, rwkv_w0, rwkv_w1, rwkv_w2, rwkv_a0, rwkv_a1, rwkv_a2,
                           rwkv_g1, rwkv_g2, rwkv_k_k, rwkv_k_a, rwkv_r_k, rwkv_ln_w, rwkv_ln_b, rwkv_w_o,
                           norm_ffn_odd, moe_router, moe_gate, moe_up, moe_down, norm_final)
    b, t = x_prompt.shape[:2]
    bd, tn = x_sample.shape[:2]
    n_pool = cache_k.shape[1]
    past = page_table.shape[1] * PAGE_SIZE
    pos_prompt = jnp.arange(t, dtype=jnp.int32)
    pos_sample = past + jnp.arange(tn, dtype=jnp.int32)
    zeros = lambda *s: jnp.zeros(s, F32)
    out_p = _trunk(x_prompt, pos_prompt, zeros(b, CONV_W - 1, CONV_DIM), zeros(b, D_MODEL),
                   zeros(b, RWKV_HEADS, RWKV_HEAD, RWKV_HEAD), wts, None)
    pages_t = lambda c: jnp.transpose(c[0], (0, 2, 3, 1)).reshape(n_pool, KV_DIM, PAGE_SIZE)
    sample = (pages_t(cache_k), pages_t(cache_v), jnp.swapaxes(cache_idx_k[0], 1, 2), page_table)
    out_s = _trunk(x_sample, pos_sample, state_conv[0], state_shift[0], state_wkv[0], wts, sample)
    y_p, k_p, v_p, ik_p, conv_p, shift_p, wkv_p = out_p
    y_s, k_s, v_s, ik_s, conv_s, shift_s, wkv_s = out_s
    return (y_p, y_s, k_p, v_p, ik_p, k_s, v_s, ik_s, conv_p, conv_s, shift_p, shift_s, wkv_p, wkv_s)
```

```python
import functools

import numpy as np
import jax
import jax.numpy as jnp
from jax import lax
from jax.experimental import pallas as pl
from jax.experimental.pallas import tpu as pltpu

F32 = jnp.float32
BF16 = jnp.bfloat16
HI = lax.Precision.HIGHEST

D_MODEL = 1024
PAGE_SIZE = 128
HEAD_DIM = 64
N_Q_HEADS = 8
N_KV_HEADS = 2
Q_PER_KV = N_Q_HEADS // N_KV_HEADS
ROT_DIM = HEAD_DIM // 4
ROPE_THETA = 500000.0
N_IDX_HEADS = 4
IDX_DIM = 64
TOPK_MAX = 256
Q_BLOCK = 128
ATTN_DIM = N_Q_HEADS * HEAD_DIM
KV_DIM = N_KV_HEADS * HEAD_DIM
CONV_DIM = D_MODEL // 2
CONV_W = 3
RWKV_HEAD = 64
RWKV_HEADS = D_MODEL // RWKV_HEAD
GN_EPS = 64e-5
D_FF = 2816
N_EXPERTS = 8
RMS_EPS = 1e-6

LANES = 128
SUBLANES = 8
VMEM_LIMIT = 56 * 1024 * 1024
INT_MIN = -2 ** 31
KEY_NEG_INF = INT_MIN + 0x7FFFFF
PROJ_COLS = ATTN_DIM + KV_DIM + KV_DIM + N_IDX_HEADS * IDX_DIM + LANES + 3 * CONV_DIM
WKV_CHUNK = 64
MOE_GROUP = 1024
MOE_BLOCK_ROWS = 256


def _cparams(sem):
    return pltpu.CompilerParams(dimension_semantics=sem, vmem_limit_bytes=VMEM_LIMIT)


def _rms(x, g):
    return x * lax.rsqrt(jnp.mean(x * x, axis=-1, keepdims=True) + RMS_EPS) * g


def _dot(a, b):
    return jnp.dot(a, b, preferred_element_type=F32)


def _dot_nt(a, b, precision=None):
    return lax.dot_general(a, b, (((1,), (1,)), ((), ())), preferred_element_type=F32, precision=precision)


def _dot_tn(a, b, precision=None):
    return lax.dot_general(a, b, (((0,), (0,)), ((), ())), preferred_element_type=F32, precision=precision)


def _group_sum(x, gmat):
    outs = []
    for c in range(x.shape[1] // LANES):
        xc = x[:, c * LANES:(c + 1) * LANES]
        hi = xc.astype(BF16)
        lo = (xc - hi.astype(F32)).astype(BF16)
        outs.append(_dot(hi, gmat) + _dot(lo, gmat))
    return jnp.concatenate(outs, axis=1)


def _rope_chunk(xc, rc, rp, rm):
    return xc * rc + pltpu.roll(xc, 8, 1) * rp + pltpu.roll(xc, LANES - 8, 1) * rm


def _proj_even_body(x_ref, g_ref, w_ref, rc_ref, rp_ref, rm_ref,
                    q_ref, k_ref, v_ref, iq_ref, ikw_ref, cb_ref, u_ref):
    xn = _rms(x_ref[...], g_ref[...]).astype(BF16)
    h = _dot(xn, w_ref[...])
    rc, rp, rm = rc_ref[...], rp_ref[...], rm_ref[...]
    col = 0
    for c in range(ATTN_DIM // LANES):
        q_ref[:, c * LANES:(c + 1) * LANES] = _rope_chunk(h[:, col:col + LANES], rc, rp, rm).astype(BF16)
        col += LANES
    k_ref[...] = _rope_chunk(h[:, col:col + LANES], rc, rp, rm)
    col += LANES
    v_ref[...] = h[:, col:col + LANES]
    col += LANES
    for c in range(N_IDX_HEADS * IDX_DIM // LANES):
        iq_ref[:, c * LANES:(c + 1) * LANES] = _rope_chunk(h[:, col:col + LANES], rc, rp, rm).astype(BF16)
        col += LANES
    ikw = h[:, col:col + LANES]
    lane = lax.broadcasted_iota(jnp.int32, ikw.shape, 1)
    ikw_ref[...] = jnp.where(lane < IDX_DIM, _rope_chunk(ikw, rc, rp, rm), ikw)
    col += LANES
    cb_ref[...] = h[:, col:col + CONV_DIM]
    col += CONV_DIM
    u_ref[...] = h[:, col:col + CONV_DIM] * h[:, col + CONV_DIM:col + 2 * CONV_DIM]


def _proj_even(x, g, w, tabs, tm):
    n = x.shape[0]
    nt = tabs[0].shape[0] // tm
    row = lambda i: (i, 0)
    const = lambda i: (0, 0)
    tab = lambda i: (i % nt, 0)
    widths = (ATTN_DIM, KV_DIM, KV_DIM, N_IDX_HEADS * IDX_DIM, LANES, CONV_DIM, CONV_DIM)
    dtypes = (BF16, F32, F32, BF16, F32, F32, F32)
    return pl.pallas_call(
        _proj_even_body,
        grid=(n // tm,),
        in_specs=[pl.BlockSpec((tm, D_MODEL), row), pl.BlockSpec((1, D_MODEL), const),
                  pl.BlockSpec((D_MODEL, PROJ_COLS), const)] + [pl.BlockSpec((tm, LANES), tab)] * 3,
        out_specs=[pl.BlockSpec((tm, wd), row) for wd in widths],
        out_shape=[jax.ShapeDtypeStruct((n, wd), dt) for wd, dt in zip(widths, dtypes)],
        compiler_params=_cparams(("parallel",)),
        name="proj_even",
    )(x, g, w, *tabs)


def _rope_tables(pos):
    half = ROT_DIM // 2
    inv_freq = ROPE_THETA ** (-jnp.arange(half, dtype=F32) / half)
    ang = pos.astype(F32)[:, None] * inv_freq[None, :]
    cos, sin = jnp.cos(ang), jnp.sin(ang)
    t = pos.shape[0]
    pad = jnp.zeros((t, HEAD_DIM - ROT_DIM), F32)
    zero = jnp.zeros((t, half), F32)
    rc = jnp.concatenate([cos, cos, pad + 1.0], axis=1)
    rp = jnp.concatenate([zero, sin, pad], axis=1)
    rm = jnp.concatenate([-sin, zero, pad], axis=1)
    return tuple(jnp.tile(a, (1, LANES // HEAD_DIM)) for a in (rc, rp, rm))


def _conv_prompt_body(cb_ref, u_ref, up_ref, w_ref, y_ref, *, tiles_per_seq):
    u = u_ref[...]
    first = pl.program_id(0) % tiles_per_seq == 0
    prev = jnp.where(first, 0.0, up_ref[...])
    p1, p2 = prev[SUBLANES - 1:SUBLANES, :], prev[SUBLANES - 2:SUBLANES - 1, :]
    r = lax.broadcasted_iota(jnp.int32, u.shape, 0)
    u1 = jnp.where(r == 0, p1, pltpu.roll(u, 1, 0))
    u2 = jnp.where(r == 0, p2, jnp.where(r == 1, p1, pltpu.roll(u, 2, 0)))
    w = w_ref[...]
    y_ref[...] = cb_ref[...] * (w[0:1, :] * u2 + w[1:2, :] * u1 + w[2:3, :] * u)


def _conv_prompt(cb, u, w, seq, tc):
    n = u.shape[0]
    row = lambda i: (i, 0)
    prev = lambda i: (jnp.maximum(i * (tc // SUBLANES) - 1, 0), 0)
    return pl.pallas_call(
        functools.partial(_conv_prompt_body, tiles_per_seq=seq // tc),
        grid=(n // tc,),
        in_specs=[pl.BlockSpec((tc, CONV_DIM), row), pl.BlockSpec((tc, CONV_DIM), row),
                  pl.BlockSpec((SUBLANES, CONV_DIM), prev), pl.BlockSpec((CONV_W, CONV_DIM), lambda i: (0, 0))],
        out_specs=pl.BlockSpec((tc, CONV_DIM), row),
        out_shape=jax.ShapeDtypeStruct((n, CONV_DIM), F32),
        compiler_params=_cparams(("parallel",)),
        name="conv_prompt",
    )(cb, u, u, w)


def _conv_sample_body(cb_ref, ue_ref, w_ref, y_ref):
    w = w_ref[...]
    for t in range(y_ref.shape[0]):
        acc = w[0:1, :] * ue_ref[t] + w[1:2, :] * ue_ref[t + 1] + w[2:3, :] * ue_ref[t + 2]
        y_ref[t] = cb_ref[t] * acc


def _conv_sample(cb_t, ue_t, w):
    return pl.pallas_call(
        _conv_sample_body,
        out_shape=jax.ShapeDtypeStruct(cb_t.shape, F32),
        name="conv_sample",
    )(cb_t, ue_t, w)


def _topk_select(scores, n_sel, tri):
    rows, width = scores.shape
    sc = scores

    def key_to_float(key):
        return lax.bitcast_convert_type(key ^ ((key >> 31) & 0x7FFFFFFF), F32)

    def count_ge(key):
        cnt = jnp.sum(jnp.where(sc >= key_to_float(key), 1.0, 0.0), axis=1, keepdims=True)
        return jnp.where(key <= KEY_NEG_INF, float(width), cnt)

    thr = jnp.where(count_ge(jnp.zeros((rows, 1), jnp.int32)) >= n_sel, 0, INT_MIN).astype(jnp.int32)

    def body(i, thr):
        cand = thr + lax.shift_left(jnp.int32(1), 30 - i)
        return jnp.where(count_ge(cand) >= n_sel, cand, thr)

    thr = lax.fori_loop(0, 31, body, thr)
    thr_f = key_to_float(thr)
    gt = sc > thr_f
    eq = sc == thr_f
    need = n_sel - jnp.sum(jnp.where(gt, 1.0, 0.0), axis=1, keepdims=True)
    off = jnp.zeros((rows, 1), F32)
    parts = []
    for c in range(width // LANES):
        eqc = jnp.where(eq[:, c * LANES:(c + 1) * LANES], 1.0, 0.0)
        incl = _dot(eqc.astype(BF16), tri)
        parts.append((incl - eqc + off) < need)
        off = off + incl[:, LANES - 1:LANES]
    tie = jnp.concatenate(parts, axis=1)
    return (gt | (eq & tie)) & (sc > -jnp.inf)


def _tri_incl():
    i = np.arange(LANES)
    return jnp.asarray((i[:, None] <= i[None, :]).astype(np.float32), BF16)


def _stack_heads(x, n):
    return jnp.concatenate([x[:, h * HEAD_DIM:(h + 1) * HEAD_DIM] for h in range(n)], axis=0)


def _index_scores(iq_st, iw_st, ik_b, keys_on_lanes=False):
    rows = iq_st.shape[0] // N_IDX_HEADS
    s = _dot(iq_st, ik_b) if keys_on_lanes else _dot_nt(iq_st, ik_b)
    term = jnp.maximum(s, 0.0) * (IDX_DIM ** -0.5) * (iw_st * (N_IDX_HEADS ** -0.5))
    acc = term[0:rows, :]
    for h in range(1, N_IDX_HEADS):
        acc = acc + term[h * rows:(h + 1) * rows, :]
    return acc


def _group_logits(qs, kg, sel, keys_on_lanes=False):
    s = (_dot(qs, kg) if keys_on_lanes else _dot_nt(qs, kg)) * (HEAD_DIM ** -0.5)
    return jnp.where(jnp.concatenate([sel] * Q_PER_KV, axis=0), s, -jnp.inf)


def _reduce_rows(x, op):
    parts = [x[c * LANES:(c + 1) * LANES, :] for c in range(x.shape[0] // LANES)]
    while len(parts) > 1:
        nxt = [op(a, b) for a, b in zip(parts[0::2], parts[1::2])]
        if len(parts) % 2:
            nxt.append(parts[-1])
        parts = nxt
    x = parts[0]
    rows = x.shape[0]
    while rows > SUBLANES:
        rows //= 2
        x = op(x[:rows, :], x[rows:, :])
    red = jnp.max if op is jnp.maximum else jnp.sum
    return red(x, axis=0, keepdims=True)


def _topk_select_keys_on_rows(sc, n_sel, tril):
    width, cols = sc.shape

    def key_to_float(key):
        return lax.bitcast_convert_type(key ^ ((key >> 31) & 0x7FFFFFFF), F32)

    def count_ge(key):
        cnt = _reduce_rows(jnp.where(sc >= key_to_float(key), 1.0, 0.0), jnp.add)
        return jnp.where(key <= KEY_NEG_INF, float(width), cnt)

    thr = jnp.where(count_ge(jnp.zeros((1, cols), jnp.int32)) >= n_sel, 0, INT_MIN).astype(jnp.int32)

    def body(i, thr):
        cand = thr + lax.shift_left(jnp.int32(1), 30 - i)
        return jnp.where(count_ge(cand) >= n_sel, cand, thr)

    thr = lax.fori_loop(0, 31, body, thr)
    thr_f = key_to_float(thr)
    gt = sc > thr_f
    eq = sc == thr_f
    need = n_sel - _reduce_rows(jnp.where(gt, 1.0, 0.0), jnp.add)
    off = jnp.zeros((1, cols), F32)
    parts = []
    for c in range(width // LANES):
        eqc = jnp.where(eq[c * LANES:(c + 1) * LANES, :], 1.0, 0.0)
        incl = _dot(tril, eqc.astype(BF16))
        parts.append((incl - eqc + off) < need)
        off = off + incl[LANES - 1:LANES, :]
    tie = jnp.concatenate(parts, axis=0)
    return (gt | (eq & tie)) & (sc > -jnp.inf)


def _prompt_attn_body(q_ref, iq_ref, iwq_ref, k_ref, v_ref, ikw_ref, tril_ref, o_ref,
                      kb_scr, vt_scr, ikb_scr, *, n_sel, widths):
    i = pl.program_id(1)

    @pl.when(i == 0)
    def _():
        for g in range(N_KV_HEADS):
            kb_scr[g] = k_ref[:, g * HEAD_DIM:(g + 1) * HEAD_DIM].astype(BF16)
        vt_scr[...] = jnp.transpose(v_ref[...]).astype(BF16)
        ikb_scr[...] = ikw_ref[:, :IDX_DIM].astype(BF16)

    def run(width):
        iw_t = jnp.transpose(iwq_ref[...])
        w_row = jnp.concatenate([iw_t[IDX_DIM + h:IDX_DIM + h + 1, :] for h in range(N_IDX_HEADS)], axis=1)
        s = _dot_nt(ikb_scr[0:width, :], _stack_heads(iq_ref[...], N_IDX_HEADS))
        term = jnp.maximum(s, 0.0) * (IDX_DIM ** -0.5) * (w_row * (N_IDX_HEADS ** -0.5))
        scores = term[:, 0:Q_BLOCK]
        for h in range(1, N_IDX_HEADS):
            scores = scores + term[:, h * Q_BLOCK:(h + 1) * Q_BLOCK]
        key_pos = lax.broadcasted_iota(jnp.int32, scores.shape, 0)
        tq = i * Q_BLOCK + lax.broadcasted_iota(jnp.int32, scores.shape, 1)
        scores = jnp.where(key_pos <= tq, scores, -jnp.inf)
        sel = _topk_select_keys_on_rows(scores, n_sel, tril_ref[...])
        sel4 = jnp.concatenate([sel] * Q_PER_KV, axis=1)
        q = q_ref[...]
        outs = []
        for g in range(N_KV_HEADS):
            qs = _stack_heads(q[:, g * Q_PER_KV * HEAD_DIM:(g + 1) * Q_PER_KV * HEAD_DIM], Q_PER_KV)
            st = jnp.where(sel4, _dot_nt(kb_scr[g, 0:width, :], qs) * (HEAD_DIM ** -0.5), -jnp.inf)
            p = jnp.exp(st - _reduce_rows(st, jnp.maximum))
            ot = _dot(vt_scr[g * HEAD_DIM:(g + 1) * HEAD_DIM, 0:width], p.astype(BF16))
            outs.append(ot / _reduce_rows(p, jnp.add))
        o = jnp.transpose(jnp.concatenate(outs, axis=0))
        for r in range(Q_PER_KV):
            o_ref[:, r * LANES:(r + 1) * LANES] = o[r * Q_BLOCK:(r + 1) * Q_BLOCK, :].astype(o_ref.dtype)

    lo = 0
    for width in widths:
        hi = width // Q_BLOCK
        pl.when((i >= lo) & (i < hi))(functools.partial(run, width))
        lo = hi


def _prompt_attention(q, iq, ikw, k, v, batch, seq):
    n_sel = min(TOPK_MAX, seq // 4)
    nqb = seq // Q_BLOCK
    n_widths = min(8, nqb)
    widths = tuple(seq * (j + 1) // n_widths for j in range(n_widths))
    blk = lambda b, i: (b * nqb + i, 0)
    full = lambda b, i: (b, 0)
    return pl.pallas_call(
        functools.partial(_prompt_attn_body, n_sel=n_sel, widths=widths),
        grid=(batch, nqb),
        in_specs=[pl.BlockSpec((Q_BLOCK, ATTN_DIM), blk), pl.BlockSpec((Q_BLOCK, N_IDX_HEADS * IDX_DIM), blk),
                  pl.BlockSpec((Q_BLOCK, LANES), blk),
                  pl.BlockSpec((seq, KV_DIM), full), pl.BlockSpec((seq, KV_DIM), full),
                  pl.BlockSpec((seq, LANES), full), pl.BlockSpec((LANES, LANES), lambda b, i: (0, 0))],
        out_specs=pl.BlockSpec((Q_BLOCK, ATTN_DIM), blk),
        out_shape=jax.ShapeDtypeStruct((batch * seq, ATTN_DIM), BF16),
        scratch_shapes=[pltpu.VMEM((N_KV_HEADS, seq, HEAD_DIM), BF16), pltpu.VMEM((KV_DIM, seq), BF16),
                        pltpu.VMEM((seq, IDX_DIM), BF16)],
        compiler_params=_cparams(("parallel", "arbitrary")),
        name="prompt_attention",
    )(q, iq, ikw, k, v, ikw, _tri_incl().T)


PAGES_PER_STEP = 16


def _sample_scores_body(pt_ref, iq_ref, iw_ref, *rest):
    page_refs, s_ref = rest[:PAGES_PER_STEP], rest[PAGES_PER_STEP]
    for j, pr in enumerate(page_refs):
        s_ref[0, :, j * PAGE_SIZE:(j + 1) * PAGE_SIZE] = _index_scores(iq_ref[0], iw_ref[0], pr[0].astype(BF16), True)


def _sample_scores(page_table, iq_st, iw_st, cache_ik):
    bd, n_pages = page_table.shape
    rows = iq_st.shape[1] // N_IDX_HEADS
    steps = n_pages // PAGES_PER_STEP
    per_b = lambda b, p, pt: (b, 0, 0)
    page = lambda j: (lambda b, p, pt: (pt[b, p * PAGES_PER_STEP + j], 0, 0))
    return pl.pallas_call(
        _sample_scores_body,
        grid_spec=pltpu.PrefetchScalarGridSpec(
            num_scalar_prefetch=1,
            grid=(bd, steps),
            in_specs=[pl.BlockSpec((1,) + iq_st.shape[1:], per_b), pl.BlockSpec((1,) + iw_st.shape[1:], per_b)]
                     + [pl.BlockSpec((1, IDX_DIM, PAGE_SIZE), page(j)) for j in range(PAGES_PER_STEP)],
            out_specs=pl.BlockSpec((1, rows, PAGES_PER_STEP * PAGE_SIZE), lambda b, p, pt: (b, 0, p)),
        ),
        out_shape=jax.ShapeDtypeStruct((bd, rows, n_pages * PAGE_SIZE), F32),
        compiler_params=_cparams(("parallel", "arbitrary")),
        name="sample_scores",
    )(page_table, iq_st, iw_st, *([cache_ik] * PAGES_PER_STEP))


def _sample_select_body(sp_ref, iq_ref, iwq_ref, ikn_ref, tri_ref, m_ref, *, n_sel, tn):
    rows = sp_ref.shape[0]
    iw = iwq_ref[...]
    iw_st = jnp.concatenate([iw[:, IDX_DIM + h:IDX_DIM + h + 1] for h in range(N_IDX_HEADS)], axis=0)
    s_new = _index_scores(_stack_heads(iq_ref[...], N_IDX_HEADS), iw_st, ikn_ref[...].astype(BF16))
    r = lax.broadcasted_iota(jnp.int32, s_new.shape, 0)
    c = lax.broadcasted_iota(jnp.int32, s_new.shape, 1)
    same = (r // tn == c // tn) & (c <= r)
    fold = jnp.where((lax.broadcasted_iota(jnp.int32, (rows, LANES), 0) % tn)
                     == lax.broadcasted_iota(jnp.int32, (rows, LANES), 1), 1.0, 0.0)
    picked = jnp.where(same, s_new, 0.0)
    hi = picked.astype(BF16)
    mid = (picked - hi.astype(F32)).astype(BF16)
    lo = (picked - hi.astype(F32) - mid.astype(F32)).astype(BF16)
    fb = fold.astype(BF16)
    new_chunk = _dot(hi, fb) + _dot(mid, fb) + _dot(lo, fb)
    lane = lax.broadcasted_iota(jnp.int32, (rows, LANES), 1)
    tpos = lax.broadcasted_iota(jnp.int32, (rows, LANES), 0) % tn
    new_chunk = jnp.where(lane <= tpos, new_chunk, -jnp.inf)
    scores = jnp.concatenate([sp_ref[...], new_chunk], axis=1)
    sel = _topk_select(scores, n_sel, tri_ref[...])
    m_ref[...] = jnp.where(sel, 1.0, 0.0)


def _sample_select(s_past, iq, ikw, tn):
    rows, past = s_past.shape
    n_sel = min(TOPK_MAX, (past + tn) // 4)
    return pl.pallas_call(
        functools.partial(_sample_select_body, n_sel=n_sel, tn=tn),
        out_shape=jax.ShapeDtypeStruct((rows, past + LANES), F32),
        compiler_params=pltpu.CompilerParams(vmem_limit_bytes=VMEM_LIMIT),
        name="sample_select",
    )(s_past, iq, ikw, ikw[:, :IDX_DIM], _tri_incl())


def _sample_attn_body(pt_ref, q_ref, m_ref, kn_ref, vn_ref, mn_ref, *rest):
    k_refs, v_refs = rest[:PAGES_PER_STEP], rest[PAGES_PER_STEP:2 * PAGES_PER_STEP]
    o_ref, m_scr, l_scr, acc_scr = rest[2 * PAGES_PER_STEP:]
    p = pl.program_id(1)

    @pl.when(p == 0)
    def _():
        m_scr[...] = jnp.full(m_scr.shape, -jnp.inf, F32)
        l_scr[...] = jnp.zeros(l_scr.shape, F32)
        acc_scr[...] = jnp.zeros(acc_scr.shape, F32)

    def update(kb, vb, sel):
        for g in range(N_KV_HEADS):
            s = _group_logits(q_ref[0, g], kb[g * HEAD_DIM:(g + 1) * HEAD_DIM, :], sel, True)
            m_old = m_scr[g]
            m_new = jnp.maximum(m_old, jnp.max(s, axis=1, keepdims=True))
            m_safe = jnp.where(m_new == -jnp.inf, 0.0, m_new)
            alpha = jnp.exp(m_old - m_safe)
            pe = jnp.exp(s - m_safe)
            l_scr[g] = alpha * l_scr[g] + jnp.sum(pe, axis=1, keepdims=True)
            acc_scr[g] = alpha * acc_scr[g] + _dot_nt(pe.astype(BF16), vb[g * HEAD_DIM:(g + 1) * HEAD_DIM, :])
            m_scr[g] = m_new

    kb = jnp.concatenate([r[0].astype(BF16) for r in k_refs], axis=1)
    vb = jnp.concatenate([r[0].astype(BF16) for r in v_refs], axis=1)
    update(kb, vb, m_ref[0] > 0.5)

    @pl.when(p == pl.num_programs(1) - 1)
    def _():
        update(kn_ref[0].astype(BF16), vn_ref[0].astype(BF16), mn_ref[0] > 0.5)
        for g in range(N_KV_HEADS):
            l = l_scr[g]
            o_ref[0, g] = acc_scr[g] / jnp.where(l == 0.0, 1.0, l)


def _sample_attention(page_table, q_st, mask, k_new, v_new, cache_k, cache_v):
    bd, n_pages = page_table.shape
    steps = n_pages // PAGES_PER_STEP
    per_b = lambda b, p, pt: (b, 0, 0)
    per_b4 = lambda b, p, pt: (b, 0, 0, 0)
    page = lambda j: (lambda b, p, pt: (pt[b, p * PAGES_PER_STEP + j], 0, 0))
    width = PAGES_PER_STEP * PAGE_SIZE
    rows = q_st.shape[2]
    mrows = mask.shape[1]
    return pl.pallas_call(
        _sample_attn_body,
        grid_spec=pltpu.PrefetchScalarGridSpec(
            num_scalar_prefetch=1,
            grid=(bd, steps),
            in_specs=[pl.BlockSpec((1,) + q_st.shape[1:], per_b4),
                      pl.BlockSpec((1, mrows, width), lambda b, p, pt: (b, 0, p)),
                      pl.BlockSpec((1, KV_DIM, PAGE_SIZE), per_b), pl.BlockSpec((1, KV_DIM, PAGE_SIZE), per_b),
                      pl.BlockSpec((1, mrows, LANES), lambda b, p, pt: (b, 0, n_pages))]
                     + [pl.BlockSpec((1, KV_DIM, PAGE_SIZE), page(j)) for j in range(PAGES_PER_STEP)] * 2,
            out_specs=pl.BlockSpec((1, N_KV_HEADS, rows, HEAD_DIM), per_b4),
            scratch_shapes=[pltpu.VMEM((N_KV_HEADS, rows, 1), F32), pltpu.VMEM((N_KV_HEADS, rows, 1), F32),
                            pltpu.VMEM((N_KV_HEADS, rows, HEAD_DIM), F32)],
        ),
        out_shape=jax.ShapeDtypeStruct((bd, N_KV_HEADS, rows, HEAD_DIM), F32),
        compiler_params=_cparams(("parallel", "arbitrary")),
        name="sample_attention",
    )(page_table, q_st, mask, k_new, v_new, mask, *([cache_k] * PAGES_PER_STEP), *([cache_v] * PAGES_PER_STEP))


def _out_even_body(x_ref, a_ref, c_ref, wa_ref, wc_ref, o_ref):
    o_ref[...] = x_ref[...] + _dot(a_ref[...], wa_ref[...]) + _dot(c_ref[...].astype(BF16), wc_ref[...])


def _out_even(x, attn, conv, wa, wc, tm):
    n = x.shape[0]
    row = lambda i: (i, 0)
    const = lambda i: (0, 0)
    return pl.pallas_call(
        _out_even_body,
        grid=(n // tm,),
        in_specs=[pl.BlockSpec((tm, D_MODEL), row), pl.BlockSpec((tm, ATTN_DIM), row),
                  pl.BlockSpec((tm, CONV_DIM), row), pl.BlockSpec((ATTN_DIM, D_MODEL), const),
                  pl.BlockSpec((CONV_DIM, D_MODEL), const)],
        out_specs=pl.BlockSpec((tm, D_MODEL), row),
        out_shape=jax.ShapeDtypeStruct((n, D_MODEL), F32),
        compiler_params=_cparams(("parallel",)),
        name="out_even",
    )(x, attn, conv, wa, wc)


def _mixer_body(*refs, n_exp, emit_sum):
    if n_exp > 1:
        x_ref, g_ref, wr_ref, wg_ref, wu_ref, wd_ref, g2_ref = refs[:7]
        outs = refs[7:]
    else:
        x_ref, g_ref, wg_ref, wu_ref, wd_ref, g2_ref = refs[:6]
        outs = refs[6:]
    n_out = 2 if emit_sum else 1
    out_refs, (xn_scr, acc_scr, cw_scr) = outs[:n_out], outs[n_out:]
    e, f = pl.program_id(1), pl.program_id(2)

    @pl.when((e == 0) & (f == 0))
    def _():
        xn = _rms(x_ref[...], g_ref[...]).astype(BF16)
        xn_scr[...] = xn
        acc_scr[...] = jnp.zeros(acc_scr.shape, F32)
        if n_exp > 1:
            logits = _dot(xn, wr_ref[...])
            lane = lax.broadcasted_iota(jnp.int32, logits.shape, 1).astype(F32)
            lg = jnp.where(lane < n_exp, logits, -jnp.inf)
            m1 = jnp.max(lg, axis=1, keepdims=True)
            i1 = jnp.min(jnp.where(lg == m1, lane, float(LANES)), axis=1, keepdims=True)
            lg2 = jnp.where(lane == i1, -jnp.inf, lg)
            m2 = jnp.max(lg2, axis=1, keepdims=True)
            i2 = jnp.min(jnp.where(lg2 == m2, lane, float(LANES)), axis=1, keepdims=True)
            e2 = jnp.exp(m2 - m1)
            cw_scr[...] = jnp.where(lane == i1, 1.0 / (1.0 + e2), 0.0) + jnp.where(lane == i2, e2 / (1.0 + e2), 0.0)

    xn = xn_scr[...]
    gate = _dot(xn, wg_ref[0])
    h = gate * jax.nn.sigmoid(gate) * _dot(xn, wu_ref[0])
    if n_exp > 1:
        lane = lax.broadcasted_iota(jnp.int32, cw_scr.shape, 1)
        h = h * jnp.sum(jnp.where(lane == e, cw_scr[...], 0.0), axis=1, keepdims=True)
    acc_scr[...] += _dot(h.astype(BF16), wd_ref[0])

    @pl.when((e == pl.num_programs(1) - 1) & (f == pl.num_programs(2) - 1))
    def _():
        y = x_ref[...] + acc_scr[...]
        if emit_sum:
            out_refs[0][...] = y
        out_refs[-1][...] = _rms(y, g2_ref[...])


def _mixer(x, g, wr, wg, wu, wd, g2, tm, tf, emit_sum):
    n = x.shape[0]
    n_exp = wg.shape[0]
    row = lambda i, e, f: (i, 0)
    const = lambda i, e, f: (0, 0)
    in_specs = [pl.BlockSpec((tm, D_MODEL), row), pl.BlockSpec((1, D_MODEL), const)]
    args = [x, g]
    if n_exp > 1:
        in_specs.append(pl.BlockSpec((D_MODEL, LANES), const))
        args.append(wr)
    in_specs += [pl.BlockSpec((1, D_MODEL, tf), lambda i, e, f: (e, 0, f)),
                 pl.BlockSpec((1, D_MODEL, tf), lambda i, e, f: (e, 0, f)),
                 pl.BlockSpec((1, tf, D_MODEL), lambda i, e, f: (e, f, 0)),
                 pl.BlockSpec((1, D_MODEL), const)]
    args += [wg, wu, wd, g2]
    n_out = 2 if emit_sum else 1
    return pl.pallas_call(
        functools.partial(_mixer_body, n_exp=n_exp, emit_sum=emit_sum),
        grid=(n // tm, n_exp, D_FF // tf),
        in_specs=in_specs,
        out_specs=[pl.BlockSpec((tm, D_MODEL), row)] * n_out,
        out_shape=[jax.ShapeDtypeStruct((n, D_MODEL), F32)] * n_out,
        scratch_shapes=[pltpu.VMEM((tm, D_MODEL), BF16), pltpu.VMEM((tm, D_MODEL), F32),
                        pltpu.VMEM((tm, LANES), F32)],
        compiler_params=_cparams(("parallel", "arbitrary", "arbitrary")),
        name="mixer_moe" if n_exp > 1 else "mixer_ffn",
    )(*args)


def _moe_route_body(x_ref, g_ref, wr_ref, tri_ref, etri_ref,
                    xn_ref, slot_ref, slot_t_ref, comb_ref, meta_ref, *, n_exp, rows_blk):
    xn = _rms(x_ref[...], g_ref[...]).astype(BF16)
    xn_ref[...] = xn
    logits = _dot(xn, wr_ref[...])
    lane = lax.broadcasted_iota(jnp.int32, logits.shape, 1).astype(F32)
    lg = jnp.where(lane < n_exp, logits, -jnp.inf)
    m1 = jnp.max(lg, axis=1, keepdims=True)
    i1 = jnp.min(jnp.where(lg == m1, lane, float(LANES)), axis=1, keepdims=True)
    lg2 = jnp.where(lane == i1, -jnp.inf, lg)
    m2 = jnp.max(lg2, axis=1, keepdims=True)
    i2 = jnp.min(jnp.where(lg2 == m2, lane, float(LANES)), axis=1, keepdims=True)
    e2 = jnp.exp(m2 - m1)
    comb_ref[...] = jnp.where(lane == i1, 1.0 / (1.0 + e2), 0.0) + jnp.where(lane == i2, e2 / (1.0 + e2), 0.0)
    member = jnp.where((lane == i1) | (lane == i2), 1.0, 0.0)
    n_tok = member.shape[0]
    pos = _dot(tri_ref[...], member.astype(BF16))
    cnt = pos[n_tok - 1:n_tok, :] + member[n_tok - 1:n_tok, :]
    padded = jnp.floor((cnt + (rows_blk - 1)) * (1.0 / rows_blk)) * rows_blk
    start = _dot(jnp.broadcast_to(padded, (SUBLANES, LANES)).astype(BF16), etri_ref[...])[0:1, :]
    slot = jnp.where(member > 0.0, start + pos, -1.0)
    slot_ref[...] = slot
    slot_t_ref[0] = jnp.transpose(slot)[0:SUBLANES, :]
    ends = start + padded
    first_row = lax.broadcasted_iota(jnp.int32, (1, LANES), 1).astype(F32) * rows_blk
    blk_exp = jnp.zeros((1, LANES), F32)
    for e in range(n_exp - 1):
        blk_exp = blk_exp + jnp.where(ends[:, e:e + 1] <= first_row, 1.0, 0.0)
    n_blk = ends[:, n_exp - 1:n_exp] * (1.0 / rows_blk)
    row = lax.broadcasted_iota(jnp.int32, (SUBLANES, LANES), 0)
    meta = jnp.where(row == 0, blk_exp, jnp.where(row == 1, n_blk, 0.0))
    meta_ref[0] = meta.astype(jnp.int32)


def _moe_experts_body(be_ref, nb_ref, xn_ref, slot_ref, slot_t_ref, comb_ref, x_ref, wg_ref, wu_ref, wd_ref,
                      g2_ref, o_ref, xs_scr, ys_scr, *, rows_blk, nf):
    g, f, j = pl.program_id(0), pl.program_id(1), pl.program_id(2)
    nj = pl.num_programs(2)
    e = be_ref[g, j]
    used = j < nb_ref[g]
    n_tok = xn_ref.shape[0]
    rows = pl.ds(pl.multiple_of(j * rows_blk, rows_blk), rows_blk)
    first = j * rows_blk

    @pl.when((f == 0) & (j == 0))
    def _():
        o_ref[...] = x_ref[...]

    @pl.when(used & (f == 0))
    def _():
        srow = slot_t_ref[0, pl.ds(e, 1), :]
        want = (first + lax.broadcasted_iota(jnp.int32, (rows_blk, n_tok), 0)).astype(F32)
        onehot = jnp.where(srow == want, 1.0, 0.0).astype(BF16)
        xs_scr[rows, :] = _dot(onehot, xn_ref[...]).astype(BF16)

    @pl.when(used)
    def _():
        xs = xs_scr[rows, :]
        gate = _dot(xs, wg_ref[0])
        h = gate * jax.nn.sigmoid(gate) * _dot(xs, wu_ref[0])
        part = _dot(h.astype(BF16), wd_ref[0])

        if nf > 1:
            @pl.when(f == 0)
            def _():
                ys_scr[rows, :] = part.astype(BF16)

        if nf > 2:
            @pl.when((f > 0) & (f < nf - 1))
            def _():
                ys_scr[rows, :] = (ys_scr[rows, :].astype(F32) + part).astype(BF16)

        def scatter(total):
            lane = lax.broadcasted_iota(jnp.int32, (n_tok, LANES), 1)
            scol = jnp.sum(jnp.where(lane == e, slot_ref[...], 0.0), axis=1, keepdims=True)
            ccol = jnp.sum(jnp.where(lane == e, comb_ref[...], 0.0), axis=1, keepdims=True)
            want = (first + lax.broadcasted_iota(jnp.int32, (n_tok, rows_blk), 1)).astype(F32)
            onehot = jnp.where(scol == want, 1.0, 0.0).astype(BF16)
            tb = total.astype(BF16)
            for c in range(D_MODEL // (2 * LANES)):
                cols = slice(c * 2 * LANES, (c + 1) * 2 * LANES)
                o_ref[:, cols] += ccol * _dot(onehot, tb[:, cols])

        @pl.when(f == nf - 1)
        def _():
            scatter(part + ys_scr[rows, :].astype(F32) if nf > 1 else part)

    @pl.when((f == nf - 1) & (j == nj - 1))
    def _():
        o_ref[...] = _rms(o_ref[...], g2_ref[...])


def _moe(x, g, wr, wg, wu, wd, g2, tg, rows_blk, tf):
    n = x.shape[0]
    n_exp = wg.shape[0]
    n_groups = n // tg
    n_blk = 2 * tg // rows_blk + n_exp
    i = np.arange(tg)
    tri = jnp.asarray((i[None, :] < i[:, None]).astype(np.float32), BF16)
    i = np.arange(LANES)
    etri = jnp.asarray((i[:, None] < i[None, :]).astype(np.float32), BF16)
    row = lambda i: (i, 0)
    const = lambda i: (0, 0)
    xn, slot, slot_t, comb, meta = pl.pallas_call(
        functools.partial(_moe_route_body, n_exp=n_exp, rows_blk=rows_blk),
        grid=(n_groups,),
        in_specs=[pl.BlockSpec((tg, D_MODEL), row), pl.BlockSpec((1, D_MODEL), const),
                  pl.BlockSpec((D_MODEL, LANES), const), pl.BlockSpec((tg, tg), const),
                  pl.BlockSpec((LANES, LANES), const)],
        out_specs=[pl.BlockSpec((tg, D_MODEL), row), pl.BlockSpec((tg, LANES), row),
                   pl.BlockSpec((1, SUBLANES, tg), lambda i: (i, 0, 0)), pl.BlockSpec((tg, LANES), row),
                   pl.BlockSpec((1, SUBLANES, LANES), lambda i: (i, 0, 0))],
        out_shape=[jax.ShapeDtypeStruct((n, D_MODEL), BF16), jax.ShapeDtypeStruct((n, LANES), F32),
                   jax.ShapeDtypeStruct((n_groups, SUBLANES, tg), F32), jax.ShapeDtypeStruct((n, LANES), F32),
                   jax.ShapeDtypeStruct((n_groups, SUBLANES, LANES), jnp.int32)],
        compiler_params=_cparams(("parallel",)),
        name="moe_route",
    )(x, g, wr, tri, etri)
    blk_exp = meta[:, 0, :]
    blk_cnt = meta[:, 1, 0]
    per_g = lambda g, f, j, be, nb: (g, 0)
    once = pl.Buffered(1)
    return pl.pallas_call(
        functools.partial(_moe_experts_body, rows_blk=rows_blk, nf=D_FF // tf),
        grid_spec=pltpu.PrefetchScalarGridSpec(
            num_scalar_prefetch=2,
            grid=(n_groups, D_FF // tf, n_blk),
            in_specs=[pl.BlockSpec((tg, D_MODEL), per_g, pipeline_mode=once),
                      pl.BlockSpec((tg, LANES), per_g, pipeline_mode=once),
                      pl.BlockSpec((1, SUBLANES, tg), lambda g, f, j, be, nb: (g, 0, 0), pipeline_mode=once),
                      pl.BlockSpec((tg, LANES), per_g, pipeline_mode=once),
                      pl.BlockSpec((tg, D_MODEL), per_g, pipeline_mode=once),
                      pl.BlockSpec((1, D_MODEL, tf), lambda g, f, j, be, nb: (be[g, j], 0, f)),
                      pl.BlockSpec((1, D_MODEL, tf), lambda g, f, j, be, nb: (be[g, j], 0, f)),
                      pl.BlockSpec((1, tf, D_MODEL), lambda g, f, j, be, nb: (be[g, j], f, 0)),
                      pl.BlockSpec((1, D_MODEL), lambda g, f, j, be, nb: (0, 0))],
            out_specs=pl.BlockSpec((tg, D_MODEL), per_g, pipeline_mode=once),
            scratch_shapes=[pltpu.VMEM((n_blk * rows_blk, D_MODEL), BF16),
                            pltpu.VMEM((n_blk * rows_blk, D_MODEL), BF16)],
        ),
        out_shape=jax.ShapeDtypeStruct((n, D_MODEL), F32),
        compiler_params=_cparams(("parallel", "arbitrary", "arbitrary")),
        name="moe_experts",
    )(blk_exp, blk_cnt, xn, slot, slot_t, comb, x, wg, wu, wd, g2)


def _group_ones():
    i = np.arange(LANES) // RWKV_HEAD
    return jnp.asarray((i[:, None] == i[None, :]).astype(np.float32), BF16)


def _rwkv_proj_body(xn_ref, xp_ref, sh_ref, mu_ref, vec_ref, wrkv_ref, w1_ref, w2_ref, a1_ref, a2_ref, g1_ref,
                    g2_ref, gm_ref, r_ref, lw_ref, k_ref, v_ref, kk_ref, ka_ref, g_ref, *, tiles_per_seq):
    xn = xn_ref[...]
    if tiles_per_seq is None:
        xp = xp_ref[...]
    else:
        first = pl.program_id(0) % tiles_per_seq == 0
        row0 = jnp.where(first, sh_ref[0], xp_ref[SUBLANES - 1:SUBLANES, :])
        r = lax.broadcasted_iota(jnp.int32, xn.shape, 0)
        xp = jnp.where(r == 0, row0, pltpu.roll(xn, 1, 0))
    xx = xp - xn
    mu = mu_ref[...]
    mix = lambda i: (xn + xx * mu[i:i + 1, :]).astype(BF16)
    vec = vec_ref[...]
    w0, a0, k_k, k_a = vec[0:1, :], vec[1:2, :], vec[2:3, :], vec[3:4, :]
    r_ref[...] = _dot(mix(0), wrkv_ref[0]).astype(r_ref.dtype)
    z = w0 + _dot(jnp.tanh(_dot(mix(1), w1_ref[...])).astype(BF16), w2_ref[...])
    softplus = jnp.maximum(-z, 0.0) + jnp.log(1.0 + jnp.exp(-jnp.abs(z)))
    lw_ref[...] = -jnp.exp(-softplus - 0.5)
    k = _dot(mix(2), wrkv_ref[1])
    v_ref[...] = _dot(mix(3), wrkv_ref[2]).astype(v_ref.dtype)
    a = jax.nn.sigmoid(a0 + _dot(_dot(mix(4), a1_ref[...]).astype(BF16), a2_ref[...]))
    g_ref[...] = _dot(jax.nn.sigmoid(_dot(mix(5), g1_ref[...])).astype(BF16), g2_ref[...]).astype(g_ref.dtype)
    kk = k * k_k
    kk = kk * lax.rsqrt(jnp.maximum(_group_sum(kk * kk, gm_ref[...]), 1e-24))
    kk_ref[...] = kk.astype(kk_ref.dtype)
    ka_ref[...] = (kk * a).astype(ka_ref.dtype)
    k_ref[...] = (k * (1.0 + (a - 1.0) * k_a)).astype(k_ref.dtype)


def _rwkv_proj(xn, xp, shift_prev, seq, mu, vec, wrkv, w1, w2, a1, a2, g1, g2, tm):
    n = xn.shape[0]
    row = lambda i: (i, 0)
    c2 = lambda i: (0, 0)
    c3 = lambda i: (0, 0, 0)
    full = lambda a: pl.BlockSpec(a.shape, c3 if a.ndim == 3 else c2)
    gm = _group_ones()
    consts = [mu, vec, wrkv, w1, w2, a1, a2, g1, g2, gm]
    if xp is None:
        tiles_per_seq = seq // tm
        prev_spec = pl.BlockSpec((SUBLANES, D_MODEL), lambda i: (jnp.maximum(i * (tm // SUBLANES) - 1, 0), 0))
        shift_spec = pl.BlockSpec((1, 1, D_MODEL), lambda i: (i // tiles_per_seq, 0, 0))
        xp = xn
    else:
        tiles_per_seq = None
        prev_spec = pl.BlockSpec((tm, D_MODEL), row)
        shift_spec = pl.BlockSpec((1, 1, D_MODEL), c3)
    dtypes = (BF16, F32, BF16, BF16, BF16, BF16, BF16)
    return pl.pallas_call(
        functools.partial(_rwkv_proj_body, tiles_per_seq=tiles_per_seq),
        grid=(n // tm,),
        in_specs=[pl.BlockSpec((tm, D_MODEL), row), prev_spec, shift_spec] + [full(a) for a in consts],
        out_specs=[pl.BlockSpec((tm, D_MODEL), row)] * 7,
        out_shape=[jax.ShapeDtypeStruct((n, D_MODEL), dt) for dt in dtypes],
        compiler_params=_cparams(("parallel",)),
        name="rwkv_proj",
    )(xn, xp, shift_prev[:, None, :], *consts)


def _wkv_body(r_ref, lw_ref, k_ref, v_ref, kk_ref, ka_ref, s0_ref, y_ref, sT_ref,
              s_scr, a_scr, r_scr, b_scr, k_scr, *, chunk):
    c = pl.program_id(1)
    n_pairs = D_MODEL // LANES
    hd = RWKV_HEAD
    zeros = jnp.zeros((hd, hd), F32)

    @pl.when(c == 0)
    def _():
        for p in range(n_pairs):
            top = jnp.concatenate([s0_ref[0, 2 * p], zeros], axis=1)
            bot = jnp.concatenate([zeros, s0_ref[0, 2 * p + 1]], axis=1)
            s_scr[p] = jnp.concatenate([top, bot], axis=0)

    row = lax.broadcasted_iota(jnp.int32, (chunk, chunk), 0)
    colm = lax.broadcasted_iota(jnp.int32, (chunk, chunk), 1)
    lw = lw_ref[...]
    cum = jnp.dot(jnp.where(row >= colm, 1.0, 0.0), lw, preferred_element_type=F32, precision=HI)
    w_incl = jnp.exp(cum)
    w_inv = jnp.exp(-cum)
    a_scr[...] = -kk_ref[...] * jnp.exp(cum - lw)
    r_scr[...] = r_ref[...] * w_incl
    b_scr[...] = ka_ref[...] * w_inv
    k_scr[...] = k_ref[...] * w_inv
    w_last = w_incl[chunk - 1:chunk, :]
    n_dbl = max(1, int(np.ceil(np.log2(chunk))))

    c2 = 2 * chunk
    head_of_row = lax.broadcasted_iota(jnp.int32, (c2, LANES), 0) // chunk
    head_of_lane = lax.broadcasted_iota(jnp.int32, (c2, LANES), 1) // hd
    own_lanes = head_of_row == head_of_lane
    tr = lax.broadcasted_iota(jnp.int32, (c2, c2), 0)
    tc = lax.broadcasted_iota(jnp.int32, (c2, c2), 1)
    same = (tr // chunk) == (tc // chunk)
    strict = same & (tr % chunk > tc % chunk)
    incl = same & (tr % chunk >= tc % chunk)

    def block_diag(x):
        return jnp.where(own_lanes, jnp.concatenate([x, x], axis=0), 0.0).astype(BF16)

    pairs = range(n_pairs)
    sls = [slice(p * LANES, (p + 1) * LANES) for p in pairs]
    ar = [jnp.concatenate([block_diag(a_scr[:, sl]), block_diag(r_scr[:, sl])], axis=0) for sl in sls]
    bk = [jnp.concatenate([block_diag(b_scr[:, sl]), block_diag(k_scr[:, sl])], axis=0) for sl in sls]
    vm = [block_diag(v_ref[:, sl]) for sl in sls]
    s_old = [s_scr[p] for p in pairs]
    gram = [_dot_nt(ar[p], bk[p]) for p in pairs]
    xs = [_dot_nt(ar[p], s_old[p].astype(BF16)) for p in pairs]
    u = [xs[p][:c2, :] + _dot(jnp.where(strict, gram[p][:c2, c2:], 0.0).astype(BF16), vm[p]) for p in pairs]
    lp = [jnp.where(strict, gram[p][:c2, :c2], 0.0).astype(BF16) for p in pairs]
    for d in range(n_dbl):
        if d + 1 < n_dbl:
            t = [_dot(lp[p], jnp.concatenate([u[p].astype(BF16), lp[p]], axis=1)) for p in pairs]
            u = [u[p] + t[p][:, :LANES] for p in pairs]
            lp = [t[p][:, LANES:].astype(BF16) for p in pairs]
        else:
            u = [u[p] + _dot(lp[p], u[p].astype(BF16)) for p in pairs]
    uv = [jnp.concatenate([u[p].astype(BF16), vm[p]], axis=0) for p in pairs]
    m_r = [jnp.concatenate([jnp.where(incl, gram[p][c2:, :c2], 0.0), jnp.where(incl, gram[p][c2:, c2:], 0.0)],
                           axis=1).astype(BF16) for p in pairs]
    y = [xs[p][c2:, :] + _dot(m_r[p], uv[p]) for p in pairs]
    for p in pairs:
        y_ref[:, sls[p]] = y[p][:chunk, :] + y[p][chunk:, :]
    s_new = [(s_old[p] + _dot_tn(uv[p], bk[p])) * w_last[:, sls[p]] for p in pairs]
    for p in pairs:
        s_scr[p] = s_new[p]

    @pl.when(c == pl.num_programs(1) - 1)
    def _():
        for p in range(n_pairs):
            s = s_scr[p]
            sT_ref[0, 2 * p] = s[:hd, :hd]
            sT_ref[0, 2 * p + 1] = s[hd:, hd:]


def _wkv(r, lw, k, v, kk, ka, s0, batch, seq, chunk):
    nc = seq // chunk
    blk = lambda b, c: (b * nc + c, 0)
    st = lambda b, c: (b, 0, 0, 0)
    state = pl.BlockSpec((1, RWKV_HEADS, RWKV_HEAD, RWKV_HEAD), st)
    return pl.pallas_call(
        functools.partial(_wkv_body, chunk=chunk),
        grid=(batch, nc),
        in_specs=[pl.BlockSpec((chunk, D_MODEL), blk)] * 6 + [state],
        out_specs=[pl.BlockSpec((chunk, D_MODEL), blk), state],
        out_shape=[jax.ShapeDtypeStruct((batch * seq, D_MODEL), F32),
                   jax.ShapeDtypeStruct(s0.shape, F32)],
        scratch_shapes=[pltpu.VMEM((D_MODEL // LANES, LANES, LANES), F32)]
                       + [pltpu.VMEM((chunk, D_MODEL), F32)] * 4,
        compiler_params=_cparams(("parallel", "arbitrary")),
        name="wkv",
    )(r, lw, k, v, kk, ka, s0)


def _rwkv_out_body(x_ref, y_ref, r_ref, k_ref, v_ref, g_ref, vec_ref, gm_ref, wo_ref, o_ref):
    gm = gm_ref[...]
    vec = vec_ref[...]
    ln_w, ln_b, r_k = vec[0:1, :], vec[1:2, :], vec[2:3, :]
    y = y_ref[...]
    mean = _group_sum(y, gm) * (1.0 / RWKV_HEAD)
    d = y - mean
    var = _group_sum(d * d, gm) * (1.0 / RWKV_HEAD)
    yn = d * lax.rsqrt(var + GN_EPS) * ln_w + ln_b
    rk = r_ref[...].astype(F32) * k_ref[...].astype(F32) * r_k
    yn = yn + _group_sum(rk, gm) * v_ref[...].astype(F32)
    o_ref[...] = x_ref[...] + _dot((yn * g_ref[...].astype(F32)).astype(BF16), wo_ref[...])


def _rwkv_out(x, y, r, k, v, g, vec, wo, tm):
    n = x.shape[0]
    row = lambda i: (i, 0)
    const = lambda i: (0, 0)
    gm = _group_ones()
    return pl.pallas_call(
        _rwkv_out_body,
        grid=(n // tm,),
        in_specs=[pl.BlockSpec((tm, D_MODEL), row)] * 6
                 + [pl.BlockSpec(vec.shape, const), pl.BlockSpec(gm.shape, const), pl.BlockSpec(wo.shape, const)],
        out_specs=pl.BlockSpec((tm, D_MODEL), row),
        out_shape=jax.ShapeDtypeStruct((n, D_MODEL), F32),
        compiler_params=_cparams(("parallel",)),
        name="rwkv_out",
    )(x, y, r, k, v, g, vec, gm, wo)


def _pick_tile(n, want):
    t = min(n, want)
    while n % t:
        t //= 2
    return t


def _trunk(x, pos, conv_prev, shift_prev, wkv_prev, wts, sample):
    batch, seq, _ = x.shape
    n = batch * seq
    xf = x.reshape(n, D_MODEL)
    tm = _pick_tile(n, 512)

    tabs = _rope_tables(pos)
    if sample is not None:
        tabs = tuple(jnp.tile(t, (batch, 1)) for t in tabs)
    q, k, v, iq, ikw, cb, u = _proj_even(xf, wts['norm_mix_even'], wts['w_in'], tabs, tm)
    if sample is None:
        attn = _prompt_attention(q, iq, ikw, k, v, batch, seq)
        conv = _conv_prompt(cb, u, wts['conv_w'], seq, _pick_tile(seq, 512))
        conv_state = u.reshape(batch, seq, CONV_DIM)[:, seq - (CONV_W - 1):]
    else:
        cache_k, cache_v, cache_ik, page_table = sample
        tpad = SUBLANES - seq

        def stack(a, heads):
            a = a.reshape(batch, seq, heads, -1).transpose(0, 2, 1, 3)
            a = jnp.pad(a, ((0, 0), (0, 0), (0, tpad), (0, 0)))
            return a.reshape(batch, heads * SUBLANES, a.shape[-1])

        iq_st = stack(iq, N_IDX_HEADS)
        iw_st = stack(ikw[:, IDX_DIM:IDX_DIM + N_IDX_HEADS], N_IDX_HEADS)
        s_past = _sample_scores(page_table, iq_st, iw_st, cache_ik)[:, :seq]
        mask = _sample_select(s_past.reshape(n, -1), iq, ikw, seq)
        mask = jnp.pad(mask.reshape(batch, seq, -1), ((0, 0), (0, tpad), (0, 0)))
        q_st = stack(q, N_Q_HEADS).reshape(batch, N_KV_HEADS, Q_PER_KV * SUBLANES, HEAD_DIM)
        padk = lambda a: jnp.pad(jnp.swapaxes(a.reshape(batch, seq, KV_DIM), 1, 2),
                                 ((0, 0), (0, 0), (0, PAGE_SIZE - seq)))
        o = _sample_attention(page_table, q_st, mask, padk(k), padk(v), cache_k, cache_v)
        attn = (o.reshape(batch, N_KV_HEADS, Q_PER_KV, SUBLANES, HEAD_DIM)[:, :, :, :seq].transpose(0, 3, 2, 1, 4)
                .reshape(n, ATTN_DIM).astype(BF16))
        u3 = jnp.concatenate([conv_prev, u.reshape(batch, seq, CONV_DIM)], axis=1)
        conv_state = u3[:, seq:]
        conv_t = _conv_sample(jnp.swapaxes(cb.reshape(batch, seq, CONV_DIM), 0, 1), jnp.swapaxes(u3, 0, 1),
                              wts['conv_w'])
        conv = jnp.swapaxes(conv_t, 0, 1).reshape(n, CONV_DIM)
    x1 = _out_even(xf, attn, conv, wts['w_out_attn'], wts['w_out_conv'], tm)
    x2, xn2 = _mixer(x1, wts['norm_ffn_even'], None, wts['ffn_gate'], wts['ffn_up'], wts['ffn_down'],
                     wts['norm_mix_odd'], tm, D_FF // 2, True)

    xn3 = xn2.reshape(batch, seq, D_MODEL)
    tm1 = _pick_tile(n, 256)
    xp = None
    if seq % tm1:
        xp = jnp.concatenate([shift_prev[:, None, :], xn3[:, :-1]], axis=1).reshape(n, D_MODEL)
    r, lw, kr, vr, kk, ka, g = _rwkv_proj(xn2, xp, shift_prev, seq, wts['rwkv_mu'], wts['rwkv_vec_in'],
                                          wts['rwkv_w_rkv'], wts['rwkv_w1'], wts['rwkv_w2'], wts['rwkv_a1'],
                                          wts['rwkv_a2'], wts['rwkv_g1'], wts['rwkv_g2'], tm1)
    if seq % WKV_CHUNK:
        sp = -(-seq // WKV_CHUNK) * WKV_CHUNK
        padded = [jnp.pad(a.reshape(batch, seq, D_MODEL), ((0, 0), (0, sp - seq), (0, 0)))
                  .reshape(batch * sp, D_MODEL) for a in (r, lw, kr, vr, kk, ka)]
        y, wkv_state = _wkv(*padded, wkv_prev, batch, sp, WKV_CHUNK)
        y = y.reshape(batch, sp, D_MODEL)[:, :seq].reshape(n, D_MODEL)
    else:
        y, wkv_state = _wkv(r, lw, kr, vr, kk, ka, wkv_prev, batch, seq, WKV_CHUNK)
    x3 = _rwkv_out(x2, y, r, kr, vr, g, wts['rwkv_vec_out'], wts['rwkv_w_o'], _pick_tile(n, 256))
    yf = _moe(x3, wts['norm_ffn_odd'], wts['moe_router'], wts['moe_gate'], wts['moe_up'], wts['moe_down'],
              wts['norm_final'], _pick_tile(n, MOE_GROUP), min(MOE_BLOCK_ROWS, _pick_tile(n, MOE_GROUP)), D_FF // 2)

    k4 = k.reshape(1, batch, seq, N_KV_HEADS, HEAD_DIM)
    v4 = v.reshape(1, batch, seq, N_KV_HEADS, HEAD_DIM)
    ik3 = ikw[:, :IDX_DIM].reshape(1, batch, seq, IDX_DIM)
    return (yf.reshape(batch, seq, D_MODEL), k4, v4, ik3, conv_state[None],
            xn3[:, -1][None], wkv_state[None])


def _prepare_weights(norm_mix_even, w_in_even, conv_w, w_out_even, norm_ffn_even, ffn_gate, ffn_up, ffn_down,
                     norm_mix_odd, rwkv_mu, rwkv_w_rkv, rwkv_w0, rwkv_w1, rwkv_w2, rwkv_a0, rwkv_a1, rwkv_a2,
                     rwkv_g1, rwkv_g2, rwkv_k_k, rwkv_k_a, rwkv_r_k, rwkv_ln_w, rwkv_ln_b, rwkv_w_o,
                     norm_ffn_odd, moe_router, moe_gate, moe_up, moe_down, norm_final):
    w_in = w_in_even[0]
    o = np.cumsum((0, ATTN_DIM, KV_DIM, KV_DIM, N_IDX_HEADS * IDX_DIM, IDX_DIM, N_IDX_HEADS,
                   CONV_DIM, CONV_DIM, CONV_DIM))
    pad = jnp.zeros((D_MODEL, LANES - IDX_DIM - N_IDX_HEADS), F32)
    w_in = jnp.concatenate([w_in[:, :o[4]], w_in[:, o[4]:o[6]], pad, w_in[:, o[6]:]], axis=1).astype(BF16)
    row = lambda a: a.reshape(1, -1)
    zeros = jnp.zeros((1, D_MODEL), F32)
    return dict(
        norm_mix_even=row(norm_mix_even[0]), w_in=w_in, conv_w=conv_w[0],
        w_out_attn=(w_out_even[0, :ATTN_DIM].reshape(N_KV_HEADS, Q_PER_KV, HEAD_DIM, D_MODEL)
                    .transpose(1, 0, 2, 3).reshape(ATTN_DIM, D_MODEL).astype(BF16)),
        w_out_conv=w_out_even[0, ATTN_DIM:].astype(BF16),
        norm_ffn_even=row(norm_ffn_even[0]),
        ffn_gate=ffn_gate.astype(BF16), ffn_up=ffn_up.astype(BF16), ffn_down=ffn_down.astype(BF16),
        norm_mix_odd=row(norm_mix_odd[0]), rwkv_mu=jnp.concatenate([rwkv_mu[0], zeros, zeros], axis=0),
        rwkv_vec_in=jnp.concatenate([row(rwkv_w0[0]), row(rwkv_a0[0]), row(rwkv_k_k[0]), row(rwkv_k_a[0]),
                                     zeros, zeros, zeros, zeros], axis=0),
        rwkv_w_rkv=rwkv_w_rkv[0].astype(BF16),
        rwkv_w1=rwkv_w1[0].astype(BF16), rwkv_w2=rwkv_w2[0].astype(BF16),
        rwkv_a1=rwkv_a1[0].astype(BF16), rwkv_a2=rwkv_a2[0].astype(BF16),
        rwkv_g1=rwkv_g1[0].astype(BF16), rwkv_g2=rwkv_g2[0].astype(BF16),
        rwkv_vec_out=jnp.concatenate([row(rwkv_ln_w[0]), row(rwkv_ln_b[0]), row(rwkv_r_k[0]),
                                      zeros, zeros, zeros, zeros, zeros], axis=0),
        rwkv_w_o=rwkv_w_o[0].astype(BF16),
        norm_ffn_odd=row(norm_ffn_odd[0]),
        moe_router=jnp.pad(moe_router[0], ((0, 0), (0, LANES - N_EXPERTS))).astype(BF16),
        moe_gate=moe_gate[0].astype(BF16), moe_up=moe_up[0].astype(BF16), moe_down=moe_down[0].astype(BF16),
        norm_final=row(norm_final),
    )


def kernel(x_prompt, x_sample, cache_k, cache_v, cache_idx_k, state_conv, state_shift, state_wkv, page_table, norm_mix_even, w_in_even, conv_w, w_out_even, norm_ffn_even, ffn_gate, ffn_up, ffn_down, norm_mix_odd, rwkv_mu, rwkv_w_rkv, rwkv_w0, rwkv_w1, rwkv_w2, rwkv_a0, rwkv_a1, rwkv_a2, rwkv_g1, rwkv_g2, rwkv_k_k, rwkv_k_a, rwkv_r_k, rwkv_ln_w, rwkv_ln_b, rwkv_w_o, norm_ffn_odd, moe_router, moe_gate, moe_up, moe_down, norm_final):
    assert w_in_even.shape[0] == 1 and rwkv_mu.shape[0] == 1, "one even and one odd layer"
    wts = _prepare_weights(norm_mix_even, w_in_even, conv_w, w_out_even, norm_ffn_even, ffn_gate, ffn_up, ffn_down,
                           norm_mix_odd, rwkv_mu, rwkv_w_rkv, rwkv_w0, rwkv_w1, rwkv_w2, rwkv_a0, rwkv_a1, rwkv_a2,
                           rwkv_g1, rwkv_g2, rwkv_k_k, rwkv_k_a, rwkv_r_k, rwkv_ln_w, rwkv_ln_b, rwkv_w_o,
                           norm_ffn_odd, moe_router, moe_gate, moe_up, moe_down, norm_final)
    b, t = x_prompt.shape[:2]
    bd, tn = x_sample.shape[:2]
    n_pool = cache_k.shape[1]
    past = page_table.shape[1] * PAGE_SIZE
    pos_prompt = jnp.arange(t, dtype=jnp.int32)
    pos_sample = past + jnp.arange(tn, dtype=jnp.int32)
    zeros = lambda *s: jnp.zeros(s, F32)
    out_p = _trunk(x_prompt, pos_prompt, zeros(b, CONV_W - 1, CONV_DIM), zeros(b, D_MODEL),
                   zeros(b, RWKV_HEADS, RWKV_HEAD, RWKV_HEAD), wts, None)
    pages_t = lambda c: jnp.transpose(c[0], (0, 2, 3, 1)).reshape(n_pool, KV_DIM, PAGE_SIZE)
    sample = (pages_t(cache_k), pages_t(cache_v), jnp.swapaxes(cache_idx_k[0], 1, 2), page_table)
    out_s = _trunk(x_sample, pos_sample, state_conv[0], state_shift[0], state_wkv[0], wts, sample)
    y_p, k_p, v_p, ik_p, conv_p, shift_p, wkv_p = out_p
    y_s, k_s, v_s, ik_s, conv_s, shift_s, wkv_s = out_s
    return (y_p, y_s, k_p, v_p, ik_p, k_s, v_s, ik_s, conv_p, conv_s, shift_p, shift_s, wkv_p, wkv_s)
```

```python
import functools

import numpy as np
import jax
import jax.numpy as jnp
from jax import lax
from jax.experimental import pallas as pl
from jax.experimental.pallas import tpu as pltpu

F32 = jnp.float32
BF16 = jnp.bfloat16
HI = lax.Precision.HIGHEST

D_MODEL = 1024
PAGE_SIZE = 128
HEAD_DIM = 64
N_Q_HEADS = 8
N_KV_HEADS = 2
Q_PER_KV = N_Q_HEADS // N_KV_HEADS
ROT_DIM = HEAD_DIM // 4
ROPE_THETA = 500000.0
N_IDX_HEADS = 4
IDX_DIM = 64
TOPK_MAX = 256
Q_BLOCK = 128
ATTN_DIM = N_Q_HEADS * HEAD_DIM
KV_DIM = N_KV_HEADS * HEAD_DIM
CONV_DIM = D_MODEL // 2
CONV_W = 3
RWKV_HEAD = 64
RWKV_HEADS = D_MODEL // RWKV_HEAD
GN_EPS = 64e-5
D_FF = 2816
N_EXPERTS = 8
RMS_EPS = 1e-6

LANES = 128
SUBLANES = 8
VMEM_LIMIT = 56 * 1024 * 1024
VMEM_LIMIT_MOE = 61 * 1024 * 1024
INT_MIN = -2 ** 31
KEY_NEG_INF = INT_MIN + 0x7FFFFF
PROJ_COLS = ATTN_DIM + KV_DIM + KV_DIM + N_IDX_HEADS * IDX_DIM + LANES + 3 * CONV_DIM
WKV_CHUNK = 64
MOE_GROUP = 1024
MOE_BLOCK_ROWS = 288


def _cparams(sem):
    return pltpu.CompilerParams(dimension_semantics=sem, vmem_limit_bytes=VMEM_LIMIT)


def _rms(x, g):
    return x * lax.rsqrt(jnp.mean(x * x, axis=-1, keepdims=True) + RMS_EPS) * g


def _dot(a, b):
    return jnp.dot(a, b, preferred_element_type=F32)


def _dot_nt(a, b, precision=None):
    return lax.dot_general(a, b, (((1,), (1,)), ((), ())), preferred_element_type=F32, precision=precision)


def _dot_tn(a, b, precision=None):
    return lax.dot_general(a, b, (((0,), (0,)), ((), ())), preferred_element_type=F32, precision=precision)


def _group_sum(x, gmat):
    outs = []
    for c in range(x.shape[1] // LANES):
        xc = x[:, c * LANES:(c + 1) * LANES]
        hi = xc.astype(BF16)
        lo = (xc - hi.astype(F32)).astype(BF16)
        outs.append(_dot(hi, gmat) + _dot(lo, gmat))
    return jnp.concatenate(outs, axis=1)


def _rope_chunk(xc, rc, rp, rm):
    return xc * rc + pltpu.roll(xc, 8, 1) * rp + pltpu.roll(xc, LANES - 8, 1) * rm


def _proj_even_body(x_ref, g_ref, w_ref, rc_ref, rp_ref, rm_ref,
                    q_ref, k_ref, v_ref, iq_ref, ikw_ref, cb_ref, u_ref):
    xn = _rms(x_ref[...], g_ref[...]).astype(BF16)
    h = _dot(xn, w_ref[...])
    rc, rp, rm = rc_ref[...], rp_ref[...], rm_ref[...]
    col = 0
    for c in range(ATTN_DIM // LANES):
        q_ref[:, c * LANES:(c + 1) * LANES] = _rope_chunk(h[:, col:col + LANES], rc, rp, rm).astype(BF16)
        col += LANES
    k_ref[...] = _rope_chunk(h[:, col:col + LANES], rc, rp, rm)
    col += LANES
    v_ref[...] = h[:, col:col + LANES]
    col += LANES
    for c in range(N_IDX_HEADS * IDX_DIM // LANES):
        iq_ref[:, c * LANES:(c + 1) * LANES] = _rope_chunk(h[:, col:col + LANES], rc, rp, rm).astype(BF16)
        col += LANES
    ikw = h[:, col:col + LANES]
    lane = lax.broadcasted_iota(jnp.int32, ikw.shape, 1)
    ikw_ref[...] = jnp.where(lane < IDX_DIM, _rope_chunk(ikw, rc, rp, rm), ikw)
    col += LANES
    cb_ref[...] = h[:, col:col + CONV_DIM]
    col += CONV_DIM
    u_ref[...] = h[:, col:col + CONV_DIM] * h[:, col + CONV_DIM:col + 2 * CONV_DIM]


def _proj_even(x, g, w, tabs, tm):
    n = x.shape[0]
    nt = tabs[0].shape[0] // tm
    row = lambda i: (i, 0)
    const = lambda i: (0, 0)
    tab = lambda i: (i % nt, 0)
    widths = (ATTN_DIM, KV_DIM, KV_DIM, N_IDX_HEADS * IDX_DIM, LANES, CONV_DIM, CONV_DIM)
    dtypes = (BF16, F32, F32, BF16, F32, F32, F32)
    return pl.pallas_call(
        _proj_even_body,
        grid=(n // tm,),
        in_specs=[pl.BlockSpec((tm, D_MODEL), row), pl.BlockSpec((1, D_MODEL), const),
                  pl.BlockSpec((D_MODEL, PROJ_COLS), const)] + [pl.BlockSpec((tm, LANES), tab)] * 3,
        out_specs=[pl.BlockSpec((tm, wd), row) for wd in widths],
        out_shape=[jax.ShapeDtypeStruct((n, wd), dt) for wd, dt in zip(widths, dtypes)],
        compiler_params=_cparams(("parallel",)),
        name="proj_even",
    )(x, g, w, *tabs)


def _rope_tables(pos):
    half = ROT_DIM // 2
    inv_freq = ROPE_THETA ** (-jnp.arange(half, dtype=F32) / half)
    ang = pos.astype(F32)[:, None] * inv_freq[None, :]
    cos, sin = jnp.cos(ang), jnp.sin(ang)
    t = pos.shape[0]
    pad = jnp.zeros((t, HEAD_DIM - ROT_DIM), F32)
    zero = jnp.zeros((t, half), F32)
    rc = jnp.concatenate([cos, cos, pad + 1.0], axis=1)
    rp = jnp.concatenate([zero, sin, pad], axis=1)
    rm = jnp.concatenate([-sin, zero, pad], axis=1)
    return tuple(jnp.tile(a, (1, LANES // HEAD_DIM)) for a in (rc, rp, rm))


def _conv_prompt_body(cb_ref, u_ref, up_ref, w_ref, y_ref, *, tiles_per_seq):
    u = u_ref[...]
    first = pl.program_id(0) % tiles_per_seq == 0
    prev = jnp.where(first, 0.0, up_ref[...])
    p1, p2 = prev[SUBLANES - 1:SUBLANES, :], prev[SUBLANES - 2:SUBLANES - 1, :]
    r = lax.broadcasted_iota(jnp.int32, u.shape, 0)
    u1 = jnp.where(r == 0, p1, pltpu.roll(u, 1, 0))
    u2 = jnp.where(r == 0, p2, jnp.where(r == 1, p1, pltpu.roll(u, 2, 0)))
    w = w_ref[...]
    y_ref[...] = cb_ref[...] * (w[0:1, :] * u2 + w[1:2, :] * u1 + w[2:3, :] * u)


def _conv_prompt(cb, u, w, seq, tc):
    n = u.shape[0]
    row = lambda i: (i, 0)
    prev = lambda i: (jnp.maximum(i * (tc // SUBLANES) - 1, 0), 0)
    return pl.pallas_call(
        functools.partial(_conv_prompt_body, tiles_per_seq=seq // tc),
        grid=(n // tc,),
        in_specs=[pl.BlockSpec((tc, CONV_DIM), row), pl.BlockSpec((tc, CONV_DIM), row),
                  pl.BlockSpec((SUBLANES, CONV_DIM), prev), pl.BlockSpec((CONV_W, CONV_DIM), lambda i: (0, 0))],
        out_specs=pl.BlockSpec((tc, CONV_DIM), row),
        out_shape=jax.ShapeDtypeStruct((n, CONV_DIM), F32),
        compiler_params=_cparams(("parallel",)),
        name="conv_prompt",
    )(cb, u, u, w)


def _conv_sample_body(cb_ref, ue_ref, w_ref, y_ref):
    w = w_ref[...]
    for t in range(y_ref.shape[0]):
        acc = w[0:1, :] * ue_ref[t] + w[1:2, :] * ue_ref[t + 1] + w[2:3, :] * ue_ref[t + 2]
        y_ref[t] = cb_ref[t] * acc


def _conv_sample(cb_t, ue_t, w):
    return pl.pallas_call(
        _conv_sample_body,
        out_shape=jax.ShapeDtypeStruct(cb_t.shape, F32),
        name="conv_sample",
    )(cb_t, ue_t, w)


def _topk_select(scores, n_sel, tri):
    rows, width = scores.shape
    sc = scores

    def key_to_float(key):
        return lax.bitcast_convert_type(key ^ ((key >> 31) & 0x7FFFFFFF), F32)

    def count_ge(key):
        cnt = jnp.sum(jnp.where(sc >= key_to_float(key), 1.0, 0.0), axis=1, keepdims=True)
        return jnp.where(key <= KEY_NEG_INF, float(width), cnt)

    thr = jnp.where(count_ge(jnp.zeros((rows, 1), jnp.int32)) >= n_sel, 0, INT_MIN).astype(jnp.int32)

    def body(i, thr):
        cand = thr + lax.shift_left(jnp.int32(1), 30 - i)
        return jnp.where(count_ge(cand) >= n_sel, cand, thr)

    thr = lax.fori_loop(0, 31, body, thr)
    thr_f = key_to_float(thr)
    gt = sc > thr_f
    eq = sc == thr_f
    need = n_sel - jnp.sum(jnp.where(gt, 1.0, 0.0), axis=1, keepdims=True)
    off = jnp.zeros((rows, 1), F32)
    parts = []
    for c in range(width // LANES):
        eqc = jnp.where(eq[:, c * LANES:(c + 1) * LANES], 1.0, 0.0)
        incl = _dot(eqc.astype(BF16), tri)
        parts.append((incl - eqc + off) < need)
        off = off + incl[:, LANES - 1:LANES]
    tie = jnp.concatenate(parts, axis=1)
    return (gt | (eq & tie)) & (sc > -jnp.inf)


def _tri_incl():
    i = np.arange(LANES)
    return jnp.asarray((i[:, None] <= i[None, :]).astype(np.float32), BF16)


def _stack_heads(x, n):
    return jnp.concatenate([x[:, h * HEAD_DIM:(h + 1) * HEAD_DIM] for h in range(n)], axis=0)


def _index_scores(iq_st, iw_st, ik_b, keys_on_lanes=False):
    rows = iq_st.shape[0] // N_IDX_HEADS
    s = _dot(iq_st, ik_b) if keys_on_lanes else _dot_nt(iq_st, ik_b)
    term = jnp.maximum(s, 0.0) * (IDX_DIM ** -0.5) * (iw_st * (N_IDX_HEADS ** -0.5))
    acc = term[0:rows, :]
    for h in range(1, N_IDX_HEADS):
        acc = acc + term[h * rows:(h + 1) * rows, :]
    return acc


def _group_logits(qs, kg, sel, keys_on_lanes=False):
    s = (_dot(qs, kg) if keys_on_lanes else _dot_nt(qs, kg)) * (HEAD_DIM ** -0.5)
    return jnp.where(jnp.concatenate([sel] * Q_PER_KV, axis=0), s, -jnp.inf)


def _reduce_rows(x, op):
    parts = [x[c * LANES:(c + 1) * LANES, :] for c in range(x.shape[0] // LANES)]
    while len(parts) > 1:
        nxt = [op(a, b) for a, b in zip(parts[0::2], parts[1::2])]
        if len(parts) % 2:
            nxt.append(parts[-1])
        parts = nxt
    x = parts[0]
    rows = x.shape[0]
    while rows > SUBLANES:
        rows //= 2
        x = op(x[:rows, :], x[rows:, :])
    red = jnp.max if op is jnp.maximum else jnp.sum
    return red(x, axis=0, keepdims=True)


def _topk_select_keys_on_rows(sc, n_sel, tril):
    width, cols = sc.shape

    def key_to_float(key):
        return lax.bitcast_convert_type(key ^ ((key >> 31) & 0x7FFFFFFF), F32)

    def count_ge(key):
        cnt = _reduce_rows(jnp.where(sc >= key_to_float(key), 1.0, 0.0), jnp.add)
        return jnp.where(key <= KEY_NEG_INF, float(width), cnt)

    thr = jnp.where(count_ge(jnp.zeros((1, cols), jnp.int32)) >= n_sel, 0, INT_MIN).astype(jnp.int32)

    def body(i, thr):
        cand = thr + lax.shift_left(jnp.int32(1), 30 - i)
        return jnp.where(count_ge(cand) >= n_sel, cand, thr)

    thr = lax.fori_loop(0, 31, body, thr)
    thr_f = key_to_float(thr)
    gt = sc > thr_f
    eq = sc == thr_f
    need = n_sel - _reduce_rows(jnp.where(gt, 1.0, 0.0), jnp.add)
    off = jnp.zeros((1, cols), F32)
    parts = []
    for c in range(width // LANES):
        eqc = jnp.where(eq[c * LANES:(c + 1) * LANES, :], 1.0, 0.0)
        incl = _dot(tril, eqc.astype(BF16))
        parts.append((incl - eqc + off) < need)
        off = off + incl[LANES - 1:LANES, :]
    tie = jnp.concatenate(parts, axis=0)
    return (gt | (eq & tie)) & (sc > -jnp.inf)


def _prompt_attn_body(q_ref, iq_ref, iwq_ref, k_ref, v_ref, ikw_ref, tril_ref, o_ref,
                      kb_scr, vt_scr, ikb_scr, *, n_sel, widths):
    i = pl.program_id(1)

    @pl.when(i == 0)
    def _():
        for g in range(N_KV_HEADS):
            kb_scr[g] = k_ref[:, g * HEAD_DIM:(g + 1) * HEAD_DIM].astype(BF16)
        vt_scr[...] = jnp.transpose(v_ref[...]).astype(BF16)
        ikb_scr[...] = ikw_ref[:, :IDX_DIM].astype(BF16)

    def run(width):
        iw_t = jnp.transpose(iwq_ref[...])
        w_row = jnp.concatenate([iw_t[IDX_DIM + h:IDX_DIM + h + 1, :] for h in range(N_IDX_HEADS)], axis=1)
        s = _dot_nt(ikb_scr[0:width, :], _stack_heads(iq_ref[...], N_IDX_HEADS))
        term = jnp.maximum(s, 0.0) * (IDX_DIM ** -0.5) * (w_row * (N_IDX_HEADS ** -0.5))
        scores = term[:, 0:Q_BLOCK]
        for h in range(1, N_IDX_HEADS):
            scores = scores + term[:, h * Q_BLOCK:(h + 1) * Q_BLOCK]
        key_pos = lax.broadcasted_iota(jnp.int32, scores.shape, 0)
        tq = i * Q_BLOCK + lax.broadcasted_iota(jnp.int32, scores.shape, 1)
        scores = jnp.where(key_pos <= tq, scores, -jnp.inf)
        sel = _topk_select_keys_on_rows(scores, n_sel, tril_ref[...])
        sel4 = jnp.concatenate([sel] * Q_PER_KV, axis=1)
        q = q_ref[...]
        outs = []
        for g in range(N_KV_HEADS):
            qs = _stack_heads(q[:, g * Q_PER_KV * HEAD_DIM:(g + 1) * Q_PER_KV * HEAD_DIM], Q_PER_KV)
            st = jnp.where(sel4, _dot_nt(kb_scr[g, 0:width, :], qs) * (HEAD_DIM ** -0.5), -jnp.inf)
            p = jnp.exp(st - _reduce_rows(st, jnp.maximum))
            ot = _dot(vt_scr[g * HEAD_DIM:(g + 1) * HEAD_DIM, 0:width], p.astype(BF16))
            outs.append(ot / _reduce_rows(p, jnp.add))
        o = jnp.transpose(jnp.concatenate(outs, axis=0))
        for r in range(Q_PER_KV):
            o_ref[:, r * LANES:(r + 1) * LANES] = o[r * Q_BLOCK:(r + 1) * Q_BLOCK, :].astype(o_ref.dtype)

    lo = 0
    for width in widths:
        hi = width // Q_BLOCK
        pl.when((i >= lo) & (i < hi))(functools.partial(run, width))
        lo = hi


def _prompt_attention(q, iq, ikw, k, v, batch, seq):
    n_sel = min(TOPK_MAX, seq // 4)
    nqb = seq // Q_BLOCK
    n_widths = min(8, nqb)
    widths = tuple(seq * (j + 1) // n_widths for j in range(n_widths))
    blk = lambda b, i: (b * nqb + i, 0)
    full = lambda b, i: (b, 0)
    return pl.pallas_call(
        functools.partial(_prompt_attn_body, n_sel=n_sel, widths=widths),
        grid=(batch, nqb),
        in_specs=[pl.BlockSpec((Q_BLOCK, ATTN_DIM), blk), pl.BlockSpec((Q_BLOCK, N_IDX_HEADS * IDX_DIM), blk),
                  pl.BlockSpec((Q_BLOCK, LANES), blk),
                  pl.BlockSpec((seq, KV_DIM), full), pl.BlockSpec((seq, KV_DIM), full),
                  pl.BlockSpec((seq, LANES), full), pl.BlockSpec((LANES, LANES), lambda b, i: (0, 0))],
        out_specs=pl.BlockSpec((Q_BLOCK, ATTN_DIM), blk),
        out_shape=jax.ShapeDtypeStruct((batch * seq, ATTN_DIM), BF16),
        scratch_shapes=[pltpu.VMEM((N_KV_HEADS, seq, HEAD_DIM), BF16), pltpu.VMEM((KV_DIM, seq), BF16),
                        pltpu.VMEM((seq, IDX_DIM), BF16)],
        compiler_params=_cparams(("parallel", "arbitrary")),
        name="prompt_attention",
    )(q, iq, ikw, k, v, ikw, _tri_incl().T)


PAGES_PER_STEP = 32


def _sample_scores_body(pt_ref, iq_ref, iw_ref, *rest):
    page_refs, s_ref = rest[:PAGES_PER_STEP], rest[PAGES_PER_STEP]
    for j, pr in enumerate(page_refs):
        s_ref[0, :, j * PAGE_SIZE:(j + 1) * PAGE_SIZE] = _index_scores(iq_ref[0], iw_ref[0], pr[0].astype(BF16), True)


def _sample_scores(page_table, iq_st, iw_st, cache_ik):
    bd, n_pages = page_table.shape
    rows = iq_st.shape[1] // N_IDX_HEADS
    assert n_pages % PAGES_PER_STEP == 0
    steps = n_pages // PAGES_PER_STEP
    per_b = lambda b, p, pt: (b, 0, 0)
    page = lambda j: (lambda b, p, pt: (pt[b, p * PAGES_PER_STEP + j], 0, 0))
    return pl.pallas_call(
        _sample_scores_body,
        grid_spec=pltpu.PrefetchScalarGridSpec(
            num_scalar_prefetch=1,
            grid=(bd, steps),
            in_specs=[pl.BlockSpec((1,) + iq_st.shape[1:], per_b), pl.BlockSpec((1,) + iw_st.shape[1:], per_b)]
                     + [pl.BlockSpec((1, IDX_DIM, PAGE_SIZE), page(j)) for j in range(PAGES_PER_STEP)],
            out_specs=pl.BlockSpec((1, rows, PAGES_PER_STEP * PAGE_SIZE), lambda b, p, pt: (b, 0, p)),
        ),
        out_shape=jax.ShapeDtypeStruct((bd, rows, n_pages * PAGE_SIZE), F32),
        compiler_params=_cparams(("parallel", "arbitrary")),
        name="sample_scores",
    )(page_table, iq_st, iw_st, *([cache_ik] * PAGES_PER_STEP))


def _sample_select_body(sp_ref, iq_ref, iwq_ref, ikn_ref, tri_ref, m_ref, *, n_sel, tn):
    rows = sp_ref.shape[0]
    iw = iwq_ref[...]
    iw_st = jnp.concatenate([iw[:, IDX_DIM + h:IDX_DIM + h + 1] for h in range(N_IDX_HEADS)], axis=0)
    s_new = _index_scores(_stack_heads(iq_ref[...], N_IDX_HEADS), iw_st, ikn_ref[...].astype(BF16))
    r = lax.broadcasted_iota(jnp.int32, s_new.shape, 0)
    c = lax.broadcasted_iota(jnp.int32, s_new.shape, 1)
    same = (r // tn == c // tn) & (c <= r)
    fold = jnp.where((lax.broadcasted_iota(jnp.int32, (rows, LANES), 0) % tn)
                     == lax.broadcasted_iota(jnp.int32, (rows, LANES), 1), 1.0, 0.0)
    picked = jnp.where(same, s_new, 0.0)
    hi = picked.astype(BF16)
    mid = (picked - hi.astype(F32)).astype(BF16)
    lo = (picked - hi.astype(F32) - mid.astype(F32)).astype(BF16)
    fb = fold.astype(BF16)
    new_chunk = _dot(hi, fb) + _dot(mid, fb) + _dot(lo, fb)
    lane = lax.broadcasted_iota(jnp.int32, (rows, LANES), 1)
    tpos = lax.broadcasted_iota(jnp.int32, (rows, LANES), 0) % tn
    new_chunk = jnp.where(lane <= tpos, new_chunk, -jnp.inf)
    scores = jnp.concatenate([sp_ref[...], new_chunk], axis=1)
    sel = _topk_select(scores, n_sel, tri_ref[...])
    m_ref[...] = jnp.where(sel, 1.0, 0.0)


def _sample_select(s_past, iq, ikw, tn):
    rows, past = s_past.shape
    n_sel = min(TOPK_MAX, (past + tn) // 4)
    return pl.pallas_call(
        functools.partial(_sample_select_body, n_sel=n_sel, tn=tn),
        out_shape=jax.ShapeDtypeStruct((rows, past + LANES), F32),
        compiler_params=pltpu.CompilerParams(vmem_limit_bytes=VMEM_LIMIT),
        name="sample_select",
    )(s_past, iq, ikw, ikw[:, :IDX_DIM], _tri_incl())


def _sample_attn_body(pt_ref, q_ref, m_ref, kn_ref, vn_ref, mn_ref, *rest):
    k_refs, v_refs = rest[:PAGES_PER_STEP], rest[PAGES_PER_STEP:2 * PAGES_PER_STEP]
    o_ref, m_scr, l_scr, acc_scr = rest[2 * PAGES_PER_STEP:]
    p = pl.program_id(1)

    @pl.when(p == 0)
    def _():
        m_scr[...] = jnp.full(m_scr.shape, -jnp.inf, F32)
        l_scr[...] = jnp.zeros(l_scr.shape, F32)
        acc_scr[...] = jnp.zeros(acc_scr.shape, F32)

    def update(kb, vb, sel):
        for g in range(N_KV_HEADS):
            s = _group_logits(q_ref[0, g], kb[g * HEAD_DIM:(g + 1) * HEAD_DIM, :], sel, True)
            m_old = m_scr[g]
            m_new = jnp.maximum(m_old, jnp.max(s, axis=1, keepdims=True))
            m_safe = jnp.where(m_new == -jnp.inf, 0.0, m_new)
            alpha = jnp.exp(m_old - m_safe)
            pe = jnp.exp(s - m_safe)
            l_scr[g] = alpha * l_scr[g] + jnp.sum(pe, axis=1, keepdims=True)
            acc_scr[g] = alpha * acc_scr[g] + _dot_nt(pe.astype(BF16), vb[g * HEAD_DIM:(g + 1) * HEAD_DIM, :])
            m_scr[g] = m_new

    kb = jnp.concatenate([r[0].astype(BF16) for r in k_refs], axis=1)
    vb = jnp.concatenate([r[0].astype(BF16) for r in v_refs], axis=1)
    update(kb, vb, m_ref[0] > 0.5)

    @pl.when(p == pl.num_programs(1) - 1)
    def _():
        update(kn_ref[0].astype(BF16), vn_ref[0].astype(BF16), mn_ref[0] > 0.5)
        for g in range(N_KV_HEADS):
            l = l_scr[g]
            o_ref[0, g] = acc_scr[g] / jnp.where(l == 0.0, 1.0, l)


def _sample_attention(page_table, q_st, mask, k_new, v_new, cache_k, cache_v):
    bd, n_pages = page_table.shape
    assert n_pages % PAGES_PER_STEP == 0
    steps = n_pages // PAGES_PER_STEP
    per_b = lambda b, p, pt: (b, 0, 0)
    per_b4 = lambda b, p, pt: (b, 0, 0, 0)
    page = lambda j: (lambda b, p, pt: (pt[b, p * PAGES_PER_STEP + j], 0, 0))
    width = PAGES_PER_STEP * PAGE_SIZE
    rows = q_st.shape[2]
    mrows = mask.shape[1]
    return pl.pallas_call(
        _sample_attn_body,
        grid_spec=pltpu.PrefetchScalarGridSpec(
            num_scalar_prefetch=1,
            grid=(bd, steps),
            in_specs=[pl.BlockSpec((1,) + q_st.shape[1:], per_b4),
                      pl.BlockSpec((1, mrows, width), lambda b, p, pt: (b, 0, p)),
                      pl.BlockSpec((1, KV_DIM, PAGE_SIZE), per_b), pl.BlockSpec((1, KV_DIM, PAGE_SIZE), per_b),
                      pl.BlockSpec((1, mrows, LANES), lambda b, p, pt: (b, 0, n_pages))]
                     + [pl.BlockSpec((1, KV_DIM, PAGE_SIZE), page(j)) for j in range(PAGES_PER_STEP)] * 2,
            out_specs=pl.BlockSpec((1, N_KV_HEADS, rows, HEAD_DIM), per_b4),
            scratch_shapes=[pltpu.VMEM((N_KV_HEADS, rows, 1), F32), pltpu.VMEM((N_KV_HEADS, rows, 1), F32),
                            pltpu.VMEM((N_KV_HEADS, rows, HEAD_DIM), F32)],
        ),
        out_shape=jax.ShapeDtypeStruct((bd, N_KV_HEADS, rows, HEAD_DIM), F32),
        compiler_params=_cparams(("parallel", "arbitrary")),
        name="sample_attention",
    )(page_table, q_st, mask, k_new, v_new, mask, *([cache_k] * PAGES_PER_STEP), *([cache_v] * PAGES_PER_STEP))


def _out_even_body(x_ref, a_ref, c_ref, wa_ref, wc_ref, o_ref):
    o_ref[...] = x_ref[...] + _dot(a_ref[...], wa_ref[...]) + _dot(c_ref[...].astype(BF16), wc_ref[...])


def _out_even(x, attn, conv, wa, wc, tm):
    n = x.shape[0]
    row = lambda i: (i, 0)
    const = lambda i: (0, 0)
    return pl.pallas_call(
        _out_even_body,
        grid=(n // tm,),
        in_specs=[pl.BlockSpec((tm, D_MODEL), row), pl.BlockSpec((tm, ATTN_DIM), row),
                  pl.BlockSpec((tm, CONV_DIM), row), pl.BlockSpec((ATTN_DIM, D_MODEL), const),
                  pl.BlockSpec((CONV_DIM, D_MODEL), const)],
        out_specs=pl.BlockSpec((tm, D_MODEL), row),
        out_shape=jax.ShapeDtypeStruct((n, D_MODEL), F32),
        compiler_params=_cparams(("parallel",)),
        name="out_even",
    )(x, attn, conv, wa, wc)


def _mixer_body(*refs, n_exp, emit_sum):
    if n_exp > 1:
        x_ref, g_ref, wr_ref, wg_ref, wu_ref, wd_ref, g2_ref = refs[:7]
        outs = refs[7:]
    else:
        x_ref, g_ref, wg_ref, wu_ref, wd_ref, g2_ref = refs[:6]
        outs = refs[6:]
    n_out = 2 if emit_sum else 1
    out_refs, (xn_scr, acc_scr, cw_scr) = outs[:n_out], outs[n_out:]
    e, f = pl.program_id(1), pl.program_id(2)

    @pl.when((e == 0) & (f == 0))
    def _():
        xn = _rms(x_ref[...], g_ref[...]).astype(BF16)
        xn_scr[...] = xn
        acc_scr[...] = jnp.zeros(acc_scr.shape, F32)
        if n_exp > 1:
            logits = _dot(xn, wr_ref[...])
            lane = lax.broadcasted_iota(jnp.int32, logits.shape, 1).astype(F32)
            lg = jnp.where(lane < n_exp, logits, -jnp.inf)
            m1 = jnp.max(lg, axis=1, keepdims=True)
            i1 = jnp.min(jnp.where(lg == m1, lane, float(LANES)), axis=1, keepdims=True)
            lg2 = jnp.where(lane == i1, -jnp.inf, lg)
            m2 = jnp.max(lg2, axis=1, keepdims=True)
            i2 = jnp.min(jnp.where(lg2 == m2, lane, float(LANES)), axis=1, keepdims=True)
            e2 = jnp.exp(m2 - m1)
            cw_scr[...] = jnp.where(lane == i1, 1.0 / (1.0 + e2), 0.0) + jnp.where(lane == i2, e2 / (1.0 + e2), 0.0)

    xn = xn_scr[...]
    gate = _dot(xn, wg_ref[0])
    h = gate * jax.nn.sigmoid(gate) * _dot(xn, wu_ref[0])
    if n_exp > 1:
        lane = lax.broadcasted_iota(jnp.int32, cw_scr.shape, 1)
        h = h * jnp.sum(jnp.where(lane == e, cw_scr[...], 0.0), axis=1, keepdims=True)
    acc_scr[...] += _dot(h.astype(BF16), wd_ref[0])

    @pl.when((e == pl.num_programs(1) - 1) & (f == pl.num_programs(2) - 1))
    def _():
        y = x_ref[...] + acc_scr[...]
        if emit_sum:
            out_refs[0][...] = y
        out_refs[-1][...] = _rms(y, g2_ref[...])


def _mixer(x, g, wr, wg, wu, wd, g2, tm, tf, emit_sum):
    n = x.shape[0]
    n_exp = wg.shape[0]
    row = lambda i, e, f: (i, 0)
    const = lambda i, e, f: (0, 0)
    in_specs = [pl.BlockSpec((tm, D_MODEL), row), pl.BlockSpec((1, D_MODEL), const)]
    args = [x, g]
    if n_exp > 1:
        in_specs.append(pl.BlockSpec((D_MODEL, LANES), const))
        args.append(wr)
    in_specs += [pl.BlockSpec((1, D_MODEL, tf), lambda i, e, f: (e, 0, f)),
                 pl.BlockSpec((1, D_MODEL, tf), lambda i, e, f: (e, 0, f)),
                 pl.BlockSpec((1, tf, D_MODEL), lambda i, e, f: (e, f, 0)),
                 pl.BlockSpec((1, D_MODEL), const)]
    args += [wg, wu, wd, g2]
    n_out = 2 if emit_sum else 1
    return pl.pallas_call(
        functools.partial(_mixer_body, n_exp=n_exp, emit_sum=emit_sum),
        grid=(n // tm, n_exp, D_FF // tf),
        in_specs=in_specs,
        out_specs=[pl.BlockSpec((tm, D_MODEL), row)] * n_out,
        out_shape=[jax.ShapeDtypeStruct((n, D_MODEL), F32)] * n_out,
        scratch_shapes=[pltpu.VMEM((tm, D_MODEL), BF16), pltpu.VMEM((tm, D_MODEL), F32),
                        pltpu.VMEM((tm, LANES), F32)],
        compiler_params=_cparams(("parallel", "arbitrary", "arbitrary")),
        name="mixer_moe" if n_exp > 1 else "mixer_ffn",
    )(*args)


def _moe_route_body(x_ref, g_ref, wr_ref, tri_ref, etri_ref,
                    xn_ref, slot_ref, slot_t_ref, comb_ref, meta_ref, *, n_exp, rows_blk):
    xn = _rms(x_ref[...], g_ref[...]).astype(BF16)
    xn_ref[...] = xn
    logits = _dot(xn, wr_ref[...])
    lane = lax.broadcasted_iota(jnp.int32, logits.shape, 1).astype(F32)
    lg = jnp.where(lane < n_exp, logits, -jnp.inf)
    m1 = jnp.max(lg, axis=1, keepdims=True)
    i1 = jnp.min(jnp.where(lg == m1, lane, float(LANES)), axis=1, keepdims=True)
    lg2 = jnp.where(lane == i1, -jnp.inf, lg)
    m2 = jnp.max(lg2, axis=1, keepdims=True)
    i2 = jnp.min(jnp.where(lg2 == m2, lane, float(LANES)), axis=1, keepdims=True)
    e2 = jnp.exp(m2 - m1)
    comb_ref[...] = jnp.where(lane == i1, 1.0 / (1.0 + e2), 0.0) + jnp.where(lane == i2, e2 / (1.0 + e2), 0.0)
    member = jnp.where((lane == i1) | (lane == i2), 1.0, 0.0)
    n_tok = member.shape[0]
    pos = _dot(tri_ref[...], member.astype(BF16))
    cnt = pos[n_tok - 1:n_tok, :] + member[n_tok - 1:n_tok, :]
    padded = jnp.floor((cnt + (rows_blk - 0.5)) * (1.0 / rows_blk)) * rows_blk
    start = _dot(jnp.broadcast_to(padded, (SUBLANES, LANES)).astype(BF16), etri_ref[...])[0:1, :]
    slot = jnp.where(member > 0.0, start + pos, -1.0)
    slot_ref[...] = slot
    slot_t_ref[0] = jnp.transpose(slot)[0:SUBLANES, :]
    ends = start + padded
    first_row = lax.broadcasted_iota(jnp.int32, (1, LANES), 1).astype(F32) * rows_blk
    blk_exp = jnp.zeros((1, LANES), F32)
    for e in range(n_exp - 1):
        blk_exp = blk_exp + jnp.where(ends[:, e:e + 1] <= first_row, 1.0, 0.0)
    n_blk = jnp.floor((ends[:, n_exp - 1:n_exp] + 0.5) * (1.0 / rows_blk))
    row = lax.broadcasted_iota(jnp.int32, (SUBLANES, LANES), 0)
    meta = jnp.where(row == 0, blk_exp, jnp.where(row == 1, n_blk, 0.0))
    meta_ref[0] = meta.astype(jnp.int32)


def _moe_experts_body(be_ref, nb_ref, xn_ref, slot_ref, slot_t_ref, comb_ref, x_ref, wg_ref, wu_ref, wd_ref,
                      g2_ref, o_ref, xs_scr, ys_scr, *, rows_blk, nf):
    g, f, j = pl.program_id(0), pl.program_id(1), pl.program_id(2)
    nj = pl.num_programs(2)
    e = be_ref[g, j]
    used = j < nb_ref[g]
    n_tok = xn_ref.shape[0]
    rows = pl.ds(pl.multiple_of(j * rows_blk, rows_blk), rows_blk)
    first = j * rows_blk

    @pl.when((f == 0) & (j == 0))
    def _():
        o_ref[...] = x_ref[...]

    @pl.when(used & (f == 0))
    def _():
        srow = slot_t_ref[0, pl.ds(e, 1), :]
        want = (first + lax.broadcasted_iota(jnp.int32, (rows_blk, n_tok), 0)).astype(F32)
        onehot = jnp.where(srow == want, 1.0, 0.0).astype(BF16)
        xs_scr[rows, :] = _dot(onehot, xn_ref[...]).astype(BF16)

    @pl.when(used)
    def _():
        xs = xs_scr[rows, :]
        gate = _dot(xs, wg_ref[0])
        h = gate * jax.nn.sigmoid(gate) * _dot(xs, wu_ref[0])
        part = _dot(h.astype(BF16), wd_ref[0])

        if nf > 1:
            @pl.when(f == 0)
            def _():
                ys_scr[rows, :] = part.astype(BF16)

        if nf > 2:
            @pl.when((f > 0) & (f < nf - 1))
            def _():
                ys_scr[rows, :] = (ys_scr[rows, :].astype(F32) + part).astype(BF16)

        def scatter(total):
            lane = lax.broadcasted_iota(jnp.int32, (n_tok, LANES), 1)
            scol = jnp.sum(jnp.where(lane == e, slot_ref[...], 0.0), axis=1, keepdims=True)
            ccol = jnp.sum(jnp.where(lane == e, comb_ref[...], 0.0), axis=1, keepdims=True)
            want = (first + lax.broadcasted_iota(jnp.int32, (n_tok, rows_blk), 1)).astype(F32)
            onehot = jnp.where(scol == want, 1.0, 0.0).astype(BF16)
            tb = total.astype(BF16)
            for c in range(D_MODEL // (2 * LANES)):
                cols = slice(c * 2 * LANES, (c + 1) * 2 * LANES)
                o_ref[:, cols] += ccol * _dot(onehot, tb[:, cols])

        @pl.when(f == nf - 1)
        def _():
            scatter(part + ys_scr[rows, :].astype(F32) if nf > 1 else part)

    @pl.when((f == nf - 1) & (j == nj - 1))
    def _():
        o_ref[...] = _rms(o_ref[...], g2_ref[...])


def _moe(x, g, wr, wg, wu, wd, g2, tg, rows_blk, tf):
    n = x.shape[0]
    n_exp = wg.shape[0]
    n_groups = n // tg
    n_blk = 2 * tg // rows_blk + n_exp
    i = np.arange(tg)
    tri = jnp.asarray((i[None, :] < i[:, None]).astype(np.float32), BF16)
    i = np.arange(LANES)
    etri = jnp.asarray((i[:, None] < i[None, :]).astype(np.float32), BF16)
    row = lambda i: (i, 0)
    const = lambda i: (0, 0)
    xn, slot, slot_t, comb, meta = pl.pallas_call(
        functools.partial(_moe_route_body, n_exp=n_exp, rows_blk=rows_blk),
        grid=(n_groups,),
        in_specs=[pl.BlockSpec((tg, D_MODEL), row), pl.BlockSpec((1, D_MODEL), const),
                  pl.BlockSpec((D_MODEL, LANES), const), pl.BlockSpec((tg, tg), const),
                  pl.BlockSpec((LANES, LANES), const)],
        out_specs=[pl.BlockSpec((tg, D_MODEL), row), pl.BlockSpec((tg, LANES), row),
                   pl.BlockSpec((1, SUBLANES, tg), lambda i: (i, 0, 0)), pl.BlockSpec((tg, LANES), row),
                   pl.BlockSpec((1, SUBLANES, LANES), lambda i: (i, 0, 0))],
        out_shape=[jax.ShapeDtypeStruct((n, D_MODEL), BF16), jax.ShapeDtypeStruct((n, LANES), F32),
                   jax.ShapeDtypeStruct((n_groups, SUBLANES, tg), F32), jax.ShapeDtypeStruct((n, LANES), F32),
                   jax.ShapeDtypeStruct((n_groups, SUBLANES, LANES), jnp.int32)],
        compiler_params=_cparams(("parallel",)),
        name="moe_route",
    )(x, g, wr, tri, etri)
    blk_exp = meta[:, 0, :]
    blk_cnt = meta[:, 1, 0]
    per_g = lambda g, f, j, be, nb: (g, 0)
    once = pl.Buffered(1)
    return pl.pallas_call(
        functools.partial(_moe_experts_body, rows_blk=rows_blk, nf=D_FF // tf),
        grid_spec=pltpu.PrefetchScalarGridSpec(
            num_scalar_prefetch=2,
            grid=(n_groups, D_FF // tf, n_blk),
            in_specs=[pl.BlockSpec((tg, D_MODEL), per_g, pipeline_mode=once),
                      pl.BlockSpec((tg, LANES), per_g, pipeline_mode=once),
                      pl.BlockSpec((1, SUBLANES, tg), lambda g, f, j, be, nb: (g, 0, 0), pipeline_mode=once),
                      pl.BlockSpec((tg, LANES), per_g, pipeline_mode=once),
                      pl.BlockSpec((tg, D_MODEL), per_g, pipeline_mode=once),
                      pl.BlockSpec((1, D_MODEL, tf), lambda g, f, j, be, nb: (be[g, j], 0, f)),
                      pl.BlockSpec((1, D_MODEL, tf), lambda g, f, j, be, nb: (be[g, j], 0, f)),
                      pl.BlockSpec((1, tf, D_MODEL), lambda g, f, j, be, nb: (be[g, j], f, 0)),
                      pl.BlockSpec((1, D_MODEL), lambda g, f, j, be, nb: (0, 0))],
            out_specs=pl.BlockSpec((tg, D_MODEL), per_g, pipeline_mode=once),
            scratch_shapes=[pltpu.VMEM((n_blk * rows_blk, D_MODEL), BF16),
                            pltpu.VMEM((n_blk * rows_blk, D_MODEL), BF16)],
        ),
        out_shape=jax.ShapeDtypeStruct((n, D_MODEL), F32),
        compiler_params=pltpu.CompilerParams(dimension_semantics=("parallel", "arbitrary", "arbitrary"),
                                             vmem_limit_bytes=VMEM_LIMIT_MOE),
        name="moe_experts",
    )(blk_exp, blk_cnt, xn, slot, slot_t, comb, x, wg, wu, wd, g2)


def _group_ones():
    i = np.arange(LANES) // RWKV_HEAD
    return jnp.asarray((i[:, None] == i[None, :]).astype(np.float32), BF16)


def _rwkv_proj_body(xn_ref, xp_ref, sh_ref, mu_ref, vec_ref, wrkv_ref, w1_ref, w2_ref, a1_ref, a2_ref, g1_ref,
                    g2_ref, gm_ref, r_ref, lw_ref, k_ref, v_ref, kk_ref, ka_ref, g_ref, *, tiles_per_seq):
    xn = xn_ref[...]
    if tiles_per_seq is None:
        xp = xp_ref[...]
    else:
        first = pl.program_id(0) % tiles_per_seq == 0
        row0 = jnp.where(first, sh_ref[0], xp_ref[SUBLANES - 1:SUBLANES, :])
        r = lax.broadcasted_iota(jnp.int32, xn.shape, 0)
        xp = jnp.where(r == 0, row0, pltpu.roll(xn, 1, 0))
    xx = xp - xn
    mu = mu_ref[...]
    mix = lambda i: (xn + xx * mu[i:i + 1, :]).astype(BF16)
    vec = vec_ref[...]
    w0, a0, k_k, k_a = vec[0:1, :], vec[1:2, :], vec[2:3, :], vec[3:4, :]
    r_ref[...] = _dot(mix(0), wrkv_ref[0]).astype(r_ref.dtype)
    z = w0 + _dot(jnp.tanh(_dot(mix(1), w1_ref[...])).astype(BF16), w2_ref[...])
    softplus = jnp.maximum(-z, 0.0) + jnp.log(1.0 + jnp.exp(-jnp.abs(z)))
    lw_ref[...] = -jnp.exp(-softplus - 0.5)
    k = _dot(mix(2), wrkv_ref[1])
    v_ref[...] = _dot(mix(3), wrkv_ref[2]).astype(v_ref.dtype)
    a = jax.nn.sigmoid(a0 + _dot(_dot(mix(4), a1_ref[...]).astype(BF16), a2_ref[...]))
    g_ref[...] = _dot(jax.nn.sigmoid(_dot(mix(5), g1_ref[...])).astype(BF16), g2_ref[...]).astype(g_ref.dtype)
    kk = k * k_k
    kk = kk * lax.rsqrt(jnp.maximum(_group_sum(kk * kk, gm_ref[...]), 1e-24))
    kk_ref[...] = kk.astype(kk_ref.dtype)
    ka_ref[...] = (kk * a).astype(ka_ref.dtype)
    k_ref[...] = (k * (1.0 + (a - 1.0) * k_a)).astype(k_ref.dtype)


def _rwkv_proj(xn, xp, shift_prev, seq, mu, vec, wrkv, w1, w2, a1, a2, g1, g2, tm):
    n = xn.shape[0]
    row = lambda i: (i, 0)
    c2 = lambda i: (0, 0)
    c3 = lambda i: (0, 0, 0)
    full = lambda a: pl.BlockSpec(a.shape, c3 if a.ndim == 3 else c2)
    gm = _group_ones()
    consts = [mu, vec, wrkv, w1, w2, a1, a2, g1, g2, gm]
    if xp is None:
        tiles_per_seq = seq // tm
        prev_spec = pl.BlockSpec((SUBLANES, D_MODEL), lambda i: (jnp.maximum(i * (tm // SUBLANES) - 1, 0), 0))
        shift_spec = pl.BlockSpec((1, 1, D_MODEL), lambda i: (i // tiles_per_seq, 0, 0))
        xp = xn
    else:
        tiles_per_seq = None
        prev_spec = pl.BlockSpec((tm, D_MODEL), row)
        shift_spec = pl.BlockSpec((1, 1, D_MODEL), c3)
    dtypes = (BF16, F32, BF16, BF16, BF16, BF16, BF16)
    return pl.pallas_call(
        functools.partial(_rwkv_proj_body, tiles_per_seq=tiles_per_seq),
        grid=(n // tm,),
        in_specs=[pl.BlockSpec((tm, D_MODEL), row), prev_spec, shift_spec] + [full(a) for a in consts],
        out_specs=[pl.BlockSpec((tm, D_MODEL), row)] * 7,
        out_shape=[jax.ShapeDtypeStruct((n, D_MODEL), dt) for dt in dtypes],
        compiler_params=_cparams(("parallel",)),
        name="rwkv_proj",
    )(xn, xp, shift_prev[:, None, :], *consts)


def _wkv_body(r_ref, lw_ref, k_ref, v_ref, kk_ref, ka_ref, s0_ref, y_ref, sT_ref,
              s_scr, a_scr, r_scr, b_scr, k_scr, *, chunk):
    c = pl.program_id(1)
    n_pairs = D_MODEL // LANES
    hd = RWKV_HEAD
    zeros = jnp.zeros((hd, hd), F32)

    @pl.when(c == 0)
    def _():
        for p in range(n_pairs):
            top = jnp.concatenate([s0_ref[0, 2 * p], zeros], axis=1)
            bot = jnp.concatenate([zeros, s0_ref[0, 2 * p + 1]], axis=1)
            s_scr[p] = jnp.concatenate([top, bot], axis=0)

    row = lax.broadcasted_iota(jnp.int32, (chunk, chunk), 0)
    colm = lax.broadcasted_iota(jnp.int32, (chunk, chunk), 1)
    lw = lw_ref[...]
    cum = jnp.dot(jnp.where(row >= colm, 1.0, 0.0), lw, preferred_element_type=F32, precision=HI)
    w_incl = jnp.exp(cum)
    w_inv = jnp.exp(-cum)
    a_scr[...] = -kk_ref[...] * jnp.exp(cum - lw)
    r_scr[...] = r_ref[...] * w_incl
    b_scr[...] = ka_ref[...] * w_inv
    k_scr[...] = k_ref[...] * w_inv
    w_last = w_incl[chunk - 1:chunk, :]
    n_dbl = max(1, int(np.ceil(np.log2(chunk))))

    c2 = 2 * chunk
    head_of_row = lax.broadcasted_iota(jnp.int32, (c2, LANES), 0) // chunk
    head_of_lane = lax.broadcasted_iota(jnp.int32, (c2, LANES), 1) // hd
    own_lanes = head_of_row == head_of_lane
    tr = lax.broadcasted_iota(jnp.int32, (c2, c2), 0)
    tc = lax.broadcasted_iota(jnp.int32, (c2, c2), 1)
    same = (tr // chunk) == (tc // chunk)
    strict = same & (tr % chunk > tc % chunk)
    incl = same & (tr % chunk >= tc % chunk)

    def block_diag(x):
        return jnp.where(own_lanes, jnp.concatenate([x, x], axis=0), 0.0).astype(BF16)

    pairs = range(n_pairs)
    sls = [slice(p * LANES, (p + 1) * LANES) for p in pairs]
    ar = [jnp.concatenate([block_diag(a_scr[:, sl]), block_diag(r_scr[:, sl])], axis=0) for sl in sls]
    bk = [jnp.concatenate([block_diag(b_scr[:, sl]), block_diag(k_scr[:, sl])], axis=0) for sl in sls]
    vm = [block_diag(v_ref[:, sl]) for sl in sls]
    s_old = [s_scr[p] for p in pairs]
    gram = [_dot_nt(ar[p], bk[p]) for p in pairs]
    xs = [_dot_nt(ar[p], s_old[p].astype(BF16)) for p in pairs]
    u = [xs[p][:c2, :] + _dot(jnp.where(strict, gram[p][:c2, c2:], 0.0).astype(BF16), vm[p]) for p in pairs]
    lp = [jnp.where(strict, gram[p][:c2, :c2], 0.0).astype(BF16) for p in pairs]
    for d in range(n_dbl):
        if d + 1 < n_dbl:
            t = [_dot(lp[p], jnp.concatenate([u[p].astype(BF16), lp[p]], axis=1)) for p in pairs]
            u = [u[p] + t[p][:, :LANES] for p in pairs]
            lp = [t[p][:, LANES:].astype(BF16) for p in pairs]
        else:
            u = [u[p] + _dot(lp[p], u[p].astype(BF16)) for p in pairs]
    uv = [jnp.concatenate([u[p].astype(BF16), vm[p]], axis=0) for p in pairs]
    m_r = [jnp.concatenate([jnp.where(incl, gram[p][c2:, :c2], 0.0), jnp.where(incl, gram[p][c2:, c2:], 0.0)],
                           axis=1).astype(BF16) for p in pairs]
    y = [xs[p][c2:, :] + _dot(m_r[p], uv[p]) for p in pairs]
    for p in pairs:
        y_ref[:, sls[p]] = y[p][:chunk, :] + y[p][chunk:, :]
    s_new = [(s_old[p] + _dot_tn(uv[p], bk[p])) * w_last[:, sls[p]] for p in pairs]
    for p in pairs:
        s_scr[p] = s_new[p]

    @pl.when(c == pl.num_programs(1) - 1)
    def _():
        for p in range(n_pairs):
            s = s_scr[p]
            sT_ref[0, 2 * p] = s[:hd, :hd]
            sT_ref[0, 2 * p + 1] = s[hd:, hd:]


def _wkv(r, lw, k, v, kk, ka, s0, batch, seq, chunk):
    nc = seq // chunk
    blk = lambda b, c: (b * nc + c, 0)
    st = lambda b, c: (b, 0, 0, 0)
    state = pl.BlockSpec((1, RWKV_HEADS, RWKV_HEAD, RWKV_HEAD), st)
    return pl.pallas_call(
        functools.partial(_wkv_body, chunk=chunk),
        grid=(batch, nc),
        in_specs=[pl.BlockSpec((chunk, D_MODEL), blk)] * 6 + [state],
        out_specs=[pl.BlockSpec((chunk, D_MODEL), blk), state],
        out_shape=[jax.ShapeDtypeStruct((batch * seq, D_MODEL), F32),
                   jax.ShapeDtypeStruct(s0.shape, F32)],
        scratch_shapes=[pltpu.VMEM((D_MODEL // LANES, LANES, LANES), F32)]
                       + [pltpu.VMEM((chunk, D_MODEL), F32)] * 4,
        compiler_params=_cparams(("parallel", "arbitrary")),
        name="wkv",
    )(r, lw, k, v, kk, ka, s0)


def _rwkv_out_body(x_ref, y_ref, r_ref, k_ref, v_ref, g_ref, vec_ref, gm_ref, wo_ref, o_ref):
    gm = gm_ref[...]
    vec = vec_ref[...]
    ln_w, ln_b, r_k = vec[0:1, :], vec[1:2, :], vec[2:3, :]
    y = y_ref[...]
    mean = _group_sum(y, gm) * (1.0 / RWKV_HEAD)
    d = y - mean
    var = _group_sum(d * d, gm) * (1.0 / RWKV_HEAD)
    yn = d * lax.rsqrt(var + GN_EPS) * ln_w + ln_b
    rk = r_ref[...].astype(F32) * k_ref[...].astype(F32) * r_k
    yn = yn + _group_sum(rk, gm) * v_ref[...].astype(F32)
    o_ref[...] = x_ref[...] + _dot((yn * g_ref[...].astype(F32)).astype(BF16), wo_ref[...])


def _rwkv_out(x, y, r, k, v, g, vec, wo, tm):
    n = x.shape[0]
    row = lambda i: (i, 0)
    const = lambda i: (0, 0)
    gm = _group_ones()
    return pl.pallas_call(
        _rwkv_out_body,
        grid=(n // tm,),
        in_specs=[pl.BlockSpec((tm, D_MODEL), row)] * 6
                 + [pl.BlockSpec(vec.shape, const), pl.BlockSpec(gm.shape, const), pl.BlockSpec(wo.shape, const)],
        out_specs=pl.BlockSpec((tm, D_MODEL), row),
        out_shape=jax.ShapeDtypeStruct((n, D_MODEL), F32),
        compiler_params=_cparams(("parallel",)),
        name="rwkv_out",
    )(x, y, r, k, v, g, vec, gm, wo)


def _pick_tile(n, want):
    t = min(n, want)
    while n % t:
        t //= 2
    return t


def _trunk(x, pos, conv_prev, shift_prev, wkv_prev, wts, sample):
    batch, seq, _ = x.shape
    n = batch * seq
    xf = x.reshape(n, D_MODEL)
    tm = _pick_tile(n, 512)

    tabs = _rope_tables(pos)
    if sample is not None:
        tabs = tuple(jnp.tile(t, (batch, 1)) for t in tabs)
    q, k, v, iq, ikw, cb, u = _proj_even(xf, wts['norm_mix_even'], wts['w_in'], tabs, tm)
    if sample is None:
        attn = _prompt_attention(q, iq, ikw, k, v, batch, seq)
        conv = _conv_prompt(cb, u, wts['conv_w'], seq, _pick_tile(seq, 512))
        conv_state = u.reshape(batch, seq, CONV_DIM)[:, seq - (CONV_W - 1):]
    else:
        cache_k, cache_v, cache_ik, page_table = sample
        tpad = SUBLANES - seq

        def stack(a, heads):
            a = a.reshape(batch, seq, heads, -1).transpose(0, 2, 1, 3)
            a = jnp.pad(a, ((0, 0), (0, 0), (0, tpad), (0, 0)))
            return a.reshape(batch, heads * SUBLANES, a.shape[-1])

        iq_st = stack(iq, N_IDX_HEADS)
        iw_st = stack(ikw[:, IDX_DIM:IDX_DIM + N_IDX_HEADS], N_IDX_HEADS)
        s_past = _sample_scores(page_table, iq_st, iw_st, cache_ik)[:, :seq]
        mask = _sample_select(s_past.reshape(n, -1), iq, ikw, seq)
        mask = jnp.pad(mask.reshape(batch, seq, -1), ((0, 0), (0, tpad), (0, 0)))
        q_st = stack(q, N_Q_HEADS).reshape(batch, N_KV_HEADS, Q_PER_KV * SUBLANES, HEAD_DIM)
        padk = lambda a: jnp.pad(jnp.swapaxes(a.reshape(batch, seq, KV_DIM), 1, 2),
                                 ((0, 0), (0, 0), (0, PAGE_SIZE - seq)))
        o = _sample_attention(page_table, q_st, mask, padk(k), padk(v), cache_k, cache_v)
        attn = (o.reshape(batch, N_KV_HEADS, Q_PER_KV, SUBLANES, HEAD_DIM)[:, :, :, :seq].transpose(0, 3, 2, 1, 4)
                .reshape(n, ATTN_DIM).astype(BF16))
        u3 = jnp.concatenate([conv_prev, u.reshape(batch, seq, CONV_DIM)], axis=1)
        conv_state = u3[:, seq:]
        conv_t = _conv_sample(jnp.swapaxes(cb.reshape(batch, seq, CONV_DIM), 0, 1), jnp.swapaxes(u3, 0, 1),
                              wts['conv_w'])
        conv = jnp.swapaxes(conv_t, 0, 1).reshape(n, CONV_DIM)
    x1 = _out_even(xf, attn, conv, wts['w_out_attn'], wts['w_out_conv'], tm)
    x2, xn2 = _mixer(x1, wts['norm_ffn_even'], None, wts['ffn_gate'], wts['ffn_up'], wts['ffn_down'],
                     wts['norm_mix_odd'], tm, D_FF // 2, True)

    xn3 = xn2.reshape(batch, seq, D_MODEL)
    tm1 = _pick_tile(n, 256)
    xp = None
    if seq % tm1:
        xp = jnp.concatenate([shift_prev[:, None, :], xn3[:, :-1]], axis=1).reshape(n, D_MODEL)
    r, lw, kr, vr, kk, ka, g = _rwkv_proj(xn2, xp, shift_prev, seq, wts['rwkv_mu'], wts['rwkv_vec_in'],
                                          wts['rwkv_w_rkv'], wts['rwkv_w1'], wts['rwkv_w2'], wts['rwkv_a1'],
                                          wts['rwkv_a2'], wts['rwkv_g1'], wts['rwkv_g2'], tm1)
    if seq % WKV_CHUNK:
        sp = -(-seq // WKV_CHUNK) * WKV_CHUNK
        padded = [jnp.pad(a.reshape(batch, seq, D_MODEL), ((0, 0), (0, sp - seq), (0, 0)))
                  .reshape(batch * sp, D_MODEL) for a in (r, lw, kr, vr, kk, ka)]
        y, wkv_state = _wkv(*padded, wkv_prev, batch, sp, WKV_CHUNK)
        y = y.reshape(batch, sp, D_MODEL)[:, :seq].reshape(n, D_MODEL)
    else:
        y, wkv_state = _wkv(r, lw, kr, vr, kk, ka, wkv_prev, batch, seq, WKV_CHUNK)
    x3 = _rwkv_out(x2, y, r, kr, vr, g, wts['rwkv_vec_out'], wts['rwkv_w_o'], _pick_tile(n, 256))
    yf = _moe(x3, wts['norm_ffn_odd'], wts['moe_router'], wts['moe_gate'], wts['moe_up'], wts['moe_down'],
              wts['norm_final'], _pick_tile(n, MOE_GROUP), min(MOE_BLOCK_ROWS, _pick_tile(n, MOE_GROUP)), D_FF // 2)

    k4 = k.reshape(1, batch, seq, N_KV_HEADS, HEAD_DIM)
    v4 = v.reshape(1, batch, seq, N_KV_HEADS, HEAD_DIM)
    ik3 = ikw[:, :IDX_DIM].reshape(1, batch, seq, IDX_DIM)
    return (yf.reshape(batch, seq, D_MODEL), k4, v4, ik3, conv_state[None],
            xn3[:, -1][None], wkv_state[None])


def _prepare_weights(norm_mix_even, w_in_even, conv_w, w_out_even, norm_ffn_even, ffn_gate, ffn_up, ffn_down,
                     norm_mix_odd, rwkv_mu, rwkv_w_rkv, rwkv_w0, rwkv_w1, rwkv_w2, rwkv_a0, rwkv_a1, rwkv_a2,
                     rwkv_g1, rwkv_g2, rwkv_k_k, rwkv_k_a, rwkv_r_k, rwkv_ln_w, rwkv_ln_b, rwkv_w_o,
                     norm_ffn_odd, moe_router, moe_gate, moe_up, moe_down, norm_final):
    w_in = w_in_even[0]
    o = np.cumsum((0, ATTN_DIM, KV_DIM, KV_DIM, N_IDX_HEADS * IDX_DIM, IDX_DIM, N_IDX_HEADS,
                   CONV_DIM, CONV_DIM, CONV_DIM))
    pad = jnp.zeros((D_MODEL, LANES - IDX_DIM - N_IDX_HEADS), F32)
    w_in = jnp.concatenate([w_in[:, :o[4]], w_in[:, o[4]:o[6]], pad, w_in[:, o[6]:]], axis=1).astype(BF16)
    row = lambda a: a.reshape(1, -1)
    zeros = jnp.zeros((1, D_MODEL), F32)
    return dict(
        norm_mix_even=row(norm_mix_even[0]), w_in=w_in, conv_w=conv_w[0],
        w_out_attn=(w_out_even[0, :ATTN_DIM].reshape(N_KV_HEADS, Q_PER_KV, HEAD_DIM, D_MODEL)
                    .transpose(1, 0, 2, 3).reshape(ATTN_DIM, D_MODEL).astype(BF16)),
        w_out_conv=w_out_even[0, ATTN_DIM:].astype(BF16),
        norm_ffn_even=row(norm_ffn_even[0]),
        ffn_gate=ffn_gate.astype(BF16), ffn_up=ffn_up.astype(BF16), ffn_down=ffn_down.astype(BF16),
        norm_mix_odd=row(norm_mix_odd[0]), rwkv_mu=jnp.concatenate([rwkv_mu[0], zeros, zeros], axis=0),
        rwkv_vec_in=jnp.concatenate([row(rwkv_w0[0]), row(rwkv_a0[0]), row(rwkv_k_k[0]), row(rwkv_k_a[0]),
                                     zeros, zeros, zeros, zeros], axis=0),
        rwkv_w_rkv=rwkv_w_rkv[0].astype(BF16),
        rwkv_w1=rwkv_w1[0].astype(BF16), rwkv_w2=rwkv_w2[0].astype(BF16),
        rwkv_a1=rwkv_a1[0].astype(BF16), rwkv_a2=rwkv_a2[0].astype(BF16),
        rwkv_g1=rwkv_g1[0].astype(BF16), rwkv_g2=rwkv_g2[0].astype(BF16),
        rwkv_vec_out=jnp.concatenate([row(rwkv_ln_w[0]), row(rwkv_ln_b[0]), row(rwkv_r_k[0]),
                                      zeros, zeros, zeros, zeros, zeros], axis=0),
        rwkv_w_o=rwkv_w_o[0].astype(BF16),
        norm_ffn_odd=row(norm_ffn_odd[0]),
        moe_router=jnp.pad(moe_router[0], ((0, 0), (0, LANES - N_EXPERTS))).astype(BF16),
        moe_gate=moe_gate[0].astype(BF16), moe_up=moe_up[0].astype(BF16), moe_down=moe_down[0].astype(BF16),
        norm_final=row(norm_final),
    )


def kernel(x_prompt, x_sample, cache_k, cache_v, cache_idx_k, state_conv, state_shift, state_wkv, page_table, norm_mix_even, w_in_even, conv_w, w_out_even, norm_ffn_even, ffn_gate, ffn_up, ffn_down, norm_mix_odd, rwkv_mu, rwkv_w_rkv, rwkv_w0, rwkv_w1, rwkv_w2, rwkv_a0, rwkv_a1, rwkv_a2, rwkv_g1, rwkv_g2, rwkv_k_k, rwkv_k_a, rwkv_r_k, rwkv_ln_w, rwkv_ln_b, rwkv_w_o, norm_ffn_odd, moe_router, moe_gate, moe_up, moe_down, norm_final):
    assert w_in_even.shape[0] == 1 and rwkv_mu.shape[0] == 1, "one even and one odd layer"
    wts = _prepare_weights(norm_mix_even, w_in_even, conv_w, w_out_even, norm_ffn_even, ffn_gate, ffn_up, ffn_down,
                           norm_mix_odd, rwkv_mu, rwkv_w_rkv, rwkv_w0, rwkv_w1, rwkv_w2, rwkv_a0, rwkv_a1, rwkv_a2,
                           rwkv_g1, rwkv_g2, rwkv_k_k, rwkv_k_a, rwkv_r_k, rwkv_ln_w, rwkv_ln_b, rwkv_w_o,
                           norm_ffn_odd, moe_router, moe_gate, moe_up, moe_down, norm_final)
    b, t = x_prompt.shape[:2]
    bd, tn = x_sample.shape[:2]
    n_pool = cache_k.shape[1]
    past = page_table.shape[1] * PAGE_SIZE
    pos_prompt = jnp.arange(t, dtype=jnp.int32)
    pos_sample = past + jnp.arange(tn, dtype=jnp.int32)
    zeros = lambda *s: jnp.zeros(s, F32)
    out_p = _trunk(x_prompt, pos_prompt, zeros(b, CONV_W - 1, CONV_DIM), zeros(b, D_MODEL),
                   zeros(b, RWKV_HEADS, RWKV_HEAD, RWKV_HEAD), wts, None)
    pages_t = lambda c: jnp.transpose(c[0], (0, 2, 3, 1)).reshape(n_pool, KV_DIM, PAGE_SIZE)
    sample = (pages_t(cache_k), pages_t(cache_v), jnp.swapaxes(cache_idx_k[0], 1, 2), page_table)
    out_s = _trunk(x_sample, pos_sample, state_conv[0], state_shift[0], state_wkv[0], wts, sample)
    y_p, k_p, v_p, ik_p, conv_p, shift_p, wkv_p = out_p
    y_s, k_s, v_s, ik_s, conv_s, shift_s, wkv_s = out_s
    return (y_p, y_s, k_p, v_p, ik_p, k_s, v_s, ik_s, conv_p, conv_s, shift_p, shift_s, wkv_p, wkv_s)
```

```python
import functools

import numpy as np
import jax
import jax.numpy as jnp
from jax import lax
from jax.experimental import pallas as pl
from jax.experimental.pallas import tpu as pltpu

F32 = jnp.float32
BF16 = jnp.bfloat16
HI = lax.Precision.HIGHEST

D_MODEL = 1024
PAGE_SIZE = 128
HEAD_DIM = 64
N_Q_HEADS = 8
N_KV_HEADS = 2
Q_PER_KV = N_Q_HEADS // N_KV_HEADS
ROT_DIM = HEAD_DIM // 4
ROPE_THETA = 500000.0
N_IDX_HEADS = 4
IDX_DIM = 64
TOPK_MAX = 256
Q_BLOCK = 128
ATTN_DIM = N_Q_HEADS * HEAD_DIM
KV_DIM = N_KV_HEADS * HEAD_DIM
CONV_DIM = D_MODEL // 2
CONV_W = 3
RWKV_HEAD = 64
RWKV_HEADS = D_MODEL // RWKV_HEAD
GN_EPS = 64e-5
D_FF = 2816
N_EXPERTS = 8
RMS_EPS = 1e-6

LANES = 128
SUBLANES = 8
BF16_SUBLANES = 2 * SUBLANES
VMEM_LIMIT = 56 * 1024 * 1024
VMEM_LIMIT_MOE = 61 * 1024 * 1024
INT_MIN = -2 ** 31
KEY_NEG_INF = INT_MIN + 0x7FFFFF
PROJ_COLS = ATTN_DIM + KV_DIM + KV_DIM + N_IDX_HEADS * IDX_DIM + LANES + 3 * CONV_DIM
WKV_CHUNK = 64
MOE_GROUP = 1024
MOE_BLOCK_ROWS = 288


def _cparams(sem):
    return pltpu.CompilerParams(dimension_semantics=sem, vmem_limit_bytes=VMEM_LIMIT)


def _rms(x, g):
    return x * lax.rsqrt(jnp.mean(x * x, axis=-1, keepdims=True) + RMS_EPS) * g


def _dot(a, b):
    return jnp.dot(a, b, preferred_element_type=F32)


def _dot_nt(a, b, precision=None):
    return lax.dot_general(a, b, (((1,), (1,)), ((), ())), preferred_element_type=F32, precision=precision)


def _dot_tn(a, b, precision=None):
    return lax.dot_general(a, b, (((0,), (0,)), ((), ())), preferred_element_type=F32, precision=precision)


def _group_sum(x, gmat):
    outs = []
    for c in range(x.shape[1] // LANES):
        xc = x[:, c * LANES:(c + 1) * LANES]
        hi = xc.astype(BF16)
        lo = (xc - hi.astype(F32)).astype(BF16)
        outs.append(_dot(hi, gmat) + _dot(lo, gmat))
    return jnp.concatenate(outs, axis=1)


def _rope_chunk(xc, rc, rp, rm):
    return xc * rc + pltpu.roll(xc, 8, 1) * rp + pltpu.roll(xc, LANES - 8, 1) * rm


def _proj_even_body(x_ref, g_ref, w_ref, rc_ref, rp_ref, rm_ref,
                    q_ref, k_ref, v_ref, iq_ref, ikw_ref, cb_ref, u_ref):
    xn = _rms(x_ref[...], g_ref[...]).astype(BF16)
    h = _dot(xn, w_ref[...])
    rc, rp, rm = rc_ref[...], rp_ref[...], rm_ref[...]
    col = 0
    for c in range(ATTN_DIM // LANES):
        q_ref[:, c * LANES:(c + 1) * LANES] = _rope_chunk(h[:, col:col + LANES], rc, rp, rm).astype(BF16)
        col += LANES
    k_ref[...] = _rope_chunk(h[:, col:col + LANES], rc, rp, rm)
    col += LANES
    v_ref[...] = h[:, col:col + LANES]
    col += LANES
    for c in range(N_IDX_HEADS * IDX_DIM // LANES):
        iq_ref[:, c * LANES:(c + 1) * LANES] = _rope_chunk(h[:, col:col + LANES], rc, rp, rm).astype(BF16)
        col += LANES
    ikw = h[:, col:col + LANES]
    lane = lax.broadcasted_iota(jnp.int32, ikw.shape, 1)
    ikw_ref[...] = jnp.where(lane < IDX_DIM, _rope_chunk(ikw, rc, rp, rm), ikw)
    col += LANES
    cb_ref[...] = h[:, col:col + CONV_DIM]
    col += CONV_DIM
    u_ref[...] = h[:, col:col + CONV_DIM] * h[:, col + CONV_DIM:col + 2 * CONV_DIM]


def _proj_even(x, g, w, tabs, tm):
    n = x.shape[0]
    nt = tabs[0].shape[0] // tm
    row = lambda i: (i, 0)
    const = lambda i: (0, 0)
    tab = lambda i: (i % nt, 0)
    widths = (ATTN_DIM, KV_DIM, KV_DIM, N_IDX_HEADS * IDX_DIM, LANES, CONV_DIM, CONV_DIM)
    dtypes = (BF16, F32, F32, BF16, F32, F32, F32)
    return pl.pallas_call(
        _proj_even_body,
        grid=(n // tm,),
        in_specs=[pl.BlockSpec((tm, D_MODEL), row), pl.BlockSpec((1, D_MODEL), const),
                  pl.BlockSpec((D_MODEL, PROJ_COLS), const)] + [pl.BlockSpec((tm, LANES), tab)] * 3,
        out_specs=[pl.BlockSpec((tm, wd), row) for wd in widths],
        out_shape=[jax.ShapeDtypeStruct((n, wd), dt) for wd, dt in zip(widths, dtypes)],
        compiler_params=_cparams(("parallel",)),
        name="proj_even",
    )(x, g, w, *tabs)


def _rope_tables(pos):
    half = ROT_DIM // 2
    inv_freq = ROPE_THETA ** (-jnp.arange(half, dtype=F32) / half)
    ang = pos.astype(F32)[:, None] * inv_freq[None, :]
    cos, sin = jnp.cos(ang), jnp.sin(ang)
    t = pos.shape[0]
    pad = jnp.zeros((t, HEAD_DIM - ROT_DIM), F32)
    zero = jnp.zeros((t, half), F32)
    rc = jnp.concatenate([cos, cos, pad + 1.0], axis=1)
    rp = jnp.concatenate([zero, sin, pad], axis=1)
    rm = jnp.concatenate([-sin, zero, pad], axis=1)
    return tuple(jnp.tile(a, (1, LANES // HEAD_DIM)) for a in (rc, rp, rm))


def _conv_prompt_body(cb_ref, u_ref, up_ref, w_ref, y_ref, *, tiles_per_seq):
    u = u_ref[...]
    first = pl.program_id(0) % tiles_per_seq == 0
    prev = jnp.where(first, 0.0, up_ref[...])
    p1, p2 = prev[SUBLANES - 1:SUBLANES, :], prev[SUBLANES - 2:SUBLANES - 1, :]
    r = lax.broadcasted_iota(jnp.int32, u.shape, 0)
    u1 = jnp.where(r == 0, p1, pltpu.roll(u, 1, 0))
    u2 = jnp.where(r == 0, p2, jnp.where(r == 1, p1, pltpu.roll(u, 2, 0)))
    w = w_ref[...]
    y_ref[...] = cb_ref[...] * (w[0:1, :] * u2 + w[1:2, :] * u1 + w[2:3, :] * u)


def _conv_prompt(cb, u, w, seq, tc):
    n = u.shape[0]
    row = lambda i: (i, 0)
    prev = lambda i: (jnp.maximum(i * (tc // SUBLANES) - 1, 0), 0)
    return pl.pallas_call(
        functools.partial(_conv_prompt_body, tiles_per_seq=seq // tc),
        grid=(n // tc,),
        in_specs=[pl.BlockSpec((tc, CONV_DIM), row), pl.BlockSpec((tc, CONV_DIM), row),
                  pl.BlockSpec((SUBLANES, CONV_DIM), prev), pl.BlockSpec((CONV_W, CONV_DIM), lambda i: (0, 0))],
        out_specs=pl.BlockSpec((tc, CONV_DIM), row),
        out_shape=jax.ShapeDtypeStruct((n, CONV_DIM), F32),
        compiler_params=_cparams(("parallel",)),
        name="conv_prompt",
    )(cb, u, u, w)


def _conv_sample_body(cb_ref, ue_ref, w_ref, y_ref):
    w = w_ref[...]
    for t in range(y_ref.shape[0]):
        acc = w[0:1, :] * ue_ref[t] + w[1:2, :] * ue_ref[t + 1] + w[2:3, :] * ue_ref[t + 2]
        y_ref[t] = cb_ref[t] * acc


def _conv_sample(cb_t, ue_t, w):
    return pl.pallas_call(
        _conv_sample_body,
        out_shape=jax.ShapeDtypeStruct(cb_t.shape, F32),
        name="conv_sample",
    )(cb_t, ue_t, w)


def _topk_select(scores, n_sel, tri):
    rows, width = scores.shape
    sc = scores

    def key_to_float(key):
        return lax.bitcast_convert_type(key ^ ((key >> 31) & 0x7FFFFFFF), F32)

    def count_ge(key):
        cnt = jnp.sum(jnp.where(sc >= key_to_float(key), 1.0, 0.0), axis=1, keepdims=True)
        return jnp.where(key <= KEY_NEG_INF, float(width), cnt)

    thr = jnp.where(count_ge(jnp.zeros((rows, 1), jnp.int32)) >= n_sel, 0, INT_MIN).astype(jnp.int32)

    def body(i, thr):
        cand = thr + lax.shift_left(jnp.int32(1), 30 - i)
        return jnp.where(count_ge(cand) >= n_sel, cand, thr)

    thr = lax.fori_loop(0, 31, body, thr)
    thr_f = key_to_float(thr)
    gt = sc > thr_f
    eq = sc == thr_f
    need = n_sel - jnp.sum(jnp.where(gt, 1.0, 0.0), axis=1, keepdims=True)
    off = jnp.zeros((rows, 1), F32)
    parts = []
    for c in range(width // LANES):
        eqc = jnp.where(eq[:, c * LANES:(c + 1) * LANES], 1.0, 0.0)
        incl = _dot(eqc.astype(BF16), tri)
        parts.append((incl - eqc + off) < need)
        off = off + incl[:, LANES - 1:LANES]
    tie = jnp.concatenate(parts, axis=1)
    return (gt | (eq & tie)) & (sc > -jnp.inf)


def _tri_incl():
    i = np.arange(LANES)
    return jnp.asarray((i[:, None] <= i[None, :]).astype(np.float32), BF16)


def _stack_heads(x, n):
    return jnp.concatenate([x[:, h * HEAD_DIM:(h + 1) * HEAD_DIM] for h in range(n)], axis=0)


def _index_scores(iq_st, iw_st, ik_b, keys_on_lanes=False):
    rows = iq_st.shape[0] // N_IDX_HEADS
    s = _dot(iq_st, ik_b) if keys_on_lanes else _dot_nt(iq_st, ik_b)
    term = jnp.maximum(s, 0.0) * (IDX_DIM ** -0.5) * (iw_st * (N_IDX_HEADS ** -0.5))
    acc = term[0:rows, :]
    for h in range(1, N_IDX_HEADS):
        acc = acc + term[h * rows:(h + 1) * rows, :]
    return acc


def _group_logits(qs, kg, sel, keys_on_lanes=False):
    s = (_dot(qs, kg) if keys_on_lanes else _dot_nt(qs, kg)) * (HEAD_DIM ** -0.5)
    return jnp.where(jnp.concatenate([sel] * Q_PER_KV, axis=0), s, -jnp.inf)


def _reduce_rows(x, op):
    parts = [x[c * LANES:(c + 1) * LANES, :] for c in range(x.shape[0] // LANES)]
    while len(parts) > 1:
        nxt = [op(a, b) for a, b in zip(parts[0::2], parts[1::2])]
        if len(parts) % 2:
            nxt.append(parts[-1])
        parts = nxt
    x = parts[0]
    rows = x.shape[0]
    while rows > SUBLANES:
        rows //= 2
        x = op(x[:rows, :], x[rows:, :])
    red = jnp.max if op is jnp.maximum else jnp.sum
    return red(x, axis=0, keepdims=True)


def _topk_select_keys_on_rows(sc, n_sel, tril):
    width, cols = sc.shape

    def key_to_float(key):
        return lax.bitcast_convert_type(key ^ ((key >> 31) & 0x7FFFFFFF), F32)

    def count_ge(key):
        cnt = _reduce_rows(jnp.where(sc >= key_to_float(key), 1.0, 0.0), jnp.add)
        return jnp.where(key <= KEY_NEG_INF, float(width), cnt)

    c0 = count_ge(jnp.zeros((1, cols), jnp.int32))
    thr = jnp.where(c0 >= n_sel, 0, INT_MIN).astype(jnp.int32)
    cnt = jnp.where(c0 >= n_sel, c0, float(width))

    def body(i, carry):
        thr, cnt = carry
        cand = thr + lax.shift_left(jnp.int32(1), 30 - i)
        c = count_ge(cand)
        return jnp.where(c >= n_sel, cand, thr), jnp.where(c >= n_sel, c, cnt)

    thr, cnt = lax.fori_loop(0, 31, body, (thr, cnt))
    thr_f = key_to_float(thr)

    def no_ties(_):
        return jnp.where((sc >= thr_f) & (sc > -jnp.inf), 1.0, 0.0)

    def ranked(_):
        gt = sc > thr_f
        eq = sc == thr_f
        need = n_sel - _reduce_rows(jnp.where(gt, 1.0, 0.0), jnp.add)
        off = jnp.zeros((1, cols), F32)
        parts = []
        for c in range(width // LANES):
            eqc = jnp.where(eq[c * LANES:(c + 1) * LANES, :], 1.0, 0.0)
            incl = _dot(tril, eqc.astype(BF16))
            parts.append((incl - eqc + off) < need)
            off = off + incl[LANES - 1:LANES, :]
        tie = jnp.concatenate(parts, axis=0)
        return jnp.where((gt | (eq & tie)) & (sc > -jnp.inf), 1.0, 0.0)

    return lax.cond(jnp.max(cnt) > n_sel, ranked, no_ties, None) > 0.5


def _prompt_attn_body(q_ref, iq_ref, iwq_ref, k_ref, v_ref, ikw_ref, tril_ref, o_ref,
                      kb_scr, vt_scr, ikb_scr, *, n_sel, widths):
    i = pl.program_id(1)

    @pl.when(i == 0)
    def _():
        for g in range(N_KV_HEADS):
            kb_scr[g] = k_ref[:, g * HEAD_DIM:(g + 1) * HEAD_DIM].astype(BF16)
        vt = jnp.transpose(v_ref[...]).astype(BF16)
        ones = jnp.ones((vt_scr.shape[1] - HEAD_DIM, vt.shape[1]), BF16)
        for g in range(N_KV_HEADS):
            vt_scr[g] = jnp.concatenate([vt[g * HEAD_DIM:(g + 1) * HEAD_DIM, :], ones], axis=0)
        ikb_scr[...] = ikw_ref[:, :IDX_DIM].astype(BF16)

    def run(width):
        iw_t = jnp.transpose(iwq_ref[...])
        w_row = jnp.concatenate([iw_t[IDX_DIM + h:IDX_DIM + h + 1, :] for h in range(N_IDX_HEADS)], axis=1)
        s = _dot_nt(ikb_scr[0:width, :], _stack_heads(iq_ref[...], N_IDX_HEADS))
        term = jnp.maximum(s, 0.0) * (w_row * (IDX_DIM ** -0.5 * N_IDX_HEADS ** -0.5))
        scores = term[:, 0:Q_BLOCK]
        for h in range(1, N_IDX_HEADS):
            scores = scores + term[:, h * Q_BLOCK:(h + 1) * Q_BLOCK]
        key_pos = lax.broadcasted_iota(jnp.int32, scores.shape, 0)
        tq = i * Q_BLOCK + lax.broadcasted_iota(jnp.int32, scores.shape, 1)
        scores = jnp.where(key_pos <= tq, scores, -jnp.inf)
        sel = _topk_select_keys_on_rows(scores, n_sel, tril_ref[...])
        sel4 = jnp.concatenate([sel] * Q_PER_KV, axis=1)
        q = q_ref[...]
        outs = []
        for g in range(N_KV_HEADS):
            qs = _stack_heads(q[:, g * Q_PER_KV * HEAD_DIM:(g + 1) * Q_PER_KV * HEAD_DIM], Q_PER_KV) * (HEAD_DIM ** -0.5)
            st = jnp.where(sel4, _dot_nt(kb_scr[g, 0:width, :], qs), -jnp.inf)
            p = jnp.exp(st - _reduce_rows(st, jnp.maximum)).astype(BF16)
            ot = _dot(vt_scr[g, :, 0:width], p)
            outs.append(ot[0:HEAD_DIM, :] / ot[HEAD_DIM:HEAD_DIM + 1, :])
        o = jnp.transpose(jnp.concatenate(outs, axis=0))
        for r in range(Q_PER_KV):
            o_ref[:, r * LANES:(r + 1) * LANES] = o[r * Q_BLOCK:(r + 1) * Q_BLOCK, :].astype(o_ref.dtype)

    lo = 0
    for width in widths:
        hi = width // Q_BLOCK
        pl.when((i >= lo) & (i < hi))(functools.partial(run, width))
        lo = hi


def _prompt_attention(q, iq, ikw, k, v, batch, seq):
    n_sel = min(TOPK_MAX, seq // 4)
    nqb = seq // Q_BLOCK
    n_widths = min(8, nqb)
    widths = tuple(seq * (j + 1) // n_widths for j in range(n_widths))
    blk = lambda b, i: (b * nqb + i, 0)
    full = lambda b, i: (b, 0)
    return pl.pallas_call(
        functools.partial(_prompt_attn_body, n_sel=n_sel, widths=widths),
        grid=(batch, nqb),
        in_specs=[pl.BlockSpec((Q_BLOCK, ATTN_DIM), blk), pl.BlockSpec((Q_BLOCK, N_IDX_HEADS * IDX_DIM), blk),
                  pl.BlockSpec((Q_BLOCK, LANES), blk),
                  pl.BlockSpec((seq, KV_DIM), full), pl.BlockSpec((seq, KV_DIM), full),
                  pl.BlockSpec((seq, LANES), full), pl.BlockSpec((LANES, LANES), lambda b, i: (0, 0))],
        out_specs=pl.BlockSpec((Q_BLOCK, ATTN_DIM), blk),
        out_shape=jax.ShapeDtypeStruct((batch * seq, ATTN_DIM), BF16),
        scratch_shapes=[pltpu.VMEM((N_KV_HEADS, seq, HEAD_DIM), BF16),
                        pltpu.VMEM((N_KV_HEADS, HEAD_DIM + BF16_SUBLANES, seq), BF16),
                        pltpu.VMEM((seq, IDX_DIM), BF16)],
        compiler_params=_cparams(("parallel", "arbitrary")),
        name="prompt_attention",
    )(q, iq, ikw, k, v, ikw, _tri_incl().T)


PAGES_PER_STEP = 32


def _sample_scores_body(pt_ref, iq_ref, iw_ref, *rest):
    page_refs, s_ref = rest[:PAGES_PER_STEP], rest[PAGES_PER_STEP]
    for j, pr in enumerate(page_refs):
        s_ref[0, :, j * PAGE_SIZE:(j + 1) * PAGE_SIZE] = _index_scores(iq_ref[0], iw_ref[0], pr[0].astype(BF16), True)


def _sample_scores(page_table, iq_st, iw_st, cache_ik):
    bd, n_pages = page_table.shape
    rows = iq_st.shape[1] // N_IDX_HEADS
    assert n_pages % PAGES_PER_STEP == 0
    steps = n_pages // PAGES_PER_STEP
    per_b = lambda b, p, pt: (b, 0, 0)
    page = lambda j: (lambda b, p, pt: (pt[b, p * PAGES_PER_STEP + j], 0, 0))
    return pl.pallas_call(
        _sample_scores_body,
        grid_spec=pltpu.PrefetchScalarGridSpec(
            num_scalar_prefetch=1,
            grid=(bd, steps),
            in_specs=[pl.BlockSpec((1,) + iq_st.shape[1:], per_b), pl.BlockSpec((1,) + iw_st.shape[1:], per_b)]
                     + [pl.BlockSpec((1, IDX_DIM, PAGE_SIZE), page(j)) for j in range(PAGES_PER_STEP)],
            out_specs=pl.BlockSpec((1, rows, PAGES_PER_STEP * PAGE_SIZE), lambda b, p, pt: (b, 0, p)),
        ),
        out_shape=jax.ShapeDtypeStruct((bd, rows, n_pages * PAGE_SIZE), F32),
        compiler_params=_cparams(("parallel", "arbitrary")),
        name="sample_scores",
    )(page_table, iq_st, iw_st, *([cache_ik] * PAGES_PER_STEP))


def _sample_select_body(sp_ref, iq_ref, iwq_ref, ikn_ref, tri_ref, m_ref, *, n_sel, tn):
    rows = sp_ref.shape[0]
    iw = iwq_ref[...]
    iw_st = jnp.concatenate([iw[:, IDX_DIM + h:IDX_DIM + h + 1] for h in range(N_IDX_HEADS)], axis=0)
    s_new = _index_scores(_stack_heads(iq_ref[...], N_IDX_HEADS), iw_st, ikn_ref[...].astype(BF16))
    r = lax.broadcasted_iota(jnp.int32, s_new.shape, 0)
    c = lax.broadcasted_iota(jnp.int32, s_new.shape, 1)
    same = (r // tn == c // tn) & (c <= r)
    fold = jnp.where((lax.broadcasted_iota(jnp.int32, (rows, LANES), 0) % tn)
                     == lax.broadcasted_iota(jnp.int32, (rows, LANES), 1), 1.0, 0.0)
    picked = jnp.where(same, s_new, 0.0)
    hi = picked.astype(BF16)
    mid = (picked - hi.astype(F32)).astype(BF16)
    lo = (picked - hi.astype(F32) - mid.astype(F32)).astype(BF16)
    fb = fold.astype(BF16)
    new_chunk = _dot(hi, fb) + _dot(mid, fb) + _dot(lo, fb)
    lane = lax.broadcasted_iota(jnp.int32, (rows, LANES), 1)
    tpos = lax.broadcasted_iota(jnp.int32, (rows, LANES), 0) % tn
    new_chunk = jnp.where(lane <= tpos, new_chunk, -jnp.inf)
    scores = jnp.concatenate([sp_ref[...], new_chunk], axis=1)
    sel = _topk_select(scores, n_sel, tri_ref[...])
    m_ref[...] = jnp.where(sel, 1.0, 0.0)


def _sample_select(s_past, iq, ikw, tn):
    rows, past = s_past.shape
    n_sel = min(TOPK_MAX, (past + tn) // 4)
    return pl.pallas_call(
        functools.partial(_sample_select_body, n_sel=n_sel, tn=tn),
        out_shape=jax.ShapeDtypeStruct((rows, past + LANES), F32),
        compiler_params=pltpu.CompilerParams(vmem_limit_bytes=VMEM_LIMIT),
        name="sample_select",
    )(s_past, iq, ikw, ikw[:, :IDX_DIM], _tri_incl())


def _sample_attn_body(pt_ref, q_ref, m_ref, kn_ref, vn_ref, mn_ref, *rest):
    k_refs, v_refs = rest[:PAGES_PER_STEP], rest[PAGES_PER_STEP:2 * PAGES_PER_STEP]
    o_ref, m_scr, l_scr, acc_scr = rest[2 * PAGES_PER_STEP:]
    p = pl.program_id(1)

    @pl.when(p == 0)
    def _():
        m_scr[...] = jnp.full(m_scr.shape, -jnp.inf, F32)
        l_scr[...] = jnp.zeros(l_scr.shape, F32)
        acc_scr[...] = jnp.zeros(acc_scr.shape, F32)

    def update(kb, vb, sel):
        for g in range(N_KV_HEADS):
            s = _group_logits(q_ref[0, g], kb[g * HEAD_DIM:(g + 1) * HEAD_DIM, :], sel, True)
            m_old = m_scr[g]
            m_new = jnp.maximum(m_old, jnp.max(s, axis=1, keepdims=True))
            m_safe = jnp.where(m_new == -jnp.inf, 0.0, m_new)
            alpha = jnp.exp(m_old - m_safe)
            pe = jnp.exp(s - m_safe)
            l_scr[g] = alpha * l_scr[g] + jnp.sum(pe, axis=1, keepdims=True)
            acc_scr[g] = alpha * acc_scr[g] + _dot_nt(pe.astype(BF16), vb[g * HEAD_DIM:(g + 1) * HEAD_DIM, :])
            m_scr[g] = m_new

    kb = jnp.concatenate([r[0].astype(BF16) for r in k_refs], axis=1)
    vb = jnp.concatenate([r[0].astype(BF16) for r in v_refs], axis=1)
    update(kb, vb, m_ref[0] > 0.5)

    @pl.when(p == pl.num_programs(1) - 1)
    def _():
        update(kn_ref[0].astype(BF16), vn_ref[0].astype(BF16), mn_ref[0] > 0.5)
        for g in range(N_KV_HEADS):
            l = l_scr[g]
            o_ref[0, g] = acc_scr[g] / jnp.where(l == 0.0, 1.0, l)


def _sample_attention(page_table, q_st, mask, k_new, v_new, cache_k, cache_v):
    bd, n_pages = page_table.shape
    assert n_pages % PAGES_PER_STEP == 0
    steps = n_pages // PAGES_PER_STEP
    per_b = lambda b, p, pt: (b, 0, 0)
    per_b4 = lambda b, p, pt: (b, 0, 0, 0)
    page = lambda j: (lambda b, p, pt: (pt[b, p * PAGES_PER_STEP + j], 0, 0))
    width = PAGES_PER_STEP * PAGE_SIZE
    rows = q_st.shape[2]
    mrows = mask.shape[1]
    return pl.pallas_call(
        _sample_attn_body,
        grid_spec=pltpu.PrefetchScalarGridSpec(
            num_scalar_prefetch=1,
            grid=(bd, steps),
            in_specs=[pl.BlockSpec((1,) + q_st.shape[1:], per_b4),
                      pl.BlockSpec((1, mrows, width), lambda b, p, pt: (b, 0, p)),
                      pl.BlockSpec((1, KV_DIM, PAGE_SIZE), per_b), pl.BlockSpec((1, KV_DIM, PAGE_SIZE), per_b),
                      pl.BlockSpec((1, mrows, LANES), lambda b, p, pt: (b, 0, n_pages))]
                     + [pl.BlockSpec((1, KV_DIM, PAGE_SIZE), page(j)) for j in range(PAGES_PER_STEP)] * 2,
            out_specs=pl.BlockSpec((1, N_KV_HEADS, rows, HEAD_DIM), per_b4),
            scratch_shapes=[pltpu.VMEM((N_KV_HEADS, rows, 1), F32), pltpu.VMEM((N_KV_HEADS, rows, 1), F32),
                            pltpu.VMEM((N_KV_HEADS, rows, HEAD_DIM), F32)],
        ),
        out_shape=jax.ShapeDtypeStruct((bd, N_KV_HEADS, rows, HEAD_DIM), F32),
        compiler_params=_cparams(("parallel", "arbitrary")),
        name="sample_attention",
    )(page_table, q_st, mask, k_new, v_new, mask, *([cache_k] * PAGES_PER_STEP), *([cache_v] * PAGES_PER_STEP))


def _out_even_body(x_ref, a_ref, c_ref, wa_ref, wc_ref, o_ref):
    o_ref[...] = x_ref[...] + _dot(a_ref[...], wa_ref[...]) + _dot(c_ref[...].astype(BF16), wc_ref[...])


def _out_even(x, attn, conv, wa, wc, tm):
    n = x.shape[0]
    row = lambda i: (i, 0)
    const = lambda i: (0, 0)
    return pl.pallas_call(
        _out_even_body,
        grid=(n // tm,),
        in_specs=[pl.BlockSpec((tm, D_MODEL), row), pl.BlockSpec((tm, ATTN_DIM), row),
                  pl.BlockSpec((tm, CONV_DIM), row), pl.BlockSpec((ATTN_DIM, D_MODEL), const),
                  pl.BlockSpec((CONV_DIM, D_MODEL), const)],
        out_specs=pl.BlockSpec((tm, D_MODEL), row),
        out_shape=jax.ShapeDtypeStruct((n, D_MODEL), F32),
        compiler_params=_cparams(("parallel",)),
        name="out_even",
    )(x, attn, conv, wa, wc)


def _mixer_body(*refs, n_exp, emit_sum):
    if n_exp > 1:
        x_ref, g_ref, wr_ref, wg_ref, wu_ref, wd_ref, g2_ref = refs[:7]
        outs = refs[7:]
    else:
        x_ref, g_ref, wg_ref, wu_ref, wd_ref, g2_ref = refs[:6]
        outs = refs[6:]
    n_out = 2 if emit_sum else 1
    out_refs, (xn_scr, acc_scr, cw_scr) = outs[:n_out], outs[n_out:]
    e, f = pl.program_id(1), pl.program_id(2)

    @pl.when((e == 0) & (f == 0))
    def _():
        xn = _rms(x_ref[...], g_ref[...]).astype(BF16)
        xn_scr[...] = xn
        acc_scr[...] = jnp.zeros(acc_scr.shape, F32)
        if n_exp > 1:
            logits = _dot(xn, wr_ref[...])
            lane = lax.broadcasted_iota(jnp.int32, logits.shape, 1).astype(F32)
            lg = jnp.where(lane < n_exp, logits, -jnp.inf)
            m1 = jnp.max(lg, axis=1, keepdims=True)
            i1 = jnp.min(jnp.where(lg == m1, lane, float(LANES)), axis=1, keepdims=True)
            lg2 = jnp.where(lane == i1, -jnp.inf, lg)
            m2 = jnp.max(lg2, axis=1, keepdims=True)
            i2 = jnp.min(jnp.where(lg2 == m2, lane, float(LANES)), axis=1, keepdims=True)
            e2 = jnp.exp(m2 - m1)
            cw_scr[...] = jnp.where(lane == i1, 1.0 / (1.0 + e2), 0.0) + jnp.where(lane == i2, e2 / (1.0 + e2), 0.0)

    xn = xn_scr[...]
    gate = _dot(xn, wg_ref[0])
    h = gate * jax.nn.sigmoid(gate) * _dot(xn, wu_ref[0])
    if n_exp > 1:
        lane = lax.broadcasted_iota(jnp.int32, cw_scr.shape, 1)
        h = h * jnp.sum(jnp.where(lane == e, cw_scr[...], 0.0), axis=1, keepdims=True)
    acc_scr[...] += _dot(h.astype(BF16), wd_ref[0])

    @pl.when((e == pl.num_programs(1) - 1) & (f == pl.num_programs(2) - 1))
    def _():
        y = x_ref[...] + acc_scr[...]
        if emit_sum:
            out_refs[0][...] = y
        out_refs[-1][...] = _rms(y, g2_ref[...])


def _mixer(x, g, wr, wg, wu, wd, g2, tm, tf, emit_sum):
    n = x.shape[0]
    n_exp = wg.shape[0]
    row = lambda i, e, f: (i, 0)
    const = lambda i, e, f: (0, 0)
    in_specs = [pl.BlockSpec((tm, D_MODEL), row), pl.BlockSpec((1, D_MODEL), const)]
    args = [x, g]
    if n_exp > 1:
        in_specs.append(pl.BlockSpec((D_MODEL, LANES), const))
        args.append(wr)
    in_specs += [pl.BlockSpec((1, D_MODEL, tf), lambda i, e, f: (e, 0, f)),
                 pl.BlockSpec((1, D_MODEL, tf), lambda i, e, f: (e, 0, f)),
                 pl.BlockSpec((1, tf, D_MODEL), lambda i, e, f: (e, f, 0)),
                 pl.BlockSpec((1, D_MODEL), const)]
    args += [wg, wu, wd, g2]
    n_out = 2 if emit_sum else 1
    return pl.pallas_call(
        functools.partial(_mixer_body, n_exp=n_exp, emit_sum=emit_sum),
        grid=(n // tm, n_exp, D_FF // tf),
        in_specs=in_specs,
        out_specs=[pl.BlockSpec((tm, D_MODEL), row)] * n_out,
        out_shape=[jax.ShapeDtypeStruct((n, D_MODEL), F32)] * n_out,
        scratch_shapes=[pltpu.VMEM((tm, D_MODEL), BF16), pltpu.VMEM((tm, D_MODEL), F32),
                        pltpu.VMEM((tm, LANES), F32)],
        compiler_params=_cparams(("parallel", "arbitrary", "arbitrary")),
        name="mixer_moe" if n_exp > 1 else "mixer_ffn",
    )(*args)


def _moe_route_body(x_ref, g_ref, wr_ref, tri_ref, etri_ref,
                    xn_ref, slot_ref, slot_t_ref, comb_ref, meta_ref, *, n_exp, rows_blk):
    xn = _rms(x_ref[...], g_ref[...]).astype(BF16)
    xn_ref[...] = xn
    logits = _dot(xn, wr_ref[...])
    lane = lax.broadcasted_iota(jnp.int32, logits.shape, 1).astype(F32)
    lg = jnp.where(lane < n_exp, logits, -jnp.inf)
    m1 = jnp.max(lg, axis=1, keepdims=True)
    i1 = jnp.min(jnp.where(lg == m1, lane, float(LANES)), axis=1, keepdims=True)
    lg2 = jnp.where(lane == i1, -jnp.inf, lg)
    m2 = jnp.max(lg2, axis=1, keepdims=True)
    i2 = jnp.min(jnp.where(lg2 == m2, lane, float(LANES)), axis=1, keepdims=True)
    e2 = jnp.exp(m2 - m1)
    comb_ref[...] = jnp.where(lane == i1, 1.0 / (1.0 + e2), 0.0) + jnp.where(lane == i2, e2 / (1.0 + e2), 0.0)
    member = jnp.where((lane == i1) | (lane == i2), 1.0, 0.0)
    n_tok = member.shape[0]
    pos = _dot(tri_ref[...], member.astype(BF16))
    cnt = pos[n_tok - 1:n_tok, :] + member[n_tok - 1:n_tok, :]
    padded = jnp.floor((cnt + (rows_blk - 0.5)) * (1.0 / rows_blk)) * rows_blk
    start = _dot(jnp.broadcast_to(padded, (SUBLANES, LANES)).astype(BF16), etri_ref[...])[0:1, :]
    slot = jnp.where(member > 0.0, start + pos, -1.0)
    slot_ref[...] = slot
    slot_t_ref[0] = jnp.transpose(slot)[0:SUBLANES, :]
    ends = start + padded
    first_row = lax.broadcasted_iota(jnp.int32, (1, LANES), 1).astype(F32) * rows_blk
    blk_exp = jnp.zeros((1, LANES), F32)
    for e in range(n_exp - 1):
        blk_exp = blk_exp + jnp.where(ends[:, e:e + 1] <= first_row, 1.0, 0.0)
    n_blk = jnp.floor((ends[:, n_exp - 1:n_exp] + 0.5) * (1.0 / rows_blk))
    row = lax.broadcasted_iota(jnp.int32, (SUBLANES, LANES), 0)
    meta = jnp.where(row == 0, blk_exp, jnp.where(row == 1, n_blk, 0.0))
    meta_ref[0] = meta.astype(jnp.int32)


def _moe_experts_body(be_ref, nb_ref, xn_ref, slot_ref, slot_t_ref, comb_ref, x_ref, wg_ref, wu_ref, wd_ref,
                      g2_ref, o_ref, xs_scr, ys_scr, *, rows_blk, nf):
    g, f, j = pl.program_id(0), pl.program_id(1), pl.program_id(2)
    nj = pl.num_programs(2)
    e = be_ref[g, j]
    used = j < nb_ref[g]
    n_tok = xn_ref.shape[0]
    rows = pl.ds(pl.multiple_of(j * rows_blk, rows_blk), rows_blk)
    first = j * rows_blk

    @pl.when((f == 0) & (j == 0))
    def _():
        o_ref[...] = x_ref[...]

    @pl.when(used & (f == 0))
    def _():
        srow = slot_t_ref[0, pl.ds(e, 1), :]
        want = (first + lax.broadcasted_iota(jnp.int32, (rows_blk, n_tok), 0)).astype(F32)
        onehot = jnp.where(srow == want, 1.0, 0.0).astype(BF16)
        xs_scr[rows, :] = _dot(onehot, xn_ref[...]).astype(BF16)

    @pl.when(used)
    def _():
        xs = xs_scr[rows, :]
        gate = _dot(xs, wg_ref[0])
        h = gate * jax.nn.sigmoid(gate) * _dot(xs, wu_ref[0])
        part = _dot(h.astype(BF16), wd_ref[0])

        if nf > 1:
            @pl.when(f == 0)
            def _():
                ys_scr[rows, :] = part.astype(BF16)

        if nf > 2:
            @pl.when((f > 0) & (f < nf - 1))
            def _():
                ys_scr[rows, :] = (ys_scr[rows, :].astype(F32) + part).astype(BF16)

        def scatter(total):
            lane = lax.broadcasted_iota(jnp.int32, (n_tok, LANES), 1)
            scol = jnp.sum(jnp.where(lane == e, slot_ref[...], 0.0), axis=1, keepdims=True)
            ccol = jnp.sum(jnp.where(lane == e, comb_ref[...], 0.0), axis=1, keepdims=True)
            want = (first + lax.broadcasted_iota(jnp.int32, (n_tok, rows_blk), 1)).astype(F32)
            onehot = jnp.where(scol == want, 1.0, 0.0).astype(BF16)
            tb = total.astype(BF16)
            for c in range(D_MODEL // (2 * LANES)):
                cols = slice(c * 2 * LANES, (c + 1) * 2 * LANES)
                o_ref[:, cols] += ccol * _dot(onehot, tb[:, cols])

        @pl.when(f == nf - 1)
        def _():
            scatter(part + ys_scr[rows, :].astype(F32) if nf > 1 else part)

    @pl.when((f == nf - 1) & (j == nj - 1))
    def _():
        o_ref[...] = _rms(o_ref[...], g2_ref[...])


def _moe(x, g, wr, wg, wu, wd, g2, tg, rows_blk, tf):
    n = x.shape[0]
    n_exp = wg.shape[0]
    n_groups = n // tg
    n_blk = 2 * tg // rows_blk + n_exp
    i = np.arange(tg)
    tri = jnp.asarray((i[None, :] < i[:, None]).astype(np.float32), BF16)
    i = np.arange(LANES)
    etri = jnp.asarray((i[:, None] < i[None, :]).astype(np.float32), BF16)
    row = lambda i: (i, 0)
    const = lambda i: (0, 0)
    xn, slot, slot_t, comb, meta = pl.pallas_call(
        functools.partial(_moe_route_body, n_exp=n_exp, rows_blk=rows_blk),
        grid=(n_groups,),
        in_specs=[pl.BlockSpec((tg, D_MODEL), row), pl.BlockSpec((1, D_MODEL), const),
                  pl.BlockSpec((D_MODEL, LANES), const), pl.BlockSpec((tg, tg), const),
                  pl.BlockSpec((LANES, LANES), const)],
        out_specs=[pl.BlockSpec((tg, D_MODEL), row), pl.BlockSpec((tg, LANES), row),
                   pl.BlockSpec((1, SUBLANES, tg), lambda i: (i, 0, 0)), pl.BlockSpec((tg, LANES), row),
                   pl.BlockSpec((1, SUBLANES, LANES), lambda i: (i, 0, 0))],
        out_shape=[jax.ShapeDtypeStruct((n, D_MODEL), BF16), jax.ShapeDtypeStruct((n, LANES), F32),
                   jax.ShapeDtypeStruct((n_groups, SUBLANES, tg), F32), jax.ShapeDtypeStruct((n, LANES), F32),
                   jax.ShapeDtypeStruct((n_groups, SUBLANES, LANES), jnp.int32)],
        compiler_params=_cparams(("parallel",)),
        name="moe_route",
    )(x, g, wr, tri, etri)
    blk_exp = meta[:, 0, :]
    blk_cnt = meta[:, 1, 0]
    per_g = lambda g, f, j, be, nb: (g, 0)
    once = pl.Buffered(1)
    return pl.pallas_call(
        functools.partial(_moe_experts_body, rows_blk=rows_blk, nf=D_FF // tf),
        grid_spec=pltpu.PrefetchScalarGridSpec(
            num_scalar_prefetch=2,
            grid=(n_groups, D_FF // tf, n_blk),
            in_specs=[pl.BlockSpec((tg, D_MODEL), per_g, pipeline_mode=once),
                      pl.BlockSpec((tg, LANES), per_g, pipeline_mode=once),
                      pl.BlockSpec((1, SUBLANES, tg), lambda g, f, j, be, nb: (g, 0, 0), pipeline_mode=once),
                      pl.BlockSpec((tg, LANES), per_g, pipeline_mode=once),
                      pl.BlockSpec((tg, D_MODEL), per_g, pipeline_mode=once),
                      pl.BlockSpec((1, D_MODEL, tf), lambda g, f, j, be, nb: (be[g, j], 0, f)),
                      pl.BlockSpec((1, D_MODEL, tf), lambda g, f, j, be, nb: (be[g, j], 0, f)),
                      pl.BlockSpec((1, tf, D_MODEL), lambda g, f, j, be, nb: (be[g, j], f, 0)),
                      pl.BlockSpec((1, D_MODEL), lambda g, f, j, be, nb: (0, 0))],
            out_specs=pl.BlockSpec((tg, D_MODEL), per_g, pipeline_mode=once),
            scratch_shapes=[pltpu.VMEM((n_blk * rows_blk, D_MODEL), BF16),
                            pltpu.VMEM((n_blk * rows_blk, D_MODEL), BF16)],
        ),
        out_shape=jax.ShapeDtypeStruct((n, D_MODEL), F32),
        compiler_params=pltpu.CompilerParams(dimension_semantics=("parallel", "arbitrary", "arbitrary"),
                                             vmem_limit_bytes=VMEM_LIMIT_MOE),
        name="moe_experts",
    )(blk_exp, blk_cnt, xn, slot, slot_t, comb, x, wg, wu, wd, g2)


def _group_ones():
    i = np.arange(LANES) // RWKV_HEAD
    return jnp.asarray((i[:, None] == i[None, :]).astype(np.float32), BF16)


def _rwkv_proj_body(xn_ref, xp_ref, sh_ref, mu_ref, vec_ref, wrkv_ref, w1_ref, w2_ref, a1_ref, a2_ref, g1_ref,
                    g2_ref, gm_ref, r_ref, lw_ref, k_ref, v_ref, kk_ref, ka_ref, g_ref, *, tiles_per_seq):
    xn = xn_ref[...]
    if tiles_per_seq is None:
        xp = xp_ref[...]
    else:
        first = pl.program_id(0) % tiles_per_seq == 0
        row0 = jnp.where(first, sh_ref[0], xp_ref[SUBLANES - 1:SUBLANES, :])
        r = lax.broadcasted_iota(jnp.int32, xn.shape, 0)
        xp = jnp.where(r == 0, row0, pltpu.roll(xn, 1, 0))
    xx = xp - xn
    mu = mu_ref[...]
    mix = lambda i: (xn + xx * mu[i:i + 1, :]).astype(BF16)
    vec = vec_ref[...]
    w0, a0, k_k, k_a = vec[0:1, :], vec[1:2, :], vec[2:3, :], vec[3:4, :]
    r_ref[...] = _dot(mix(0), wrkv_ref[0]).astype(r_ref.dtype)
    z = w0 + _dot(jnp.tanh(_dot(mix(1), w1_ref[...])).astype(BF16), w2_ref[...])
    softplus = jnp.maximum(-z, 0.0) + jnp.log(1.0 + jnp.exp(-jnp.abs(z)))
    lw_ref[...] = -jnp.exp(-softplus - 0.5)
    k = _dot(mix(2), wrkv_ref[1])
    v_ref[...] = _dot(mix(3), wrkv_ref[2]).astype(v_ref.dtype)
    a = jax.nn.sigmoid(a0 + _dot(_dot(mix(4), a1_ref[...]).astype(BF16), a2_ref[...]))
    g_ref[...] = _dot(jax.nn.sigmoid(_dot(mix(5), g1_ref[...])).astype(BF16), g2_ref[...]).astype(g_ref.dtype)
    kk = k * k_k
    kk = kk * lax.rsqrt(jnp.maximum(_group_sum(kk * kk, gm_ref[...]), 1e-24))
    kk_ref[...] = kk.astype(kk_ref.dtype)
    ka_ref[...] = (kk * a).astype(ka_ref.dtype)
    k_ref[...] = (k * (1.0 + (a - 1.0) * k_a)).astype(k_ref.dtype)


def _rwkv_proj(xn, xp, shift_prev, seq, mu, vec, wrkv, w1, w2, a1, a2, g1, g2, tm):
    n = xn.shape[0]
    row = lambda i: (i, 0)
    c2 = lambda i: (0, 0)
    c3 = lambda i: (0, 0, 0)
    full = lambda a: pl.BlockSpec(a.shape, c3 if a.ndim == 3 else c2)
    gm = _group_ones()
    consts = [mu, vec, wrkv, w1, w2, a1, a2, g1, g2, gm]
    if xp is None:
        tiles_per_seq = seq // tm
        prev_spec = pl.BlockSpec((SUBLANES, D_MODEL), lambda i: (jnp.maximum(i * (tm // SUBLANES) - 1, 0), 0))
        shift_spec = pl.BlockSpec((1, 1, D_MODEL), lambda i: (i // tiles_per_seq, 0, 0))
        xp = xn
    else:
        tiles_per_seq = None
        prev_spec = pl.BlockSpec((tm, D_MODEL), row)
        shift_spec = pl.BlockSpec((1, 1, D_MODEL), c3)
    dtypes = (BF16, F32, BF16, BF16, BF16, BF16, BF16)
    return pl.pallas_call(
        functools.partial(_rwkv_proj_body, tiles_per_seq=tiles_per_seq),
        grid=(n // tm,),
        in_specs=[pl.BlockSpec((tm, D_MODEL), row), prev_spec, shift_spec] + [full(a) for a in consts],
        out_specs=[pl.BlockSpec((tm, D_MODEL), row)] * 7,
        out_shape=[jax.ShapeDtypeStruct((n, D_MODEL), dt) for dt in dtypes],
        compiler_params=_cparams(("parallel",)),
        name="rwkv_proj",
    )(xn, xp, shift_prev[:, None, :], *consts)


def _wkv_body(r_ref, lw_ref, k_ref, v_ref, kk_ref, ka_ref, s0_ref, y_ref, sT_ref,
              s_scr, a_scr, r_scr, b_scr, k_scr, *, chunk):
    c = pl.program_id(1)
    n_pairs = D_MODEL // LANES
    hd = RWKV_HEAD
    zeros = jnp.zeros((hd, hd), F32)

    @pl.when(c == 0)
    def _():
        for p in range(n_pairs):
            top = jnp.concatenate([s0_ref[0, 2 * p], zeros], axis=1)
            bot = jnp.concatenate([zeros, s0_ref[0, 2 * p + 1]], axis=1)
            s_scr[p] = jnp.concatenate([top, bot], axis=0)

    row = lax.broadcasted_iota(jnp.int32, (chunk, chunk), 0)
    colm = lax.broadcasted_iota(jnp.int32, (chunk, chunk), 1)
    lw = lw_ref[...]
    cum = jnp.dot(jnp.where(row >= colm, 1.0, 0.0), lw, preferred_element_type=F32, precision=HI)
    w_incl = jnp.exp(cum)
    w_inv = jnp.exp(-cum)
    a_scr[...] = -kk_ref[...] * jnp.exp(cum - lw)
    r_scr[...] = r_ref[...] * w_incl
    b_scr[...] = ka_ref[...] * w_inv
    k_scr[...] = k_ref[...] * w_inv
    w_last = w_incl[chunk - 1:chunk, :]
    n_dbl = max(1, int(np.ceil(np.log2(chunk))))

    c2 = 2 * chunk
    head_of_row = lax.broadcasted_iota(jnp.int32, (c2, LANES), 0) // chunk
    head_of_lane = lax.broadcasted_iota(jnp.int32, (c2, LANES), 1) // hd
    own_lanes = head_of_row == head_of_lane
    tr = lax.broadcasted_iota(jnp.int32, (c2, c2), 0)
    tc = lax.broadcasted_iota(jnp.int32, (c2, c2), 1)
    same = (tr // chunk) == (tc // chunk)
    strict = same & (tr % chunk > tc % chunk)
    incl = same & (tr % chunk >= tc % chunk)

    def block_diag(x):
        return jnp.where(own_lanes, jnp.concatenate([x, x], axis=0), 0.0).astype(BF16)

    pairs = range(n_pairs)
    sls = [slice(p * LANES, (p + 1) * LANES) for p in pairs]
    ar = [jnp.concatenate([block_diag(a_scr[:, sl]), block_diag(r_scr[:, sl])], axis=0) for sl in sls]
    bk = [jnp.concatenate([block_diag(b_scr[:, sl]), block_diag(k_scr[:, sl])], axis=0) for sl in sls]
    vm = [block_diag(v_ref[:, sl]) for sl in sls]
    s_old = [s_scr[p] for p in pairs]
    gram = [_dot_nt(ar[p], bk[p]) for p in pairs]
    xs = [_dot_nt(ar[p], s_old[p].astype(BF16)) for p in pairs]
    u = [xs[p][:c2, :] + _dot(jnp.where(strict, gram[p][:c2, c2:], 0.0).astype(BF16), vm[p]) for p in pairs]
    lp = [jnp.where(strict, gram[p][:c2, :c2], 0.0).astype(BF16) for p in pairs]
    for d in range(n_dbl):
        if d + 1 < n_dbl:
            t = [_dot(lp[p], jnp.concatenate([u[p].astype(BF16), lp[p]], axis=1)) for p in pairs]
            u = [u[p] + t[p][:, :LANES] for p in pairs]
            lp = [t[p][:, LANES:].astype(BF16) for p in pairs]
        else:
            u = [u[p] + _dot(lp[p], u[p].astype(BF16)) for p in pairs]
    uv = [jnp.concatenate([u[p].astype(BF16), vm[p]], axis=0) for p in pairs]
    m_r = [jnp.concatenate([jnp.where(incl, gram[p][c2:, :c2], 0.0), jnp.where(incl, gram[p][c2:, c2:], 0.0)],
                           axis=1).astype(BF16) for p in pairs]
    y = [xs[p][c2:, :] + _dot(m_r[p], uv[p]) for p in pairs]
    for p in pairs:
        y_ref[:, sls[p]] = y[p][:chunk, :] + y[p][chunk:, :]
    s_new = [(s_old[p] + _dot_tn(uv[p], bk[p])) * w_last[:, sls[p]] for p in pairs]
    for p in pairs:
        s_scr[p] = s_new[p]

    @pl.when(c == pl.num_programs(1) - 1)
    def _():
        for p in range(n_pairs):
            s = s_scr[p]
            sT_ref[0, 2 * p] = s[:hd, :hd]
            sT_ref[0, 2 * p + 1] = s[hd:, hd:]


def _wkv(r, lw, k, v, kk, ka, s0, batch, seq, chunk):
    nc = seq // chunk
    blk = lambda b, c: (b * nc + c, 0)
    st = lambda b, c: (b, 0, 0, 0)
    state = pl.BlockSpec((1, RWKV_HEADS, RWKV_HEAD, RWKV_HEAD), st)
    return pl.pallas_call(
        functools.partial(_wkv_body, chunk=chunk),
        grid=(batch, nc),
        in_specs=[pl.BlockSpec((chunk, D_MODEL), blk)] * 6 + [state],
        out_specs=[pl.BlockSpec((chunk, D_MODEL), blk), state],
        out_shape=[jax.ShapeDtypeStruct((batch * seq, D_MODEL), F32),
                   jax.ShapeDtypeStruct(s0.shape, F32)],
        scratch_shapes=[pltpu.VMEM((D_MODEL // LANES, LANES, LANES), F32)]
                       + [pltpu.VMEM((chunk, D_MODEL), F32)] * 4,
        compiler_params=_cparams(("parallel", "arbitrary")),
        name="wkv",
    )(r, lw, k, v, kk, ka, s0)


def _rwkv_out_body(x_ref, y_ref, r_ref, k_ref, v_ref, g_ref, vec_ref, gm_ref, wo_ref, o_ref):
    gm = gm_ref[...]
    vec = vec_ref[...]
    ln_w, ln_b, r_k = vec[0:1, :], vec[1:2, :], vec[2:3, :]
    y = y_ref[...]
    mean = _group_sum(y, gm) * (1.0 / RWKV_HEAD)
    d = y - mean
    var = _group_sum(d * d, gm) * (1.0 / RWKV_HEAD)
    yn = d * lax.rsqrt(var + GN_EPS) * ln_w + ln_b
    rk = r_ref[...].astype(F32) * k_ref[...].astype(F32) * r_k
    yn = yn + _group_sum(rk, gm) * v_ref[...].astype(F32)
    o_ref[...] = x_ref[...] + _dot((yn * g_ref[...].astype(F32)).astype(BF16), wo_ref[...])


def _rwkv_out(x, y, r, k, v, g, vec, wo, tm):
    n = x.shape[0]
    row = lambda i: (i, 0)
    const = lambda i: (0, 0)
    gm = _group_ones()
    return pl.pallas_call(
        _rwkv_out_body,
        grid=(n // tm,),
        in_specs=[pl.BlockSpec((tm, D_MODEL), row)] * 6
                 + [pl.BlockSpec(vec.shape, const), pl.BlockSpec(gm.shape, const), pl.BlockSpec(wo.shape, const)],
        out_specs=pl.BlockSpec((tm, D_MODEL), row),
        out_shape=jax.ShapeDtypeStruct((n, D_MODEL), F32),
        compiler_params=_cparams(("parallel",)),
        name="rwkv_out",
    )(x, y, r, k, v, g, vec, gm, wo)


def _pick_tile(n, want):
    t = min(n, want)
    while n % t:
        t //= 2
    return t


def _trunk(x, pos, conv_prev, shift_prev, wkv_prev, wts, sample):
    batch, seq, _ = x.shape
    n = batch * seq
    xf = x.reshape(n, D_MODEL)
    tm = _pick_tile(n, 512)

    tabs = _rope_tables(pos)
    if sample is not None:
        tabs = tuple(jnp.tile(t, (batch, 1)) for t in tabs)
    q, k, v, iq, ikw, cb, u = _proj_even(xf, wts['norm_mix_even'], wts['w_in'], tabs, tm)
    if sample is None:
        attn = _prompt_attention(q, iq, ikw, k, v, batch, seq)
        conv = _conv_prompt(cb, u, wts['conv_w'], seq, _pick_tile(seq, 512))
        conv_state = u.reshape(batch, seq, CONV_DIM)[:, seq - (CONV_W - 1):]
    else:
        cache_k, cache_v, cache_ik, page_table = sample
        tpad = SUBLANES - seq

        def stack(a, heads):
            a = a.reshape(batch, seq, heads, -1).transpose(0, 2, 1, 3)
            a = jnp.pad(a, ((0, 0), (0, 0), (0, tpad), (0, 0)))
            return a.reshape(batch, heads * SUBLANES, a.shape[-1])

        iq_st = stack(iq, N_IDX_HEADS)
        iw_st = stack(ikw[:, IDX_DIM:IDX_DIM + N_IDX_HEADS], N_IDX_HEADS)
        s_past = _sample_scores(page_table, iq_st, iw_st, cache_ik)[:, :seq]
        mask = _sample_select(s_past.reshape(n, -1), iq, ikw, seq)
        mask = jnp.pad(mask.reshape(batch, seq, -1), ((0, 0), (0, tpad), (0, 0)))
        q_st = stack(q, N_Q_HEADS).reshape(batch, N_KV_HEADS, Q_PER_KV * SUBLANES, HEAD_DIM)
        padk = lambda a: jnp.pad(jnp.swapaxes(a.reshape(batch, seq, KV_DIM), 1, 2),
                                 ((0, 0), (0, 0), (0, PAGE_SIZE - seq)))
        o = _sample_attention(page_table, q_st, mask, padk(k), padk(v), cache_k, cache_v)
        attn = (o.reshape(batch, N_KV_HEADS, Q_PER_KV, SUBLANES, HEAD_DIM)[:, :, :, :seq].transpose(0, 3, 2, 1, 4)
                .reshape(n, ATTN_DIM).astype(BF16))
        u3 = jnp.concatenate([conv_prev, u.reshape(batch, seq, CONV_DIM)], axis=1)
        conv_state = u3[:, seq:]
        conv_t = _conv_sample(jnp.swapaxes(cb.reshape(batch, seq, CONV_DIM), 0, 1), jnp.swapaxes(u3, 0, 1),
                              wts['conv_w'])
        conv = jnp.swapaxes(conv_t, 0, 1).reshape(n, CONV_DIM)
    x1 = _out_even(xf, attn, conv, wts['w_out_attn'], wts['w_out_conv'], tm)
    x2, xn2 = _mixer(x1, wts['norm_ffn_even'], None, wts['ffn_gate'], wts['ffn_up'], wts['ffn_down'],
                     wts['norm_mix_odd'], tm, D_FF // 2, True)

    xn3 = xn2.reshape(batch, seq, D_MODEL)
    tm1 = _pick_tile(n, 256)
    xp = None
    if seq % tm1:
        xp = jnp.concatenate([shift_prev[:, None, :], xn3[:, :-1]], axis=1).reshape(n, D_MODEL)
    r, lw, kr, vr, kk, ka, g = _rwkv_proj(xn2, xp, shift_prev, seq, wts['rwkv_mu'], wts['rwkv_vec_in'],
                                          wts['rwkv_w_rkv'], wts['rwkv_w1'], wts['rwkv_w2'], wts['rwkv_a1'],
                                          wts['rwkv_a2'], wts['rwkv_g1'], wts['rwkv_g2'], tm1)
    if seq % WKV_CHUNK:
        sp = -(-seq // WKV_CHUNK) * WKV_CHUNK
        padded = [jnp.pad(a.reshape(batch, seq, D_MODEL), ((0, 0), (0, sp - seq), (0, 0)))
                  .reshape(batch * sp, D_MODEL) for a in (r, lw, kr, vr, kk, ka)]
        y, wkv_state = _wkv(*padded, wkv_prev, batch, sp, WKV_CHUNK)
        y = y.reshape(batch, sp, D_MODEL)[:, :seq].reshape(n, D_MODEL)
    else:
        y, wkv_state = _wkv(r, lw, kr, vr, kk, ka, wkv_prev, batch, seq, WKV_CHUNK)
    x3 = _rwkv_out(x2, y, r, kr, vr, g, wts['rwkv_vec_out'], wts['rwkv_w_o'], _pick_tile(n, 256))
    yf = _moe(x3, wts['norm_ffn_odd'], wts['moe_router'], wts['moe_gate'], wts['moe_up'], wts['moe_down'],
              wts['norm_final'], _pick_tile(n, MOE_GROUP), min(MOE_BLOCK_ROWS, _pick_tile(n, MOE_GROUP)), D_FF // 2)

    k4 = k.reshape(1, batch, seq, N_KV_HEADS, HEAD_DIM)
    v4 = v.reshape(1, batch, seq, N_KV_HEADS, HEAD_DIM)
    ik3 = ikw[:, :IDX_DIM].reshape(1, batch, seq, IDX_DIM)
    return (yf.reshape(batch, seq, D_MODEL), k4, v4, ik3, conv_state[None],
            xn3[:, -1][None], wkv_state[None])


def _prepare_weights(norm_mix_even, w_in_even, conv_w, w_out_even, norm_ffn_even, ffn_gate, ffn_up, ffn_down,
                     norm_mix_odd, rwkv_mu, rwkv_w_rkv, rwkv_w0, rwkv_w1, rwkv_w2, rwkv_a0, rwkv_a1, rwkv_a2,
                     rwkv_g1, rwkv_g2, rwkv_k_k, rwkv_k_a, rwkv_r_k, rwkv_ln_w, rwkv_ln_b, rwkv_w_o,
                     norm_ffn_odd, moe_router, moe_gate, moe_up, moe_down, norm_final):
    w_in = w_in_even[0]
    o = np.cumsum((0, ATTN_DIM, KV_DIM, KV_DIM, N_IDX_HEADS * IDX_DIM, IDX_DIM, N_IDX_HEADS,
                   CONV_DIM, CONV_DIM, CONV_DIM))
    pad = jnp.zeros((D_MODEL, LANES - IDX_DIM - N_IDX_HEADS), F32)
    w_in = jnp.concatenate([w_in[:, :o[4]], w_in[:, o[4]:o[6]], pad, w_in[:, o[6]:]], axis=1).astype(BF16)
    row = lambda a: a.reshape(1, -1)
    zeros = jnp.zeros((1, D_MODEL), F32)
    return dict(
        norm_mix_even=row(norm_mix_even[0]), w_in=w_in, conv_w=conv_w[0],
        w_out_attn=(w_out_even[0, :ATTN_DIM].reshape(N_KV_HEADS, Q_PER_KV, HEAD_DIM, D_MODEL)
                    .transpose(1, 0, 2, 3).reshape(ATTN_DIM, D_MODEL).astype(BF16)),
        w_out_conv=w_out_even[0, ATTN_DIM:].astype(BF16),
        norm_ffn_even=row(norm_ffn_even[0]),
        ffn_gate=ffn_gate.astype(BF16), ffn_up=ffn_up.astype(BF16), ffn_down=ffn_down.astype(BF16),
        norm_mix_odd=row(norm_mix_odd[0]), rwkv_mu=jnp.concatenate([rwkv_mu[0], zeros, zeros], axis=0),
        rwkv_vec_in=jnp.concatenate([row(rwkv_w0[0]), row(rwkv_a0[0]), row(rwkv_k_k[0]), row(rwkv_k_a[0]),
                                     zeros, zeros, zeros, zeros], axis=0),
        rwkv_w_rkv=rwkv_w_rkv[0].astype(BF16),
        rwkv_w1=rwkv_w1[0].astype(BF16), rwkv_w2=rwkv_w2[0].astype(BF16),
        rwkv_a1=rwkv_a1[0].astype(BF16), rwkv_a2=rwkv_a2[0].astype(BF16),
        rwkv_g1=rwkv_g1[0].astype(BF16), rwkv_g2=rwkv_g2[0].astype(BF16),
        rwkv_vec_out=jnp.concatenate([row(rwkv_ln_w[0]), row(rwkv_ln_b[0]), row(rwkv_r_k[0]),
                                      zeros, zeros, zeros, zeros, zeros], axis=0),
        rwkv_w_o=rwkv_w_o[0].astype(BF16),
        norm_ffn_odd=row(norm_ffn_odd[0]),
        moe_router=jnp.pad(moe_router[0], ((0, 0), (0, LANES - N_EXPERTS))).astype(BF16),
        moe_gate=moe_gate[0].astype(BF16), moe_up=moe_up[0].astype(BF16), moe_down=moe_down[0].astype(BF16),
        norm_final=row(norm_final),
    )


def kernel(x_prompt, x_sample, cache_k, cache_v, cache_idx_k, state_conv, state_shift, state_wkv, page_table, norm_mix_even, w_in_even, conv_w, w_out_even, norm_ffn_even, ffn_gate, ffn_up, ffn_down, norm_mix_odd, rwkv_mu, rwkv_w_rkv, rwkv_w0, rwkv_w1, rwkv_w2, rwkv_a0, rwkv_a1, rwkv_a2, rwkv_g1, rwkv_g2, rwkv_k_k, rwkv_k_a, rwkv_r_k, rwkv_ln_w, rwkv_ln_b, rwkv_w_o, norm_ffn_odd, moe_router, moe_gate, moe_up, moe_down, norm_final):
    assert w_in_even.shape[0] == 1 and rwkv_mu.shape[0] == 1, "one even and one odd layer"
    wts = _prepare_weights(norm_mix_even, w_in_even, conv_w, w_out_even, norm_ffn_even, ffn_gate, ffn_up, ffn_down,
                           norm_mix_odd, rwkv_mu, rwkv_w_rkv, rwkv_w0, rwkv_w1, rwkv_w2, rwkv_a0, rwkv_a1, rwkv_a2,
                           rwkv_g1, rwkv_g2, rwkv_k_k, rwkv_k_a, rwkv_r_k, rwkv_ln_w, rwkv_ln_b, rwkv_w_o,
                           norm_ffn_odd, moe_router, moe_gate, moe_up, moe_down, norm_final)
    b, t = x_prompt.shape[:2]
    bd, tn = x_sample.shape[:2]
    n_pool = cache_k.shape[1]
    past = page_table.shape[1] * PAGE_SIZE
    pos_prompt = jnp.arange(t, dtype=jnp.int32)
    pos_sample = past + jnp.arange(tn, dtype=jnp.int32)
    zeros = lambda *s: jnp.zeros(s, F32)
    out_p = _trunk(x_prompt, pos_prompt, zeros(b, CONV_W - 1, CONV_DIM), zeros(b, D_MODEL),
                   zeros(b, RWKV_HEADS, RWKV_HEAD, RWKV_HEAD), wts, None)
    pages_t = lambda c: jnp.transpose(c[0], (0, 2, 3, 1)).reshape(n_pool, KV_DIM, PAGE_SIZE)
    sample = (pages_t(cache_k), pages_t(cache_v), jnp.swapaxes(cache_idx_k[0], 1, 2), page_table)
    out_s = _trunk(x_sample, pos_sample, state_conv[0], state_shift[0], state_wkv[0], wts, sample)
    y_p, k_p, v_p, ik_p, conv_p, shift_p, wkv_p = out_p
    y_s, k_s, v_s, ik_s, conv_s, shift_s, wkv_s = out_s
    return (y_p, y_s, k_p, v_p, ik_p, k_s, v_s, ik_s, conv_p, conv_s, shift_p, shift_s, wkv_p, wkv_s)
```

```python
import functools

import numpy as np
import jax
import jax.numpy as jnp
from jax import lax
from jax.experimental import pallas as pl
from jax.experimental.pallas import tpu as pltpu

F32 = jnp.float32
BF16 = jnp.bfloat16
HI = lax.Precision.HIGHEST

D_MODEL = 1024
PAGE_SIZE = 128
HEAD_DIM = 64
N_Q_HEADS = 8
N_KV_HEADS = 2
Q_PER_KV = N_Q_HEADS // N_KV_HEADS
ROT_DIM = HEAD_DIM // 4
ROPE_THETA = 500000.0
N_IDX_HEADS = 4
IDX_DIM = 64
TOPK_MAX = 256
Q_BLOCK = 128
ATTN_DIM = N_Q_HEADS * HEAD_DIM
KV_DIM = N_KV_HEADS * HEAD_DIM
CONV_DIM = D_MODEL // 2
CONV_W = 3
RWKV_HEAD = 64
RWKV_HEADS = D_MODEL // RWKV_HEAD
GN_EPS = 64e-5
D_FF = 2816
N_EXPERTS = 8
RMS_EPS = 1e-6

LANES = 128
SUBLANES = 8
VMEM_LIMIT = 56 * 1024 * 1024
VMEM_LIMIT_MOE = 61 * 1024 * 1024
INT_MIN = -2 ** 31
KEY_NEG_INF = INT_MIN + 0x7FFFFF
PROJ_COLS = ATTN_DIM + KV_DIM + KV_DIM + N_IDX_HEADS * IDX_DIM + LANES + 3 * CONV_DIM
WKV_CHUNK = 64
MOE_GROUP = 1024
MOE_BLOCK_ROWS = 288


def _cparams(sem):
    return pltpu.CompilerParams(dimension_semantics=sem, vmem_limit_bytes=VMEM_LIMIT)


def _rms(x, g):
    return x * lax.rsqrt(jnp.mean(x * x, axis=-1, keepdims=True) + RMS_EPS) * g


def _dot(a, b):
    return jnp.dot(a, b, preferred_element_type=F32)


def _dot_nt(a, b, precision=None):
    return lax.dot_general(a, b, (((1,), (1,)), ((), ())), preferred_element_type=F32, precision=precision)


def _dot_tn(a, b, precision=None):
    return lax.dot_general(a, b, (((0,), (0,)), ((), ())), preferred_element_type=F32, precision=precision)


def _group_sum(x, gmat):
    outs = []
    for c in range(x.shape[1] // LANES):
        xc = x[:, c * LANES:(c + 1) * LANES]
        hi = xc.astype(BF16)
        lo = (xc - hi.astype(F32)).astype(BF16)
        outs.append(_dot(hi, gmat) + _dot(lo, gmat))
    return jnp.concatenate(outs, axis=1)


def _rope_chunk(xc, rc, rp, rm):
    return xc * rc + pltpu.roll(xc, 8, 1) * rp + pltpu.roll(xc, LANES - 8, 1) * rm


def _proj_even_body(x_ref, g_ref, w_ref, rc_ref, rp_ref, rm_ref,
                    q_ref, k_ref, v_ref, iq_ref, ikw_ref, cb_ref, u_ref):
    xn = _rms(x_ref[...], g_ref[...]).astype(BF16)
    h = _dot(xn, w_ref[...])
    rc, rp, rm = rc_ref[...], rp_ref[...], rm_ref[...]
    col = 0
    for c in range(ATTN_DIM // LANES):
        q_ref[:, c * LANES:(c + 1) * LANES] = _rope_chunk(h[:, col:col + LANES], rc, rp, rm).astype(BF16)
        col += LANES
    k_ref[...] = _rope_chunk(h[:, col:col + LANES], rc, rp, rm)
    col += LANES
    v_ref[...] = h[:, col:col + LANES]
    col += LANES
    for c in range(N_IDX_HEADS * IDX_DIM // LANES):
        iq_ref[:, c * LANES:(c + 1) * LANES] = _rope_chunk(h[:, col:col + LANES], rc, rp, rm).astype(BF16)
        col += LANES
    ikw = h[:, col:col + LANES]
    lane = lax.broadcasted_iota(jnp.int32, ikw.shape, 1)
    ikw_ref[...] = jnp.where(lane < IDX_DIM, _rope_chunk(ikw, rc, rp, rm), ikw)
    col += LANES
    cb_ref[...] = h[:, col:col + CONV_DIM]
    col += CONV_DIM
    u_ref[...] = h[:, col:col + CONV_DIM] * h[:, col + CONV_DIM:col + 2 * CONV_DIM]


def _proj_even(x, g, w, tabs, tm):
    n = x.shape[0]
    nt = tabs[0].shape[0] // tm
    row = lambda i: (i, 0)
    const = lambda i: (0, 0)
    tab = lambda i: (i % nt, 0)
    widths = (ATTN_DIM, KV_DIM, KV_DIM, N_IDX_HEADS * IDX_DIM, LANES, CONV_DIM, CONV_DIM)
    dtypes = (BF16, F32, F32, BF16, F32, F32, F32)
    return pl.pallas_call(
        _proj_even_body,
        grid=(n // tm,),
        in_specs=[pl.BlockSpec((tm, D_MODEL), row), pl.BlockSpec((1, D_MODEL), const),
                  pl.BlockSpec((D_MODEL, PROJ_COLS), const)] + [pl.BlockSpec((tm, LANES), tab)] * 3,
        out_specs=[pl.BlockSpec((tm, wd), row) for wd in widths],
        out_shape=[jax.ShapeDtypeStruct((n, wd), dt) for wd, dt in zip(widths, dtypes)],
        compiler_params=_cparams(("parallel",)),
        name="proj_even",
    )(x, g, w, *tabs)


def _rope_tables(pos):
    half = ROT_DIM // 2
    inv_freq = ROPE_THETA ** (-jnp.arange(half, dtype=F32) / half)
    ang = pos.astype(F32)[:, None] * inv_freq[None, :]
    cos, sin = jnp.cos(ang), jnp.sin(ang)
    t = pos.shape[0]
    pad = jnp.zeros((t, HEAD_DIM - ROT_DIM), F32)
    zero = jnp.zeros((t, half), F32)
    rc = jnp.concatenate([cos, cos, pad + 1.0], axis=1)
    rp = jnp.concatenate([zero, sin, pad], axis=1)
    rm = jnp.concatenate([-sin, zero, pad], axis=1)
    return tuple(jnp.tile(a, (1, LANES // HEAD_DIM)) for a in (rc, rp, rm))


def _conv_prompt_body(cb_ref, u_ref, up_ref, w_ref, y_ref, *, tiles_per_seq):
    u = u_ref[...]
    first = pl.program_id(0) % tiles_per_seq == 0
    prev = jnp.where(first, 0.0, up_ref[...])
    p1, p2 = prev[SUBLANES - 1:SUBLANES, :], prev[SUBLANES - 2:SUBLANES - 1, :]
    r = lax.broadcasted_iota(jnp.int32, u.shape, 0)
    u1 = jnp.where(r == 0, p1, pltpu.roll(u, 1, 0))
    u2 = jnp.where(r == 0, p2, jnp.where(r == 1, p1, pltpu.roll(u, 2, 0)))
    w = w_ref[...]
    y_ref[...] = cb_ref[...] * (w[0:1, :] * u2 + w[1:2, :] * u1 + w[2:3, :] * u)


def _conv_prompt(cb, u, w, seq, tc):
    n = u.shape[0]
    row = lambda i: (i, 0)
    prev = lambda i: (jnp.maximum(i * (tc // SUBLANES) - 1, 0), 0)
    return pl.pallas_call(
        functools.partial(_conv_prompt_body, tiles_per_seq=seq // tc),
        grid=(n // tc,),
        in_specs=[pl.BlockSpec((tc, CONV_DIM), row), pl.BlockSpec((tc, CONV_DIM), row),
                  pl.BlockSpec((SUBLANES, CONV_DIM), prev), pl.BlockSpec((CONV_W, CONV_DIM), lambda i: (0, 0))],
        out_specs=pl.BlockSpec((tc, CONV_DIM), row),
        out_shape=jax.ShapeDtypeStruct((n, CONV_DIM), F32),
        compiler_params=_cparams(("parallel",)),
        name="conv_prompt",
    )(cb, u, u, w)


def _conv_sample_body(cb_ref, ue_ref, w_ref, y_ref):
    w = w_ref[...]
    for t in range(y_ref.shape[0]):
        acc = w[0:1, :] * ue_ref[t] + w[1:2, :] * ue_ref[t + 1] + w[2:3, :] * ue_ref[t + 2]
        y_ref[t] = cb_ref[t] * acc


def _conv_sample(cb_t, ue_t, w):
    return pl.pallas_call(
        _conv_sample_body,
        out_shape=jax.ShapeDtypeStruct(cb_t.shape, F32),
        name="conv_sample",
    )(cb_t, ue_t, w)


def _topk_select(scores, n_sel, tri):
    rows, width = scores.shape
    sc = scores

    def key_to_float(key):
        return lax.bitcast_convert_type(key ^ ((key >> 31) & 0x7FFFFFFF), F32)

    def count_ge(key):
        cnt = jnp.sum(jnp.where(sc >= key_to_float(key), 1.0, 0.0), axis=1, keepdims=True)
        return jnp.where(key <= KEY_NEG_INF, float(width), cnt)

    thr = jnp.where(count_ge(jnp.zeros((rows, 1), jnp.int32)) >= n_sel, 0, INT_MIN).astype(jnp.int32)

    def body(i, thr):
        cand = thr + lax.shift_left(jnp.int32(1), 30 - i)
        return jnp.where(count_ge(cand) >= n_sel, cand, thr)

    thr = lax.fori_loop(0, 31, body, thr)
    thr_f = key_to_float(thr)
    gt = sc > thr_f
    eq = sc == thr_f
    need = n_sel - jnp.sum(jnp.where(gt, 1.0, 0.0), axis=1, keepdims=True)
    off = jnp.zeros((rows, 1), F32)
    parts = []
    for c in range(width // LANES):
        eqc = jnp.where(eq[:, c * LANES:(c + 1) * LANES], 1.0, 0.0)
        incl = _dot(eqc.astype(BF16), tri)
        parts.append((incl - eqc + off) < need)
        off = off + incl[:, LANES - 1:LANES]
    tie = jnp.concatenate(parts, axis=1)
    return (gt | (eq & tie)) & (sc > -jnp.inf)


def _tri_incl():
    i = np.arange(LANES)
    return jnp.asarray((i[:, None] <= i[None, :]).astype(np.float32), BF16)


def _stack_heads(x, n):
    return jnp.concatenate([x[:, h * HEAD_DIM:(h + 1) * HEAD_DIM] for h in range(n)], axis=0)


def _index_scores(iq_st, iw_st, ik_b, keys_on_lanes=False):
    rows = iq_st.shape[0] // N_IDX_HEADS
    s = _dot(iq_st, ik_b) if keys_on_lanes else _dot_nt(iq_st, ik_b)
    term = jnp.maximum(s, 0.0) * (IDX_DIM ** -0.5) * (iw_st * (N_IDX_HEADS ** -0.5))
    acc = term[0:rows, :]
    for h in range(1, N_IDX_HEADS):
        acc = acc + term[h * rows:(h + 1) * rows, :]
    return acc


def _group_logits(qs, kg, sel, keys_on_lanes=False):
    s = (_dot(qs, kg) if keys_on_lanes else _dot_nt(qs, kg)) * (HEAD_DIM ** -0.5)
    return jnp.where(jnp.concatenate([sel] * Q_PER_KV, axis=0), s, -jnp.inf)


def _reduce_rows(x, op):
    parts = [x[c * LANES:(c + 1) * LANES, :] for c in range(x.shape[0] // LANES)]
    while len(parts) > 1:
        nxt = [op(a, b) for a, b in zip(parts[0::2], parts[1::2])]
        if len(parts) % 2:
            nxt.append(parts[-1])
        parts = nxt
    x = parts[0]
    rows = x.shape[0]
    while rows > SUBLANES:
        rows //= 2
        x = op(x[:rows, :], x[rows:, :])
    red = jnp.max if op is jnp.maximum else jnp.sum
    return red(x, axis=0, keepdims=True)


def _topk_select_keys_on_rows(sc, n_sel, tril):
    width, cols = sc.shape

    def key_to_float(key):
        return lax.bitcast_convert_type(key ^ ((key >> 31) & 0x7FFFFFFF), F32)

    def count_ge(key):
        cnt = _reduce_rows(jnp.where(sc >= key_to_float(key), 1.0, 0.0), jnp.add)
        return jnp.where(key <= KEY_NEG_INF, float(width), cnt)

    thr = jnp.where(count_ge(jnp.zeros((1, cols), jnp.int32)) >= n_sel, 0, INT_MIN).astype(jnp.int32)

    def body(i, thr):
        cand = thr + lax.shift_left(jnp.int32(1), 30 - i)
        return jnp.where(count_ge(cand) >= n_sel, cand, thr)

    thr = lax.fori_loop(0, 31, body, thr)
    thr_f = key_to_float(thr)
    gt = sc > thr_f
    eq = sc == thr_f
    need = n_sel - _reduce_rows(jnp.where(gt, 1.0, 0.0), jnp.add)
    off = jnp.zeros((1, cols), F32)
    parts = []
    for c in range(width // LANES):
        eqc = jnp.where(eq[c * LANES:(c + 1) * LANES, :], 1.0, 0.0)
        incl = _dot(tril, eqc.astype(BF16))
        parts.append((incl - eqc + off) < need)
        off = off + incl[LANES - 1:LANES, :]
    tie = jnp.concatenate(parts, axis=0)
    return (gt | (eq & tie)) & (sc > -jnp.inf)


def _prompt_attn_body(q_ref, iq_ref, iwq_ref, k_ref, v_ref, ikw_ref, tril_ref, o_ref,
                      kb_scr, vt_scr, ikb_scr, *, n_sel, widths):
    i = pl.program_id(1)

    @pl.when(i == 0)
    def _():
        for g in range(N_KV_HEADS):
            kb_scr[g] = k_ref[:, g * HEAD_DIM:(g + 1) * HEAD_DIM].astype(BF16)
        vt_scr[...] = jnp.transpose(v_ref[...]).astype(BF16)
        ikb_scr[...] = ikw_ref[:, :IDX_DIM].astype(BF16)

    def run(width):
        iw_t = jnp.transpose(iwq_ref[...])
        w_row = jnp.concatenate([iw_t[IDX_DIM + h:IDX_DIM + h + 1, :] for h in range(N_IDX_HEADS)], axis=1)
        s = _dot_nt(ikb_scr[0:width, :], _stack_heads(iq_ref[...], N_IDX_HEADS))
        term = jnp.maximum(s, 0.0) * (w_row * (IDX_DIM ** -0.5 * N_IDX_HEADS ** -0.5))
        scores = term[:, 0:Q_BLOCK]
        for h in range(1, N_IDX_HEADS):
            scores = scores + term[:, h * Q_BLOCK:(h + 1) * Q_BLOCK]
        key_pos = lax.broadcasted_iota(jnp.int32, scores.shape, 0)
        tq = i * Q_BLOCK + lax.broadcasted_iota(jnp.int32, scores.shape, 1)
        scores = jnp.where(key_pos <= tq, scores, -jnp.inf)
        sel = _topk_select_keys_on_rows(scores, n_sel, tril_ref[...])
        sel4 = jnp.concatenate([sel] * Q_PER_KV, axis=1)
        q = q_ref[...]
        outs = []
        for g in range(N_KV_HEADS):
            qs = _stack_heads(q[:, g * Q_PER_KV * HEAD_DIM:(g + 1) * Q_PER_KV * HEAD_DIM], Q_PER_KV) * (HEAD_DIM ** -0.5)
            st = jnp.where(sel4, _dot_nt(kb_scr[g, 0:width, :], qs), -jnp.inf)
            p = jnp.exp(st - _reduce_rows(st, jnp.maximum))
            ot = _dot(vt_scr[g * HEAD_DIM:(g + 1) * HEAD_DIM, 0:width], p.astype(BF16))
            outs.append(ot / _reduce_rows(p, jnp.add))
        o = jnp.transpose(jnp.concatenate(outs, axis=0))
        for r in range(Q_PER_KV):
            o_ref[:, r * LANES:(r + 1) * LANES] = o[r * Q_BLOCK:(r + 1) * Q_BLOCK, :].astype(o_ref.dtype)

    lo = 0
    for width in widths:
        hi = width // Q_BLOCK
        pl.when((i >= lo) & (i < hi))(functools.partial(run, width))
        lo = hi


def _prompt_attention(q, iq, ikw, k, v, batch, seq):
    n_sel = min(TOPK_MAX, seq // 4)
    nqb = seq // Q_BLOCK
    n_widths = min(8, nqb)
    widths = tuple(seq * (j + 1) // n_widths for j in range(n_widths))
    blk = lambda b, i: (b * nqb + i, 0)
    full = lambda b, i: (b, 0)
    return pl.pallas_call(
        functools.partial(_prompt_attn_body, n_sel=n_sel, widths=widths),
        grid=(batch, nqb),
        in_specs=[pl.BlockSpec((Q_BLOCK, ATTN_DIM), blk), pl.BlockSpec((Q_BLOCK, N_IDX_HEADS * IDX_DIM), blk),
                  pl.BlockSpec((Q_BLOCK, LANES), blk),
                  pl.BlockSpec((seq, KV_DIM), full), pl.BlockSpec((seq, KV_DIM), full),
                  pl.BlockSpec((seq, LANES), full), pl.BlockSpec((LANES, LANES), lambda b, i: (0, 0))],
        out_specs=pl.BlockSpec((Q_BLOCK, ATTN_DIM), blk),
        out_shape=jax.ShapeDtypeStruct((batch * seq, ATTN_DIM), BF16),
        scratch_shapes=[pltpu.VMEM((N_KV_HEADS, seq, HEAD_DIM), BF16), pltpu.VMEM((KV_DIM, seq), BF16),
                        pltpu.VMEM((seq, IDX_DIM), BF16)],
        compiler_params=_cparams(("parallel", "arbitrary")),
        name="prompt_attention",
    )(q, iq, ikw, k, v, ikw, _tri_incl().T)


PAGES_PER_STEP = 32


def _sample_scores_body(pt_ref, iq_ref, iw_ref, *rest):
    page_refs, s_ref = rest[:PAGES_PER_STEP], rest[PAGES_PER_STEP]
    for j, pr in enumerate(page_refs):
        s_ref[0, :, j * PAGE_SIZE:(j + 1) * PAGE_SIZE] = _index_scores(iq_ref[0], iw_ref[0], pr[0].astype(BF16), True)


def _sample_scores(page_table, iq_st, iw_st, cache_ik):
    bd, n_pages = page_table.shape
    rows = iq_st.shape[1] // N_IDX_HEADS
    assert n_pages % PAGES_PER_STEP == 0
    steps = n_pages // PAGES_PER_STEP
    per_b = lambda b, p, pt: (b, 0, 0)
    page = lambda j: (lambda b, p, pt: (pt[b, p * PAGES_PER_STEP + j], 0, 0))
    return pl.pallas_call(
        _sample_scores_body,
        grid_spec=pltpu.PrefetchScalarGridSpec(
            num_scalar_prefetch=1,
            grid=(bd, steps),
            in_specs=[pl.BlockSpec((1,) + iq_st.shape[1:], per_b), pl.BlockSpec((1,) + iw_st.shape[1:], per_b)]
                     + [pl.BlockSpec((1, IDX_DIM, PAGE_SIZE), page(j)) for j in range(PAGES_PER_STEP)],
            out_specs=pl.BlockSpec((1, rows, PAGES_PER_STEP * PAGE_SIZE), lambda b, p, pt: (b, 0, p)),
        ),
        out_shape=jax.ShapeDtypeStruct((bd, rows, n_pages * PAGE_SIZE), F32),
        compiler_params=_cparams(("parallel", "arbitrary")),
        name="sample_scores",
    )(page_table, iq_st, iw_st, *([cache_ik] * PAGES_PER_STEP))


def _sample_select_body(sp_ref, iq_ref, iwq_ref, ikn_ref, tri_ref, m_ref, *, n_sel, tn):
    rows = sp_ref.shape[0]
    iw = iwq_ref[...]
    iw_st = jnp.concatenate([iw[:, IDX_DIM + h:IDX_DIM + h + 1] for h in range(N_IDX_HEADS)], axis=0)
    s_new = _index_scores(_stack_heads(iq_ref[...], N_IDX_HEADS), iw_st, ikn_ref[...].astype(BF16))
    r = lax.broadcasted_iota(jnp.int32, s_new.shape, 0)
    c = lax.broadcasted_iota(jnp.int32, s_new.shape, 1)
    same = (r // tn == c // tn) & (c <= r)
    fold = jnp.where((lax.broadcasted_iota(jnp.int32, (rows, LANES), 0) % tn)
                     == lax.broadcasted_iota(jnp.int32, (rows, LANES), 1), 1.0, 0.0)
    picked = jnp.where(same, s_new, 0.0)
    hi = picked.astype(BF16)
    mid = (picked - hi.astype(F32)).astype(BF16)
    lo = (picked - hi.astype(F32) - mid.astype(F32)).astype(BF16)
    fb = fold.astype(BF16)
    new_chunk = _dot(hi, fb) + _dot(mid, fb) + _dot(lo, fb)
    lane = lax.broadcasted_iota(jnp.int32, (rows, LANES), 1)
    tpos = lax.broadcasted_iota(jnp.int32, (rows, LANES), 0) % tn
    new_chunk = jnp.where(lane <= tpos, new_chunk, -jnp.inf)
    scores = jnp.concatenate([sp_ref[...], new_chunk], axis=1)
    sel = _topk_select(scores, n_sel, tri_ref[...])
    m_ref[...] = jnp.where(sel, 1.0, 0.0)


def _sample_select(s_past, iq, ikw, tn):
    rows, past = s_past.shape
    n_sel = min(TOPK_MAX, (past + tn) // 4)
    return pl.pallas_call(
        functools.partial(_sample_select_body, n_sel=n_sel, tn=tn),
        out_shape=jax.ShapeDtypeStruct((rows, past + LANES), F32),
        compiler_params=pltpu.CompilerParams(vmem_limit_bytes=VMEM_LIMIT),
        name="sample_select",
    )(s_past, iq, ikw, ikw[:, :IDX_DIM], _tri_incl())


def _sample_attn_body(pt_ref, q_ref, m_ref, kn_ref, vn_ref, mn_ref, *rest):
    k_refs, v_refs = rest[:PAGES_PER_STEP], rest[PAGES_PER_STEP:2 * PAGES_PER_STEP]
    o_ref, m_scr, l_scr, acc_scr = rest[2 * PAGES_PER_STEP:]
    p = pl.program_id(1)

    @pl.when(p == 0)
    def _():
        m_scr[...] = jnp.full(m_scr.shape, -jnp.inf, F32)
        l_scr[...] = jnp.zeros(l_scr.shape, F32)
        acc_scr[...] = jnp.zeros(acc_scr.shape, F32)

    def update(kb, vb, sel):
        for g in range(N_KV_HEADS):
            s = _group_logits(q_ref[0, g], kb[g * HEAD_DIM:(g + 1) * HEAD_DIM, :], sel, True)
            m_old = m_scr[g]
            m_new = jnp.maximum(m_old, jnp.max(s, axis=1, keepdims=True))
            m_safe = jnp.where(m_new == -jnp.inf, 0.0, m_new)
            alpha = jnp.exp(m_old - m_safe)
            pe = jnp.exp(s - m_safe)
            l_scr[g] = alpha * l_scr[g] + jnp.sum(pe, axis=1, keepdims=True)
            acc_scr[g] = alpha * acc_scr[g] + _dot_nt(pe.astype(BF16), vb[g * HEAD_DIM:(g + 1) * HEAD_DIM, :])
            m_scr[g] = m_new

    kb = jnp.concatenate([r[0].astype(BF16) for r in k_refs], axis=1)
    vb = jnp.concatenate([r[0].astype(BF16) for r in v_refs], axis=1)
    update(kb, vb, m_ref[0] > 0.5)

    @pl.when(p == pl.num_programs(1) - 1)
    def _():
        update(kn_ref[0].astype(BF16), vn_ref[0].astype(BF16), mn_ref[0] > 0.5)
        for g in range(N_KV_HEADS):
            l = l_scr[g]
            o_ref[0, g] = acc_scr[g] / jnp.where(l == 0.0, 1.0, l)


def _sample_attention(page_table, q_st, mask, k_new, v_new, cache_k, cache_v):
    bd, n_pages = page_table.shape
    assert n_pages % PAGES_PER_STEP == 0
    steps = n_pages // PAGES_PER_STEP
    per_b = lambda b, p, pt: (b, 0, 0)
    per_b4 = lambda b, p, pt: (b, 0, 0, 0)
    page = lambda j: (lambda b, p, pt: (pt[b, p * PAGES_PER_STEP + j], 0, 0))
    width = PAGES_PER_STEP * PAGE_SIZE
    rows = q_st.shape[2]
    mrows = mask.shape[1]
    return pl.pallas_call(
        _sample_attn_body,
        grid_spec=pltpu.PrefetchScalarGridSpec(
            num_scalar_prefetch=1,
            grid=(bd, steps),
            in_specs=[pl.BlockSpec((1,) + q_st.shape[1:], per_b4),
                      pl.BlockSpec((1, mrows, width), lambda b, p, pt: (b, 0, p)),
                      pl.BlockSpec((1, KV_DIM, PAGE_SIZE), per_b), pl.BlockSpec((1, KV_DIM, PAGE_SIZE), per_b),
                      pl.BlockSpec((1, mrows, LANES), lambda b, p, pt: (b, 0, n_pages))]
                     + [pl.BlockSpec((1, KV_DIM, PAGE_SIZE), page(j)) for j in range(PAGES_PER_STEP)] * 2,
            out_specs=pl.BlockSpec((1, N_KV_HEADS, rows, HEAD_DIM), per_b4),
            scratch_shapes=[pltpu.VMEM((N_KV_HEADS, rows, 1), F32), pltpu.VMEM((N_KV_HEADS, rows, 1), F32),
                            pltpu.VMEM((N_KV_HEADS, rows, HEAD_DIM), F32)],
        ),
        out_shape=jax.ShapeDtypeStruct((bd, N_KV_HEADS, rows, HEAD_DIM), F32),
        compiler_params=_cparams(("parallel", "arbitrary")),
        name="sample_attention",
    )(page_table, q_st, mask, k_new, v_new, mask, *([cache_k] * PAGES_PER_STEP), *([cache_v] * PAGES_PER_STEP))


def _out_even_body(x_ref, a_ref, c_ref, wa_ref, wc_ref, o_ref):
    o_ref[...] = x_ref[...] + _dot(a_ref[...], wa_ref[...]) + _dot(c_ref[...].astype(BF16), wc_ref[...])


def _out_even(x, attn, conv, wa, wc, tm):
    n = x.shape[0]
    row = lambda i: (i, 0)
    const = lambda i: (0, 0)
    return pl.pallas_call(
        _out_even_body,
        grid=(n // tm,),
        in_specs=[pl.BlockSpec((tm, D_MODEL), row), pl.BlockSpec((tm, ATTN_DIM), row),
                  pl.BlockSpec((tm, CONV_DIM), row), pl.BlockSpec((ATTN_DIM, D_MODEL), const),
                  pl.BlockSpec((CONV_DIM, D_MODEL), const)],
        out_specs=pl.BlockSpec((tm, D_MODEL), row),
        out_shape=jax.ShapeDtypeStruct((n, D_MODEL), F32),
        compiler_params=_cparams(("parallel",)),
        name="out_even",
    )(x, attn, conv, wa, wc)


def _mixer_body(*refs, n_exp, emit_sum):
    if n_exp > 1:
        x_ref, g_ref, wr_ref, wg_ref, wu_ref, wd_ref, g2_ref = refs[:7]
        outs = refs[7:]
    else:
        x_ref, g_ref, wg_ref, wu_ref, wd_ref, g2_ref = refs[:6]
        outs = refs[6:]
    n_out = 2 if emit_sum else 1
    out_refs, (xn_scr, acc_scr, cw_scr) = outs[:n_out], outs[n_out:]
    e, f = pl.program_id(1), pl.program_id(2)

    @pl.when((e == 0) & (f == 0))
    def _():
        xn = _rms(x_ref[...], g_ref[...]).astype(BF16)
        xn_scr[...] = xn
        acc_scr[...] = jnp.zeros(acc_scr.shape, F32)
        if n_exp > 1:
            logits = _dot(xn, wr_ref[...])
            lane = lax.broadcasted_iota(jnp.int32, logits.shape, 1).astype(F32)
            lg = jnp.where(lane < n_exp, logits, -jnp.inf)
            m1 = jnp.max(lg, axis=1, keepdims=True)
            i1 = jnp.min(jnp.where(lg == m1, lane, float(LANES)), axis=1, keepdims=True)
            lg2 = jnp.where(lane == i1, -jnp.inf, lg)
            m2 = jnp.max(lg2, axis=1, keepdims=True)
            i2 = jnp.min(jnp.where(lg2 == m2, lane, float(LANES)), axis=1, keepdims=True)
            e2 = jnp.exp(m2 - m1)
            cw_scr[...] = jnp.where(lane == i1, 1.0 / (1.0 + e2), 0.0) + jnp.where(lane == i2, e2 / (1.0 + e2), 0.0)

    xn = xn_scr[...]
    gate = _dot(xn, wg_ref[0])
    h = gate * jax.nn.sigmoid(gate) * _dot(xn, wu_ref[0])
    if n_exp > 1:
        lane = lax.broadcasted_iota(jnp.int32, cw_scr.shape, 1)
        h = h * jnp.sum(jnp.where(lane == e, cw_scr[...], 0.0), axis=1, keepdims=True)
    acc_scr[...] += _dot(h.astype(BF16), wd_ref[0])

    @pl.when((e == pl.num_programs(1) - 1) & (f == pl.num_programs(2) - 1))
    def _():
        y = x_ref[...] + acc_scr[...]
        if emit_sum:
            out_refs[0][...] = y
        out_refs[-1][...] = _rms(y, g2_ref[...])


def _mixer(x, g, wr, wg, wu, wd, g2, tm, tf, emit_sum):
    n = x.shape[0]
    n_exp = wg.shape[0]
    row = lambda i, e, f: (i, 0)
    const = lambda i, e, f: (0, 0)
    in_specs = [pl.BlockSpec((tm, D_MODEL), row), pl.BlockSpec((1, D_MODEL), const)]
    args = [x, g]
    if n_exp > 1:
        in_specs.append(pl.BlockSpec((D_MODEL, LANES), const))
        args.append(wr)
    in_specs += [pl.BlockSpec((1, D_MODEL, tf), lambda i, e, f: (e, 0, f)),
                 pl.BlockSpec((1, D_MODEL, tf), lambda i, e, f: (e, 0, f)),
                 pl.BlockSpec((1, tf, D_MODEL), lambda i, e, f: (e, f, 0)),
                 pl.BlockSpec((1, D_MODEL), const)]
    args += [wg, wu, wd, g2]
    n_out = 2 if emit_sum else 1
    return pl.pallas_call(
        functools.partial(_mixer_body, n_exp=n_exp, emit_sum=emit_sum),
        grid=(n // tm, n_exp, D_FF // tf),
        in_specs=in_specs,
        out_specs=[pl.BlockSpec((tm, D_MODEL), row)] * n_out,
        out_shape=[jax.ShapeDtypeStruct((n, D_MODEL), F32)] * n_out,
        scratch_shapes=[pltpu.VMEM((tm, D_MODEL), BF16), pltpu.VMEM((tm, D_MODEL), F32),
                        pltpu.VMEM((tm, LANES), F32)],
        compiler_params=_cparams(("parallel", "arbitrary", "arbitrary")),
        name="mixer_moe" if n_exp > 1 else "mixer_ffn",
    )(*args)


def _moe_route_body(x_ref, g_ref, wr_ref, tri_ref, etri_ref,
                    xn_ref, slot_ref, slot_t_ref, comb_ref, meta_ref, *, n_exp, rows_blk):
    xn = _rms(x_ref[...], g_ref[...]).astype(BF16)
    xn_ref[...] = xn
    logits = _dot(xn, wr_ref[...])
    lane = lax.broadcasted_iota(jnp.int32, logits.shape, 1).astype(F32)
    lg = jnp.where(lane < n_exp, logits, -jnp.inf)
    m1 = jnp.max(lg, axis=1, keepdims=True)
    i1 = jnp.min(jnp.where(lg == m1, lane, float(LANES)), axis=1, keepdims=True)
    lg2 = jnp.where(lane == i1, -jnp.inf, lg)
    m2 = jnp.max(lg2, axis=1, keepdims=True)
    i2 = jnp.min(jnp.where(lg2 == m2, lane, float(LANES)), axis=1, keepdims=True)
    e2 = jnp.exp(m2 - m1)
    comb_ref[...] = jnp.where(lane == i1, 1.0 / (1.0 + e2), 0.0) + jnp.where(lane == i2, e2 / (1.0 + e2), 0.0)
    member = jnp.where((lane == i1) | (lane == i2), 1.0, 0.0)
    n_tok = member.shape[0]
    pos = _dot(tri_ref[...], member.astype(BF16))
    cnt = pos[n_tok - 1:n_tok, :] + member[n_tok - 1:n_tok, :]
    padded = jnp.floor((cnt + (rows_blk - 0.5)) * (1.0 / rows_blk)) * rows_blk
    start = _dot(jnp.broadcast_to(padded, (SUBLANES, LANES)).astype(BF16), etri_ref[...])[0:1, :]
    slot = jnp.where(member > 0.0, start + pos, -1.0)
    slot_ref[...] = slot
    slot_t_ref[0] = jnp.transpose(slot)[0:SUBLANES, :]
    ends = start + padded
    first_row = lax.broadcasted_iota(jnp.int32, (1, LANES), 1).astype(F32) * rows_blk
    blk_exp = jnp.zeros((1, LANES), F32)
    for e in range(n_exp - 1):
        blk_exp = blk_exp + jnp.where(ends[:, e:e + 1] <= first_row, 1.0, 0.0)
    n_blk = jnp.floor((ends[:, n_exp - 1:n_exp] + 0.5) * (1.0 / rows_blk))
    row = lax.broadcasted_iota(jnp.int32, (SUBLANES, LANES), 0)
    meta = jnp.where(row == 0, blk_exp, jnp.where(row == 1, n_blk, 0.0))
    meta_ref[0] = meta.astype(jnp.int32)


def _moe_experts_body(be_ref, nb_ref, xn_ref, slot_ref, slot_t_ref, comb_ref, x_ref, wg_ref, wu_ref, wd_ref,
                      g2_ref, o_ref, xs_scr, ys_scr, *, rows_blk, nf):
    g, f, j = pl.program_id(0), pl.program_id(1), pl.program_id(2)
    nj = pl.num_programs(2)
    e = be_ref[g, j]
    used = j < nb_ref[g]
    n_tok = xn_ref.shape[0]
    rows = pl.ds(pl.multiple_of(j * rows_blk, rows_blk), rows_blk)
    first = j * rows_blk

    @pl.when((f == 0) & (j == 0))
    def _():
        o_ref[...] = x_ref[...]

    @pl.when(used & (f == 0))
    def _():
        srow = slot_t_ref[0, pl.ds(e, 1), :]
        want = (first + lax.broadcasted_iota(jnp.int32, (rows_blk, n_tok), 0)).astype(F32)
        onehot = jnp.where(srow == want, 1.0, 0.0).astype(BF16)
        xs_scr[rows, :] = _dot(onehot, xn_ref[...]).astype(BF16)

    @pl.when(used)
    def _():
        xs = xs_scr[rows, :]
        gate = _dot(xs, wg_ref[0])
        h = gate * jax.nn.sigmoid(gate) * _dot(xs, wu_ref[0])
        part = _dot(h.astype(BF16), wd_ref[0])

        if nf > 1:
            @pl.when(f == 0)
            def _():
                ys_scr[rows, :] = part.astype(BF16)

        if nf > 2:
            @pl.when((f > 0) & (f < nf - 1))
            def _():
                ys_scr[rows, :] = (ys_scr[rows, :].astype(F32) + part).astype(BF16)

        def scatter(total):
            lane = lax.broadcasted_iota(jnp.int32, (n_tok, LANES), 1)
            scol = jnp.sum(jnp.where(lane == e, slot_ref[...], 0.0), axis=1, keepdims=True)
            ccol = jnp.sum(jnp.where(lane == e, comb_ref[...], 0.0), axis=1, keepdims=True)
            want = (first + lax.broadcasted_iota(jnp.int32, (n_tok, rows_blk), 1)).astype(F32)
            onehot = jnp.where(scol == want, 1.0, 0.0).astype(BF16)
            tb = total.astype(BF16)
            for c in range(D_MODEL // (2 * LANES)):
                cols = slice(c * 2 * LANES, (c + 1) * 2 * LANES)
                o_ref[:, cols] += ccol * _dot(onehot, tb[:, cols])

        @pl.when(f == nf - 1)
        def _():
            scatter(part + ys_scr[rows, :].astype(F32) if nf > 1 else part)

    @pl.when((f == nf - 1) & (j == nj - 1))
    def _():
        o_ref[...] = _rms(o_ref[...], g2_ref[...])


def _moe(x, g, wr, wg, wu, wd, g2, tg, rows_blk, tf):
    n = x.shape[0]
    n_exp = wg.shape[0]
    n_groups = n // tg
    n_blk = 2 * tg // rows_blk + n_exp
    i = np.arange(tg)
    tri = jnp.asarray((i[None, :] < i[:, None]).astype(np.float32), BF16)
    i = np.arange(LANES)
    etri = jnp.asarray((i[:, None] < i[None, :]).astype(np.float32), BF16)
    row = lambda i: (i, 0)
    const = lambda i: (0, 0)
    xn, slot, slot_t, comb, meta = pl.pallas_call(
        functools.partial(_moe_route_body, n_exp=n_exp, rows_blk=rows_blk),
        grid=(n_groups,),
        in_specs=[pl.BlockSpec((tg, D_MODEL), row), pl.BlockSpec((1, D_MODEL), const),
                  pl.BlockSpec((D_MODEL, LANES), const), pl.BlockSpec((tg, tg), const),
                  pl.BlockSpec((LANES, LANES), const)],
        out_specs=[pl.BlockSpec((tg, D_MODEL), row), pl.BlockSpec((tg, LANES), row),
                   pl.BlockSpec((1, SUBLANES, tg), lambda i: (i, 0, 0)), pl.BlockSpec((tg, LANES), row),
                   pl.BlockSpec((1, SUBLANES, LANES), lambda i: (i, 0, 0))],
        out_shape=[jax.ShapeDtypeStruct((n, D_MODEL), BF16), jax.ShapeDtypeStruct((n, LANES), F32),
                   jax.ShapeDtypeStruct((n_groups, SUBLANES, tg), F32), jax.ShapeDtypeStruct((n, LANES), F32),
                   jax.ShapeDtypeStruct((n_groups, SUBLANES, LANES), jnp.int32)],
        compiler_params=_cparams(("parallel",)),
        name="moe_route",
    )(x, g, wr, tri, etri)
    blk_exp = meta[:, 0, :]
    blk_cnt = meta[:, 1, 0]
    per_g = lambda g, f, j, be, nb: (g, 0)
    once = pl.Buffered(1)
    return pl.pallas_call(
        functools.partial(_moe_experts_body, rows_blk=rows_blk, nf=D_FF // tf),
        grid_spec=pltpu.PrefetchScalarGridSpec(
            num_scalar_prefetch=2,
            grid=(n_groups, D_FF // tf, n_blk),
            in_specs=[pl.BlockSpec((tg, D_MODEL), per_g, pipeline_mode=once),
                      pl.BlockSpec((tg, LANES), per_g, pipeline_mode=once),
                      pl.BlockSpec((1, SUBLANES, tg), lambda g, f, j, be, nb: (g, 0, 0), pipeline_mode=once),
                      pl.BlockSpec((tg, LANES), per_g, pipeline_mode=once),
                      pl.BlockSpec((tg, D_MODEL), per_g, pipeline_mode=once),
                      pl.BlockSpec((1, D_MODEL, tf), lambda g, f, j, be, nb: (be[g, j], 0, f)),
                      pl.BlockSpec((1, D_MODEL, tf), lambda g, f, j, be, nb: (be[g, j], 0, f)),
                      pl.BlockSpec((1, tf, D_MODEL), lambda g, f, j, be, nb: (be[g, j], f, 0)),
                      pl.BlockSpec((1, D_MODEL), lambda g, f, j, be, nb: (0, 0))],
            out_specs=pl.BlockSpec((tg, D_MODEL), per_g, pipeline_mode=once),
            scratch_shapes=[pltpu.VMEM((n_blk * rows_blk, D_MODEL), BF16),
                            pltpu.VMEM((n_blk * rows_blk, D_MODEL), BF16)],
        ),
        out_shape=jax.ShapeDtypeStruct((n, D_MODEL), F32),
        compiler_params=pltpu.CompilerParams(dimension_semantics=("parallel", "arbitrary", "arbitrary"),
                                             vmem_limit_bytes=VMEM_LIMIT_MOE),
        name="moe_experts",
    )(blk_exp, blk_cnt, xn, slot, slot_t, comb, x, wg, wu, wd, g2)


def _group_ones():
    i = np.arange(LANES) // RWKV_HEAD
    return jnp.asarray((i[:, None] == i[None, :]).astype(np.float32), BF16)


def _rwkv_proj_body(xn_ref, xp_ref, sh_ref, mu_ref, vec_ref, wrkv_ref, w1_ref, w2_ref, a1_ref, a2_ref, g1_ref,
                    g2_ref, gm_ref, r_ref, lw_ref, k_ref, v_ref, kk_ref, ka_ref, g_ref, *, tiles_per_seq):
    xn = xn_ref[...]
    if tiles_per_seq is None:
        xp = xp_ref[...]
    else:
        first = pl.program_id(0) % tiles_per_seq == 0
        row0 = jnp.where(first, sh_ref[0], xp_ref[SUBLANES - 1:SUBLANES, :])
        r = lax.broadcasted_iota(jnp.int32, xn.shape, 0)
        xp = jnp.where(r == 0, row0, pltpu.roll(xn, 1, 0))
    xx = xp - xn
    mu = mu_ref[...]
    mix = lambda i: (xn + xx * mu[i:i + 1, :]).astype(BF16)
    vec = vec_ref[...]
    w0, a0, k_k, k_a = vec[0:1, :], vec[1:2, :], vec[2:3, :], vec[3:4, :]
    r_ref[...] = _dot(mix(0), wrkv_ref[0]).astype(r_ref.dtype)
    z = w0 + _dot(jnp.tanh(_dot(mix(1), w1_ref[...])).astype(BF16), w2_ref[...])
    lw_ref[...] = jax.nn.sigmoid(z) * (-float(np.exp(-0.5)))
    k = _dot(mix(2), wrkv_ref[1])
    v_ref[...] = _dot(mix(3), wrkv_ref[2]).astype(v_ref.dtype)
    a = jax.nn.sigmoid(a0 + _dot(_dot(mix(4), a1_ref[...]).astype(BF16), a2_ref[...]))
    g_ref[...] = _dot(jax.nn.sigmoid(_dot(mix(5), g1_ref[...])).astype(BF16), g2_ref[...]).astype(g_ref.dtype)
    kk = k * k_k
    kk = kk * lax.rsqrt(jnp.maximum(_group_sum(kk * kk, gm_ref[...]), 1e-24))
    kk_ref[...] = kk.astype(kk_ref.dtype)
    ka_ref[...] = (kk * a).astype(ka_ref.dtype)
    k_ref[...] = (k * (1.0 + (a - 1.0) * k_a)).astype(k_ref.dtype)


def _rwkv_proj(xn, xp, shift_prev, seq, mu, vec, wrkv, w1, w2, a1, a2, g1, g2, tm):
    n = xn.shape[0]
    row = lambda i: (i, 0)
    c2 = lambda i: (0, 0)
    c3 = lambda i: (0, 0, 0)
    full = lambda a: pl.BlockSpec(a.shape, c3 if a.ndim == 3 else c2)
    gm = _group_ones()
    consts = [mu, vec, wrkv, w1, w2, a1, a2, g1, g2, gm]
    if xp is None:
        tiles_per_seq = seq // tm
        prev_spec = pl.BlockSpec((SUBLANES, D_MODEL), lambda i: (jnp.maximum(i * (tm // SUBLANES) - 1, 0), 0))
        shift_spec = pl.BlockSpec((1, 1, D_MODEL), lambda i: (i // tiles_per_seq, 0, 0))
        xp = xn
    else:
        tiles_per_seq = None
        prev_spec = pl.BlockSpec((tm, D_MODEL), row)
        shift_spec = pl.BlockSpec((1, 1, D_MODEL), c3)
    dtypes = (BF16, F32, BF16, BF16, BF16, BF16, BF16)
    return pl.pallas_call(
        functools.partial(_rwkv_proj_body, tiles_per_seq=tiles_per_seq),
        grid=(n // tm,),
        in_specs=[pl.BlockSpec((tm, D_MODEL), row), prev_spec, shift_spec] + [full(a) for a in consts],
        out_specs=[pl.BlockSpec((tm, D_MODEL), row)] * 7,
        out_shape=[jax.ShapeDtypeStruct((n, D_MODEL), dt) for dt in dtypes],
        compiler_params=_cparams(("parallel",)),
        name="rwkv_proj",
    )(xn, xp, shift_prev[:, None, :], *consts)


def _wkv_body(r_ref, lw_ref, k_ref, v_ref, kk_ref, ka_ref, s0_ref, y_ref, sT_ref,
              s_scr, a_scr, r_scr, b_scr, k_scr, *, chunk):
    c = pl.program_id(1)
    n_pairs = D_MODEL // LANES
    hd = RWKV_HEAD
    zeros = jnp.zeros((hd, hd), F32)

    @pl.when(c == 0)
    def _():
        for p in range(n_pairs):
            top = jnp.concatenate([s0_ref[0, 2 * p], zeros], axis=1)
            bot = jnp.concatenate([zeros, s0_ref[0, 2 * p + 1]], axis=1)
            s_scr[p] = jnp.concatenate([top, bot], axis=0)

    row = lax.broadcasted_iota(jnp.int32, (chunk, chunk), 0)
    colm = lax.broadcasted_iota(jnp.int32, (chunk, chunk), 1)
    lw = lw_ref[...]
    cum = jnp.dot(jnp.where(row >= colm, 1.0, 0.0), lw, preferred_element_type=F32, precision=HI)
    w_incl = jnp.exp(cum)
    w_inv = jnp.exp(-cum)
    a_scr[...] = -kk_ref[...] * jnp.exp(cum - lw)
    r_scr[...] = r_ref[...] * w_incl
    b_scr[...] = ka_ref[...] * w_inv
    k_scr[...] = k_ref[...] * w_inv
    w_last = w_incl[chunk - 1:chunk, :]
    n_dbl = max(1, int(np.ceil(np.log2(chunk))))

    c2 = 2 * chunk
    head_of_row = lax.broadcasted_iota(jnp.int32, (c2, LANES), 0) // chunk
    head_of_lane = lax.broadcasted_iota(jnp.int32, (c2, LANES), 1) // hd
    own_lanes = head_of_row == head_of_lane
    tr = lax.broadcasted_iota(jnp.int32, (c2, c2), 0)
    tc = lax.broadcasted_iota(jnp.int32, (c2, c2), 1)
    same = (tr // chunk) == (tc // chunk)
    strict = same & (tr % chunk > tc % chunk)
    incl = same & (tr % chunk >= tc % chunk)

    def block_diag(x):
        return jnp.where(own_lanes, jnp.concatenate([x, x], axis=0), 0.0).astype(BF16)

    pairs = range(n_pairs)
    sls = [slice(p * LANES, (p + 1) * LANES) for p in pairs]
    ar = [jnp.concatenate([block_diag(a_scr[:, sl]), block_diag(r_scr[:, sl])], axis=0) for sl in sls]
    bk = [jnp.concatenate([block_diag(b_scr[:, sl]), block_diag(k_scr[:, sl])], axis=0) for sl in sls]
    vm = [block_diag(v_ref[:, sl]) for sl in sls]
    s_old = [s_scr[p] for p in pairs]
    gram = [_dot_nt(ar[p], bk[p]) for p in pairs]
    xs = [_dot_nt(ar[p], s_old[p].astype(BF16)) for p in pairs]
    u = [xs[p][:c2, :] + _dot(jnp.where(strict, gram[p][:c2, c2:], 0.0).astype(BF16), vm[p]) for p in pairs]
    lp = [jnp.where(strict, gram[p][:c2, :c2], 0.0).astype(BF16) for p in pairs]
    for d in range(n_dbl):
        if d + 1 < n_dbl:
            t = [_dot(lp[p], jnp.concatenate([u[p].astype(BF16), lp[p]], axis=1)) for p in pairs]
            u = [u[p] + t[p][:, :LANES] for p in pairs]
            lp = [t[p][:, LANES:].astype(BF16) for p in pairs]
        else:
            u = [u[p] + _dot(lp[p], u[p].astype(BF16)) for p in pairs]
    uv = [jnp.concatenate([u[p].astype(BF16), vm[p]], axis=0) for p in pairs]
    m_r = [jnp.concatenate([jnp.where(incl, gram[p][c2:, :c2], 0.0), jnp.where(incl, gram[p][c2:, c2:], 0.0)],
                           axis=1).astype(BF16) for p in pairs]
    y = [xs[p][c2:, :] + _dot(m_r[p], uv[p]) for p in pairs]
    for p in pairs:
        y_ref[:, sls[p]] = y[p][:chunk, :] + y[p][chunk:, :]
    s_new = [(s_old[p] + _dot_tn(uv[p], bk[p])) * w_last[:, sls[p]] for p in pairs]
    for p in pairs:
        s_scr[p] = s_new[p]

    @pl.when(c == pl.num_programs(1) - 1)
    def _():
        for p in range(n_pairs):
            s = s_scr[p]
            sT_ref[0, 2 * p] = s[:hd, :hd]
            sT_ref[0, 2 * p + 1] = s[hd:, hd:]


def _wkv(r, lw, k, v, kk, ka, s0, batch, seq, chunk):
    nc = seq // chunk
    blk = lambda b, c: (b * nc + c, 0)
    st = lambda b, c: (b, 0, 0, 0)
    state = pl.BlockSpec((1, RWKV_HEADS, RWKV_HEAD, RWKV_HEAD), st)
    return pl.pallas_call(
        functools.partial(_wkv_body, chunk=chunk),
        grid=(batch, nc),
        in_specs=[pl.BlockSpec((chunk, D_MODEL), blk)] * 6 + [state],
        out_specs=[pl.BlockSpec((chunk, D_MODEL), blk), state],
        out_shape=[jax.ShapeDtypeStruct((batch * seq, D_MODEL), F32),
                   jax.ShapeDtypeStruct(s0.shape, F32)],
        scratch_shapes=[pltpu.VMEM((D_MODEL // LANES, LANES, LANES), F32)]
                       + [pltpu.VMEM((chunk, D_MODEL), F32)] * 4,
        compiler_params=_cparams(("parallel", "arbitrary")),
        name="wkv",
    )(r, lw, k, v, kk, ka, s0)


def _rwkv_out_body(x_ref, y_ref, r_ref, k_ref, v_ref, g_ref, vec_ref, gm_ref, wo_ref, o_ref):
    gm = gm_ref[...]
    vec = vec_ref[...]
    ln_w, ln_b, r_k = vec[0:1, :], vec[1:2, :], vec[2:3, :]
    y = y_ref[...]
    mean = _group_sum(y, gm) * (1.0 / RWKV_HEAD)
    d = y - mean
    var = _group_sum(d * d, gm) * (1.0 / RWKV_HEAD)
    yn = d * lax.rsqrt(var + GN_EPS) * ln_w + ln_b
    rk = r_ref[...].astype(F32) * k_ref[...].astype(F32) * r_k
    yn = yn + _group_sum(rk, gm) * v_ref[...].astype(F32)
    o_ref[...] = x_ref[...] + _dot((yn * g_ref[...].astype(F32)).astype(BF16), wo_ref[...])


def _rwkv_out(x, y, r, k, v, g, vec, wo, tm):
    n = x.shape[0]
    row = lambda i: (i, 0)
    const = lambda i: (0, 0)
    gm = _group_ones()
    return pl.pallas_call(
        _rwkv_out_body,
        grid=(n // tm,),
        in_specs=[pl.BlockSpec((tm, D_MODEL), row)] * 6
                 + [pl.BlockSpec(vec.shape, const), pl.BlockSpec(gm.shape, const), pl.BlockSpec(wo.shape, const)],
        out_specs=pl.BlockSpec((tm, D_MODEL), row),
        out_shape=jax.ShapeDtypeStruct((n, D_MODEL), F32),
        compiler_params=_cparams(("parallel",)),
        name="rwkv_out",
    )(x, y, r, k, v, g, vec, gm, wo)


def _pick_tile(n, want):
    t = min(n, want)
    while n % t:
        t //= 2
    return t


def _trunk(x, pos, conv_prev, shift_prev, wkv_prev, wts, sample):
    batch, seq, _ = x.shape
    n = batch * seq
    xf = x.reshape(n, D_MODEL)
    tm = _pick_tile(n, 512)

    tabs = _rope_tables(pos)
    if sample is not None:
        tabs = tuple(jnp.tile(t, (batch, 1)) for t in tabs)
    q, k, v, iq, ikw, cb, u = _proj_even(xf, wts['norm_mix_even'], wts['w_in'], tabs, tm)
    if sample is None:
        attn = _prompt_attention(q, iq, ikw, k, v, batch, seq)
        conv = _conv_prompt(cb, u, wts['conv_w'], seq, _pick_tile(seq, 512))
        conv_state = u.reshape(batch, seq, CONV_DIM)[:, seq - (CONV_W - 1):]
    else:
        cache_k, cache_v, cache_ik, page_table = sample
        tpad = SUBLANES - seq

        def stack(a, heads):
            a = a.reshape(batch, seq, heads, -1).transpose(0, 2, 1, 3)
            a = jnp.pad(a, ((0, 0), (0, 0), (0, tpad), (0, 0)))
            return a.reshape(batch, heads * SUBLANES, a.shape[-1])

        iq_st = stack(iq, N_IDX_HEADS)
        iw_st = stack(ikw[:, IDX_DIM:IDX_DIM + N_IDX_HEADS], N_IDX_HEADS)
        s_past = _sample_scores(page_table, iq_st, iw_st, cache_ik)[:, :seq]
        mask = _sample_select(s_past.reshape(n, -1), iq, ikw, seq)
        mask = jnp.pad(mask.reshape(batch, seq, -1), ((0, 0), (0, tpad), (0, 0)))
        q_st = stack(q, N_Q_HEADS).reshape(batch, N_KV_HEADS, Q_PER_KV * SUBLANES, HEAD_DIM)
        padk = lambda a: jnp.pad(jnp.swapaxes(a.reshape(batch, seq, KV_DIM), 1, 2),
                                 ((0, 0), (0, 0), (0, PAGE_SIZE - seq)))
        o = _sample_attention(page_table, q_st, mask, padk(k), padk(v), cache_k, cache_v)
        attn = (o.reshape(batch, N_KV_HEADS, Q_PER_KV, SUBLANES, HEAD_DIM)[:, :, :, :seq].transpose(0, 3, 2, 1, 4)
                .reshape(n, ATTN_DIM).astype(BF16))
        u3 = jnp.concatenate([conv_prev, u.reshape(batch, seq, CONV_DIM)], axis=1)
        conv_state = u3[:, seq:]
        conv_t = _conv_sample(jnp.swapaxes(cb.reshape(batch, seq, CONV_DIM), 0, 1), jnp.swapaxes(u3, 0, 1),
                              wts['conv_w'])
        conv = jnp.swapaxes(conv_t, 0, 1).reshape(n, CONV_DIM)
    x1 = _out_even(xf, attn, conv, wts['w_out_attn'], wts['w_out_conv'], tm)
    x2, xn2 = _mixer(x1, wts['norm_ffn_even'], None, wts['ffn_gate'], wts['ffn_up'], wts['ffn_down'],
                     wts['norm_mix_odd'], tm, D_FF // 2, True)

    xn3 = xn2.reshape(batch, seq, D_MODEL)
    tm1 = _pick_tile(n, 256)
    xp = None
    if seq % tm1:
        xp = jnp.concatenate([shift_prev[:, None, :], xn3[:, :-1]], axis=1).reshape(n, D_MODEL)
    r, lw, kr, vr, kk, ka, g = _rwkv_proj(xn2, xp, shift_prev, seq, wts['rwkv_mu'], wts['rwkv_vec_in'],
                                          wts['rwkv_w_rkv'], wts['rwkv_w1'], wts['rwkv_w2'], wts['rwkv_a1'],
                                          wts['rwkv_a2'], wts['rwkv_g1'], wts['rwkv_g2'], tm1)
    if seq % WKV_CHUNK:
        sp = -(-seq // WKV_CHUNK) * WKV_CHUNK
        padded = [jnp.pad(a.reshape(batch, seq, D_MODEL), ((0, 0), (0, sp - seq), (0, 0)))
                  .reshape(batch * sp, D_MODEL) for a in (r, lw, kr, vr, kk, ka)]
        y, wkv_state = _wkv(*padded, wkv_prev, batch, sp, WKV_CHUNK)
        y = y.reshape(batch, sp, D_MODEL)[:, :seq].reshape(n, D_MODEL)
    else:
        y, wkv_state = _wkv(r, lw, kr, vr, kk, ka, wkv_prev, batch, seq, WKV_CHUNK)
    x3 = _rwkv_out(x2, y, r, kr, vr, g, wts['rwkv_vec_out'], wts['rwkv_w_o'], _pick_tile(n, 256))
    yf = _moe(x3, wts['norm_ffn_odd'], wts['moe_router'], wts['moe_gate'], wts['moe_up'], wts['moe_down'],
              wts['norm_final'], _pick_tile(n, MOE_GROUP), min(MOE_BLOCK_ROWS, _pick_tile(n, MOE_GROUP)), D_FF // 2)

    k4 = k.reshape(1, batch, seq, N_KV_HEADS, HEAD_DIM)
    v4 = v.reshape(1, batch, seq, N_KV_HEADS, HEAD_DIM)
    ik3 = ikw[:, :IDX_DIM].reshape(1, batch, seq, IDX_DIM)
    return (yf.reshape(batch, seq, D_MODEL), k4, v4, ik3, conv_state[None],
            xn3[:, -1][None], wkv_state[None])


def _prepare_weights(norm_mix_even, w_in_even, conv_w, w_out_even, norm_ffn_even, ffn_gate, ffn_up, ffn_down,
                     norm_mix_odd, rwkv_mu, rwkv_w_rkv, rwkv_w0, rwkv_w1, rwkv_w2, rwkv_a0, rwkv_a1, rwkv_a2,
                     rwkv_g1, rwkv_g2, rwkv_k_k, rwkv_k_a, rwkv_r_k, rwkv_ln_w, rwkv_ln_b, rwkv_w_o,
                     norm_ffn_odd, moe_router, moe_gate, moe_up, moe_down, norm_final):
    w_in = w_in_even[0]
    o = np.cumsum((0, ATTN_DIM, KV_DIM, KV_DIM, N_IDX_HEADS * IDX_DIM, IDX_DIM, N_IDX_HEADS,
                   CONV_DIM, CONV_DIM, CONV_DIM))
    pad = jnp.zeros((D_MODEL, LANES - IDX_DIM - N_IDX_HEADS), F32)
    w_in = jnp.concatenate([w_in[:, :o[4]], w_in[:, o[4]:o[6]], pad, w_in[:, o[6]:]], axis=1).astype(BF16)
    row = lambda a: a.reshape(1, -1)
    zeros = jnp.zeros((1, D_MODEL), F32)
    return dict(
        norm_mix_even=row(norm_mix_even[0]), w_in=w_in, conv_w=conv_w[0],
        w_out_attn=(w_out_even[0, :ATTN_DIM].reshape(N_KV_HEADS, Q_PER_KV, HEAD_DIM, D_MODEL)
                    .transpose(1, 0, 2, 3).reshape(ATTN_DIM, D_MODEL).astype(BF16)),
        w_out_conv=w_out_even[0, ATTN_DIM:].astype(BF16),
        norm_ffn_even=row(norm_ffn_even[0]),
        ffn_gate=ffn_gate.astype(BF16), ffn_up=ffn_up.astype(BF16), ffn_down=ffn_down.astype(BF16),
        norm_mix_odd=row(norm_mix_odd[0]), rwkv_mu=jnp.concatenate([rwkv_mu[0], zeros, zeros], axis=0),
        rwkv_vec_in=jnp.concatenate([row(rwkv_w0[0]), row(rwkv_a0[0]), row(rwkv_k_k[0]), row(rwkv_k_a[0]),
                                     zeros, zeros, zeros, zeros], axis=0),
        rwkv_w_rkv=rwkv_w_rkv[0].astype(BF16),
        rwkv_w1=rwkv_w1[0].astype(BF16), rwkv_w2=rwkv_w2[0].astype(BF16),
        rwkv_a1=rwkv_a1[0].astype(BF16), rwkv_a2=rwkv_a2[0].astype(BF16),
        rwkv_g1=rwkv_g1[0].astype(BF16), rwkv_g2=rwkv_g2[0].astype(BF16),
        rwkv_vec_out=jnp.concatenate([row(rwkv_ln_w[0]), row(rwkv_ln_b[0]), row(rwkv_r_k[0]),
                                      zeros, zeros, zeros, zeros, zeros], axis=0),
        rwkv_w_o=rwkv_w_o[0].astype(BF16),
        norm_ffn_odd=row(norm_ffn_odd[0]),
        moe_router=jnp.pad(moe_router[0], ((0, 0), (0, LANES - N_EXPERTS))).astype(BF16),
        moe_gate=moe_gate[0].astype(BF16), moe_up=moe_up[0].astype(BF16), moe_down=moe_down[0].astype(BF16),
        norm_final=row(norm_final),
    )


def kernel(x_prompt, x_sample, cache_k, cache_v, cache_idx_k, state_conv, state_shift, state_wkv, page_table, norm_mix_even, w_in_even, conv_w, w_out_even, norm_ffn_even, ffn_gate, ffn_up, ffn_down, norm_mix_odd, rwkv_mu, rwkv_w_rkv, rwkv_w0, rwkv_w1, rwkv_w2, rwkv_a0, rwkv_a1, rwkv_a2, rwkv_g1, rwkv_g2, rwkv_k_k, rwkv_k_a, rwkv_r_k, rwkv_ln_w, rwkv_ln_b, rwkv_w_o, norm_ffn_odd, moe_router, moe_gate, moe_up, moe_down, norm_final):
    assert w_in_even.shape[0] == 1 and rwkv_mu.shape[0] == 1, "one even and one odd layer"
    wts = _prepare_weights(norm_mix_even, w_in_even, conv_w, w_out_even, norm_ffn_even, ffn_gate, ffn_up, ffn_down,
                           norm_mix_odd, rwkv_mu, rwkv_w_rkv, rwkv_w0, rwkv_w1, rwkv_w2, rwkv_a0, rwkv_a1, rwkv_a2,
                           rwkv_g1, rwkv_g2, rwkv_k_k, rwkv_k_a, rwkv_r_k, rwkv_ln_w, rwkv_ln_b, rwkv_w_o,
                           norm_ffn_odd, moe_router, moe_gate, moe_up, moe_down, norm_final)
    b, t = x_prompt.shape[:2]
    bd, tn = x_sample.shape[:2]
    n_pool = cache_k.shape[1]
    past = page_table.shape[1] * PAGE_SIZE
    pos_prompt = jnp.arange(t, dtype=jnp.int32)
    pos_sample = past + jnp.arange(tn, dtype=jnp.int32)
    zeros = lambda *s: jnp.zeros(s, F32)
    out_p = _trunk(x_prompt, pos_prompt, zeros(b, CONV_W - 1, CONV_DIM), zeros(b, D_MODEL),
                   zeros(b, RWKV_HEADS, RWKV_HEAD, RWKV_HEAD), wts, None)
    pages_t = lambda c: jnp.transpose(c[0], (0, 2, 3, 1)).reshape(n_pool, KV_DIM, PAGE_SIZE)
    sample = (pages_t(cache_k), pages_t(cache_v), jnp.swapaxes(cache_idx_k[0], 1, 2), page_table)
    out_s = _trunk(x_sample, pos_sample, state_conv[0], state_shift[0], state_wkv[0], wts, sample)
    y_p, k_p, v_p, ik_p, conv_p, shift_p, wkv_p = out_p
    y_s, k_s, v_s, ik_s, conv_s, shift_s, wkv_s = out_s
    return (y_p, y_s, k_p, v_p, ik_p, k_s, v_s, ik_s, conv_p, conv_s, shift_p, shift_s, wkv_p, wkv_s)
```

```python
import functools

import numpy as np
import jax
import jax.numpy as jnp
from jax import lax
from jax.experimental import pallas as pl
from jax.experimental.pallas import tpu as pltpu

F32 = jnp.float32
BF16 = jnp.bfloat16

D_MODEL = 1024
PAGE_SIZE = 128
HEAD_DIM = 64
N_Q_HEADS = 8
N_KV_HEADS = 2
Q_PER_KV = N_Q_HEADS // N_KV_HEADS
ROT_DIM = HEAD_DIM // 4
ROPE_THETA = 500000.0
N_IDX_HEADS = 4
IDX_DIM = 64
TOPK_MAX = 256
Q_BLOCK = 128
ATTN_DIM = N_Q_HEADS * HEAD_DIM
KV_DIM = N_KV_HEADS * HEAD_DIM
CONV_DIM = D_MODEL // 2
CONV_W = 3
RWKV_HEAD = 64
RWKV_HEADS = D_MODEL // RWKV_HEAD
GN_EPS = 64e-5
D_FF = 2816
N_EXPERTS = 8
RMS_EPS = 1e-6

LANES = 128
SUBLANES = 8
VMEM_LIMIT = 56 * 1024 * 1024
VMEM_LIMIT_MOE = 61 * 1024 * 1024
INT_MIN = -2 ** 31
KEY_NEG_INF = INT_MIN + 0x7FFFFF
PROJ_COLS = ATTN_DIM + KV_DIM + KV_DIM + N_IDX_HEADS * IDX_DIM + LANES + 3 * CONV_DIM
WKV_CHUNK = 64
MOE_GROUP = 1024
MOE_BLOCK_ROWS = 288


def _cparams(sem):
    return pltpu.CompilerParams(dimension_semantics=sem, vmem_limit_bytes=VMEM_LIMIT)


def _rms(x, g):
    return x * lax.rsqrt(jnp.mean(x * x, axis=-1, keepdims=True) + RMS_EPS) * g


def _dot(a, b):
    return jnp.dot(a, b, preferred_element_type=F32)


def _dot_nt(a, b):
    return lax.dot_general(a, b, (((1,), (1,)), ((), ())), preferred_element_type=F32)


def _dot_tn(a, b):
    return lax.dot_general(a, b, (((0,), (0,)), ((), ())), preferred_element_type=F32)


def _group_sum(x, gmat):
    outs = []
    for c in range(x.shape[1] // LANES):
        xc = x[:, c * LANES:(c + 1) * LANES]
        hi = xc.astype(BF16)
        lo = (xc - hi.astype(F32)).astype(BF16)
        outs.append(_dot(hi, gmat) + _dot(lo, gmat))
    return jnp.concatenate(outs, axis=1)


def _rope_chunk(xc, rc, rp, rm):
    return xc * rc + pltpu.roll(xc, 8, 1) * rp + pltpu.roll(xc, LANES - 8, 1) * rm


def _proj_even_body(x_ref, g_ref, w_ref, rc_ref, rp_ref, rm_ref,
                    q_ref, k_ref, v_ref, iq_ref, ikw_ref, cb_ref, u_ref):
    xn = _rms(x_ref[...], g_ref[...]).astype(BF16)
    h = _dot(xn, w_ref[...])
    rc, rp, rm = rc_ref[...], rp_ref[...], rm_ref[...]
    col = 0
    for c in range(ATTN_DIM // LANES):
        q_ref[:, c * LANES:(c + 1) * LANES] = _rope_chunk(h[:, col:col + LANES], rc, rp, rm).astype(BF16)
        col += LANES
    k_ref[...] = _rope_chunk(h[:, col:col + LANES], rc, rp, rm)
    col += LANES
    v_ref[...] = h[:, col:col + LANES]
    col += LANES
    for c in range(N_IDX_HEADS * IDX_DIM // LANES):
        iq_ref[:, c * LANES:(c + 1) * LANES] = _rope_chunk(h[:, col:col + LANES], rc, rp, rm).astype(BF16)
        col += LANES
    ikw = h[:, col:col + LANES]
    lane = lax.broadcasted_iota(jnp.int32, ikw.shape, 1)
    ikw_ref[...] = jnp.where(lane < IDX_DIM, _rope_chunk(ikw, rc, rp, rm), ikw)
    col += LANES
    cb_ref[...] = h[:, col:col + CONV_DIM]
    col += CONV_DIM
    u_ref[...] = h[:, col:col + CONV_DIM] * h[:, col + CONV_DIM:col + 2 * CONV_DIM]


def _proj_even(x, g, w, tabs, tm):
    n = x.shape[0]
    nt = tabs[0].shape[0] // tm
    row = lambda i: (i, 0)
    const = lambda i: (0, 0)
    tab = lambda i: (i % nt, 0)
    widths = (ATTN_DIM, KV_DIM, KV_DIM, N_IDX_HEADS * IDX_DIM, LANES, CONV_DIM, CONV_DIM)
    dtypes = (BF16, F32, F32, BF16, F32, F32, F32)
    return pl.pallas_call(
        _proj_even_body,
        grid=(n // tm,),
        in_specs=[pl.BlockSpec((tm, D_MODEL), row), pl.BlockSpec((1, D_MODEL), const),
                  pl.BlockSpec((D_MODEL, PROJ_COLS), const)] + [pl.BlockSpec((tm, LANES), tab)] * 3,
        out_specs=[pl.BlockSpec((tm, wd), row) for wd in widths],
        out_shape=[jax.ShapeDtypeStruct((n, wd), dt) for wd, dt in zip(widths, dtypes)],
        compiler_params=_cparams(("parallel",)),
        name="proj_even",
    )(x, g, w, *tabs)


def _rope_tables(pos):
    half = ROT_DIM // 2
    inv_freq = ROPE_THETA ** (-jnp.arange(half, dtype=F32) / half)
    ang = pos.astype(F32)[:, None] * inv_freq[None, :]
    cos, sin = jnp.cos(ang), jnp.sin(ang)
    t = pos.shape[0]
    pad = jnp.zeros((t, HEAD_DIM - ROT_DIM), F32)
    zero = jnp.zeros((t, half), F32)
    rc = jnp.concatenate([cos, cos, pad + 1.0], axis=1)
    rp = jnp.concatenate([zero, sin, pad], axis=1)
    rm = jnp.concatenate([-sin, zero, pad], axis=1)
    return tuple(jnp.tile(a, (1, LANES // HEAD_DIM)) for a in (rc, rp, rm))


def _conv_prompt_body(cb_ref, u_ref, up_ref, w_ref, y_ref, *, tiles_per_seq):
    u = u_ref[...]
    first = pl.program_id(0) % tiles_per_seq == 0
    prev = jnp.where(first, 0.0, up_ref[...])
    p1, p2 = prev[SUBLANES - 1:SUBLANES, :], prev[SUBLANES - 2:SUBLANES - 1, :]
    r = lax.broadcasted_iota(jnp.int32, u.shape, 0)
    u1 = jnp.where(r == 0, p1, pltpu.roll(u, 1, 0))
    u2 = jnp.where(r == 0, p2, jnp.where(r == 1, p1, pltpu.roll(u, 2, 0)))
    w = w_ref[...]
    y_ref[...] = cb_ref[...] * (w[0:1, :] * u2 + w[1:2, :] * u1 + w[2:3, :] * u)


def _conv_prompt(cb, u, w, seq, tc):
    n = u.shape[0]
    row = lambda i: (i, 0)
    prev = lambda i: (jnp.maximum(i * (tc // SUBLANES) - 1, 0), 0)
    return pl.pallas_call(
        functools.partial(_conv_prompt_body, tiles_per_seq=seq // tc),
        grid=(n // tc,),
        in_specs=[pl.BlockSpec((tc, CONV_DIM), row), pl.BlockSpec((tc, CONV_DIM), row),
                  pl.BlockSpec((SUBLANES, CONV_DIM), prev), pl.BlockSpec((CONV_W, CONV_DIM), lambda i: (0, 0))],
        out_specs=pl.BlockSpec((tc, CONV_DIM), row),
        out_shape=jax.ShapeDtypeStruct((n, CONV_DIM), F32),
        compiler_params=_cparams(("parallel",)),
        name="conv_prompt",
    )(cb, u, u, w)


def _conv_sample_body(cb_ref, ue_ref, w_ref, y_ref):
    w = w_ref[...]
    for t in range(y_ref.shape[0]):
        acc = w[0:1, :] * ue_ref[t] + w[1:2, :] * ue_ref[t + 1] + w[2:3, :] * ue_ref[t + 2]
        y_ref[t] = cb_ref[t] * acc


def _conv_sample(cb_t, ue_t, w):
    return pl.pallas_call(
        _conv_sample_body,
        out_shape=jax.ShapeDtypeStruct(cb_t.shape, F32),
        name="conv_sample",
    )(cb_t, ue_t, w)


def _topk_select(scores, n_sel, tri):
    rows, width = scores.shape
    sc = scores

    def key_to_float(key):
        return lax.bitcast_convert_type(key ^ ((key >> 31) & 0x7FFFFFFF), F32)

    def count_ge(key):
        cnt = jnp.sum(jnp.where(sc >= key_to_float(key), 1.0, 0.0), axis=1, keepdims=True)
        return jnp.where(key <= KEY_NEG_INF, float(width), cnt)

    thr = jnp.where(count_ge(jnp.zeros((rows, 1), jnp.int32)) >= n_sel, 0, INT_MIN).astype(jnp.int32)

    def body(i, thr):
        cand = thr + lax.shift_left(jnp.int32(1), 30 - i)
        return jnp.where(count_ge(cand) >= n_sel, cand, thr)

    thr = lax.fori_loop(0, 31, body, thr)
    thr_f = key_to_float(thr)
    gt = sc > thr_f
    eq = sc == thr_f
    need = n_sel - jnp.sum(jnp.where(gt, 1.0, 0.0), axis=1, keepdims=True)
    off = jnp.zeros((rows, 1), F32)
    parts = []
    for c in range(width // LANES):
        eqc = jnp.where(eq[:, c * LANES:(c + 1) * LANES], 1.0, 0.0)
        incl = _dot(eqc.astype(BF16), tri)
        parts.append((incl - eqc + off) < need)
        off = off + incl[:, LANES - 1:LANES]
    tie = jnp.concatenate(parts, axis=1)
    return (gt | (eq & tie)) & (sc > -jnp.inf)


def _tri_incl():
    i = np.arange(LANES)
    return jnp.asarray((i[:, None] <= i[None, :]).astype(np.float32), BF16)


def _stack_heads(x, n):
    return jnp.concatenate([x[:, h * HEAD_DIM:(h + 1) * HEAD_DIM] for h in range(n)], axis=0)


def _index_scores(iq_st, iw_st, ik_b, keys_on_lanes=False):
    rows = iq_st.shape[0] // N_IDX_HEADS
    s = _dot(iq_st, ik_b) if keys_on_lanes else _dot_nt(iq_st, ik_b)
    term = jnp.maximum(s, 0.0) * (IDX_DIM ** -0.5) * (iw_st * (N_IDX_HEADS ** -0.5))
    acc = term[0:rows, :]
    for h in range(1, N_IDX_HEADS):
        acc = acc + term[h * rows:(h + 1) * rows, :]
    return acc


def _group_logits(qs, kg, sel, keys_on_lanes=False):
    s = (_dot(qs, kg) if keys_on_lanes else _dot_nt(qs, kg)) * (HEAD_DIM ** -0.5)
    return jnp.where(jnp.concatenate([sel] * Q_PER_KV, axis=0), s, -jnp.inf)


def _reduce_rows(x, op):
    parts = [x[c * LANES:(c + 1) * LANES, :] for c in range(x.shape[0] // LANES)]
    while len(parts) > 1:
        nxt = [op(a, b) for a, b in zip(parts[0::2], parts[1::2])]
        if len(parts) % 2:
            nxt.append(parts[-1])
        parts = nxt
    x = parts[0]
    rows = x.shape[0]
    while rows > SUBLANES:
        rows //= 2
        x = op(x[:rows, :], x[rows:, :])
    red = jnp.max if op is jnp.maximum else jnp.sum
    return red(x, axis=0, keepdims=True)


def _topk_select_keys_on_rows(sc, n_sel, tril):
    width, cols = sc.shape

    def key_to_float(key):
        return lax.bitcast_convert_type(key ^ ((key >> 31) & 0x7FFFFFFF), F32)

    def count_ge(key):
        cnt = _reduce_rows(jnp.where(sc >= key_to_float(key), 1.0, 0.0), jnp.add)
        return jnp.where(key <= KEY_NEG_INF, float(width), cnt)

    thr = jnp.where(count_ge(jnp.zeros((1, cols), jnp.int32)) >= n_sel, 0, INT_MIN).astype(jnp.int32)

    def body(i, thr):
        cand = thr + lax.shift_left(jnp.int32(1), 30 - i)
        return jnp.where(count_ge(cand) >= n_sel, cand, thr)

    thr = lax.fori_loop(0, 31, body, thr)
    thr_f = key_to_float(thr)
    gt = sc > thr_f
    eq = sc == thr_f
    need = n_sel - _reduce_rows(jnp.where(gt, 1.0, 0.0), jnp.add)
    off = jnp.zeros((1, cols), F32)
    parts = []
    for c in range(width // LANES):
        eqc = jnp.where(eq[c * LANES:(c + 1) * LANES, :], 1.0, 0.0)
        incl = _dot(tril, eqc.astype(BF16))
        parts.append((incl - eqc + off) < need)
        off = off + incl[LANES - 1:LANES, :]
    tie = jnp.concatenate(parts, axis=0)
    return (gt | (eq & tie)) & (sc > -jnp.inf)


def _prompt_attn_body(q_ref, iq_ref, iwq_ref, k_ref, v_ref, ikw_ref, tril_ref, o_ref,
                      kb_scr, vt_scr, ikb_scr, *, n_sel, widths):
    i = pl.program_id(1)

    @pl.when(i == 0)
    def _():
        for g in range(N_KV_HEADS):
            kb_scr[g] = k_ref[:, g * HEAD_DIM:(g + 1) * HEAD_DIM].astype(BF16)
        vt_scr[...] = jnp.transpose(v_ref[...]).astype(BF16)
        ikb_scr[...] = ikw_ref[:, :IDX_DIM].astype(BF16)

    def run(width):
        iw_t = jnp.transpose(iwq_ref[...])
        w_row = jnp.concatenate([iw_t[IDX_DIM + h:IDX_DIM + h + 1, :] for h in range(N_IDX_HEADS)], axis=1)
        s = _dot_nt(ikb_scr[0:width, :], _stack_heads(iq_ref[...], N_IDX_HEADS))
        term = jnp.maximum(s, 0.0) * (w_row * (IDX_DIM ** -0.5 * N_IDX_HEADS ** -0.5))
        scores = term[:, 0:Q_BLOCK]
        for h in range(1, N_IDX_HEADS):
            scores = scores + term[:, h * Q_BLOCK:(h + 1) * Q_BLOCK]
        key_pos = lax.broadcasted_iota(jnp.int32, scores.shape, 0)
        tq = i * Q_BLOCK + lax.broadcasted_iota(jnp.int32, scores.shape, 1)
        scores = jnp.where(key_pos <= tq, scores, -jnp.inf)
        sel = _topk_select_keys_on_rows(scores, n_sel, tril_ref[...])
        sel4 = jnp.concatenate([sel] * Q_PER_KV, axis=1)
        q = q_ref[...]
        outs = []
        for g in range(N_KV_HEADS):
            qs = _stack_heads(q[:, g * Q_PER_KV * HEAD_DIM:(g + 1) * Q_PER_KV * HEAD_DIM], Q_PER_KV) * (HEAD_DIM ** -0.5)
            st = jnp.where(sel4, _dot_nt(kb_scr[g, 0:width, :], qs), -jnp.inf)
            p = jnp.exp(st - _reduce_rows(st, jnp.maximum))
            ot = _dot(vt_scr[g * HEAD_DIM:(g + 1) * HEAD_DIM, 0:width], p.astype(BF16))
            outs.append(ot / _reduce_rows(p, jnp.add))
        o = jnp.transpose(jnp.concatenate(outs, axis=0))
        for r in range(Q_PER_KV):
            o_ref[:, r * LANES:(r + 1) * LANES] = o[r * Q_BLOCK:(r + 1) * Q_BLOCK, :].astype(o_ref.dtype)

    lo = 0
    for width in widths:
        hi = width // Q_BLOCK
        pl.when((i >= lo) & (i < hi))(functools.partial(run, width))
        lo = hi


def _prompt_attention(q, iq, ikw, k, v, batch, seq):
    n_sel = min(TOPK_MAX, seq // 4)
    nqb = seq // Q_BLOCK
    n_widths = min(8, nqb)
    widths = tuple(seq * (j + 1) // n_widths for j in range(n_widths))
    blk = lambda b, i: (b * nqb + i, 0)
    full = lambda b, i: (b, 0)
    return pl.pallas_call(
        functools.partial(_prompt_attn_body, n_sel=n_sel, widths=widths),
        grid=(batch, nqb),
        in_specs=[pl.BlockSpec((Q_BLOCK, ATTN_DIM), blk), pl.BlockSpec((Q_BLOCK, N_IDX_HEADS * IDX_DIM), blk),
                  pl.BlockSpec((Q_BLOCK, LANES), blk),
                  pl.BlockSpec((seq, KV_DIM), full), pl.BlockSpec((seq, KV_DIM), full),
                  pl.BlockSpec((seq, LANES), full), pl.BlockSpec((LANES, LANES), lambda b, i: (0, 0))],
        out_specs=pl.BlockSpec((Q_BLOCK, ATTN_DIM), blk),
        out_shape=jax.ShapeDtypeStruct((batch * seq, ATTN_DIM), BF16),
        scratch_shapes=[pltpu.VMEM((N_KV_HEADS, seq, HEAD_DIM), BF16), pltpu.VMEM((KV_DIM, seq), BF16),
                        pltpu.VMEM((seq, IDX_DIM), BF16)],
        compiler_params=_cparams(("parallel", "arbitrary")),
        name="prompt_attention",
    )(q, iq, ikw, k, v, ikw, _tri_incl().T)


PAGES_PER_STEP = 32


def _sample_scores_body(pt_ref, iq_ref, iw_ref, *rest):
    page_refs, s_ref = rest[:PAGES_PER_STEP], rest[PAGES_PER_STEP]
    for j, pr in enumerate(page_refs):
        s_ref[0, :, j * PAGE_SIZE:(j + 1) * PAGE_SIZE] = _index_scores(iq_ref[0], iw_ref[0], pr[0].astype(BF16), True)


def _sample_scores(page_table, iq_st, iw_st, cache_ik):
    bd, n_pages = page_table.shape
    rows = iq_st.shape[1] // N_IDX_HEADS
    assert n_pages % PAGES_PER_STEP == 0
    steps = n_pages // PAGES_PER_STEP
    per_b = lambda b, p, pt: (b, 0, 0)
    page = lambda j: (lambda b, p, pt: (pt[b, p * PAGES_PER_STEP + j], 0, 0))
    return pl.pallas_call(
        _sample_scores_body,
        grid_spec=pltpu.PrefetchScalarGridSpec(
            num_scalar_prefetch=1,
            grid=(bd, steps),
            in_specs=[pl.BlockSpec((1,) + iq_st.shape[1:], per_b), pl.BlockSpec((1,) + iw_st.shape[1:], per_b)]
                     + [pl.BlockSpec((1, IDX_DIM, PAGE_SIZE), page(j)) for j in range(PAGES_PER_STEP)],
            out_specs=pl.BlockSpec((1, rows, PAGES_PER_STEP * PAGE_SIZE), lambda b, p, pt: (b, 0, p)),
        ),
        out_shape=jax.ShapeDtypeStruct((bd, rows, n_pages * PAGE_SIZE), F32),
        compiler_params=_cparams(("parallel", "arbitrary")),
        name="sample_scores",
    )(page_table, iq_st, iw_st, *([cache_ik] * PAGES_PER_STEP))


def _sample_select_body(sp_ref, iq_ref, iwq_ref, ikn_ref, tri_ref, m_ref, *, n_sel, tn):
    rows = sp_ref.shape[0]
    iw = iwq_ref[...]
    iw_st = jnp.concatenate([iw[:, IDX_DIM + h:IDX_DIM + h + 1] for h in range(N_IDX_HEADS)], axis=0)
    s_new = _index_scores(_stack_heads(iq_ref[...], N_IDX_HEADS), iw_st, ikn_ref[...].astype(BF16))
    r = lax.broadcasted_iota(jnp.int32, s_new.shape, 0)
    c = lax.broadcasted_iota(jnp.int32, s_new.shape, 1)
    same = (r // tn == c // tn) & (c <= r)
    fold = jnp.where((lax.broadcasted_iota(jnp.int32, (rows, LANES), 0) % tn)
                     == lax.broadcasted_iota(jnp.int32, (rows, LANES), 1), 1.0, 0.0)
    picked = jnp.where(same, s_new, 0.0)
    hi = picked.astype(BF16)
    mid = (picked - hi.astype(F32)).astype(BF16)
    lo = (picked - hi.astype(F32) - mid.astype(F32)).astype(BF16)
    fb = fold.astype(BF16)
    new_chunk = _dot(hi, fb) + _dot(mid, fb) + _dot(lo, fb)
    lane = lax.broadcasted_iota(jnp.int32, (rows, LANES), 1)
    tpos = lax.broadcasted_iota(jnp.int32, (rows, LANES), 0) % tn
    new_chunk = jnp.where(lane <= tpos, new_chunk, -jnp.inf)
    scores = jnp.concatenate([sp_ref[...], new_chunk], axis=1)
    sel = _topk_select(scores, n_sel, tri_ref[...])
    m_ref[...] = jnp.where(sel, 1.0, 0.0)


def _sample_select(s_past, iq, ikw, tn):
    rows, past = s_past.shape
    n_sel = min(TOPK_MAX, (past + tn) // 4)
    return pl.pallas_call(
        functools.partial(_sample_select_body, n_sel=n_sel, tn=tn),
        out_shape=jax.ShapeDtypeStruct((rows, past + LANES), F32),
        compiler_params=pltpu.CompilerParams(vmem_limit_bytes=VMEM_LIMIT),
        name="sample_select",
    )(s_past, iq, ikw, ikw[:, :IDX_DIM], _tri_incl())


def _sample_attn_body(pt_ref, q_ref, m_ref, kn_ref, vn_ref, mn_ref, *rest):
    k_refs, v_refs = rest[:PAGES_PER_STEP], rest[PAGES_PER_STEP:2 * PAGES_PER_STEP]
    o_ref, m_scr, l_scr, acc_scr = rest[2 * PAGES_PER_STEP:]
    p = pl.program_id(1)

    @pl.when(p == 0)
    def _():
        m_scr[...] = jnp.full(m_scr.shape, -jnp.inf, F32)
        l_scr[...] = jnp.zeros(l_scr.shape, F32)
        acc_scr[...] = jnp.zeros(acc_scr.shape, F32)

    def update(kb, vb, sel):
        for g in range(N_KV_HEADS):
            s = _group_logits(q_ref[0, g], kb[g * HEAD_DIM:(g + 1) * HEAD_DIM, :], sel, True)
            m_old = m_scr[g]
            m_new = jnp.maximum(m_old, jnp.max(s, axis=1, keepdims=True))
            m_safe = jnp.where(m_new == -jnp.inf, 0.0, m_new)
            alpha = jnp.exp(m_old - m_safe)
            pe = jnp.exp(s - m_safe)
            l_scr[g] = alpha * l_scr[g] + jnp.sum(pe, axis=1, keepdims=True)
            acc_scr[g] = alpha * acc_scr[g] + _dot_nt(pe.astype(BF16), vb[g * HEAD_DIM:(g + 1) * HEAD_DIM, :])
            m_scr[g] = m_new

    kb = jnp.concatenate([r[0].astype(BF16) for r in k_refs], axis=1)
    vb = jnp.concatenate([r[0].astype(BF16) for r in v_refs], axis=1)
    update(kb, vb, m_ref[0] > 0.5)

    @pl.when(p == pl.num_programs(1) - 1)
    def _():
        update(kn_ref[0].astype(BF16), vn_ref[0].astype(BF16), mn_ref[0] > 0.5)
        for g in range(N_KV_HEADS):
            l = l_scr[g]
            o_ref[0, g] = acc_scr[g] / jnp.where(l == 0.0, 1.0, l)


def _sample_attention(page_table, q_st, mask, k_new, v_new, cache_k, cache_v):
    bd, n_pages = page_table.shape
    assert n_pages % PAGES_PER_STEP == 0
    steps = n_pages // PAGES_PER_STEP
    per_b = lambda b, p, pt: (b, 0, 0)
    per_b4 = lambda b, p, pt: (b, 0, 0, 0)
    page = lambda j: (lambda b, p, pt: (pt[b, p * PAGES_PER_STEP + j], 0, 0))
    width = PAGES_PER_STEP * PAGE_SIZE
    rows = q_st.shape[2]
    mrows = mask.shape[1]
    return pl.pallas_call(
        _sample_attn_body,
        grid_spec=pltpu.PrefetchScalarGridSpec(
            num_scalar_prefetch=1,
            grid=(bd, steps),
            in_specs=[pl.BlockSpec((1,) + q_st.shape[1:], per_b4),
                      pl.BlockSpec((1, mrows, width), lambda b, p, pt: (b, 0, p)),
                      pl.BlockSpec((1, KV_DIM, PAGE_SIZE), per_b), pl.BlockSpec((1, KV_DIM, PAGE_SIZE), per_b),
                      pl.BlockSpec((1, mrows, LANES), lambda b, p, pt: (b, 0, n_pages))]
                     + [pl.BlockSpec((1, KV_DIM, PAGE_SIZE), page(j)) for j in range(PAGES_PER_STEP)] * 2,
            out_specs=pl.BlockSpec((1, N_KV_HEADS, rows, HEAD_DIM), per_b4),
            scratch_shapes=[pltpu.VMEM((N_KV_HEADS, rows, 1), F32), pltpu.VMEM((N_KV_HEADS, rows, 1), F32),
                            pltpu.VMEM((N_KV_HEADS, rows, HEAD_DIM), F32)],
        ),
        out_shape=jax.ShapeDtypeStruct((bd, N_KV_HEADS, rows, HEAD_DIM), F32),
        compiler_params=_cparams(("parallel", "arbitrary")),
        name="sample_attention",
    )(page_table, q_st, mask, k_new, v_new, mask, *([cache_k] * PAGES_PER_STEP), *([cache_v] * PAGES_PER_STEP))


def _out_even_body(x_ref, a_ref, c_ref, wa_ref, wc_ref, o_ref):
    o_ref[...] = x_ref[...] + _dot(a_ref[...], wa_ref[...]) + _dot(c_ref[...].astype(BF16), wc_ref[...])


def _out_even(x, attn, conv, wa, wc, tm):
    n = x.shape[0]
    row = lambda i: (i, 0)
    const = lambda i: (0, 0)
    return pl.pallas_call(
        _out_even_body,
        grid=(n // tm,),
        in_specs=[pl.BlockSpec((tm, D_MODEL), row), pl.BlockSpec((tm, ATTN_DIM), row),
                  pl.BlockSpec((tm, CONV_DIM), row), pl.BlockSpec((ATTN_DIM, D_MODEL), const),
                  pl.BlockSpec((CONV_DIM, D_MODEL), const)],
        out_specs=pl.BlockSpec((tm, D_MODEL), row),
        out_shape=jax.ShapeDtypeStruct((n, D_MODEL), F32),
        compiler_params=_cparams(("parallel",)),
        name="out_even",
    )(x, attn, conv, wa, wc)


def _ffn_body(x_ref, g_ref, wg_ref, wu_ref, wd_ref, g2_ref, sum_ref, nrm_ref, xn_scr, acc_scr):
    f = pl.program_id(1)

    @pl.when(f == 0)
    def _():
        xn_scr[...] = _rms(x_ref[...], g_ref[...]).astype(BF16)
        acc_scr[...] = jnp.zeros(acc_scr.shape, F32)

    xn = xn_scr[...]
    gate = _dot(xn, wg_ref[...])
    h = gate * jax.nn.sigmoid(gate) * _dot(xn, wu_ref[...])
    acc_scr[...] += _dot(h.astype(BF16), wd_ref[...])

    @pl.when(f == pl.num_programs(1) - 1)
    def _():
        y = x_ref[...] + acc_scr[...]
        sum_ref[...] = y
        nrm_ref[...] = _rms(y, g2_ref[...])


def _ffn(x, g, wg, wu, wd, g2, tm, tf):
    n = x.shape[0]
    row = lambda i, f: (i, 0)
    const = lambda i, f: (0, 0)
    return pl.pallas_call(
        _ffn_body,
        grid=(n // tm, D_FF // tf),
        in_specs=[pl.BlockSpec((tm, D_MODEL), row), pl.BlockSpec((1, D_MODEL), const),
                  pl.BlockSpec((D_MODEL, tf), lambda i, f: (0, f)), pl.BlockSpec((D_MODEL, tf), lambda i, f: (0, f)),
                  pl.BlockSpec((tf, D_MODEL), lambda i, f: (f, 0)), pl.BlockSpec((1, D_MODEL), const)],
        out_specs=[pl.BlockSpec((tm, D_MODEL), row)] * 2,
        out_shape=[jax.ShapeDtypeStruct((n, D_MODEL), F32)] * 2,
        scratch_shapes=[pltpu.VMEM((tm, D_MODEL), BF16), pltpu.VMEM((tm, D_MODEL), F32)],
        compiler_params=_cparams(("parallel", "arbitrary")),
        name="ffn",
    )(x, g, wg, wu, wd, g2)


def _moe_route_body(x_ref, g_ref, wr_ref, tri_ref, etri_ref,
                    xn_ref, slot_ref, slot_t_ref, comb_ref, meta_ref, *, n_exp, rows_blk):
    xn = _rms(x_ref[...], g_ref[...]).astype(BF16)
    xn_ref[...] = xn
    logits = _dot(xn, wr_ref[...])
    lane = lax.broadcasted_iota(jnp.int32, logits.shape, 1).astype(F32)
    lg = jnp.where(lane < n_exp, logits, -jnp.inf)
    m1 = jnp.max(lg, axis=1, keepdims=True)
    i1 = jnp.min(jnp.where(lg == m1, lane, float(LANES)), axis=1, keepdims=True)
    lg2 = jnp.where(lane == i1, -jnp.inf, lg)
    m2 = jnp.max(lg2, axis=1, keepdims=True)
    i2 = jnp.min(jnp.where(lg2 == m2, lane, float(LANES)), axis=1, keepdims=True)
    e2 = jnp.exp(m2 - m1)
    comb_ref[...] = jnp.where(lane == i1, 1.0 / (1.0 + e2), 0.0) + jnp.where(lane == i2, e2 / (1.0 + e2), 0.0)
    member = jnp.where((lane == i1) | (lane == i2), 1.0, 0.0)
    n_tok = member.shape[0]
    pos = _dot(tri_ref[...], member.astype(BF16))
    cnt = pos[n_tok - 1:n_tok, :] + member[n_tok - 1:n_tok, :]
    padded = jnp.floor((cnt + (rows_blk - 0.5)) * (1.0 / rows_blk)) * rows_blk
    start = _dot(jnp.broadcast_to(padded, (SUBLANES, LANES)).astype(BF16), etri_ref[...])[0:1, :]
    slot = jnp.where(member > 0.0, start + pos, -1.0)
    slot_ref[...] = slot
    slot_t_ref[0] = jnp.transpose(slot)[0:SUBLANES, :]
    ends = start + padded
    first_row = lax.broadcasted_iota(jnp.int32, (1, LANES), 1).astype(F32) * rows_blk
    blk_exp = jnp.zeros((1, LANES), F32)
    for e in range(n_exp - 1):
        blk_exp = blk_exp + jnp.where(ends[:, e:e + 1] <= first_row, 1.0, 0.0)
    n_blk = jnp.floor((ends[:, n_exp - 1:n_exp] + 0.5) * (1.0 / rows_blk))
    row = lax.broadcasted_iota(jnp.int32, (SUBLANES, LANES), 0)
    meta = jnp.where(row == 0, blk_exp, jnp.where(row == 1, n_blk, 0.0))
    meta_ref[0] = meta.astype(jnp.int32)


def _moe_experts_body(be_ref, nb_ref, xn_ref, slot_ref, slot_t_ref, comb_ref, x_ref, wg_ref, wu_ref, wd_ref,
                      g2_ref, o_ref, xs_scr, ys_scr, *, rows_blk, nf):
    g, f, j = pl.program_id(0), pl.program_id(1), pl.program_id(2)
    nj = pl.num_programs(2)
    e = be_ref[g, j]
    used = j < nb_ref[g]
    n_tok = xn_ref.shape[0]
    rows = pl.ds(pl.multiple_of(j * rows_blk, rows_blk), rows_blk)
    first = j * rows_blk

    @pl.when((f == 0) & (j == 0))
    def _():
        o_ref[...] = x_ref[...]

    @pl.when(used & (f == 0))
    def _():
        srow = slot_t_ref[0, pl.ds(e, 1), :]
        want = (first + lax.broadcasted_iota(jnp.int32, (rows_blk, n_tok), 0)).astype(F32)
        onehot = jnp.where(srow == want, 1.0, 0.0).astype(BF16)
        xs_scr[rows, :] = _dot(onehot, xn_ref[...]).astype(BF16)

    @pl.when(used)
    def _():
        xs = xs_scr[rows, :]
        gate = _dot(xs, wg_ref[0])
        h = gate * jax.nn.sigmoid(gate) * _dot(xs, wu_ref[0])
        part = _dot(h.astype(BF16), wd_ref[0])

        if nf > 1:
            @pl.when(f == 0)
            def _():
                ys_scr[rows, :] = part.astype(BF16)

        if nf > 2:
            @pl.when((f > 0) & (f < nf - 1))
            def _():
                ys_scr[rows, :] = (ys_scr[rows, :].astype(F32) + part).astype(BF16)

        def scatter(total):
            lane = lax.broadcasted_iota(jnp.int32, (n_tok, LANES), 1)
            scol = jnp.sum(jnp.where(lane == e, slot_ref[...], 0.0), axis=1, keepdims=True)
            ccol = jnp.sum(jnp.where(lane == e, comb_ref[...], 0.0), axis=1, keepdims=True)
            want = (first + lax.broadcasted_iota(jnp.int32, (n_tok, rows_blk), 1)).astype(F32)
            onehot = jnp.where(scol == want, 1.0, 0.0).astype(BF16)
            tb = total.astype(BF16)
            for c in range(D_MODEL // (2 * LANES)):
                cols = slice(c * 2 * LANES, (c + 1) * 2 * LANES)
                o_ref[:, cols] += ccol * _dot(onehot, tb[:, cols])

        @pl.when(f == nf - 1)
        def _():
            scatter(part + ys_scr[rows, :].astype(F32) if nf > 1 else part)

    @pl.when((f == nf - 1) & (j == nj - 1))
    def _():
        o_ref[...] = _rms(o_ref[...], g2_ref[...])


def _moe(x, g, wr, wg, wu, wd, g2, tg, rows_blk, tf):
    n = x.shape[0]
    n_exp = wg.shape[0]
    n_groups = n // tg
    n_blk = 2 * tg // rows_blk + n_exp
    i = np.arange(tg)
    tri = jnp.asarray((i[None, :] < i[:, None]).astype(np.float32), BF16)
    i = np.arange(LANES)
    etri = jnp.asarray((i[:, None] < i[None, :]).astype(np.float32), BF16)
    row = lambda i: (i, 0)
    const = lambda i: (0, 0)
    xn, slot, slot_t, comb, meta = pl.pallas_call(
        functools.partial(_moe_route_body, n_exp=n_exp, rows_blk=rows_blk),
        grid=(n_groups,),
        in_specs=[pl.BlockSpec((tg, D_MODEL), row), pl.BlockSpec((1, D_MODEL), const),
                  pl.BlockSpec((D_MODEL, LANES), const), pl.BlockSpec((tg, tg), const),
                  pl.BlockSpec((LANES, LANES), const)],
        out_specs=[pl.BlockSpec((tg, D_MODEL), row), pl.BlockSpec((tg, LANES), row),
                   pl.BlockSpec((1, SUBLANES, tg), lambda i: (i, 0, 0)), pl.BlockSpec((tg, LANES), row),
                   pl.BlockSpec((1, SUBLANES, LANES), lambda i: (i, 0, 0))],
        out_shape=[jax.ShapeDtypeStruct((n, D_MODEL), BF16), jax.ShapeDtypeStruct((n, LANES), F32),
                   jax.ShapeDtypeStruct((n_groups, SUBLANES, tg), F32), jax.ShapeDtypeStruct((n, LANES), F32),
                   jax.ShapeDtypeStruct((n_groups, SUBLANES, LANES), jnp.int32)],
        compiler_params=_cparams(("parallel",)),
        name="moe_route",
    )(x, g, wr, tri, etri)
    blk_exp = meta[:, 0, :]
    blk_cnt = meta[:, 1, 0]
    per_g = lambda g, f, j, be, nb: (g, 0)
    once = pl.Buffered(1)
    return pl.pallas_call(
        functools.partial(_moe_experts_body, rows_blk=rows_blk, nf=D_FF // tf),
        grid_spec=pltpu.PrefetchScalarGridSpec(
            num_scalar_prefetch=2,
            grid=(n_groups, D_FF // tf, n_blk),
            in_specs=[pl.BlockSpec((tg, D_MODEL), per_g, pipeline_mode=once),
                      pl.BlockSpec((tg, LANES), per_g, pipeline_mode=once),
                      pl.BlockSpec((1, SUBLANES, tg), lambda g, f, j, be, nb: (g, 0, 0), pipeline_mode=once),
                      pl.BlockSpec((tg, LANES), per_g, pipeline_mode=once),
                      pl.BlockSpec((tg, D_MODEL), per_g, pipeline_mode=once),
                      pl.BlockSpec((1, D_MODEL, tf), lambda g, f, j, be, nb: (be[g, j], 0, f)),
                      pl.BlockSpec((1, D_MODEL, tf), lambda g, f, j, be, nb: (be[g, j], 0, f)),
                      pl.BlockSpec((1, tf, D_MODEL), lambda g, f, j, be, nb: (be[g, j], f, 0)),
                      pl.BlockSpec((1, D_MODEL), lambda g, f, j, be, nb: (0, 0))],
            out_specs=pl.BlockSpec((tg, D_MODEL), per_g, pipeline_mode=once),
            scratch_shapes=[pltpu.VMEM((n_blk * rows_blk, D_MODEL), BF16),
                            pltpu.VMEM((n_blk * rows_blk, D_MODEL), BF16)],
        ),
        out_shape=jax.ShapeDtypeStruct((n, D_MODEL), F32),
        compiler_params=pltpu.CompilerParams(dimension_semantics=("parallel", "arbitrary", "arbitrary"),
                                             vmem_limit_bytes=VMEM_LIMIT_MOE),
        name="moe_experts",
    )(blk_exp, blk_cnt, xn, slot, slot_t, comb, x, wg, wu, wd, g2)


def _group_ones():
    i = np.arange(LANES) // RWKV_HEAD
    return jnp.asarray((i[:, None] == i[None, :]).astype(np.float32), BF16)


def _rwkv_proj_body(xn_ref, xp_ref, sh_ref, mu_ref, vec_ref, wrkv_ref, w1_ref, w2_ref, a1_ref, a2_ref, g1_ref,
                    g2_ref, gm_ref, r_ref, lw_ref, k_ref, v_ref, kk_ref, ka_ref, g_ref, *, tiles_per_seq):
    xn = xn_ref[...]
    if tiles_per_seq is None:
        xp = xp_ref[...]
    else:
        first = pl.program_id(0) % tiles_per_seq == 0
        row0 = jnp.where(first, sh_ref[0], xp_ref[SUBLANES - 1:SUBLANES, :])
        r = lax.broadcasted_iota(jnp.int32, xn.shape, 0)
        xp = jnp.where(r == 0, row0, pltpu.roll(xn, 1, 0))
    xx = xp - xn
    mu = mu_ref[...]
    mix = lambda i: (xn + xx * mu[i:i + 1, :]).astype(BF16)
    vec = vec_ref[...]
    w0, a0, k_k, k_a = vec[0:1, :], vec[1:2, :], vec[2:3, :], vec[3:4, :]
    r_ref[...] = _dot(mix(0), wrkv_ref[0]).astype(r_ref.dtype)
    z = w0 + _dot(jnp.tanh(_dot(mix(1), w1_ref[...])).astype(BF16), w2_ref[...])
    lw_ref[...] = jax.nn.sigmoid(z) * (-float(np.exp(-0.5)))
    k = _dot(mix(2), wrkv_ref[1])
    v_ref[...] = _dot(mix(3), wrkv_ref[2]).astype(v_ref.dtype)
    a = jax.nn.sigmoid(a0 + _dot(_dot(mix(4), a1_ref[...]).astype(BF16), a2_ref[...]))
    g_ref[...] = _dot(jax.nn.sigmoid(_dot(mix(5), g1_ref[...])).astype(BF16), g2_ref[...]).astype(g_ref.dtype)
    kk = k * k_k
    kk = kk * lax.rsqrt(jnp.maximum(_group_sum(kk * kk, gm_ref[...]), 1e-24))
    kk_ref[...] = kk.astype(kk_ref.dtype)
    ka_ref[...] = (kk * a).astype(ka_ref.dtype)
    k_ref[...] = (k * (1.0 + (a - 1.0) * k_a)).astype(k_ref.dtype)


def _rwkv_proj(xn, xp, shift_prev, seq, mu, vec, wrkv, w1, w2, a1, a2, g1, g2, tm):
    n = xn.shape[0]
    row = lambda i: (i, 0)
    c2 = lambda i: (0, 0)
    c3 = lambda i: (0, 0, 0)
    full = lambda a: pl.BlockSpec(a.shape, c3 if a.ndim == 3 else c2)
    gm = _group_ones()
    consts = [mu, vec, wrkv, w1, w2, a1, a2, g1, g2, gm]
    if xp is None:
        tiles_per_seq = seq // tm
        prev_spec = pl.BlockSpec((SUBLANES, D_MODEL), lambda i: (jnp.maximum(i * (tm // SUBLANES) - 1, 0), 0))
        shift_spec = pl.BlockSpec((1, 1, D_MODEL), lambda i: (i // tiles_per_seq, 0, 0))
        xp = xn
    else:
        tiles_per_seq = None
        prev_spec = pl.BlockSpec((tm, D_MODEL), row)
        shift_spec = pl.BlockSpec((1, 1, D_MODEL), c3)
    dtypes = (BF16, F32, BF16, BF16, BF16, BF16, BF16)
    return pl.pallas_call(
        functools.partial(_rwkv_proj_body, tiles_per_seq=tiles_per_seq),
        grid=(n // tm,),
        in_specs=[pl.BlockSpec((tm, D_MODEL), row), prev_spec, shift_spec] + [full(a) for a in consts],
        out_specs=[pl.BlockSpec((tm, D_MODEL), row)] * 7,
        out_shape=[jax.ShapeDtypeStruct((n, D_MODEL), dt) for dt in dtypes],
        compiler_params=_cparams(("parallel",)),
        name="rwkv_proj",
    )(xn, xp, shift_prev[:, None, :], *consts)


def _wkv_body(r_ref, lw_ref, k_ref, v_ref, kk_ref, ka_ref, s0_ref, y_ref, sT_ref,
              s_scr, a_scr, r_scr, b_scr, k_scr, *, chunk):
    c = pl.program_id(1)
    n_pairs = D_MODEL // LANES
    hd = RWKV_HEAD
    zeros = jnp.zeros((hd, hd), F32)

    @pl.when(c == 0)
    def _():
        for p in range(n_pairs):
            top = jnp.concatenate([s0_ref[0, 2 * p], zeros], axis=1)
            bot = jnp.concatenate([zeros, s0_ref[0, 2 * p + 1]], axis=1)
            s_scr[p] = jnp.concatenate([top, bot], axis=0)

    row = lax.broadcasted_iota(jnp.int32, (chunk, chunk), 0)
    colm = lax.broadcasted_iota(jnp.int32, (chunk, chunk), 1)
    lw = lw_ref[...]
    tri = jnp.where(row >= colm, 1.0, 0.0).astype(BF16)
    lw_hi = lw.astype(BF16)
    lw_mid = (lw - lw_hi.astype(F32)).astype(BF16)
    lw_lo = (lw - lw_hi.astype(F32) - lw_mid.astype(F32)).astype(BF16)
    cum = _dot(tri, lw_hi) + _dot(tri, lw_mid) + _dot(tri, lw_lo)
    w_incl = jnp.exp(cum)
    w_inv = jnp.exp(-cum)
    a_scr[...] = -kk_ref[...] * jnp.exp(cum - lw)
    r_scr[...] = r_ref[...] * w_incl
    b_scr[...] = ka_ref[...] * w_inv
    k_scr[...] = k_ref[...] * w_inv
    w_last = w_incl[chunk - 1:chunk, :]
    n_dbl = max(1, int(np.ceil(np.log2(chunk))))

    c2 = 2 * chunk
    head_of_row = lax.broadcasted_iota(jnp.int32, (c2, LANES), 0) // chunk
    head_of_lane = lax.broadcasted_iota(jnp.int32, (c2, LANES), 1) // hd
    own_lanes = head_of_row == head_of_lane
    tr = lax.broadcasted_iota(jnp.int32, (c2, c2), 0)
    tc = lax.broadcasted_iota(jnp.int32, (c2, c2), 1)
    same = (tr // chunk) == (tc // chunk)
    strict = same & (tr % chunk > tc % chunk)
    incl = same & (tr % chunk >= tc % chunk)

    def block_diag(x):
        return jnp.where(own_lanes, jnp.concatenate([x, x], axis=0), 0.0).astype(BF16)

    pairs = range(n_pairs)
    sls = [slice(p * LANES, (p + 1) * LANES) for p in pairs]
    ar = [jnp.concatenate([block_diag(a_scr[:, sl]), block_diag(r_scr[:, sl])], axis=0) for sl in sls]
    bk = [jnp.concatenate([block_diag(b_scr[:, sl]), block_diag(k_scr[:, sl])], axis=0) for sl in sls]
    vm = [block_diag(v_ref[:, sl]) for sl in sls]
    s_old = [s_scr[p] for p in pairs]
    gram = [_dot_nt(ar[p], bk[p]) for p in pairs]
    xs = [_dot_nt(ar[p], s_old[p].astype(BF16)) for p in pairs]
    u = [xs[p][:c2, :] + _dot(jnp.where(strict, gram[p][:c2, c2:], 0.0).astype(BF16), vm[p]) for p in pairs]
    lp = [jnp.where(strict, gram[p][:c2, :c2], 0.0).astype(BF16) for p in pairs]
    for d in range(n_dbl):
        if d + 1 < n_dbl:
            t = [_dot(lp[p], jnp.concatenate([u[p].astype(BF16), lp[p]], axis=1)) for p in pairs]
            u = [u[p] + t[p][:, :LANES] for p in pairs]
            lp = [t[p][:, LANES:].astype(BF16) for p in pairs]
        else:
            u = [u[p] + _dot(lp[p], u[p].astype(BF16)) for p in pairs]
    uv = [jnp.concatenate([u[p].astype(BF16), vm[p]], axis=0) for p in pairs]
    m_r = [jnp.concatenate([jnp.where(incl, gram[p][c2:, :c2], 0.0), jnp.where(incl, gram[p][c2:, c2:], 0.0)],
                           axis=1).astype(BF16) for p in pairs]
    y = [xs[p][c2:, :] + _dot(m_r[p], uv[p]) for p in pairs]
    for p in pairs:
        y_ref[:, sls[p]] = y[p][:chunk, :] + y[p][chunk:, :]
    s_new = [(s_old[p] + _dot_tn(uv[p], bk[p])) * w_last[:, sls[p]] for p in pairs]
    for p in pairs:
        s_scr[p] = s_new[p]

    @pl.when(c == pl.num_programs(1) - 1)
    def _():
        for p in range(n_pairs):
            s = s_scr[p]
            sT_ref[0, 2 * p] = s[:hd, :hd]
            sT_ref[0, 2 * p + 1] = s[hd:, hd:]


def _wkv(r, lw, k, v, kk, ka, s0, batch, seq, chunk):
    nc = seq // chunk
    blk = lambda b, c: (b * nc + c, 0)
    st = lambda b, c: (b, 0, 0, 0)
    state = pl.BlockSpec((1, RWKV_HEADS, RWKV_HEAD, RWKV_HEAD), st)
    return pl.pallas_call(
        functools.partial(_wkv_body, chunk=chunk),
        grid=(batch, nc),
        in_specs=[pl.BlockSpec((chunk, D_MODEL), blk)] * 6 + [state],
        out_specs=[pl.BlockSpec((chunk, D_MODEL), blk), state],
        out_shape=[jax.ShapeDtypeStruct((batch * seq, D_MODEL), F32),
                   jax.ShapeDtypeStruct(s0.shape, F32)],
        scratch_shapes=[pltpu.VMEM((D_MODEL // LANES, LANES, LANES), F32)]
                       + [pltpu.VMEM((chunk, D_MODEL), F32)] * 4,
        compiler_params=_cparams(("parallel", "arbitrary")),
        name="wkv",
    )(r, lw, k, v, kk, ka, s0)


def _rwkv_out_body(x_ref, y_ref, r_ref, k_ref, v_ref, g_ref, vec_ref, gm_ref, wo_ref, o_ref):
    gm = gm_ref[...]
    vec = vec_ref[...]
    ln_w, ln_b, r_k = vec[0:1, :], vec[1:2, :], vec[2:3, :]
    y = y_ref[...]
    mean = _group_sum(y, gm) * (1.0 / RWKV_HEAD)
    d = y - mean
    var = _group_sum(d * d, gm) * (1.0 / RWKV_HEAD)
    yn = d * lax.rsqrt(var + GN_EPS) * ln_w + ln_b
    rk = r_ref[...].astype(F32) * k_ref[...].astype(F32) * r_k
    yn = yn + _group_sum(rk, gm) * v_ref[...].astype(F32)
    o_ref[...] = x_ref[...] + _dot((yn * g_ref[...].astype(F32)).astype(BF16), wo_ref[...])


def _rwkv_out(x, y, r, k, v, g, vec, wo, tm):
    n = x.shape[0]
    row = lambda i: (i, 0)
    const = lambda i: (0, 0)
    gm = _group_ones()
    return pl.pallas_call(
        _rwkv_out_body,
        grid=(n // tm,),
        in_specs=[pl.BlockSpec((tm, D_MODEL), row)] * 6
                 + [pl.BlockSpec(vec.shape, const), pl.BlockSpec(gm.shape, const), pl.BlockSpec(wo.shape, const)],
        out_specs=pl.BlockSpec((tm, D_MODEL), row),
        out_shape=jax.ShapeDtypeStruct((n, D_MODEL), F32),
        compiler_params=_cparams(("parallel",)),
        name="rwkv_out",
    )(x, y, r, k, v, g, vec, gm, wo)


def _pick_tile(n, want):
    t = min(n, want)
    while n % t:
        t //= 2
    return t


def _trunk(x, pos, conv_prev, shift_prev, wkv_prev, wts, sample):
    batch, seq, _ = x.shape
    n = batch * seq
    xf = x.reshape(n, D_MODEL)
    tm = _pick_tile(n, 512)

    tabs = _rope_tables(pos)
    if sample is not None:
        tabs = tuple(jnp.tile(t, (batch, 1)) for t in tabs)
    q, k, v, iq, ikw, cb, u = _proj_even(xf, wts['norm_mix_even'], wts['w_in'], tabs, tm)
    if sample is None:
        attn = _prompt_attention(q, iq, ikw, k, v, batch, seq)
        conv = _conv_prompt(cb, u, wts['conv_w'], seq, _pick_tile(seq, 512))
        conv_state = u.reshape(batch, seq, CONV_DIM)[:, seq - (CONV_W - 1):]
    else:
        cache_k, cache_v, cache_ik, page_table = sample
        tpad = SUBLANES - seq

        def stack(a, heads):
            a = a.reshape(batch, seq, heads, -1).transpose(0, 2, 1, 3)
            a = jnp.pad(a, ((0, 0), (0, 0), (0, tpad), (0, 0)))
            return a.reshape(batch, heads * SUBLANES, a.shape[-1])

        iq_st = stack(iq, N_IDX_HEADS)
        iw_st = stack(ikw[:, IDX_DIM:IDX_DIM + N_IDX_HEADS], N_IDX_HEADS)
        s_past = _sample_scores(page_table, iq_st, iw_st, cache_ik)[:, :seq]
        mask = _sample_select(s_past.reshape(n, -1), iq, ikw, seq)
        mask = jnp.pad(mask.reshape(batch, seq, -1), ((0, 0), (0, tpad), (0, 0)))
        q_st = stack(q, N_Q_HEADS).reshape(batch, N_KV_HEADS, Q_PER_KV * SUBLANES, HEAD_DIM)
        padk = lambda a: jnp.pad(jnp.swapaxes(a.reshape(batch, seq, KV_DIM), 1, 2),
                                 ((0, 0), (0, 0), (0, PAGE_SIZE - seq)))
        o = _sample_attention(page_table, q_st, mask, padk(k), padk(v), cache_k, cache_v)
        attn = (o.reshape(batch, N_KV_HEADS, Q_PER_KV, SUBLANES, HEAD_DIM)[:, :, :, :seq].transpose(0, 3, 2, 1, 4)
                .reshape(n, ATTN_DIM).astype(BF16))
        u3 = jnp.concatenate([conv_prev, u.reshape(batch, seq, CONV_DIM)], axis=1)
        conv_state = u3[:, seq:]
        conv_t = _conv_sample(jnp.swapaxes(cb.reshape(batch, seq, CONV_DIM), 0, 1), jnp.swapaxes(u3, 0, 1),
                              wts['conv_w'])
        conv = jnp.swapaxes(conv_t, 0, 1).reshape(n, CONV_DIM)
    x1 = _out_even(xf, attn, conv, wts['w_out_attn'], wts['w_out_conv'], tm)
    x2, xn2 = _ffn(x1, wts['norm_ffn_even'], wts['ffn_gate'], wts['ffn_up'], wts['ffn_down'],
                   wts['norm_mix_odd'], tm, D_FF // 2)

    xn3 = xn2.reshape(batch, seq, D_MODEL)
    tm1 = _pick_tile(n, 256)
    xp = None
    if seq % tm1:
        xp = jnp.concatenate([shift_prev[:, None, :], xn3[:, :-1]], axis=1).reshape(n, D_MODEL)
    r, lw, kr, vr, kk, ka, g = _rwkv_proj(xn2, xp, shift_prev, seq, wts['rwkv_mu'], wts['rwkv_vec_in'],
                                          wts['rwkv_w_rkv'], wts['rwkv_w1'], wts['rwkv_w2'], wts['rwkv_a1'],
                                          wts['rwkv_a2'], wts['rwkv_g1'], wts['rwkv_g2'], tm1)
    if seq % WKV_CHUNK:
        sp = -(-seq // WKV_CHUNK) * WKV_CHUNK
        padded = [jnp.pad(a.reshape(batch, seq, D_MODEL), ((0, 0), (0, sp - seq), (0, 0)))
                  .reshape(batch * sp, D_MODEL) for a in (r, lw, kr, vr, kk, ka)]
        y, wkv_state = _wkv(*padded, wkv_prev, batch, sp, WKV_CHUNK)
        y = y.reshape(batch, sp, D_MODEL)[:, :seq].reshape(n, D_MODEL)
    else:
        y, wkv_state = _wkv(r, lw, kr, vr, kk, ka, wkv_prev, batch, seq, WKV_CHUNK)
    x3 = _rwkv_out(x2, y, r, kr, vr, g, wts['rwkv_vec_out'], wts['rwkv_w_o'], _pick_tile(n, 256))
    yf = _moe(x3, wts['norm_ffn_odd'], wts['moe_router'], wts['moe_gate'], wts['moe_up'], wts['moe_down'],
              wts['norm_final'], _pick_tile(n, MOE_GROUP), min(MOE_BLOCK_ROWS, _pick_tile(n, MOE_GROUP)), D_FF // 2)

    k4 = k.reshape(1, batch, seq, N_KV_HEADS, HEAD_DIM)
    v4 = v.reshape(1, batch, seq, N_KV_HEADS, HEAD_DIM)
    ik3 = ikw[:, :IDX_DIM].reshape(1, batch, seq, IDX_DIM)
    return (yf.reshape(batch, seq, D_MODEL), k4, v4, ik3, conv_state[None],
            xn3[:, -1][None], wkv_state[None])


def _prepare_weights(norm_mix_even, w_in_even, conv_w, w_out_even, norm_ffn_even, ffn_gate, ffn_up, ffn_down,
                     norm_mix_odd, rwkv_mu, rwkv_w_rkv, rwkv_w0, rwkv_w1, rwkv_w2, rwkv_a0, rwkv_a1, rwkv_a2,
                     rwkv_g1, rwkv_g2, rwkv_k_k, rwkv_k_a, rwkv_r_k, rwkv_ln_w, rwkv_ln_b, rwkv_w_o,
                     norm_ffn_odd, moe_router, moe_gate, moe_up, moe_down, norm_final):
    w_in = w_in_even[0]
    o = np.cumsum((0, ATTN_DIM, KV_DIM, KV_DIM, N_IDX_HEADS * IDX_DIM, IDX_DIM, N_IDX_HEADS,
                   CONV_DIM, CONV_DIM, CONV_DIM))
    pad = jnp.zeros((D_MODEL, LANES - IDX_DIM - N_IDX_HEADS), F32)
    w_in = jnp.concatenate([w_in[:, :o[4]], w_in[:, o[4]:o[6]], pad, w_in[:, o[6]:]], axis=1).astype(BF16)
    row = lambda a: a.reshape(1, -1)
    zeros = jnp.zeros((1, D_MODEL), F32)
    return dict(
        norm_mix_even=row(norm_mix_even[0]), w_in=w_in, conv_w=conv_w[0],
        w_out_attn=(w_out_even[0, :ATTN_DIM].reshape(N_KV_HEADS, Q_PER_KV, HEAD_DIM, D_MODEL)
                    .transpose(1, 0, 2, 3).reshape(ATTN_DIM, D_MODEL).astype(BF16)),
        w_out_conv=w_out_even[0, ATTN_DIM:].astype(BF16),
        norm_ffn_even=row(norm_ffn_even[0]),
        ffn_gate=ffn_gate[0].astype(BF16), ffn_up=ffn_up[0].astype(BF16), ffn_down=ffn_down[0].astype(BF16),
        norm_mix_odd=row(norm_mix_odd[0]), rwkv_mu=jnp.concatenate([rwkv_mu[0], zeros, zeros], axis=0),
        rwkv_vec_in=jnp.concatenate([row(rwkv_w0[0]), row(rwkv_a0[0]), row(rwkv_k_k[0]), row(rwkv_k_a[0]),
                                     zeros, zeros, zeros, zeros], axis=0),
        rwkv_w_rkv=rwkv_w_rkv[0].astype(BF16),
        rwkv_w1=rwkv_w1[0].astype(BF16), rwkv_w2=rwkv_w2[0].astype(BF16),
        rwkv_a1=rwkv_a1[0].astype(BF16), rwkv_a2=rwkv_a2[0].astype(BF16),
        rwkv_g1=rwkv_g1[0].astype(BF16), rwkv_g2=rwkv_g2[0].astype(BF16),
        rwkv_vec_out=jnp.concatenate([row(rwkv_ln_w[0]), row(rwkv_ln_b[0]), row(rwkv_r_k[0]),
                                      zeros, zeros, zeros, zeros, zeros], axis=0),
        rwkv_w_o=rwkv_w_o[0].astype(BF16),
        norm_ffn_odd=row(norm_ffn_odd[0]),
        moe_router=jnp.pad(moe_router[0], ((0, 0), (0, LANES - N_EXPERTS))).astype(BF16),
        moe_gate=moe_gate[0].astype(BF16), moe_up=moe_up[0].astype(BF16), moe_down=moe_down[0].astype(BF16),
        norm_final=row(norm_final),
    )


def kernel(x_prompt, x_sample, cache_k, cache_v, cache_idx_k, state_conv, state_shift, state_wkv, page_table, norm_mix_even, w_in_even, conv_w, w_out_even, norm_ffn_even, ffn_gate, ffn_up, ffn_down, norm_mix_odd, rwkv_mu, rwkv_w_rkv, rwkv_w0, rwkv_w1, rwkv_w2, rwkv_a0, rwkv_a1, rwkv_a2, rwkv_g1, rwkv_g2, rwkv_k_k, rwkv_k_a, rwkv_r_k, rwkv_ln_w, rwkv_ln_b, rwkv_w_o, norm_ffn_odd, moe_router, moe_gate, moe_up, moe_down, norm_final):
    assert w_in_even.shape[0] == 1 and rwkv_mu.shape[0] == 1, "one even and one odd layer"
    wts = _prepare_weights(norm_mix_even, w_in_even, conv_w, w_out_even, norm_ffn_even, ffn_gate, ffn_up, ffn_down,
                           norm_mix_odd, rwkv_mu, rwkv_w_rkv, rwkv_w0, rwkv_w1, rwkv_w2, rwkv_a0, rwkv_a1, rwkv_a2,
                           rwkv_g1, rwkv_g2, rwkv_k_k, rwkv_k_a, rwkv_r_k, rwkv_ln_w, rwkv_ln_b, rwkv_w_o,
                           norm_ffn_odd, moe_router, moe_gate, moe_up, moe_down, norm_final)
    b, t = x_prompt.shape[:2]
    bd, tn = x_sample.shape[:2]
    n_pool = cache_k.shape[1]
    past = page_table.shape[1] * PAGE_SIZE
    pos_prompt = jnp.arange(t, dtype=jnp.int32)
    pos_sample = past + jnp.arange(tn, dtype=jnp.int32)
    zeros = lambda *s: jnp.zeros(s, F32)
    out_p = _trunk(x_prompt, pos_prompt, zeros(b, CONV_W - 1, CONV_DIM), zeros(b, D_MODEL),
                   zeros(b, RWKV_HEADS, RWKV_HEAD, RWKV_HEAD), wts, None)
    pages_t = lambda c: jnp.transpose(c[0], (0, 2, 3, 1)).reshape(n_pool, KV_DIM, PAGE_SIZE)
    sample = (pages_t(cache_k), pages_t(cache_v), jnp.swapaxes(cache_idx_k[0], 1, 2), page_table)
    out_s = _trunk(x_sample, pos_sample, state_conv[0], state_shift[0], state_wkv[0], wts, sample)
    y_p, k_p, v_p, ik_p, conv_p, shift_p, wkv_p = out_p
    y_s, k_s, v_s, ik_s, conv_s, shift_s, wkv_s = out_s
    return (y_p, y_s, k_p, v_p, ik_p, k_s, v_s, ik_s, conv_p, conv_s, shift_p, shift_s, wkv_p, wkv_s)
```

```python
import functools

import numpy as np
import jax
import jax.numpy as jnp
from jax import lax
from jax.experimental import pallas as pl
from jax.experimental.pallas import tpu as pltpu

F32 = jnp.float32
BF16 = jnp.bfloat16

D_MODEL = 1024
PAGE_SIZE = 128
HEAD_DIM = 64
N_Q_HEADS = 8
N_KV_HEADS = 2
Q_PER_KV = N_Q_HEADS // N_KV_HEADS
ROT_DIM = HEAD_DIM // 4
ROPE_THETA = 500000.0
N_IDX_HEADS = 4
IDX_DIM = 64
TOPK_MAX = 256
Q_BLOCK = 128
ATTN_DIM = N_Q_HEADS * HEAD_DIM
KV_DIM = N_KV_HEADS * HEAD_DIM
CONV_DIM = D_MODEL // 2
CONV_W = 3
RWKV_HEAD = 64
RWKV_HEADS = D_MODEL // RWKV_HEAD
GN_EPS = 64e-5
D_FF = 2816
N_EXPERTS = 8
RMS_EPS = 1e-6

LANES = 128
SUBLANES = 8
VMEM_LIMIT = 56 * 1024 * 1024
VMEM_LIMIT_MOE = 61 * 1024 * 1024
INT_MIN = -2 ** 31
KEY_NEG_INF = INT_MIN + 0x7FFFFF
PROJ_COLS = ATTN_DIM + KV_DIM + KV_DIM + N_IDX_HEADS * IDX_DIM + LANES + 3 * CONV_DIM
WKV_CHUNK = 64
MOE_GROUP = 1024
MOE_BLOCK_ROWS = 288


def _cparams(sem):
    return pltpu.CompilerParams(dimension_semantics=sem, vmem_limit_bytes=VMEM_LIMIT)


def _rms(x, g):
    return x * lax.rsqrt(jnp.mean(x * x, axis=-1, keepdims=True) + RMS_EPS) * g


def _dot(a, b):
    return jnp.dot(a, b, preferred_element_type=F32)


def _dot_nt(a, b):
    return lax.dot_general(a, b, (((1,), (1,)), ((), ())), preferred_element_type=F32)


def _dot_tn(a, b):
    return lax.dot_general(a, b, (((0,), (0,)), ((), ())), preferred_element_type=F32)


def _group_sum(x, gmat):
    outs = []
    for c in range(x.shape[1] // LANES):
        xc = x[:, c * LANES:(c + 1) * LANES]
        hi = xc.astype(BF16)
        lo = (xc - hi.astype(F32)).astype(BF16)
        outs.append(_dot(hi, gmat) + _dot(lo, gmat))
    return jnp.concatenate(outs, axis=1)


def _rope_chunk(xc, rc, rp, rm):
    half = ROT_DIM // 2
    return xc * rc + pltpu.roll(xc, half, 1) * rp + pltpu.roll(xc, LANES - half, 1) * rm


def _proj_even_body(x_ref, g_ref, w_ref, rc_ref, rp_ref, rm_ref,
                    q_ref, k_ref, v_ref, iq_ref, ikw_ref, cb_ref, u_ref):
    xn = _rms(x_ref[...], g_ref[...]).astype(BF16)
    h = _dot(xn, w_ref[...])
    rc, rp, rm = rc_ref[...], rp_ref[...], rm_ref[...]
    col = 0
    for c in range(ATTN_DIM // LANES):
        q_ref[:, c * LANES:(c + 1) * LANES] = _rope_chunk(h[:, col:col + LANES], rc, rp, rm).astype(BF16)
        col += LANES
    k_ref[...] = _rope_chunk(h[:, col:col + LANES], rc, rp, rm)
    col += LANES
    v_ref[...] = h[:, col:col + LANES]
    col += LANES
    for c in range(N_IDX_HEADS * IDX_DIM // LANES):
        iq_ref[:, c * LANES:(c + 1) * LANES] = _rope_chunk(h[:, col:col + LANES], rc, rp, rm).astype(BF16)
        col += LANES
    ikw = h[:, col:col + LANES]
    lane = lax.broadcasted_iota(jnp.int32, ikw.shape, 1)
    ikw_ref[...] = jnp.where(lane < IDX_DIM, _rope_chunk(ikw, rc, rp, rm), ikw)
    col += LANES
    cb_ref[...] = h[:, col:col + CONV_DIM]
    col += CONV_DIM
    u_ref[...] = h[:, col:col + CONV_DIM] * h[:, col + CONV_DIM:col + 2 * CONV_DIM]


def _proj_even(x, g, w, tabs, tm):
    n = x.shape[0]
    nt = tabs[0].shape[0] // tm
    row = lambda i: (i, 0)
    const = lambda i: (0, 0)
    tab = lambda i: (i % nt, 0)
    widths = (ATTN_DIM, KV_DIM, KV_DIM, N_IDX_HEADS * IDX_DIM, LANES, CONV_DIM, CONV_DIM)
    dtypes = (BF16, F32, F32, BF16, F32, F32, F32)
    return pl.pallas_call(
        _proj_even_body,
        grid=(n // tm,),
        in_specs=[pl.BlockSpec((tm, D_MODEL), row), pl.BlockSpec((1, D_MODEL), const),
                  pl.BlockSpec((D_MODEL, PROJ_COLS), const)] + [pl.BlockSpec((tm, LANES), tab)] * 3,
        out_specs=[pl.BlockSpec((tm, wd), row) for wd in widths],
        out_shape=[jax.ShapeDtypeStruct((n, wd), dt) for wd, dt in zip(widths, dtypes)],
        compiler_params=_cparams(("parallel",)),
        name="proj_even",
    )(x, g, w, *tabs)


def _rope_tables(pos):
    half = ROT_DIM // 2
    inv_freq = ROPE_THETA ** (-jnp.arange(half, dtype=F32) / half)
    ang = pos.astype(F32)[:, None] * inv_freq[None, :]
    cos, sin = jnp.cos(ang), jnp.sin(ang)
    t = pos.shape[0]
    pad = jnp.zeros((t, HEAD_DIM - ROT_DIM), F32)
    zero = jnp.zeros((t, half), F32)
    rc = jnp.concatenate([cos, cos, pad + 1.0], axis=1)
    rp = jnp.concatenate([zero, sin, pad], axis=1)
    rm = jnp.concatenate([-sin, zero, pad], axis=1)
    return tuple(jnp.tile(a, (1, LANES // HEAD_DIM)) for a in (rc, rp, rm))


def _conv_prompt_body(cb_ref, u_ref, up_ref, w_ref, y_ref, *, tiles_per_seq):
    u = u_ref[...]
    first = pl.program_id(0) % tiles_per_seq == 0
    prev = jnp.where(first, 0.0, up_ref[...])
    p1, p2 = prev[SUBLANES - 1:SUBLANES, :], prev[SUBLANES - 2:SUBLANES - 1, :]
    r = lax.broadcasted_iota(jnp.int32, u.shape, 0)
    u1 = jnp.where(r == 0, p1, pltpu.roll(u, 1, 0))
    u2 = jnp.where(r == 0, p2, jnp.where(r == 1, p1, pltpu.roll(u, 2, 0)))
    w = w_ref[...]
    y_ref[...] = cb_ref[...] * (w[0:1, :] * u2 + w[1:2, :] * u1 + w[2:3, :] * u)


def _conv_prompt(cb, u, w, seq, tc):
    n = u.shape[0]
    row = lambda i: (i, 0)
    prev = lambda i: (jnp.maximum(i * (tc // SUBLANES) - 1, 0), 0)
    return pl.pallas_call(
        functools.partial(_conv_prompt_body, tiles_per_seq=seq // tc),
        grid=(n // tc,),
        in_specs=[pl.BlockSpec((tc, CONV_DIM), row), pl.BlockSpec((tc, CONV_DIM), row),
                  pl.BlockSpec((SUBLANES, CONV_DIM), prev), pl.BlockSpec((CONV_W, CONV_DIM), lambda i: (0, 0))],
        out_specs=pl.BlockSpec((tc, CONV_DIM), row),
        out_shape=jax.ShapeDtypeStruct((n, CONV_DIM), F32),
        compiler_params=_cparams(("parallel",)),
        name="conv_prompt",
    )(cb, u, u, w)


def _conv_sample_body(cb_ref, ue_ref, w_ref, y_ref):
    w = w_ref[...]
    for t in range(y_ref.shape[0]):
        acc = w[0:1, :] * ue_ref[t] + w[1:2, :] * ue_ref[t + 1] + w[2:3, :] * ue_ref[t + 2]
        y_ref[t] = cb_ref[t] * acc


def _conv_sample(cb_t, ue_t, w):
    return pl.pallas_call(
        _conv_sample_body,
        out_shape=jax.ShapeDtypeStruct(cb_t.shape, F32),
        name="conv_sample",
    )(cb_t, ue_t, w)


def _topk_select(scores, n_sel, tri):
    rows, width = scores.shape
    sc = scores

    def key_to_float(key):
        return lax.bitcast_convert_type(key ^ ((key >> 31) & 0x7FFFFFFF), F32)

    def count_ge(key):
        cnt = jnp.sum(jnp.where(sc >= key_to_float(key), 1.0, 0.0), axis=1, keepdims=True)
        return jnp.where(key <= KEY_NEG_INF, float(width), cnt)

    thr = jnp.where(count_ge(jnp.zeros((rows, 1), jnp.int32)) >= n_sel, 0, INT_MIN).astype(jnp.int32)

    def body(i, thr):
        cand = thr + lax.shift_left(jnp.int32(1), 30 - i)
        return jnp.where(count_ge(cand) >= n_sel, cand, thr)

    thr = lax.fori_loop(0, 31, body, thr)
    thr_f = key_to_float(thr)
    gt = sc > thr_f
    eq = sc == thr_f
    need = n_sel - jnp.sum(jnp.where(gt, 1.0, 0.0), axis=1, keepdims=True)
    off = jnp.zeros((rows, 1), F32)
    parts = []
    for c in range(width // LANES):
        eqc = jnp.where(eq[:, c * LANES:(c + 1) * LANES], 1.0, 0.0)
        incl = _dot(eqc.astype(BF16), tri)
        parts.append((incl - eqc + off) < need)
        off = off + incl[:, LANES - 1:LANES]
    tie = jnp.concatenate(parts, axis=1)
    return (gt | (eq & tie)) & (sc > -jnp.inf)


def _tri_incl():
    i = np.arange(LANES)
    return jnp.asarray((i[:, None] <= i[None, :]).astype(np.float32), BF16)


def _stack_heads(x, n):
    return jnp.concatenate([x[:, h * HEAD_DIM:(h + 1) * HEAD_DIM] for h in range(n)], axis=0)


def _index_scores(iq_st, iw_st, ik_b, keys_on_lanes=False):
    rows = iq_st.shape[0] // N_IDX_HEADS
    s = _dot(iq_st, ik_b) if keys_on_lanes else _dot_nt(iq_st, ik_b)
    term = jnp.maximum(s, 0.0) * (IDX_DIM ** -0.5) * (iw_st * (N_IDX_HEADS ** -0.5))
    acc = term[0:rows, :]
    for h in range(1, N_IDX_HEADS):
        acc = acc + term[h * rows:(h + 1) * rows, :]
    return acc


def _group_logits(qs, kg, sel):
    s = _dot(qs, kg) * (HEAD_DIM ** -0.5)
    return jnp.where(jnp.concatenate([sel] * Q_PER_KV, axis=0), s, -jnp.inf)


def _reduce_rows(x, op):
    parts = [x[c * LANES:(c + 1) * LANES, :] for c in range(x.shape[0] // LANES)]
    while len(parts) > 1:
        nxt = [op(a, b) for a, b in zip(parts[0::2], parts[1::2])]
        if len(parts) % 2:
            nxt.append(parts[-1])
        parts = nxt
    x = parts[0]
    rows = x.shape[0]
    while rows > SUBLANES:
        rows //= 2
        x = op(x[:rows, :], x[rows:, :])
    red = jnp.max if op is jnp.maximum else jnp.sum
    return red(x, axis=0, keepdims=True)


def _topk_select_keys_on_rows(sc, n_sel, tril):
    width, cols = sc.shape

    def key_to_float(key):
        return lax.bitcast_convert_type(key ^ ((key >> 31) & 0x7FFFFFFF), F32)

    def count_ge(key):
        cnt = _reduce_rows(jnp.where(sc >= key_to_float(key), 1.0, 0.0), jnp.add)
        return jnp.where(key <= KEY_NEG_INF, float(width), cnt)

    thr = jnp.where(count_ge(jnp.zeros((1, cols), jnp.int32)) >= n_sel, 0, INT_MIN).astype(jnp.int32)

    def body(i, thr):
        cand = thr + lax.shift_left(jnp.int32(1), 30 - i)
        return jnp.where(count_ge(cand) >= n_sel, cand, thr)

    thr = lax.fori_loop(0, 31, body, thr)
    thr_f = key_to_float(thr)
    gt = sc > thr_f
    eq = sc == thr_f
    need = n_sel - _reduce_rows(jnp.where(gt, 1.0, 0.0), jnp.add)
    off = jnp.zeros((1, cols), F32)
    parts = []
    for c in range(width // LANES):
        eqc = jnp.where(eq[c * LANES:(c + 1) * LANES, :], 1.0, 0.0)
        incl = _dot(tril, eqc.astype(BF16))
        parts.append((incl - eqc + off) < need)
        off = off + incl[LANES - 1:LANES, :]
    tie = jnp.concatenate(parts, axis=0)
    return (gt | (eq & tie)) & (sc > -jnp.inf)


def _prompt_attn_body(q_ref, iq_ref, iwq_ref, k_ref, v_ref, ikw_ref, tril_ref, o_ref,
                      kb_scr, vt_scr, ikb_scr, *, n_sel, widths):
    i = pl.program_id(1)

    @pl.when(i == 0)
    def _():
        for g in range(N_KV_HEADS):
            kb_scr[g] = k_ref[:, g * HEAD_DIM:(g + 1) * HEAD_DIM].astype(BF16)
        vt_scr[...] = jnp.transpose(v_ref[...]).astype(BF16)
        ikb_scr[...] = ikw_ref[:, :IDX_DIM].astype(BF16)

    def run(width):
        iw_t = jnp.transpose(iwq_ref[...])
        w_row = jnp.concatenate([iw_t[IDX_DIM + h:IDX_DIM + h + 1, :] for h in range(N_IDX_HEADS)], axis=1)
        s = _dot_nt(ikb_scr[0:width, :], _stack_heads(iq_ref[...], N_IDX_HEADS))
        term = jnp.maximum(s, 0.0) * (w_row * (IDX_DIM ** -0.5 * N_IDX_HEADS ** -0.5))
        scores = term[:, 0:Q_BLOCK]
        for h in range(1, N_IDX_HEADS):
            scores = scores + term[:, h * Q_BLOCK:(h + 1) * Q_BLOCK]
        key_pos = lax.broadcasted_iota(jnp.int32, scores.shape, 0)
        tq = i * Q_BLOCK + lax.broadcasted_iota(jnp.int32, scores.shape, 1)
        scores = jnp.where(key_pos <= tq, scores, -jnp.inf)
        sel = _topk_select_keys_on_rows(scores, n_sel, tril_ref[...])
        sel4 = jnp.concatenate([sel] * Q_PER_KV, axis=1)
        q = q_ref[...]
        outs = []
        for g in range(N_KV_HEADS):
            qs = _stack_heads(q[:, g * Q_PER_KV * HEAD_DIM:(g + 1) * Q_PER_KV * HEAD_DIM], Q_PER_KV) * (HEAD_DIM ** -0.5)
            st = jnp.where(sel4, _dot_nt(kb_scr[g, 0:width, :], qs), -jnp.inf)
            p = jnp.exp(st - _reduce_rows(st, jnp.maximum))
            ot = _dot(vt_scr[g * HEAD_DIM:(g + 1) * HEAD_DIM, 0:width], p.astype(BF16))
            outs.append(ot / _reduce_rows(p, jnp.add))
        o = jnp.transpose(jnp.concatenate(outs, axis=0))
        for r in range(Q_PER_KV):
            o_ref[:, r * LANES:(r + 1) * LANES] = o[r * Q_BLOCK:(r + 1) * Q_BLOCK, :].astype(o_ref.dtype)

    lo = 0
    for width in widths:
        hi = width // Q_BLOCK
        pl.when((i >= lo) & (i < hi))(functools.partial(run, width))
        lo = hi


def _prompt_attention(q, iq, ikw, k, v, batch, seq):
    n_sel = min(TOPK_MAX, seq // 4)
    nqb = seq // Q_BLOCK
    n_widths = min(8, nqb)
    widths = tuple(seq * (j + 1) // n_widths for j in range(n_widths))
    blk = lambda b, i: (b * nqb + i, 0)
    full = lambda b, i: (b, 0)
    return pl.pallas_call(
        functools.partial(_prompt_attn_body, n_sel=n_sel, widths=widths),
        grid=(batch, nqb),
        in_specs=[pl.BlockSpec((Q_BLOCK, ATTN_DIM), blk), pl.BlockSpec((Q_BLOCK, N_IDX_HEADS * IDX_DIM), blk),
                  pl.BlockSpec((Q_BLOCK, LANES), blk),
                  pl.BlockSpec((seq, KV_DIM), full), pl.BlockSpec((seq, KV_DIM), full),
                  pl.BlockSpec((seq, LANES), full), pl.BlockSpec((LANES, LANES), lambda b, i: (0, 0))],
        out_specs=pl.BlockSpec((Q_BLOCK, ATTN_DIM), blk),
        out_shape=jax.ShapeDtypeStruct((batch * seq, ATTN_DIM), BF16),
        scratch_shapes=[pltpu.VMEM((N_KV_HEADS, seq, HEAD_DIM), BF16), pltpu.VMEM((KV_DIM, seq), BF16),
                        pltpu.VMEM((seq, IDX_DIM), BF16)],
        compiler_params=_cparams(("parallel", "arbitrary")),
        name="prompt_attention",
    )(q, iq, ikw, k, v, ikw, _tri_incl().T)


PAGES_PER_STEP = 64


def _sample_scores_body(pt_ref, iq_ref, iw_ref, *rest):
    page_refs, s_ref = rest[:PAGES_PER_STEP], rest[PAGES_PER_STEP]
    for j, pr in enumerate(page_refs):
        s_ref[0, :, j * PAGE_SIZE:(j + 1) * PAGE_SIZE] = _index_scores(iq_ref[0], iw_ref[0], pr[0].astype(BF16), True)


def _sample_scores(page_table, iq_st, iw_st, cache_ik):
    bd, n_pages = page_table.shape
    rows = iq_st.shape[1] // N_IDX_HEADS
    assert n_pages % PAGES_PER_STEP == 0
    steps = n_pages // PAGES_PER_STEP
    per_b = lambda b, p, pt: (b, 0, 0)
    page = lambda j: (lambda b, p, pt: (pt[b, p * PAGES_PER_STEP + j], 0, 0))
    return pl.pallas_call(
        _sample_scores_body,
        grid_spec=pltpu.PrefetchScalarGridSpec(
            num_scalar_prefetch=1,
            grid=(bd, steps),
            in_specs=[pl.BlockSpec((1,) + iq_st.shape[1:], per_b), pl.BlockSpec((1,) + iw_st.shape[1:], per_b)]
                     + [pl.BlockSpec((1, IDX_DIM, PAGE_SIZE), page(j)) for j in range(PAGES_PER_STEP)],
            out_specs=pl.BlockSpec((1, rows, PAGES_PER_STEP * PAGE_SIZE), lambda b, p, pt: (b, 0, p)),
        ),
        out_shape=jax.ShapeDtypeStruct((bd, rows, n_pages * PAGE_SIZE), F32),
        compiler_params=_cparams(("parallel", "arbitrary")),
        name="sample_scores",
    )(page_table, iq_st, iw_st, *([cache_ik] * PAGES_PER_STEP))


def _sample_select_body(sp_ref, iq_ref, iwq_ref, ikn_ref, tri_ref, m_ref, *, n_sel, tn):
    rows = sp_ref.shape[0]
    iw = iwq_ref[...]
    iw_st = jnp.concatenate([iw[:, IDX_DIM + h:IDX_DIM + h + 1] for h in range(N_IDX_HEADS)], axis=0)
    s_new = _index_scores(_stack_heads(iq_ref[...], N_IDX_HEADS), iw_st, ikn_ref[...].astype(BF16))
    r = lax.broadcasted_iota(jnp.int32, s_new.shape, 0)
    c = lax.broadcasted_iota(jnp.int32, s_new.shape, 1)
    same = (r // tn == c // tn) & (c <= r)
    fold = jnp.where((lax.broadcasted_iota(jnp.int32, (rows, LANES), 0) % tn)
                     == lax.broadcasted_iota(jnp.int32, (rows, LANES), 1), 1.0, 0.0)
    picked = jnp.where(same, s_new, 0.0)
    hi = picked.astype(BF16)
    mid = (picked - hi.astype(F32)).astype(BF16)
    lo = (picked - hi.astype(F32) - mid.astype(F32)).astype(BF16)
    fb = fold.astype(BF16)
    new_chunk = _dot(hi, fb) + _dot(mid, fb) + _dot(lo, fb)
    lane = lax.broadcasted_iota(jnp.int32, (rows, LANES), 1)
    tpos = lax.broadcasted_iota(jnp.int32, (rows, LANES), 0) % tn
    new_chunk = jnp.where(lane <= tpos, new_chunk, -jnp.inf)
    scores = jnp.concatenate([sp_ref[...], new_chunk], axis=1)
    sel = _topk_select(scores, n_sel, tri_ref[...])
    m_ref[...] = jnp.where(sel, 1.0, 0.0)


def _sample_select(s_past, iq, ikw, tn):
    rows, past = s_past.shape
    n_sel = min(TOPK_MAX, (past + tn) // 4)
    return pl.pallas_call(
        functools.partial(_sample_select_body, n_sel=n_sel, tn=tn),
        out_shape=jax.ShapeDtypeStruct((rows, past + LANES), F32),
        compiler_params=pltpu.CompilerParams(vmem_limit_bytes=VMEM_LIMIT),
        name="sample_select",
    )(s_past, iq, ikw, ikw[:, :IDX_DIM], _tri_incl())


def _sample_attn_body(pt_ref, q_ref, m_ref, kn_ref, vn_ref, mn_ref, *rest):
    k_refs, v_refs = rest[:PAGES_PER_STEP], rest[PAGES_PER_STEP:2 * PAGES_PER_STEP]
    o_ref, m_scr, l_scr, acc_scr = rest[2 * PAGES_PER_STEP:]
    p = pl.program_id(1)

    @pl.when(p == 0)
    def _():
        m_scr[...] = jnp.full(m_scr.shape, -jnp.inf, F32)
        l_scr[...] = jnp.zeros(l_scr.shape, F32)
        acc_scr[...] = jnp.zeros(acc_scr.shape, F32)

    def update(kb, vb, sel):
        for g in range(N_KV_HEADS):
            s = _group_logits(q_ref[0, g], kb[g * HEAD_DIM:(g + 1) * HEAD_DIM, :], sel)
            m_old = m_scr[g]
            m_new = jnp.maximum(m_old, jnp.max(s, axis=1, keepdims=True))
            m_safe = jnp.where(m_new == -jnp.inf, 0.0, m_new)
            alpha = jnp.exp(m_old - m_safe)
            pe = jnp.exp(s - m_safe)
            l_scr[g] = alpha * l_scr[g] + jnp.sum(pe, axis=1, keepdims=True)
            acc_scr[g] = alpha * acc_scr[g] + _dot_nt(pe.astype(BF16), vb[g * HEAD_DIM:(g + 1) * HEAD_DIM, :])
            m_scr[g] = m_new

    kb = jnp.concatenate([r[0].astype(BF16) for r in k_refs], axis=1)
    vb = jnp.concatenate([r[0].astype(BF16) for r in v_refs], axis=1)
    update(kb, vb, m_ref[0] > 0.5)

    @pl.when(p == pl.num_programs(1) - 1)
    def _():
        update(kn_ref[0].astype(BF16), vn_ref[0].astype(BF16), mn_ref[0] > 0.5)
        for g in range(N_KV_HEADS):
            l = l_scr[g]
            o_ref[0, g] = acc_scr[g] / jnp.where(l == 0.0, 1.0, l)


def _sample_attention(page_table, q_st, mask, k_new, v_new, cache_k, cache_v):
    bd, n_pages = page_table.shape
    assert n_pages % PAGES_PER_STEP == 0
    steps = n_pages // PAGES_PER_STEP
    per_b = lambda b, p, pt: (b, 0, 0)
    per_b4 = lambda b, p, pt: (b, 0, 0, 0)
    page = lambda j: (lambda b, p, pt: (pt[b, p * PAGES_PER_STEP + j], 0, 0))
    width = PAGES_PER_STEP * PAGE_SIZE
    rows = q_st.shape[2]
    mrows = mask.shape[1]
    return pl.pallas_call(
        _sample_attn_body,
        grid_spec=pltpu.PrefetchScalarGridSpec(
            num_scalar_prefetch=1,
            grid=(bd, steps),
            in_specs=[pl.BlockSpec((1,) + q_st.shape[1:], per_b4),
                      pl.BlockSpec((1, mrows, width), lambda b, p, pt: (b, 0, p)),
                      pl.BlockSpec((1, KV_DIM, PAGE_SIZE), per_b), pl.BlockSpec((1, KV_DIM, PAGE_SIZE), per_b),
                      pl.BlockSpec((1, mrows, LANES), lambda b, p, pt: (b, 0, n_pages))]
                     + [pl.BlockSpec((1, KV_DIM, PAGE_SIZE), page(j)) for j in range(PAGES_PER_STEP)] * 2,
            out_specs=pl.BlockSpec((1, N_KV_HEADS, rows, HEAD_DIM), per_b4),
            scratch_shapes=[pltpu.VMEM((N_KV_HEADS, rows, 1), F32), pltpu.VMEM((N_KV_HEADS, rows, 1), F32),
                            pltpu.VMEM((N_KV_HEADS, rows, HEAD_DIM), F32)],
        ),
        out_shape=jax.ShapeDtypeStruct((bd, N_KV_HEADS, rows, HEAD_DIM), F32),
        compiler_params=_cparams(("parallel", "arbitrary")),
        name="sample_attention",
    )(page_table, q_st, mask, k_new, v_new, mask, *([cache_k] * PAGES_PER_STEP), *([cache_v] * PAGES_PER_STEP))


def _out_even_body(x_ref, a_ref, c_ref, wa_ref, wc_ref, o_ref):
    o_ref[...] = x_ref[...] + _dot(a_ref[...], wa_ref[...]) + _dot(c_ref[...].astype(BF16), wc_ref[...])


def _out_even(x, attn, conv, wa, wc, tm):
    n = x.shape[0]
    row = lambda i: (i, 0)
    const = lambda i: (0, 0)
    return pl.pallas_call(
        _out_even_body,
        grid=(n // tm,),
        in_specs=[pl.BlockSpec((tm, D_MODEL), row), pl.BlockSpec((tm, ATTN_DIM), row),
                  pl.BlockSpec((tm, CONV_DIM), row), pl.BlockSpec((ATTN_DIM, D_MODEL), const),
                  pl.BlockSpec((CONV_DIM, D_MODEL), const)],
        out_specs=pl.BlockSpec((tm, D_MODEL), row),
        out_shape=jax.ShapeDtypeStruct((n, D_MODEL), F32),
        compiler_params=_cparams(("parallel",)),
        name="out_even",
    )(x, attn, conv, wa, wc)


def _ffn_body(x_ref, g_ref, wg_ref, wu_ref, wd_ref, g2_ref, sum_ref, nrm_ref, xn_scr, acc_scr):
    f = pl.program_id(1)

    @pl.when(f == 0)
    def _():
        xn_scr[...] = _rms(x_ref[...], g_ref[...]).astype(BF16)
        acc_scr[...] = jnp.zeros(acc_scr.shape, F32)

    xn = xn_scr[...]
    gate = _dot(xn, wg_ref[...])
    h = gate * jax.nn.sigmoid(gate) * _dot(xn, wu_ref[...])
    acc_scr[...] += _dot(h.astype(BF16), wd_ref[...])

    @pl.when(f == pl.num_programs(1) - 1)
    def _():
        y = x_ref[...] + acc_scr[...]
        sum_ref[...] = y
        nrm_ref[...] = _rms(y, g2_ref[...])


def _ffn(x, g, wg, wu, wd, g2, tm, tf):
    n = x.shape[0]
    row = lambda i, f: (i, 0)
    const = lambda i, f: (0, 0)
    return pl.pallas_call(
        _ffn_body,
        grid=(n // tm, D_FF // tf),
        in_specs=[pl.BlockSpec((tm, D_MODEL), row), pl.BlockSpec((1, D_MODEL), const),
                  pl.BlockSpec((D_MODEL, tf), lambda i, f: (0, f)), pl.BlockSpec((D_MODEL, tf), lambda i, f: (0, f)),
                  pl.BlockSpec((tf, D_MODEL), lambda i, f: (f, 0)), pl.BlockSpec((1, D_MODEL), const)],
        out_specs=[pl.BlockSpec((tm, D_MODEL), row)] * 2,
        out_shape=[jax.ShapeDtypeStruct((n, D_MODEL), F32)] * 2,
        scratch_shapes=[pltpu.VMEM((tm, D_MODEL), BF16), pltpu.VMEM((tm, D_MODEL), F32)],
        compiler_params=_cparams(("parallel", "arbitrary")),
        name="ffn",
    )(x, g, wg, wu, wd, g2)


def _moe_route_body(x_ref, g_ref, wr_ref, tri_ref, etri_ref,
                    xn_ref, slot_ref, slot_t_ref, comb_ref, meta_ref, *, n_exp, rows_blk):
    xn = _rms(x_ref[...], g_ref[...]).astype(BF16)
    xn_ref[...] = xn
    logits = _dot(xn, wr_ref[...])
    lane = lax.broadcasted_iota(jnp.int32, logits.shape, 1).astype(F32)
    lg = jnp.where(lane < n_exp, logits, -jnp.inf)
    m1 = jnp.max(lg, axis=1, keepdims=True)
    i1 = jnp.min(jnp.where(lg == m1, lane, float(LANES)), axis=1, keepdims=True)
    lg2 = jnp.where(lane == i1, -jnp.inf, lg)
    m2 = jnp.max(lg2, axis=1, keepdims=True)
    i2 = jnp.min(jnp.where(lg2 == m2, lane, float(LANES)), axis=1, keepdims=True)
    e2 = jnp.exp(m2 - m1)
    comb_ref[...] = jnp.where(lane == i1, 1.0 / (1.0 + e2), 0.0) + jnp.where(lane == i2, e2 / (1.0 + e2), 0.0)
    member = jnp.where((lane == i1) | (lane == i2), 1.0, 0.0)
    n_tok = member.shape[0]
    pos = _dot(tri_ref[...], member.astype(BF16))
    cnt = pos[n_tok - 1:n_tok, :] + member[n_tok - 1:n_tok, :]
    padded = jnp.floor((cnt + (rows_blk - 0.5)) * (1.0 / rows_blk)) * rows_blk
    start = _dot(jnp.broadcast_to(padded, (SUBLANES, LANES)).astype(BF16), etri_ref[...])[0:1, :]
    slot = jnp.where(member > 0.0, start + pos, -1.0)
    slot_ref[...] = slot
    slot_t_ref[0] = jnp.transpose(slot)[0:SUBLANES, :]
    ends = start + padded
    first_row = lax.broadcasted_iota(jnp.int32, (1, LANES), 1).astype(F32) * rows_blk
    blk_exp = jnp.zeros((1, LANES), F32)
    for e in range(n_exp - 1):
        blk_exp = blk_exp + jnp.where(ends[:, e:e + 1] <= first_row, 1.0, 0.0)
    n_blk = jnp.floor((ends[:, n_exp - 1:n_exp] + 0.5) * (1.0 / rows_blk))
    row = lax.broadcasted_iota(jnp.int32, (SUBLANES, LANES), 0)
    meta = jnp.where(row == 0, blk_exp, jnp.where(row == 1, n_blk, 0.0))
    meta_ref[0] = meta.astype(jnp.int32)


def _moe_experts_body(be_ref, nb_ref, xn_ref, slot_ref, slot_t_ref, comb_ref, x_ref, wg_ref, wu_ref, wd_ref,
                      g2_ref, o_ref, xs_scr, ys_scr, *, rows_blk, nf):
    g, f, j = pl.program_id(0), pl.program_id(1), pl.program_id(2)
    nj = pl.num_programs(2)
    e = be_ref[g, j]
    used = j < nb_ref[g]
    n_tok = xn_ref.shape[0]
    rows = pl.ds(pl.multiple_of(j * rows_blk, rows_blk), rows_blk)
    first = j * rows_blk

    @pl.when((f == 0) & (j == 0))
    def _():
        o_ref[...] = x_ref[...]

    @pl.when(used & (f == 0))
    def _():
        srow = slot_t_ref[0, pl.ds(e, 1), :]
        want = (first + lax.broadcasted_iota(jnp.int32, (rows_blk, n_tok), 0)).astype(F32)
        onehot = jnp.where(srow == want, 1.0, 0.0).astype(BF16)
        xs_scr[rows, :] = _dot(onehot, xn_ref[...]).astype(BF16)

    @pl.when(used)
    def _():
        xs = xs_scr[rows, :]
        gate = _dot(xs, wg_ref[0])
        h = gate * jax.nn.sigmoid(gate) * _dot(xs, wu_ref[0])
        part = _dot(h.astype(BF16), wd_ref[0])

        if nf > 1:
            @pl.when(f == 0)
            def _():
                ys_scr[rows, :] = part.astype(BF16)

        if nf > 2:
            @pl.when((f > 0) & (f < nf - 1))
            def _():
                ys_scr[rows, :] = (ys_scr[rows, :].astype(F32) + part).astype(BF16)

        def scatter(total):
            lane = lax.broadcasted_iota(jnp.int32, (n_tok, LANES), 1)
            scol = jnp.sum(jnp.where(lane == e, slot_ref[...], 0.0), axis=1, keepdims=True)
            ccol = jnp.sum(jnp.where(lane == e, comb_ref[...], 0.0), axis=1, keepdims=True)
            want = (first + lax.broadcasted_iota(jnp.int32, (n_tok, rows_blk), 1)).astype(F32)
            onehot = jnp.where(scol == want, 1.0, 0.0).astype(BF16)
            tb = total.astype(BF16)
            for c in range(D_MODEL // (2 * LANES)):
                cols = slice(c * 2 * LANES, (c + 1) * 2 * LANES)
                o_ref[:, cols] += ccol * _dot(onehot, tb[:, cols])

        @pl.when(f == nf - 1)
        def _():
            scatter(part + ys_scr[rows, :].astype(F32) if nf > 1 else part)

    @pl.when((f == nf - 1) & (j == nj - 1))
    def _():
        o_ref[...] = _rms(o_ref[...], g2_ref[...])


def _moe(x, g, wr, wg, wu, wd, g2, tg, rows_blk, tf):
    n = x.shape[0]
    n_exp = wg.shape[0]
    n_groups = n // tg
    n_blk = 2 * tg // rows_blk + n_exp
    i = np.arange(tg)
    tri = jnp.asarray((i[None, :] < i[:, None]).astype(np.float32), BF16)
    i = np.arange(LANES)
    etri = jnp.asarray((i[:, None] < i[None, :]).astype(np.float32), BF16)
    row = lambda i: (i, 0)
    const = lambda i: (0, 0)
    xn, slot, slot_t, comb, meta = pl.pallas_call(
        functools.partial(_moe_route_body, n_exp=n_exp, rows_blk=rows_blk),
        grid=(n_groups,),
        in_specs=[pl.BlockSpec((tg, D_MODEL), row), pl.BlockSpec((1, D_MODEL), const),
                  pl.BlockSpec((D_MODEL, LANES), const), pl.BlockSpec((tg, tg), const),
                  pl.BlockSpec((LANES, LANES), const)],
        out_specs=[pl.BlockSpec((tg, D_MODEL), row), pl.BlockSpec((tg, LANES), row),
                   pl.BlockSpec((1, SUBLANES, tg), lambda i: (i, 0, 0)), pl.BlockSpec((tg, LANES), row),
                   pl.BlockSpec((1, SUBLANES, LANES), lambda i: (i, 0, 0))],
        out_shape=[jax.ShapeDtypeStruct((n, D_MODEL), BF16), jax.ShapeDtypeStruct((n, LANES), F32),
                   jax.ShapeDtypeStruct((n_groups, SUBLANES, tg), F32), jax.ShapeDtypeStruct((n, LANES), F32),
                   jax.ShapeDtypeStruct((n_groups, SUBLANES, LANES), jnp.int32)],
        compiler_params=_cparams(("parallel",)),
        name="moe_route",
    )(x, g, wr, tri, etri)
    blk_exp = meta[:, 0, :]
    blk_cnt = meta[:, 1, 0]
    per_g = lambda g, f, j, be, nb: (g, 0)
    once = pl.Buffered(1)
    return pl.pallas_call(
        functools.partial(_moe_experts_body, rows_blk=rows_blk, nf=D_FF // tf),
        grid_spec=pltpu.PrefetchScalarGridSpec(
            num_scalar_prefetch=2,
            grid=(n_groups, D_FF // tf, n_blk),
            in_specs=[pl.BlockSpec((tg, D_MODEL), per_g, pipeline_mode=once),
                      pl.BlockSpec((tg, LANES), per_g, pipeline_mode=once),
                      pl.BlockSpec((1, SUBLANES, tg), lambda g, f, j, be, nb: (g, 0, 0), pipeline_mode=once),
                      pl.BlockSpec((tg, LANES), per_g, pipeline_mode=once),
                      pl.BlockSpec((tg, D_MODEL), per_g, pipeline_mode=once),
                      pl.BlockSpec((1, D_MODEL, tf), lambda g, f, j, be, nb: (be[g, j], 0, f)),
                      pl.BlockSpec((1, D_MODEL, tf), lambda g, f, j, be, nb: (be[g, j], 0, f)),
                      pl.BlockSpec((1, tf, D_MODEL), lambda g, f, j, be, nb: (be[g, j], f, 0)),
                      pl.BlockSpec((1, D_MODEL), lambda g, f, j, be, nb: (0, 0))],
            out_specs=pl.BlockSpec((tg, D_MODEL), per_g, pipeline_mode=once),
            scratch_shapes=[pltpu.VMEM((n_blk * rows_blk, D_MODEL), BF16),
                            pltpu.VMEM((n_blk * rows_blk, D_MODEL), BF16)],
        ),
        out_shape=jax.ShapeDtypeStruct((n, D_MODEL), F32),
        compiler_params=pltpu.CompilerParams(dimension_semantics=("parallel", "arbitrary", "arbitrary"),
                                             vmem_limit_bytes=VMEM_LIMIT_MOE),
        name="moe_experts",
    )(blk_exp, blk_cnt, xn, slot, slot_t, comb, x, wg, wu, wd, g2)


def _group_ones():
    i = np.arange(LANES) // RWKV_HEAD
    return jnp.asarray((i[:, None] == i[None, :]).astype(np.float32), BF16)


def _rwkv_proj_body(xn_ref, xp_ref, sh_ref, mu_ref, vec_ref, wrkv_ref, w1_ref, w2_ref, a1_ref, a2_ref, g1_ref,
                    g2_ref, gm_ref, r_ref, lw_ref, k_ref, v_ref, kk_ref, ka_ref, g_ref, *, tiles_per_seq):
    xn = xn_ref[...]
    if tiles_per_seq is None:
        xp = xp_ref[...]
    else:
        first = pl.program_id(0) % tiles_per_seq == 0
        row0 = jnp.where(first, sh_ref[0], xp_ref[SUBLANES - 1:SUBLANES, :])
        r = lax.broadcasted_iota(jnp.int32, xn.shape, 0)
        xp = jnp.where(r == 0, row0, pltpu.roll(xn, 1, 0))
    xx = xp - xn
    mu = mu_ref[...]
    mix = lambda i: (xn + xx * mu[i:i + 1, :]).astype(BF16)
    vec = vec_ref[...]
    w0, a0, k_k, k_a = vec[0:1, :], vec[1:2, :], vec[2:3, :], vec[3:4, :]
    r_ref[...] = _dot(mix(0), wrkv_ref[0]).astype(r_ref.dtype)
    z = w0 + _dot(jnp.tanh(_dot(mix(1), w1_ref[...])).astype(BF16), w2_ref[...])
    lw_ref[...] = jax.nn.sigmoid(z) * (-float(np.exp(-0.5)))
    k = _dot(mix(2), wrkv_ref[1])
    v_ref[...] = _dot(mix(3), wrkv_ref[2]).astype(v_ref.dtype)
    a = jax.nn.sigmoid(a0 + _dot(_dot(mix(4), a1_ref[...]).astype(BF16), a2_ref[...]))
    g_ref[...] = _dot(jax.nn.sigmoid(_dot(mix(5), g1_ref[...])).astype(BF16), g2_ref[...]).astype(g_ref.dtype)
    kk = k * k_k
    kk = kk * lax.rsqrt(jnp.maximum(_group_sum(kk * kk, gm_ref[...]), 1e-24))
    kk_ref[...] = kk.astype(kk_ref.dtype)
    ka_ref[...] = (kk * a).astype(ka_ref.dtype)
    k_ref[...] = (k * (1.0 + (a - 1.0) * k_a)).astype(k_ref.dtype)


def _rwkv_proj(xn, xp, shift_prev, seq, mu, vec, wrkv, w1, w2, a1, a2, g1, g2, tm):
    n = xn.shape[0]
    row = lambda i: (i, 0)
    c2 = lambda i: (0, 0)
    c3 = lambda i: (0, 0, 0)
    full = lambda a: pl.BlockSpec(a.shape, c3 if a.ndim == 3 else c2)
    gm = _group_ones()
    consts = [mu, vec, wrkv, w1, w2, a1, a2, g1, g2, gm]
    if xp is None:
        tiles_per_seq = seq // tm
        prev_spec = pl.BlockSpec((SUBLANES, D_MODEL), lambda i: (jnp.maximum(i * (tm // SUBLANES) - 1, 0), 0))
        shift_spec = pl.BlockSpec((1, 1, D_MODEL), lambda i: (i // tiles_per_seq, 0, 0))
        xp = xn
    else:
        tiles_per_seq = None
        prev_spec = pl.BlockSpec((tm, D_MODEL), row)
        shift_spec = pl.BlockSpec((1, 1, D_MODEL), c3)
    dtypes = (BF16, F32, BF16, BF16, BF16, BF16, BF16)
    return pl.pallas_call(
        functools.partial(_rwkv_proj_body, tiles_per_seq=tiles_per_seq),
        grid=(n // tm,),
        in_specs=[pl.BlockSpec((tm, D_MODEL), row), prev_spec, shift_spec] + [full(a) for a in consts],
        out_specs=[pl.BlockSpec((tm, D_MODEL), row)] * 7,
        out_shape=[jax.ShapeDtypeStruct((n, D_MODEL), dt) for dt in dtypes],
        compiler_params=_cparams(("parallel",)),
        name="rwkv_proj",
    )(xn, xp, shift_prev[:, None, :], *consts)


def _wkv_body(r_ref, lw_ref, k_ref, v_ref, kk_ref, ka_ref, s0_ref, y_ref, sT_ref,
              s_scr, a_scr, r_scr, b_scr, k_scr, *, chunk):
    c = pl.program_id(1)
    n_pairs = D_MODEL // LANES
    hd = RWKV_HEAD
    zeros = jnp.zeros((hd, hd), F32)

    @pl.when(c == 0)
    def _():
        for p in range(n_pairs):
            top = jnp.concatenate([s0_ref[0, 2 * p], zeros], axis=1)
            bot = jnp.concatenate([zeros, s0_ref[0, 2 * p + 1]], axis=1)
            s_scr[p] = jnp.concatenate([top, bot], axis=0)

    row = lax.broadcasted_iota(jnp.int32, (chunk, chunk), 0)
    colm = lax.broadcasted_iota(jnp.int32, (chunk, chunk), 1)
    lw = lw_ref[...]
    tri = jnp.where(row >= colm, 1.0, 0.0).astype(BF16)
    lw_hi = lw.astype(BF16)
    lw_mid = (lw - lw_hi.astype(F32)).astype(BF16)
    lw_lo = (lw - lw_hi.astype(F32) - lw_mid.astype(F32)).astype(BF16)
    cum = _dot(tri, lw_hi) + _dot(tri, lw_mid) + _dot(tri, lw_lo)
    w_incl = jnp.exp(cum)
    w_inv = jnp.exp(-cum)
    a_scr[...] = -kk_ref[...] * jnp.exp(cum - lw)
    r_scr[...] = r_ref[...] * w_incl
    b_scr[...] = ka_ref[...] * w_inv
    k_scr[...] = k_ref[...] * w_inv
    w_last = w_incl[chunk - 1:chunk, :]
    n_dbl = max(1, int(np.ceil(np.log2(chunk))))

    c2 = 2 * chunk
    head_of_row = lax.broadcasted_iota(jnp.int32, (c2, LANES), 0) // chunk
    head_of_lane = lax.broadcasted_iota(jnp.int32, (c2, LANES), 1) // hd
    own_lanes = head_of_row == head_of_lane
    tr = lax.broadcasted_iota(jnp.int32, (c2, c2), 0)
    tc = lax.broadcasted_iota(jnp.int32, (c2, c2), 1)
    same = (tr // chunk) == (tc // chunk)
    strict = same & (tr % chunk > tc % chunk)
    incl = same & (tr % chunk >= tc % chunk)

    def block_diag(x):
        return jnp.where(own_lanes, jnp.concatenate([x, x], axis=0), 0.0).astype(BF16)

    pairs = range(n_pairs)
    sls = [slice(p * LANES, (p + 1) * LANES) for p in pairs]
    ar = [jnp.concatenate([block_diag(a_scr[:, sl]), block_diag(r_scr[:, sl])], axis=0) for sl in sls]
    bk = [jnp.concatenate([block_diag(b_scr[:, sl]), block_diag(k_scr[:, sl])], axis=0) for sl in sls]
    vm = [block_diag(v_ref[:, sl]) for sl in sls]
    s_old = [s_scr[p] for p in pairs]
    gram = [_dot_nt(ar[p], bk[p]) for p in pairs]
    xs = [_dot_nt(ar[p], s_old[p].astype(BF16)) for p in pairs]
    u = [xs[p][:c2, :] + _dot(jnp.where(strict, gram[p][:c2, c2:], 0.0).astype(BF16), vm[p]) for p in pairs]
    lp = [jnp.where(strict, gram[p][:c2, :c2], 0.0).astype(BF16) for p in pairs]
    for d in range(n_dbl):
        if d + 1 < n_dbl:
            t = [_dot(lp[p], jnp.concatenate([u[p].astype(BF16), lp[p]], axis=1)) for p in pairs]
            u = [u[p] + t[p][:, :LANES] for p in pairs]
            lp = [t[p][:, LANES:].astype(BF16) for p in pairs]
        else:
            u = [u[p] + _dot(lp[p], u[p].astype(BF16)) for p in pairs]
    uv = [jnp.concatenate([u[p].astype(BF16), vm[p]], axis=0) for p in pairs]
    m_r = [jnp.concatenate([jnp.where(incl, gram[p][c2:, :c2], 0.0), jnp.where(incl, gram[p][c2:, c2:], 0.0)],
                           axis=1).astype(BF16) for p in pairs]
    y = [xs[p][c2:, :] + _dot(m_r[p], uv[p]) for p in pairs]
    for p in pairs:
        y_ref[:, sls[p]] = y[p][:chunk, :] + y[p][chunk:, :]
    s_new = [(s_old[p] + _dot_tn(uv[p], bk[p])) * w_last[:, sls[p]] for p in pairs]
    for p in pairs:
        s_scr[p] = s_new[p]

    @pl.when(c == pl.num_programs(1) - 1)
    def _():
        for p in range(n_pairs):
            s = s_scr[p]
            sT_ref[0, 2 * p] = s[:hd, :hd]
            sT_ref[0, 2 * p + 1] = s[hd:, hd:]


def _wkv(r, lw, k, v, kk, ka, s0, batch, seq, chunk):
    nc = seq // chunk
    blk = lambda b, c: (b * nc + c, 0)
    st = lambda b, c: (b, 0, 0, 0)
    state = pl.BlockSpec((1, RWKV_HEADS, RWKV_HEAD, RWKV_HEAD), st)
    return pl.pallas_call(
        functools.partial(_wkv_body, chunk=chunk),
        grid=(batch, nc),
        in_specs=[pl.BlockSpec((chunk, D_MODEL), blk)] * 6 + [state],
        out_specs=[pl.BlockSpec((chunk, D_MODEL), blk), state],
        out_shape=[jax.ShapeDtypeStruct((batch * seq, D_MODEL), F32),
                   jax.ShapeDtypeStruct(s0.shape, F32)],
        scratch_shapes=[pltpu.VMEM((D_MODEL // LANES, LANES, LANES), F32)]
                       + [pltpu.VMEM((chunk, D_MODEL), F32)] * 4,
        compiler_params=_cparams(("parallel", "arbitrary")),
        name="wkv",
    )(r, lw, k, v, kk, ka, s0)


def _rwkv_out_body(x_ref, y_ref, r_ref, k_ref, v_ref, g_ref, vec_ref, gm_ref, wo_ref, o_ref):
    gm = gm_ref[...]
    vec = vec_ref[...]
    ln_w, ln_b, r_k = vec[0:1, :], vec[1:2, :], vec[2:3, :]
    y = y_ref[...]
    mean = _group_sum(y, gm) * (1.0 / RWKV_HEAD)
    d = y - mean
    var = _group_sum(d * d, gm) * (1.0 / RWKV_HEAD)
    yn = d * lax.rsqrt(var + GN_EPS) * ln_w + ln_b
    rk = r_ref[...].astype(F32) * k_ref[...].astype(F32) * r_k
    yn = yn + _group_sum(rk, gm) * v_ref[...].astype(F32)
    o_ref[...] = x_ref[...] + _dot((yn * g_ref[...].astype(F32)).astype(BF16), wo_ref[...])


def _rwkv_out(x, y, r, k, v, g, vec, wo, tm):
    n = x.shape[0]
    row = lambda i: (i, 0)
    const = lambda i: (0, 0)
    gm = _group_ones()
    return pl.pallas_call(
        _rwkv_out_body,
        grid=(n // tm,),
        in_specs=[pl.BlockSpec((tm, D_MODEL), row)] * 6
                 + [pl.BlockSpec(vec.shape, const), pl.BlockSpec(gm.shape, const), pl.BlockSpec(wo.shape, const)],
        out_specs=pl.BlockSpec((tm, D_MODEL), row),
        out_shape=jax.ShapeDtypeStruct((n, D_MODEL), F32),
        compiler_params=_cparams(("parallel",)),
        name="rwkv_out",
    )(x, y, r, k, v, g, vec, gm, wo)


def _pick_tile(n, want):
    t = min(n, want)
    while n % t:
        t //= 2
    return t


def _trunk(x, pos, conv_prev, shift_prev, wkv_prev, wts, sample):
    batch, seq, _ = x.shape
    n = batch * seq
    xf = x.reshape(n, D_MODEL)
    tm = _pick_tile(n, 512)

    tabs = _rope_tables(pos)
    if sample is not None:
        tabs = tuple(jnp.tile(t, (batch, 1)) for t in tabs)
    q, k, v, iq, ikw, cb, u = _proj_even(xf, wts['norm_mix_even'], wts['w_in'], tabs, tm)
    if sample is None:
        attn = _prompt_attention(q, iq, ikw, k, v, batch, seq)
        conv = _conv_prompt(cb, u, wts['conv_w'], seq, _pick_tile(seq, 512))
        conv_state = u.reshape(batch, seq, CONV_DIM)[:, seq - (CONV_W - 1):]
    else:
        cache_k, cache_v, cache_ik, page_table = sample
        tpad = SUBLANES - seq

        def stack(a, heads):
            a = a.reshape(batch, seq, heads, -1).transpose(0, 2, 1, 3)
            a = jnp.pad(a, ((0, 0), (0, 0), (0, tpad), (0, 0)))
            return a.reshape(batch, heads * SUBLANES, a.shape[-1])

        iq_st = stack(iq, N_IDX_HEADS)
        iw_st = stack(ikw[:, IDX_DIM:IDX_DIM + N_IDX_HEADS], N_IDX_HEADS)
        s_past = _sample_scores(page_table, iq_st, iw_st, cache_ik)[:, :seq]
        mask = _sample_select(s_past.reshape(n, -1), iq, ikw, seq)
        mask = jnp.pad(mask.reshape(batch, seq, -1), ((0, 0), (0, tpad), (0, 0)))
        q_st = stack(q, N_Q_HEADS).reshape(batch, N_KV_HEADS, Q_PER_KV * SUBLANES, HEAD_DIM)
        padk = lambda a: jnp.pad(jnp.swapaxes(a.reshape(batch, seq, KV_DIM), 1, 2),
                                 ((0, 0), (0, 0), (0, PAGE_SIZE - seq)))
        o = _sample_attention(page_table, q_st, mask, padk(k), padk(v), cache_k, cache_v)
        attn = (o.reshape(batch, N_KV_HEADS, Q_PER_KV, SUBLANES, HEAD_DIM)[:, :, :, :seq].transpose(0, 3, 2, 1, 4)
                .reshape(n, ATTN_DIM).astype(BF16))
        u3 = jnp.concatenate([conv_prev, u.reshape(batch, seq, CONV_DIM)], axis=1)
        conv_state = u3[:, seq:]
        conv_t = _conv_sample(jnp.swapaxes(cb.reshape(batch, seq, CONV_DIM), 0, 1), jnp.swapaxes(u3, 0, 1),
                              wts['conv_w'])
        conv = jnp.swapaxes(conv_t, 0, 1).reshape(n, CONV_DIM)
    x1 = _out_even(xf, attn, conv, wts['w_out_attn'], wts['w_out_conv'], tm)
    x2, xn2 = _ffn(x1, wts['norm_ffn_even'], wts['ffn_gate'], wts['ffn_up'], wts['ffn_down'],
                   wts['norm_mix_odd'], tm, D_FF // 2)

    xn3 = xn2.reshape(batch, seq, D_MODEL)
    tm1 = _pick_tile(n, 256)
    xp = None
    if seq % tm1:
        xp = jnp.concatenate([shift_prev[:, None, :], xn3[:, :-1]], axis=1).reshape(n, D_MODEL)
    r, lw, kr, vr, kk, ka, g = _rwkv_proj(xn2, xp, shift_prev, seq, wts['rwkv_mu'], wts['rwkv_vec_in'],
                                          wts['rwkv_w_rkv'], wts['rwkv_w1'], wts['rwkv_w2'], wts['rwkv_a1'],
                                          wts['rwkv_a2'], wts['rwkv_g1'], wts['rwkv_g2'], tm1)
    if seq % WKV_CHUNK:
        sp = -(-seq // WKV_CHUNK) * WKV_CHUNK
        padded = [jnp.pad(a.reshape(batch, seq, D_MODEL), ((0, 0), (0, sp - seq), (0, 0)))
                  .reshape(batch * sp, D_MODEL) for a in (r, lw, kr, vr, kk, ka)]
        y, wkv_state = _wkv(*padded, wkv_prev, batch, sp, WKV_CHUNK)
        y = y.reshape(batch, sp, D_MODEL)[:, :seq].reshape(n, D_MODEL)
    else:
        y, wkv_state = _wkv(r, lw, kr, vr, kk, ka, wkv_prev, batch, seq, WKV_CHUNK)
    x3 = _rwkv_out(x2, y, r, kr, vr, g, wts['rwkv_vec_out'], wts['rwkv_w_o'], _pick_tile(n, 256))
    yf = _moe(x3, wts['norm_ffn_odd'], wts['moe_router'], wts['moe_gate'], wts['moe_up'], wts['moe_down'],
              wts['norm_final'], _pick_tile(n, MOE_GROUP), min(MOE_BLOCK_ROWS, _pick_tile(n, MOE_GROUP)), D_FF // 2)

    k4 = k.reshape(1, batch, seq, N_KV_HEADS, HEAD_DIM)
    v4 = v.reshape(1, batch, seq, N_KV_HEADS, HEAD_DIM)
    ik3 = ikw[:, :IDX_DIM].reshape(1, batch, seq, IDX_DIM)
    return (yf.reshape(batch, seq, D_MODEL), k4, v4, ik3, conv_state[None],
            xn3[:, -1][None], wkv_state[None])


def _prepare_weights(norm_mix_even, w_in_even, conv_w, w_out_even, norm_ffn_even, ffn_gate, ffn_up, ffn_down,
                     norm_mix_odd, rwkv_mu, rwkv_w_rkv, rwkv_w0, rwkv_w1, rwkv_w2, rwkv_a0, rwkv_a1, rwkv_a2,
                     rwkv_g1, rwkv_g2, rwkv_k_k, rwkv_k_a, rwkv_r_k, rwkv_ln_w, rwkv_ln_b, rwkv_w_o,
                     norm_ffn_odd, moe_router, moe_gate, moe_up, moe_down, norm_final):
    w_in = w_in_even[0]
    o = np.cumsum((0, ATTN_DIM, KV_DIM, KV_DIM, N_IDX_HEADS * IDX_DIM, IDX_DIM, N_IDX_HEADS,
                   CONV_DIM, CONV_DIM, CONV_DIM))
    pad = jnp.zeros((D_MODEL, LANES - IDX_DIM - N_IDX_HEADS), F32)
    w_in = jnp.concatenate([w_in[:, :o[4]], w_in[:, o[4]:o[6]], pad, w_in[:, o[6]:]], axis=1).astype(BF16)
    row = lambda a: a.reshape(1, -1)
    zeros = jnp.zeros((1, D_MODEL), F32)
    return dict(
        norm_mix_even=row(norm_mix_even[0]), w_in=w_in, conv_w=conv_w[0],
        w_out_attn=(w_out_even[0, :ATTN_DIM].reshape(N_KV_HEADS, Q_PER_KV, HEAD_DIM, D_MODEL)
                    .transpose(1, 0, 2, 3).reshape(ATTN_DIM, D_MODEL).astype(BF16)),
        w_out_conv=w_out_even[0, ATTN_DIM:].astype(BF16),
        norm_ffn_even=row(norm_ffn_even[0]),
        ffn_gate=ffn_gate[0].astype(BF16), ffn_up=ffn_up[0].astype(BF16), ffn_down=ffn_down[0].astype(BF16),
        norm_mix_odd=row(norm_mix_odd[0]), rwkv_mu=jnp.concatenate([rwkv_mu[0], zeros, zeros], axis=0),
        rwkv_vec_in=jnp.concatenate([row(rwkv_w0[0]), row(rwkv_a0[0]), row(rwkv_k_k[0]), row(rwkv_k_a[0]),
                                     zeros, zeros, zeros, zeros], axis=0),
        rwkv_w_rkv=rwkv_w_rkv[0].astype(BF16),
        rwkv_w1=rwkv_w1[0].astype(BF16), rwkv_w2=rwkv_w2[0].astype(BF16),
        rwkv_a1=rwkv_a1[0].astype(BF16), rwkv_a2=rwkv_a2[0].astype(BF16),
        rwkv_g1=rwkv_g1[0].astype(BF16), rwkv_g2=rwkv_g2[0].astype(BF16),
        rwkv_vec_out=jnp.concatenate([row(rwkv_ln_w[0]), row(rwkv_ln_b[0]), row(rwkv_r_k[0]),
                                      zeros, zeros, zeros, zeros, zeros], axis=0),
        rwkv_w_o=rwkv_w_o[0].astype(BF16),
        norm_ffn_odd=row(norm_ffn_odd[0]),
        moe_router=jnp.pad(moe_router[0], ((0, 0), (0, LANES - N_EXPERTS))).astype(BF16),
        moe_gate=moe_gate[0].astype(BF16), moe_up=moe_up[0].astype(BF16), moe_down=moe_down[0].astype(BF16),
        norm_final=row(norm_final),
    )


def kernel(x_prompt, x_sample, cache_k, cache_v, cache_idx_k, state_conv, state_shift, state_wkv, page_table, norm_mix_even, w_in_even, conv_w, w_out_even, norm_ffn_even, ffn_gate, ffn_up, ffn_down, norm_mix_odd, rwkv_mu, rwkv_w_rkv, rwkv_w0, rwkv_w1, rwkv_w2, rwkv_a0, rwkv_a1, rwkv_a2, rwkv_g1, rwkv_g2, rwkv_k_k, rwkv_k_a, rwkv_r_k, rwkv_ln_w, rwkv_ln_b, rwkv_w_o, norm_ffn_odd, moe_router, moe_gate, moe_up, moe_down, norm_final):
    assert w_in_even.shape[0] == 1 and rwkv_mu.shape[0] == 1, "one even and one odd layer"
    wts = _prepare_weights(norm_mix_even, w_in_even, conv_w, w_out_even, norm_ffn_even, ffn_gate, ffn_up, ffn_down,
                           norm_mix_odd, rwkv_mu, rwkv_w_rkv, rwkv_w0, rwkv_w1, rwkv_w2, rwkv_a0, rwkv_a1, rwkv_a2,
                           rwkv_g1, rwkv_g2, rwkv_k_k, rwkv_k_a, rwkv_r_k, rwkv_ln_w, rwkv_ln_b, rwkv_w_o,
                           norm_ffn_odd, moe_router, moe_gate, moe_up, moe_down, norm_final)
    b, t = x_prompt.shape[:2]
    bd, tn = x_sample.shape[:2]
    n_pool = cache_k.shape[1]
    past = page_table.shape[1] * PAGE_SIZE
    pos_prompt = jnp.arange(t, dtype=jnp.int32)
    pos_sample = past + jnp.arange(tn, dtype=jnp.int32)
    zeros = lambda *s: jnp.zeros(s, F32)
    out_p = _trunk(x_prompt, pos_prompt, zeros(b, CONV_W - 1, CONV_DIM), zeros(b, D_MODEL),
                   zeros(b, RWKV_HEADS, RWKV_HEAD, RWKV_HEAD), wts, None)
    pages_t = lambda c: jnp.transpose(c[0], (0, 2, 3, 1)).reshape(n_pool, KV_DIM, PAGE_SIZE)
    sample = (pages_t(cache_k), pages_t(cache_v), jnp.swapaxes(cache_idx_k[0], 1, 2), page_table)
    out_s = _trunk(x_sample, pos_sample, state_conv[0], state_shift[0], state_wkv[0], wts, sample)
    y_p, k_p, v_p, ik_p, conv_p, shift_p, wkv_p = out_p
    y_s, k_s, v_s, ik_s, conv_s, shift_s, wkv_s = out_s
    return (y_p, y_s, k_p, v_p, ik_p, k_s, v_s, ik_s, conv_p, conv_s, shift_p, shift_s, wkv_p, wkv_s)
```
